```python
import math
import jax, jax.numpy as jnp
from jax import lax
import numpy as np

D_MODEL = 1024
BATCH = 16
SEQ = 4096
DEPTH = 4

MEM_LEN = 256
N_MEM_HEADS = 4
MEM_HEAD_DIM = D_MODEL // N_MEM_HEADS
D_FF = 2816
MIX_WIDTH = D_MODEL
GROUP_WIDTH = MIX_WIDTH // 4
MLSTM_HEADS = 4
MLSTM_HEAD_DIM = GROUP_WIDTH // MLSTM_HEADS
MLSTM_CHUNK = 64
CONV_WIDTH = 4
POOL_GROUPS = 4
POOL_WINDOWS = (2, 4, 8, 16)
POOL_GROUP_DIM = GROUP_WIDTH // POOL_GROUPS
DIL_HEADS = 4
DIL_HEAD_DIM = GROUP_WIDTH // DIL_HEADS
DIL_PATTERNS = ((128, 1), (512, 4), (2048, 16))
DIFF_HEADS = 4
DIFF_HEAD_DIM = GROUP_WIDTH // DIFF_HEADS
DIFF_QK_HALF = DIFF_HEAD_DIM // 2
ATTN_BLOCK = 128
T5_BUCKETS = 32
T5_MAX_DIST = 2048
N_BIAS_HEADS = DIL_HEADS + DIFF_HEADS
IN_WIDTHS = (GROUP_WIDTH, GROUP_WIDTH, GROUP_WIDTH, GROUP_WIDTH, MLSTM_HEADS, MLSTM_HEADS,
             GROUP_WIDTH,
             GROUP_WIDTH, GROUP_WIDTH, GROUP_WIDTH,
             GROUP_WIDTH, GROUP_WIDTH, GROUP_WIDTH)
IN_WIDTH = sum(IN_WIDTHS)
RMS_EPS = 1e-6
SUBLN_EPS = 1e-5

kernel_name = "hybrid_parallel_groups_mlstm_pool_dilated_diff"


def _rms_norm(x, g, eps=RMS_EPS):
    xf = x.astype(jnp.float32)
    y = xf * lax.rsqrt(jnp.mean(xf * xf, axis=-1, keepdims=True) + eps)
    return (y * g.astype(jnp.float32)).astype(x.dtype)


def _swiglu(u, w_gate, w_up, w_down):
    return (jax.nn.silu(u @ w_gate) * (u @ w_up)) @ w_down


def _split_cols(z, widths):
    out, off = [], 0
    for w in widths:
        out.append(z[..., off:off + w])
        off += w
    return out


def _t5_bucket(dist):
    max_exact = T5_BUCKETS // 2
    d = jnp.maximum(dist, 1).astype(jnp.float32)
    large = max_exact + (jnp.log(d / max_exact) / math.log(T5_MAX_DIST / max_exact)
                         * (T5_BUCKETS - max_exact)).astype(jnp.int32)
    large = jnp.minimum(large, T5_BUCKETS - 1)
    return jnp.where(dist < max_exact, dist, large)


def _causal_conv(u, w, b):
    K, S = w.shape[0], u.shape[1]
    up = jnp.pad(u, ((0, 0), (K - 1, 0), (0, 0)))
    out = b
    for j in range(K):
        out = out + up[:, j:j + S] * w[j]
    return out


def _mlstm(q, k, v, o, ig, fg, conv_w, conv_b, gate_b, norm_g):
    B, S, _ = q.shape
    H, dh, L = MLSTM_HEADS, MLSTM_HEAD_DIM, MLSTM_CHUNK
    f32 = jnp.float32
    qk = jax.nn.silu(_causal_conv(jnp.concatenate([q, k], axis=-1), conv_w, conv_b))
    q, k = qk[..., :GROUP_WIDTH], qk[..., GROUP_WIDTH:]
    nc = S // L

    def to_chunks(t):
        return t.astype(f32).reshape(B, nc, L, H, dh).transpose(1, 0, 3, 2, 4)

    def gate_chunks(t):
        return t.reshape(B, nc, L, H).transpose(1, 0, 3, 2)

    qc, kc, vc = to_chunks(q), to_chunks(k) * (dh ** -0.5), to_chunks(v)
    gb = gate_b.astype(f32)
    igc = gate_chunks(ig.astype(f32) + gb[:H])
    lfc = gate_chunks(jax.nn.log_sigmoid(fg.astype(f32) + gb[H:]))
    causal = jnp.tril(jnp.ones((L, L), dtype=bool))

    def step(carry, xs):
        C, n, m = carry
        qq, kk, vv, ii, lf = xs
        b = jnp.cumsum(lf, axis=-1)
        dmat = b[..., :, None] - b[..., None, :] + ii[..., None, :]
        dmat = jnp.where(causal, dmat, -jnp.inf)
        inter_log = b + m[..., None]
        m_t = jnp.maximum(inter_log, jnp.max(dmat, axis=-1))
        inter_w = jnp.exp(inter_log - m_t)
        sc = jnp.einsum('bhtd,bhsd->bhts', qq, kk) * jnp.exp(dmat - m_t[..., None])
        num = inter_w[..., None] * jnp.einsum('bhtd,bhde->bhte', qq, C) \
            + jnp.einsum('bhts,bhse->bhte', sc, vv)
        den = inter_w * jnp.einsum('bhtd,bhd->bht', qq, n) + jnp.sum(sc, axis=-1)
        hh = num / jnp.maximum(jnp.abs(den), jnp.exp(-m_t))[..., None]
        b_last = b[..., -1]
        m_new = m_t[..., -1]
        w_s = jnp.exp(b_last[..., None] - b + ii - m_new[..., None])
        carry_w = jnp.exp(b_last + m - m_new)
        C_new = carry_w[..., None, None] * C + jnp.einsum('bhs,bhsd,bhse->bhde', w_s, kk, vv)
        n_new = carry_w[..., None] * n + jnp.einsum('bhs,bhsd->bhd', w_s, kk)
        return (C_new, n_new, m_new), hh

    init = (jnp.zeros((B, H, dh, dh), f32), jnp.zeros((B, H, dh), f32), jnp.zeros((B, H), f32))
    _, hc = lax.scan(step, init, (qc, kc, vc, igc, lfc))
    hs = hc.transpose(1, 0, 3, 2, 4).reshape(B, S, H, dh)
    mu = jnp.mean(hs, axis=-1, keepdims=True)
    var = jnp.mean(jnp.square(hs - mu), axis=-1, keepdims=True)
    hn = (hs - mu) * lax.rsqrt(var + RMS_EPS) * norm_g.astype(f32).reshape(H, dh)
    y = hn.reshape(B, S, H * dh) * jax.nn.sigmoid(o.astype(f32))
    return y.astype(o.dtype)


def _multiscale_pool(u, w_grp, scale):
    B, S, _ = u.shape
    G, c = POOL_GROUPS, POOL_GROUP_DIM
    f32 = jnp.float32
    ug = u.astype(f32).reshape(B, S, G, c)
    cs = jnp.pad(jnp.cumsum(ug, axis=1), ((0, 0), (1, 0), (0, 0), (0, 0)))
    t = jnp.arange(S)[:, None]
    win = jnp.array(POOL_WINDOWS, dtype=jnp.int32)[None, :]
    lo = jnp.maximum(t + 1 - win, 0)
    g = jnp.arange(G)[None, :]
    total = cs[:, 1:] - cs[:, lo, g]
    mean = total / jnp.minimum(t + 1, win).astype(f32)[None, :, :, None]
    y = jnp.einsum('bsgc,gcd->bsgd', mean - ug, w_grp.astype(f32))
    return (y.reshape(B, S, G * c) * scale.astype(f32)).astype(u.dtype)


def _dilated_branch(q, k, v, bias, dil, n_back):
    B, S, H, dh = q.shape
    f32 = jnp.float32
    L = S // dil
    blk = n_back
    nb = -(-L // blk)
    Lp = nb * blk

    def to_blocks(t):
        t = t.reshape(B, L, dil, H, dh).transpose(0, 2, 3, 1, 4)
        t = jnp.pad(t, ((0, 0), (0, 0), (0, 0), (0, Lp - L), (0, 0)))
        return t.reshape(B, dil, H, nb, blk, dh)

    def with_prev(t):
        prev = jnp.pad(t, ((0, 0), (0, 0), (0, 0), (1, 0), (0, 0), (0, 0)))[:, :, :, :nb]
        return jnp.concatenate([prev, t], axis=4)

    qb = to_blocks(q)
    kb = with_prev(to_blocks(k))
    vb = with_prev(to_blocks(v)).astype(f32)
    s = jnp.einsum('brhnqc,brhnkc->brhnqk', qb, kb).astype(f32) * (dh ** -0.5)
    qi = jnp.arange(blk)[:, None]
    kj = jnp.arange(2 * blk)[None, :]
    delta = qi + blk - kj
    key_idx = jnp.arange(nb)[:, None, None] * blk - blk + kj[None]
    valid = ((delta >= 0) & (delta <= n_back))[None] & (key_idx >= 0)
    s = s + bias[:, jnp.clip(delta, 0, n_back)][None, None, :, None]
    s = jnp.where(valid[None, None, None], s, -jnp.inf)
    m = jnp.max(s, axis=-1, keepdims=True)
    p = jnp.exp(s - m)
    l = jnp.sum(p, axis=-1)
    o = jnp.einsum('brhnqk,brhnkc->brhnqc', p, vb) / l[..., None]
    lse = m[..., 0] + jnp.log(l)
    o = o.reshape(B, dil, H, Lp, dh)[:, :, :, :L].transpose(0, 3, 1, 2, 4).reshape(B, S, H, dh)
    lse = lse.reshape(B, dil, H, Lp)[:, :, :, :L].transpose(0, 3, 1, 2).reshape(B, S, H)
    return o, lse


def _dilated_attention(q, k, v, dil_biases):
    B, S, H, dh = q.shape
    outs, lses = [], []
    for (w, d), b in zip(DIL_PATTERNS, dil_biases):
        o, lse = _dilated_branch(q, k, v, b, d, w // d)
        outs.append(o)
        lses.append(lse)
    wts = jax.nn.softmax(jnp.stack(lses), axis=0)
    o = jnp.sum(wts[..., None] * jnp.stack(outs), axis=0)
    return o.reshape(B, S, H * dh).astype(q.dtype)


def _diff_attention(q, k, v, lam_vecs, subln_g, bias_by_dist, lam_init):
    B, S, H, dv = q.shape
    dk = DIFF_QK_HALF
    f32 = jnp.float32
    scale = dk ** -0.5
    lv = lam_vecs.astype(f32)
    lam = jnp.exp(jnp.sum(lv[0] * lv[1])) - jnp.exp(jnp.sum(lv[2] * lv[3])) + lam_init
    kh = k.transpose(0, 2, 1, 3)
    k1, k2 = kh[..., :dk], kh[..., dk:]
    vh = v.transpose(0, 2, 1, 3).astype(f32)
    nq = S // ATTN_BLOCK
    qb = q.transpose(0, 2, 1, 3).reshape(B, H, nq, ATTN_BLOCK, 2 * dk).transpose(2, 0, 1, 3, 4)
    key_pos = jnp.arange(S)

    def block(args):
        qblk, start = args
        dist = start + jnp.arange(ATTN_BLOCK)[:, None] - key_pos[None, :]
        causal = dist >= 0
        bias = bias_by_dist[:, jnp.clip(dist, 0, S - 1)]

        def attn_map(qq, kk):
            s = jnp.einsum('bhqd,bhkd->bhqk', qq, kk).astype(f32) * scale + bias
            return jax.nn.softmax(jnp.where(causal, s, -jnp.inf), axis=-1)

        a = attn_map(qblk[..., :dk], k1) - lam * attn_map(qblk[..., dk:], k2)
        return jnp.einsum('bhqk,bhkd->bhqd', a, vh)

    o = lax.map(block, (qb, jnp.arange(nq) * ATTN_BLOCK))
    o = o.transpose(1, 0, 3, 2, 4).reshape(B, S, H, dv)
    o = o * lax.rsqrt(jnp.mean(o * o, axis=-1, keepdims=True) + SUBLN_EPS) * subln_g.astype(f32)
    o = o * (1.0 - lam_init)
    return o.reshape(B, S, H * dv).astype(q.dtype)


def _hybrid_mixer(u, w_in, w_out, conv_w, conv_b, gate_b, mlstm_g, pool_w, pool_scale,
                  dil_biases, lam_vecs, subln_g, diff_bias, lam_init):
    B, S, _ = u.shape
    z = u @ w_in
    qa, ka, va, oa, ia, fa, pb, qc, kc, vc, qd, kd, vd = _split_cols(z, IN_WIDTHS)
    ya = _mlstm(qa, ka, va, oa, ia, fa, conv_w, conv_b, gate_b, mlstm_g)
    yb = _multiscale_pool(pb, pool_w, pool_scale)
    yc = _dilated_attention(qc.reshape(B, S, DIL_HEADS, DIL_HEAD_DIM),
                            kc.reshape(B, S, DIL_HEADS, DIL_HEAD_DIM),
                            vc.reshape(B, S, DIL_HEADS, DIL_HEAD_DIM), dil_biases)
    yd = _diff_attention(qd.reshape(B, S, DIFF_HEADS, DIFF_HEAD_DIM),
                         kd.reshape(B, S, DIFF_HEADS, DIFF_HEAD_DIM),
                         vd.reshape(B, S, DIFF_HEADS, DIFF_HEAD_DIM),
                         lam_vecs, subln_g, diff_bias, lam_init)
    y = jnp.concatenate([ya, yb, yc, yd], axis=-1)
    return y @ w_out


def _cross_attention(u, mem_n, wq, wkv, wo):
    B, S, D = u.shape
    M = mem_n.shape[1]
    q = (u @ wq).reshape(B, S, N_MEM_HEADS, MEM_HEAD_DIM)
    kv = mem_n @ wkv
    k = kv[..., :D].reshape(B, M, N_MEM_HEADS, MEM_HEAD_DIM)
    v = kv[..., D:].reshape(B, M, N_MEM_HEADS, MEM_HEAD_DIM)
    s = jnp.einsum('bshd,bmhd->bhsm', q, k).astype(jnp.float32) * (MEM_HEAD_DIM ** -0.5)
    p = jax.nn.softmax(s, axis=-1)
    o = jnp.einsum('bhsm,bmhd->bshd', p, v.astype(jnp.float32)).astype(u.dtype)
    return o.reshape(B, S, D) @ wo


def setup_inputs(seed: int = 0) -> dict:
    key = jax.random.key(seed)
    keys = jax.random.split(key, 40)
    counter = [0]
    f32 = jnp.float32
    Lr, D, F = DEPTH, D_MODEL, D_FF

    def next_key():
        kk = keys[counter[0]]
        counter[0] += 1
        return kk

    def nrm(shape, scale):
        return jax.random.normal(next_key(), shape, f32) * scale

    def gain(shape):
        return 1.0 + 0.05 * jax.random.normal(next_key(), shape, f32)

    fgate_init = jnp.broadcast_to(jnp.linspace(3.0, 6.0, MLSTM_HEADS, dtype=f32), (Lr, MLSTM_HEADS))
    return {
        "x": nrm((BATCH, SEQ, D), 1.0),
        "mem": nrm((BATCH, MEM_LEN, D), 1.0),
        "t5_bias": nrm((T5_BUCKETS, N_BIAS_HEADS), 0.5),
        "ffn1_norm": gain((Lr, D)),
        "ffn1_w_gate": nrm((Lr, D, F), D ** -0.5),
        "ffn1_w_up": nrm((Lr, D, F), D ** -0.5),
        "ffn1_w_down": nrm((Lr, F, D), F ** -0.5),
        "mix_norm": gain((Lr, D)),
        "w_in": nrm((Lr, D, IN_WIDTH), D ** -0.5),
        "mlstm_conv_w": nrm((Lr, CONV_WIDTH, 2 * GROUP_WIDTH), CONV_WIDTH ** -0.5),
        "mlstm_conv_b": nrm((Lr, 2 * GROUP_WIDTH), 0.02),
        "mlstm_gate_b": jnp.concatenate([nrm((Lr, MLSTM_HEADS), 0.1),
                                         fgate_init + nrm((Lr, MLSTM_HEADS), 0.1)], axis=-1),
        "mlstm_norm": gain((Lr, GROUP_WIDTH)),
        "pool_w": nrm((Lr, POOL_GROUPS, POOL_GROUP_DIM, POOL_GROUP_DIM), POOL_GROUP_DIM ** -0.5),
        "pool_scale": gain((Lr, GROUP_WIDTH)),
        "diff_lambda": nrm((Lr, 4, DIFF_QK_HALF), 0.1),
        "diff_subln": gain((Lr, DIFF_HEAD_DIM)),
        "w_out": nrm((Lr, MIX_WIDTH, D), MIX_WIDTH ** -0.5),
        "xattn_norm": gain((Lr, D)),
        "mem_norm": gain((Lr, D)),
        "xattn_wq": nrm((Lr, D, D), D ** -0.5),
        "xattn_wkv": nrm((Lr, D, 2 * D), D ** -0.5),
        "xattn_wo": nrm((Lr, D, D), D ** -0.5),
        "ffn2_norm": gain((Lr, D)),
        "ffn2_w_gate": nrm((Lr, D, F), D ** -0.5),
        "ffn2_w_up": nrm((Lr, D, F), D ** -0.5),
        "ffn2_w_down": nrm((Lr, F, D), F ** -0.5),
        "final_norm": gain((D,)),
    }


def reference(x, mem, t5_bias, ffn1_norm, ffn1_w_gate, ffn1_w_up, ffn1_w_down, mix_norm, w_in,
              mlstm_conv_w, mlstm_conv_b, mlstm_gate_b, mlstm_norm, pool_w, pool_scale,
              diff_lambda, diff_subln, w_out, xattn_norm, mem_norm, xattn_wq, xattn_wkv, xattn_wo,
              ffn2_norm, ffn2_w_gate, ffn2_w_up, ffn2_w_down, final_norm):
    S = x.shape[1]
    dil_biases = [t5_bias[_t5_bucket(jnp.arange(w // d + 1) * d), :DIL_HEADS].T
                  for (w, d) in DIL_PATTERNS]
    diff_bias = t5_bias[_t5_bucket(jnp.arange(S)), DIL_HEADS:].T
    h = x
    for l in range(DEPTH):
        lam_init = 0.8 - 0.6 * math.exp(-0.3 * l)
        h = h + 0.5 * _swiglu(_rms_norm(h, ffn1_norm[l]), ffn1_w_gate[l], ffn1_w_up[l], ffn1_w_down[l])
        h = h + _hybrid_mixer(_rms_norm(h, mix_norm[l]), w_in[l], w_out[l],
                              mlstm_conv_w[l], mlstm_conv_b[l], mlstm_gate_b[l], mlstm_norm[l],
                              pool_w[l], pool_scale[l], dil_biases,
                              diff_lambda[l], diff_subln[l], diff_bias, lam_init)
        h = h + _cross_attention(_rms_norm(h, xattn_norm[l]), _rms_norm(mem, mem_norm[l]),
                                 xattn_wq[l], xattn_wkv[l], xattn_wo[l])
        h = h + 0.5 * _swiglu(_rms_norm(h, ffn2_norm[l]), ffn2_w_gate[l], ffn2_w_up[l], ffn2_w_down[l])
    return _rms_norm(h, final_norm)
```

```python
import functools
import math

import jax
import jax.numpy as jnp
import numpy as np
from jax import lax
from jax.experimental import pallas as pl
from jax.experimental.pallas import tpu as pltpu

F32 = jnp.float32
BF16 = jnp.bfloat16
HIGHEST = lax.Precision.HIGHEST

D_MODEL = 1024
D_FF = 2816
DEPTH = 4
GROUP = 256
HEADS = 4
HEAD_DIM = GROUP // HEADS
MEM_LEN = 256
MEM_HEADS = 4
MEM_HEAD_DIM = D_MODEL // MEM_HEADS
MLSTM_CHUNK = 64
CONV_WIDTH = 4
POOL_WINDOWS = (2, 4, 8, 16)
DIL_PATTERNS = ((128, 1), (512, 4), (2048, 16))
DIL_BACK = 128
DIFF_QK_HALF = HEAD_DIM // 2
T5_BUCKETS = 32
T5_MAX_DIST = 2048
RMS_EPS = 1e-6
SUBLN_EPS = 1e-5
NEG = -1e30
TILE = 128

VMEM_LIMIT_BYTES = 56 * 1024 * 1024


def _rms(xf, g, eps=RMS_EPS):
    return xf * lax.rsqrt(jnp.mean(xf * xf, axis=-1, keepdims=True) + eps) * g


def _const_spec(shape):
    zeros = (0,) * len(shape)
    return pl.BlockSpec(shape, lambda *_: zeros, pipeline_mode=pl.Buffered(1))


def _params(*sem):
    return pltpu.CompilerParams(dimension_semantics=sem, vmem_limit_bytes=VMEM_LIMIT_BYTES)


def _group_mask(rows, cols, row_group, col_group):
    r = lax.broadcasted_iota(jnp.int32, (rows, cols), 0) // row_group
    c = lax.broadcasted_iota(jnp.int32, (rows, cols), 1) // col_group
    return r == c


def _tile_rows(x, reps, mask):
    return jnp.where(mask, jnp.concatenate([x] * reps, axis=0), jnp.zeros((), x.dtype))


def _dot(a, b):
    return jnp.dot(a, b, preferred_element_type=F32)


def _dot_nt(a, b):
    return lax.dot_general(a, b, (((1,), (1,)), ((), ())), preferred_element_type=F32)


def _dot_tn(a, b):
    return lax.dot_general(a, b, (((0,), (0,)), ((), ())), preferred_element_type=F32)


def _group_sum(x, ones_bd):
    return jnp.dot(x, ones_bd, precision=HIGHEST, preferred_element_type=F32)


FFN_ROWS = 512
FFN_COLS = 256


def _ffn_body(x_ref, g_ref, wg_ref, wu_ref, wd_ref, fg_ref, o_ref, act_ref, *, final):
    x = x_ref[...]
    u = _rms(x, g_ref[...]).astype(BF16)
    for c in range(D_FF // FFN_COLS):
        sl = slice(c * FFN_COLS, (c + 1) * FFN_COLS)
        gate = _dot(u, wg_ref[:, sl])
        up = _dot(u, wu_ref[:, sl])
        act_ref[:, sl] = (gate * jax.nn.sigmoid(gate) * up).astype(BF16)
    y = x + 0.5 * _dot(act_ref[...], wd_ref[...])
    if final:
        y = _rms(y, fg_ref[...])
    o_ref[...] = y


def _ffn(h, g, wg, wu, wd, fg, *, final):
    n = h.shape[0]
    row = pl.BlockSpec((FFN_ROWS, D_MODEL), lambda i: (i, 0))
    return pl.pallas_call(
        functools.partial(_ffn_body, final=final),
        grid=(n // FFN_ROWS,),
        in_specs=[row, _const_spec((1, D_MODEL)), _const_spec((D_MODEL, D_FF)),
                  _const_spec((D_MODEL, D_FF)), _const_spec((D_FF, D_MODEL)),
                  _const_spec((1, D_MODEL))],
        out_specs=row,
        out_shape=jax.ShapeDtypeStruct((n, D_MODEL), F32),
        scratch_shapes=[pltpu.VMEM((FFN_ROWS, D_FF), BF16)],
        compiler_params=_params("parallel"),
        name="ffn_final" if final else "ffn",
    )(h, g, wg, wu, wd, fg)


PROJ_ROWS = 512
PROJ_OUTS = (("a", 4 * GROUP, F32), ("g", 2 * GROUP, F32), ("p", GROUP, F32),
             ("c", 3 * GROUP, BF16), ("d", 3 * GROUP, BF16))
PROJ_WIDTH = sum(w for _, w, _ in PROJ_OUTS)


def _in_proj_body(x_ref, g_ref, w_ref, *o_refs):
    u = _rms(x_ref[...], g_ref[...]).astype(BF16)
    off = 0
    for o_ref, (_, width, dtype) in zip(o_refs, PROJ_OUTS):
        for c in range(width // GROUP):
            z = _dot(u, w_ref[:, off + c * GROUP: off + (c + 1) * GROUP])
            o_ref[:, c * GROUP:(c + 1) * GROUP] = z.astype(dtype)
        off += width


def _in_proj(h, g, w):
    n = h.shape[0]
    return pl.pallas_call(
        _in_proj_body,
        grid=(n // PROJ_ROWS,),
        in_specs=[pl.BlockSpec((PROJ_ROWS, D_MODEL), lambda i: (i, 0)),
                  _const_spec((1, D_MODEL)), _const_spec((D_MODEL, PROJ_WIDTH))],
        out_specs=[pl.BlockSpec((PROJ_ROWS, w_), lambda i: (i, 0)) for _, w_, _ in PROJ_OUTS],
        out_shape=[jax.ShapeDtypeStruct((n, w_), dt) for _, w_, dt in PROJ_OUTS],
        compiler_params=_params("parallel"),
        name="in_proj",
    )(h, g, w)


ML_ROWS = 512
ML_HALO = 8


def _chunk_scan(x, rin, op, fill):
    s = 1
    while s < MLSTM_CHUNK:
        x = op(x, jnp.where(rin >= s, pltpu.roll(x, s, 0), fill))
        s *= 2
    return x


def _mlstm_body(za_ref, zg_ref, cw_ref, cb_ref, gb_ref, ng_ref, y_ref,
                buf_ref, tail_ref, hh_ref, c_ref, n_ref, m_ref):
    L = MLSTM_CHUNK

    @pl.when(pl.program_id(1) == 0)
    def _():
        tail_ref[...] = jnp.zeros_like(tail_ref)
        c_ref[...] = jnp.zeros_like(c_ref)
        n_ref[...] = jnp.zeros_like(n_ref)
        m_ref[...] = jnp.zeros_like(m_ref)

    buf_ref[0:ML_HALO, :] = tail_ref[...]
    buf_ref[ML_HALO:, :] = za_ref[:, 0:2 * GROUP]
    tail_ref[...] = za_ref[ML_ROWS - ML_HALO:, 0:2 * GROUP]
    conv = cb_ref[...]
    for j in range(CONV_WIDTH):
        conv = conv + buf_ref[pl.ds(ML_HALO - (CONV_WIDTH - 1) + j, ML_ROWS), :] * cw_ref[j:j + 1, :]
    qk = conv * jax.nn.sigmoid(conv)
    q = qk[:, :GROUP]
    k = qk[:, GROUP:] * (HEAD_DIM ** -0.5)
    v = za_ref[:, 2 * GROUP:3 * GROUP].astype(BF16)

    ii = zg_ref[:, :GROUP] + gb_ref[:, :GROUP]
    fx = zg_ref[:, GROUP:] + gb_ref[:, GROUP:]
    lf = jnp.minimum(fx, 0.0) - jnp.log1p(jnp.exp(-jnp.abs(fx)))
    rin = lax.broadcasted_iota(jnp.int32, (ML_ROWS, GROUP), 0) % L
    b = _chunk_scan(lf, rin, jnp.add, 0.0)
    a = ii - b
    ca = _chunk_scan(a, rin, jnp.maximum, NEG)

    bd = _group_mask(GROUP, GROUP, HEAD_DIM, HEAD_DIM)
    ones_bd = bd.astype(F32)
    row = lax.broadcasted_iota(jnp.int32, (L, GROUP), 0)
    key = lax.broadcasted_iota(jnp.int32, (L, GROUP), 1) % L
    causal = key <= row
    diag = key == row

    m_prev = m_ref[...]
    for c in range(ML_ROWS // L):
        rs = slice(c * L, (c + 1) * L)
        q_c, k_c, v_c = q[rs], k[rs], v[rs]
        q_b = q_c.astype(BF16)
        a_c, b_c = a[rs], b[rs]
        g = jnp.maximum(m_prev, ca[rs])
        g_last = g[L - 1:L]
        a_row = jnp.sum(jnp.where(diag, a_c, 0.0), axis=0, keepdims=True)
        decay = jnp.exp(jnp.where(causal, a_row - g, NEG))
        sc = _dot_nt(q_b, _tile_rows(k_c.astype(BF16), HEADS, bd)) * decay
        inter = jnp.exp(m_prev - g)
        num = inter * _dot(q_b, c_ref[...].astype(BF16)) + _dot(sc.astype(BF16), _tile_rows(v_c, HEADS, bd))
        den = inter * _group_sum(q_c * n_ref[...], ones_bd) + _group_sum(sc, ones_bd)
        hh_ref[rs, :] = num / jnp.maximum(jnp.abs(den), jnp.exp(-(b_c + g)))
        kw = k_c * jnp.exp(a_c - g_last)
        carry = jnp.exp(m_prev - g_last)
        c_ref[...] = carry * c_ref[...] + jnp.where(bd, _dot_tn(kw.astype(BF16), v_c), 0.0)
        n_ref[...] = carry * n_ref[...] + jnp.sum(kw, axis=0, keepdims=True)
        m_prev = b_c[L - 1:L] + g_last
    m_ref[...] = m_prev

    hh = hh_ref[...]
    mu = _group_sum(hh, ones_bd) * (1.0 / HEAD_DIM)
    dev = hh - mu
    var = _group_sum(dev * dev, ones_bd) * (1.0 / HEAD_DIM)
    o_gate = jax.nn.sigmoid(za_ref[:, 3 * GROUP:])
    y_ref[...] = (dev * lax.rsqrt(var + RMS_EPS) * ng_ref[...] * o_gate).astype(BF16)


def _mlstm(za, zg, conv_w, conv_b, gate_b, norm_g, *, batch, seq):
    za = za.reshape(batch, seq, 4 * GROUP)
    zg = zg.reshape(batch, seq, 2 * GROUP)
    y = pl.pallas_call(
        _mlstm_body,
        grid=(batch, seq // ML_ROWS),
        in_specs=[pl.BlockSpec((None, ML_ROWS, 4 * GROUP), lambda b, j: (b, j, 0)),
                  pl.BlockSpec((None, ML_ROWS, 2 * GROUP), lambda b, j: (b, j, 0)),
                  _const_spec((CONV_WIDTH, 2 * GROUP)), _const_spec((1, 2 * GROUP)),
                  _const_spec((1, 2 * GROUP)), _const_spec((1, GROUP))],
        out_specs=pl.BlockSpec((None, ML_ROWS, GROUP), lambda b, j: (b, j, 0)),
        out_shape=jax.ShapeDtypeStruct((batch, seq, GROUP), BF16),
        scratch_shapes=[pltpu.VMEM((ML_ROWS + ML_HALO, 2 * GROUP), F32),
                        pltpu.VMEM((ML_HALO, 2 * GROUP), F32),
                        pltpu.VMEM((ML_ROWS, GROUP), F32),
                        pltpu.VMEM((GROUP, GROUP), F32),
                        pltpu.VMEM((1, GROUP), F32),
                        pltpu.VMEM((1, GROUP), F32)],
        compiler_params=_params("parallel", "arbitrary"),
        name="mlstm",
    )(za, zg, conv_w, conv_b, gate_b, norm_g)
    return y.reshape(batch * seq, GROUP)


POOL_ROWS = 512
POOL_HALO = 16


def _pool_body(u_ref, w_ref, s_ref, y_ref, buf_ref, tail_ref):
    j = pl.program_id(1)

    @pl.when(j == 0)
    def _():
        tail_ref[...] = jnp.zeros_like(tail_ref)

    buf_ref[0:POOL_HALO, :] = tail_ref[...]
    buf_ref[POOL_HALO:, :] = u_ref[...]
    tail_ref[...] = u_ref[POOL_ROWS - POOL_HALO:, :]
    sums, s = [], buf_ref[...]
    for shift in (1, 2, 4, 8):
        s = s + pltpu.roll(s, shift, 0)
        sums.append(s[POOL_HALO:])
    u = u_ref[...]
    lane_group = lax.broadcasted_iota(jnp.int32, (POOL_ROWS, GROUP), 1) // HEAD_DIM
    t = j * POOL_ROWS + lax.broadcasted_iota(jnp.int32, (POOL_ROWS, GROUP), 0)
    total, win = sums[3], jnp.full((POOL_ROWS, GROUP), POOL_WINDOWS[3], jnp.int32)
    for gi in (2, 1, 0):
        total = jnp.where(lane_group == gi, sums[gi], total)
        win = jnp.where(lane_group == gi, POOL_WINDOWS[gi], win)
    mean = total / jnp.minimum(t + 1, win).astype(F32)
    y = _dot((mean - u).astype(BF16), w_ref[...]) * s_ref[...]
    y_ref[...] = y.astype(BF16)


def _pool(zp, w_bd, scale, *, batch, seq):
    zp = zp.reshape(batch, seq, GROUP)
    y = pl.pallas_call(
        _pool_body,
        grid=(batch, seq // POOL_ROWS),
        in_specs=[pl.BlockSpec((None, POOL_ROWS, GROUP), lambda b, j: (b, j, 0)),
                  _const_spec((GROUP, GROUP)), _const_spec((1, GROUP))],
        out_specs=pl.BlockSpec((None, POOL_ROWS, GROUP), lambda b, j: (b, j, 0)),
        out_shape=jax.ShapeDtypeStruct((batch, seq, GROUP), BF16),
        scratch_shapes=[pltpu.VMEM((POOL_ROWS + POOL_HALO, GROUP), F32),
                        pltpu.VMEM((POOL_HALO, GROUP), F32)],
        compiler_params=_params("parallel", "arbitrary"),
        name="pool",
    )(zp, w_bd, scale)
    return y.reshape(batch * seq, GROUP)


def _toeplitz_body(w_ref, o_ref):
    x = jnp.broadcast_to(w_ref[...], (TILE, 2 * TILE))
    o_ref[...] = pltpu.roll(x, 0, 1, stride=1, stride_axis=0)[:, :TILE]


def _toeplitz(rows):
    n = rows.shape[0]
    return pl.pallas_call(
        _toeplitz_body,
        grid=(n,),
        in_specs=[pl.BlockSpec((None, 1, 2 * TILE), lambda i: (i, 0, 0))],
        out_specs=pl.BlockSpec((None, TILE, TILE), lambda i: (i, 0, 0)),
        out_shape=jax.ShapeDtypeStruct((n, TILE, TILE), F32),
        compiler_params=_params("parallel"),
        name="toeplitz",
    )(rows.reshape(n, 1, 2 * TILE))


_TOEPLITZ_X = np.where(np.arange(2 * TILE) <= TILE, -np.arange(2 * TILE), 2 * TILE - np.arange(2 * TILE))


def _t5_bucket(dist):
    max_exact = T5_BUCKETS // 2
    d = jnp.maximum(dist, 1).astype(F32)
    large = max_exact + (jnp.log(d / max_exact) / math.log(T5_MAX_DIST / max_exact)
                         * (T5_BUCKETS - max_exact)).astype(jnp.int32)
    large = jnp.minimum(large, T5_BUCKETS - 1)
    return jnp.where(dist < max_exact, dist, large)


def _bias_tiles(t5_bias, seq):
    x = jnp.asarray(_TOEPLITZ_X, jnp.int32)
    rows = []
    for (w, d) in DIL_PATTERNS:
        tab = t5_bias[_t5_bucket(jnp.arange(DIL_BACK + 1) * d), :HEADS].T
        prev = jnp.where(x <= 0, tab[:, jnp.clip(x + DIL_BACK, 0, DIL_BACK)], NEG)
        cur = jnp.where(x >= 0, tab[:, jnp.clip(x, 0, DIL_BACK)], NEG)
        rows.append(jnp.stack([prev, cur], axis=1))
    dil = _toeplitz(jnp.stack(rows).reshape(-1, 2 * TILE))
    dil = dil.reshape(len(DIL_PATTERNS), HEADS, 2, TILE, TILE).transpose(0, 3, 1, 2, 4)
    dil = dil.reshape(len(DIL_PATTERNS), TILE, HEADS * 2 * TILE)
    tab = t5_bias[_t5_bucket(jnp.arange(seq)), HEADS:].T
    nblk = seq // TILE
    dist = jnp.arange(nblk)[:, None] * TILE + x[None, :]
    rows = jnp.where(dist >= 0, tab[:, jnp.clip(dist, 0, seq - 1)], NEG)
    diff = _toeplitz(rows.transpose(1, 0, 2).reshape(-1, 2 * TILE))
    diff = diff.reshape(nblk, HEADS, TILE, TILE).transpose(0, 2, 1, 3).reshape(nblk, TILE, HEADS * TILE)
    return dil, diff


def _dil_body(*refs, subs, merge):
    if merge:
        (q_ref, k_ref, v_ref, kp_ref, vp_ref, bias_ref,
         o1_ref, l1_ref, o2_ref, l2_ref, y_ref) = refs
    else:
        q_ref, k_ref, v_ref, kp_ref, vp_ref, bias_ref, o_ref, lse_ref = refs
    first = pl.program_id(2) == 0
    kmask = _group_mask(HEADS * 2 * TILE, GROUP, 2 * TILE, HEAD_DIM)
    head_of_lane = lax.broadcasted_iota(jnp.int32, (TILE, GROUP), 1) // HEAD_DIM
    ones_exp = kmask.astype(BF16)
    lane = lax.broadcasted_iota(jnp.int32, (TILE, HEADS * 2 * TILE), 1)
    before_start = jnp.where((lane % (2 * TILE)) < TILE, NEG, 0.0)
    for sb in range(subs):
        rs = slice(sb * TILE, (sb + 1) * TILE)
        if sb == 0:
            k2 = jnp.concatenate([kp_ref[...], k_ref[rs, :]], axis=0)
            v2 = jnp.concatenate([vp_ref[...], v_ref[rs, :]], axis=0)
        else:
            k2 = k_ref[(sb - 1) * TILE:(sb + 1) * TILE, :]
            v2 = v_ref[(sb - 1) * TILE:(sb + 1) * TILE, :]
        s = _dot_nt(q_ref[rs, :], _tile_rows(k2, HEADS, kmask)) + bias_ref[...]
        if sb == 0:
            s = s + jnp.where(first, before_start, 0.0)
        p, m_x = [], jnp.zeros((TILE, GROUP), F32)
        for h in range(HEADS):
            s_h = s[:, h * 2 * TILE:(h + 1) * 2 * TILE]
            m_h = jnp.max(s_h, axis=-1, keepdims=True)
            p.append(jnp.exp(s_h - m_h))
            m_x = jnp.where(head_of_lane == h, m_h, m_x)
        p = jnp.concatenate(p, axis=-1).astype(BF16)
        acc = _dot(p, _tile_rows(v2, HEADS, kmask))
        l_x = _dot(p, ones_exp)
        o = acc / l_x
        lse = m_x + jnp.log(l_x)
        if merge:
            o1, l1, o2, l2 = o1_ref[rs, :], l1_ref[rs, :], o2_ref[rs, :], l2_ref[rs, :]
            top = jnp.maximum(jnp.maximum(l1, l2), lse)
            w1, w2, w3 = jnp.exp(l1 - top), jnp.exp(l2 - top), jnp.exp(lse - top)
            y_ref[rs, :] = ((w1 * o1 + w2 * o2 + w3 * o) / (w1 + w2 + w3)).astype(BF16)
        else:
            o_ref[rs, :] = o
            lse_ref[rs, :] = lse


def _dilated_pattern(zc, bias, dil, *, batch, seq, merge_with=None):
    length = seq // dil
    rows = min(length, 512)
    subs = rows // TILE
    zc = zc.reshape(batch, length, dil * 3 * GROUP)
    blk = lambda which: pl.BlockSpec((None, rows, GROUP), lambda b, r, n: (b, n, 3 * r + which))
    prev = lambda which: pl.BlockSpec(
        (None, TILE, GROUP), lambda b, r, n: (b, jnp.maximum(n * subs - 1, 0), 3 * r + which))
    nat = pl.BlockSpec((None, rows, GROUP), lambda b, r, n: (b, n, r))
    in_specs = [blk(0), blk(1), blk(2), prev(1), prev(2), _const_spec((TILE, HEADS * 2 * TILE))]
    args = [zc, zc, zc, zc, zc, bias]
    if merge_with is None:
        out_specs = [nat, nat]
        out_shape = [jax.ShapeDtypeStruct((batch, length, dil * GROUP), F32)] * 2
    else:
        in_specs += [nat] * len(merge_with)
        args += [t.reshape(batch, length, dil * GROUP) for t in merge_with]
        out_specs = nat
        out_shape = jax.ShapeDtypeStruct((batch, length, dil * GROUP), BF16)
    out = pl.pallas_call(
        functools.partial(_dil_body, subs=subs, merge=merge_with is not None),
        grid=(batch, dil, length // rows),
        in_specs=in_specs, out_specs=out_specs, out_shape=out_shape,
        compiler_params=_params("parallel", "parallel", "parallel"),
        name=f"dilated_d{dil}",
    )(*args)
    if merge_with is None:
        return [t.reshape(batch * seq, GROUP) for t in out]
    return out.reshape(batch * seq, GROUP)


def _dilated(zc, dil_bias, *, batch, seq):
    o1, l1 = _dilated_pattern(zc, dil_bias[0], DIL_PATTERNS[0][1], batch=batch, seq=seq)
    o2, l2 = _dilated_pattern(zc, dil_bias[1], DIL_PATTERNS[1][1], batch=batch, seq=seq)
    return _dilated_pattern(zc, dil_bias[2], DIL_PATTERNS[2][1], batch=batch, seq=seq,
                            merge_with=(o1, l1, o2, l2))


def _diff_body(q_ref, k_ref, v_ref, bias_ref, lam_ref, sg_ref, y_ref, *, lam_init):
    i = pl.program_id(1)
    scale = DIFF_QK_HALF ** -0.5
    lane_in_head = lax.broadcasted_iota(jnp.int32, (HEADS * TILE, GROUP), 1) % HEAD_DIM
    vmask = _group_mask(HEADS * TILE, GROUP, TILE, HEAD_DIM)
    kmasks = [vmask & ((lane_in_head >= mp * DIFF_QK_HALF) & (lane_in_head < (mp + 1) * DIFF_QK_HALF))
              for mp in range(2)]
    ones_exp = vmask.astype(BF16)
    head_of_lane = lax.broadcasted_iota(jnp.int32, (TILE, GROUP), 1) // HEAD_DIM
    q = q_ref[...]

    def step(j, carry):
        ms, accs, ls = carry
        rows = pl.ds(pl.multiple_of(j * TILE, TILE), TILE)
        k_t, v_t = k_ref[rows, :], v_ref[rows, :]
        bias = bias_ref[i - j]
        v_exp = _tile_rows(v_t, HEADS, vmask)
        new_ms, new_accs, new_ls = [], [], []
        for mp in range(2):
            s = _dot_nt(q, _tile_rows(k_t, HEADS, kmasks[mp])) * scale + bias
            p, alpha = [], jnp.zeros((TILE, GROUP), F32)
            for h in range(HEADS):
                s_h = s[:, h * TILE:(h + 1) * TILE]
                m_old = ms[mp * HEADS + h]
                m_new = jnp.maximum(m_old, jnp.max(s_h, axis=-1, keepdims=True))
                p.append(jnp.exp(s_h - m_new))
                alpha = jnp.where(head_of_lane == h, jnp.exp(m_old - m_new), alpha)
                new_ms.append(m_new)
            p = jnp.concatenate(p, axis=-1).astype(BF16)
            new_accs.append(alpha * accs[mp] + _dot(p, v_exp))
            new_ls.append(alpha * ls[mp] + _dot(p, ones_exp))
        return tuple(new_ms), tuple(new_accs), tuple(new_ls)

    init = (tuple(jnp.full((TILE, 1), NEG, F32) for _ in range(2 * HEADS)),
            tuple(jnp.zeros((TILE, GROUP), F32) for _ in range(2)),
            tuple(jnp.zeros((TILE, GROUP), F32) for _ in range(2)))
    _, accs, ls = lax.fori_loop(0, i + 1, step, init)

    lv = lam_ref[...]
    lam = (jnp.exp(jnp.sum(lv[0:1] * lv[1:2], axis=-1, keepdims=True))
           - jnp.exp(jnp.sum(lv[2:3] * lv[3:4], axis=-1, keepdims=True)) + lam_init)
    o = accs[0] / ls[0] - lam * (accs[1] / ls[1])
    ones_bd = _group_mask(GROUP, GROUP, HEAD_DIM, HEAD_DIM).astype(F32)
    ms_o = _group_sum(o * o, ones_bd) * (1.0 / HEAD_DIM)
    o = o * lax.rsqrt(ms_o + SUBLN_EPS) * sg_ref[...]
    y_ref[...] = (o * (1.0 - lam_init)).astype(BF16)


def _diff_attention(zd, bias, lam_vecs, subln_g, *, lam_init, batch, seq):
    zd = zd.reshape(batch, seq, 3 * GROUP)
    nblk = seq // TILE
    y = pl.pallas_call(
        functools.partial(_diff_body, lam_init=lam_init),
        grid=(batch, nblk),
        in_specs=[pl.BlockSpec((None, TILE, GROUP), lambda b, i: (b, i, 0)),
                  pl.BlockSpec((None, seq, GROUP), lambda b, i: (b, 0, 1)),
                  pl.BlockSpec((None, seq, GROUP), lambda b, i: (b, 0, 2)),
                  _const_spec((nblk, TILE, HEADS * TILE)),
                  _const_spec((4, DIFF_QK_HALF)), _const_spec((1, GROUP))],
        out_specs=pl.BlockSpec((None, TILE, GROUP), lambda b, i: (b, i, 0)),
        out_shape=jax.ShapeDtypeStruct((batch, seq, GROUP), BF16),
        compiler_params=_params("parallel", "arbitrary"),
        name="diff_attn",
    )(zd, zd, zd, bias, lam_vecs, subln_g)
    return y.reshape(batch * seq, GROUP)


OUT_ROWS = 512


def _out_proj_body(h_ref, ya_ref, yb_ref, yc_ref, yd_ref, w_ref, o_ref):
    y = h_ref[...]
    for gi, y_ref in enumerate((ya_ref, yb_ref, yc_ref, yd_ref)):
        y = y + _dot(y_ref[...], w_ref[gi * GROUP:(gi + 1) * GROUP, :])
    o_ref[...] = y


def _out_proj(h, ya, yb, yc, yd, w):
    n = h.shape[0]
    row = pl.BlockSpec((OUT_ROWS, D_MODEL), lambda i: (i, 0))
    grp = pl.BlockSpec((OUT_ROWS, GROUP), lambda i: (i, 0))
    return pl.pallas_call(
        _out_proj_body,
        grid=(n // OUT_ROWS,),
        in_specs=[row, grp, grp, grp, grp, _const_spec((D_MODEL, D_MODEL))],
        out_specs=row,
        out_shape=jax.ShapeDtypeStruct((n, D_MODEL), F32),
        compiler_params=_params("parallel"),
        name="out_proj",
    )(h, ya, yb, yc, yd, w)


KV_ROWS = 512


def _mem_kv_body(m_ref, g_ref, w_ref, k_ref, v_ref):
    u = _rms(m_ref[...], g_ref[...]).astype(BF16)
    for c in range(D_MODEL // GROUP):
        sl = slice(c * GROUP, (c + 1) * GROUP)
        k_ref[:, sl] = _dot(u, w_ref[:, sl]).astype(BF16)
        v_ref[:, sl] = _dot(u, w_ref[:, D_MODEL + c * GROUP: D_MODEL + (c + 1) * GROUP]).astype(BF16)


def _mem_kv(mem, g, w):
    n = mem.shape[0]
    row = pl.BlockSpec((KV_ROWS, D_MODEL), lambda i: (i, 0))
    return pl.pallas_call(
        _mem_kv_body,
        grid=(n // KV_ROWS,),
        in_specs=[row, _const_spec((1, D_MODEL)), _const_spec((D_MODEL, 2 * D_MODEL))],
        out_specs=[row, row],
        out_shape=[jax.ShapeDtypeStruct((n, D_MODEL), BF16)] * 2,
        compiler_params=_params("parallel"),
        name="mem_kv",
    )(mem, g, w)


XATTN_ROWS = 512


def _xattn_body(x_ref, g_ref, wq_ref, k_ref, v_ref, wo_ref, o_ref, q_scr, a_scr):
    x = x_ref[...]
    u = _rms(x, g_ref[...]).astype(BF16)
    for c in range(D_MODEL // GROUP):
        sl = slice(c * GROUP, (c + 1) * GROUP)
        q_scr[:, sl] = _dot(u, wq_ref[:, sl]).astype(BF16)
    for h in range(MEM_HEADS):
        sl = slice(h * MEM_HEAD_DIM, (h + 1) * MEM_HEAD_DIM)
        s = _dot_nt(q_scr[:, sl], k_ref[:, sl]) * (MEM_HEAD_DIM ** -0.5)
        e = jnp.exp(s - jnp.max(s, axis=-1, keepdims=True))
        l = jnp.sum(e, axis=-1, keepdims=True)
        a_scr[:, sl] = (_dot(e.astype(BF16), v_ref[:, sl]) / l).astype(BF16)
    o_ref[...] = x + _dot(a_scr[...], wo_ref[...])


def _xattn(h, g, wq, k, v, wo, *, batch, seq):
    h3 = h.reshape(batch, seq, D_MODEL)
    k3 = k.reshape(batch, MEM_LEN, D_MODEL)
    v3 = v.reshape(batch, MEM_LEN, D_MODEL)
    row = pl.BlockSpec((None, XATTN_ROWS, D_MODEL), lambda b, i: (b, i, 0))
    mem = pl.BlockSpec((None, MEM_LEN, D_MODEL), lambda b, i: (b, 0, 0))
    out = pl.pallas_call(
        _xattn_body,
        grid=(batch, seq // XATTN_ROWS),
        in_specs=[row, _const_spec((1, D_MODEL)), _const_spec((D_MODEL, D_MODEL)), mem, mem,
                  _const_spec((D_MODEL, D_MODEL))],
        out_specs=row,
        out_shape=jax.ShapeDtypeStruct((batch, seq, D_MODEL), F32),
        scratch_shapes=[pltpu.VMEM((XATTN_ROWS, D_MODEL), BF16), pltpu.VMEM((XATTN_ROWS, D_MODEL), BF16)],
        compiler_params=_params("parallel", "parallel"),
        name="xattn",
    )(h3, g, wq, k3, v3, wo)
    return out.reshape(batch * seq, D_MODEL)


def _per_head_lanes(x):
    return jnp.repeat(x, HEAD_DIM, axis=-1)


def _in_proj_weight(w_in):
    g = GROUP
    q_a, k_a, v_a, o_a = (w_in[:, i * g:(i + 1) * g] for i in range(4))
    ig = w_in[:, 4 * g:4 * g + HEADS]
    fg = w_in[:, 4 * g + HEADS:4 * g + 2 * HEADS]
    rest = w_in[:, 4 * g + 2 * HEADS:]
    pool, q_c, k_c, v_c, q_d, k_d, v_d = (rest[:, i * g:(i + 1) * g] for i in range(7))
    q_c = q_c * (HEAD_DIM ** -0.5)
    cols = [q_a, k_a, v_a, o_a, _per_head_lanes(ig), _per_head_lanes(fg), pool,
            q_c, k_c, v_c, q_d, k_d, v_d]
    return jnp.concatenate(cols, axis=1).astype(BF16)


def _block_diag(w):
    g, c, _ = w.shape
    eye = jnp.eye(g, dtype=w.dtype)
    return (eye[:, None, :, None] * w[:, :, None, :]).reshape(g * c, g * c)


def kernel(x, mem, t5_bias, ffn1_norm, ffn1_w_gate, ffn1_w_up, ffn1_w_down, mix_norm, w_in,
           mlstm_conv_w, mlstm_conv_b, mlstm_gate_b, mlstm_norm, pool_w, pool_scale,
           diff_lambda, diff_subln, w_out, xattn_norm, mem_norm, xattn_wq, xattn_wkv, xattn_wo,
           ffn2_norm, ffn2_w_gate, ffn2_w_up, ffn2_w_down, final_norm):
    batch, seq, _ = x.shape
    n = batch * seq
    dil_bias, diff_bias = _bias_tiles(t5_bias, seq)
    h = x.reshape(n, D_MODEL)
    mem2 = mem.reshape(batch * MEM_LEN, D_MODEL)
    row = lambda v: v.reshape(1, -1)
    for l in range(DEPTH):
        lam_init = 0.8 - 0.6 * math.exp(-0.3 * l)
        h = _ffn(h, row(ffn1_norm[l]), ffn1_w_gate[l].astype(BF16), ffn1_w_up[l].astype(BF16),
                 ffn1_w_down[l].astype(BF16), row(final_norm), final=False)
        za, zg, zp, zc, zd = _in_proj(h, row(mix_norm[l]), _in_proj_weight(w_in[l]))
        ya = _mlstm(za, zg, mlstm_conv_w[l], row(mlstm_conv_b[l]),
                    row(_per_head_lanes(mlstm_gate_b[l].reshape(2, HEADS))), row(mlstm_norm[l]),
                    batch=batch, seq=seq)
        yb = _pool(zp, _block_diag(pool_w[l]).astype(BF16), row(pool_scale[l]), batch=batch, seq=seq)
        yc = _dilated(zc, dil_bias, batch=batch, seq=seq)
        yd = _diff_attention(zd, diff_bias, diff_lambda[l], row(jnp.tile(diff_subln[l], HEADS)),
                             lam_init=lam_init, batch=batch, seq=seq)
        h = _out_proj(h, ya, yb, yc, yd, w_out[l].astype(BF16))
        k_mem, v_mem = _mem_kv(mem2, row(mem_norm[l]), xattn_wkv[l].astype(BF16))
        h = _xattn(h, row(xattn_norm[l]), xattn_wq[l].astype(BF16), k_mem, v_mem,
                   xattn_wo[l].astype(BF16), batch=batch, seq=seq)
        h = _ffn(h, row(ffn2_norm[l]), ffn2_w_gate[l].astype(BF16), ffn2_w_up[l].astype(BF16),
                 ffn2_w_down[l].astype(BF16), row(final_norm), final=(l == DEPTH - 1))
    return h.reshape(batch, seq, D_MODEL)
```

```python
import functools
import math

import jax
import jax.numpy as jnp
import numpy as np
from jax import lax
from jax.experimental import pallas as pl
from jax.experimental.pallas import tpu as pltpu

F32 = jnp.float32
BF16 = jnp.bfloat16
HIGHEST = lax.Precision.HIGHEST

D_MODEL = 1024
D_FF = 2816
DEPTH = 4
GROUP = 256
HEADS = 4
HEAD_DIM = GROUP // HEADS
MEM_LEN = 256
MEM_HEADS = 4
MEM_HEAD_DIM = D_MODEL // MEM_HEADS
MLSTM_CHUNK = 64
CONV_WIDTH = 4
POOL_WINDOWS = (2, 4, 8, 16)
DIL_PATTERNS = ((128, 1), (512, 4), (2048, 16))
DIL_BACK = 128
DIFF_QK_HALF = HEAD_DIM // 2
T5_BUCKETS = 32
T5_MAX_DIST = 2048
RMS_EPS = 1e-6
SUBLN_EPS = 1e-5
NEG = -1e30
LOG2E = math.log2(math.e)
DIFF_MIN_OFFSET = -3
TILE = 128

VMEM_LIMIT_BYTES = 56 * 1024 * 1024


def _rms(xf, g, eps=RMS_EPS):
    return xf * lax.rsqrt(jnp.mean(xf * xf, axis=-1, keepdims=True) + eps) * g


def _const_spec(shape):
    zeros = (0,) * len(shape)
    return pl.BlockSpec(shape, lambda *_: zeros, pipeline_mode=pl.Buffered(1))


def _params(*sem):
    return pltpu.CompilerParams(dimension_semantics=sem, vmem_limit_bytes=VMEM_LIMIT_BYTES)


def _group_mask(rows, cols, row_group, col_group):
    r = lax.broadcasted_iota(jnp.int32, (rows, cols), 0) // row_group
    c = lax.broadcasted_iota(jnp.int32, (rows, cols), 1) // col_group
    return r == c


def _tile_rows(x, reps, mask):
    return jnp.where(mask, jnp.concatenate([x] * reps, axis=0), jnp.zeros((), x.dtype))


def _dot(a, b):
    return jnp.dot(a, b, preferred_element_type=F32)


def _dot_nt(a, b):
    return lax.dot_general(a, b, (((1,), (1,)), ((), ())), preferred_element_type=F32)


def _dot_tn(a, b):
    return lax.dot_general(a, b, (((0,), (0,)), ((), ())), preferred_element_type=F32)


def _group_sum(x, ones_bd):
    return jnp.dot(x, ones_bd, precision=HIGHEST, preferred_element_type=F32)


FFN_ROWS = 512
FFN_COLS = 256


def _ffn_body(x_ref, g_ref, wg_ref, wu_ref, wd_ref, fg_ref, o_ref, act_ref, *, final):
    x = x_ref[...]
    u = _rms(x, g_ref[...]).astype(BF16)
    for c in range(D_FF // FFN_COLS):
        sl = slice(c * FFN_COLS, (c + 1) * FFN_COLS)
        gate = _dot(u, wg_ref[:, sl])
        up = _dot(u, wu_ref[:, sl])
        act_ref[:, sl] = (gate * jax.nn.sigmoid(gate) * up).astype(BF16)
    y = x + 0.5 * _dot(act_ref[...], wd_ref[...])
    if final:
        y = _rms(y, fg_ref[...])
    o_ref[...] = y


def _ffn(h, g, wg, wu, wd, fg, *, final):
    n = h.shape[0]
    row = pl.BlockSpec((FFN_ROWS, D_MODEL), lambda i: (i, 0))
    return pl.pallas_call(
        functools.partial(_ffn_body, final=final),
        grid=(n // FFN_ROWS,),
        in_specs=[row, _const_spec((1, D_MODEL)), _const_spec((D_MODEL, D_FF)),
                  _const_spec((D_MODEL, D_FF)), _const_spec((D_FF, D_MODEL)),
                  _const_spec((1, D_MODEL))],
        out_specs=row,
        out_shape=jax.ShapeDtypeStruct((n, D_MODEL), F32),
        scratch_shapes=[pltpu.VMEM((FFN_ROWS, D_FF), BF16)],
        compiler_params=_params("parallel"),
        name="ffn_final" if final else "ffn",
    )(h, g, wg, wu, wd, fg)


PROJ_ROWS = 512
PROJ_OUTS = (("a", 4 * GROUP, F32), ("g", 2 * GROUP, F32), ("p", GROUP, F32),
             ("c", 3 * GROUP, BF16), ("d", 2 * GROUP, BF16))
PROJ_WIDTH = sum(w for _, w, _ in PROJ_OUTS)


def _in_proj_body(x_ref, g_ref, w_ref, wvt_ref, *o_refs):
    u = _rms(x_ref[...], g_ref[...]).astype(BF16)
    off = 0
    for o_ref, (_, width, dtype) in zip(o_refs, PROJ_OUTS):
        for c in range(width // GROUP):
            z = _dot(u, w_ref[:, off + c * GROUP: off + (c + 1) * GROUP])
            o_ref[:, c * GROUP:(c + 1) * GROUP] = z.astype(dtype)
        off += width
    vt_ref = o_refs[-1]
    for t in range(PROJ_ROWS // TILE):
        vt_ref[t] = _dot_nt(wvt_ref[...], u[t * TILE:(t + 1) * TILE]).astype(BF16)


def _in_proj(h, g, w, wvt):
    n = h.shape[0]
    tiles = PROJ_ROWS // TILE
    return pl.pallas_call(
        _in_proj_body,
        grid=(n // PROJ_ROWS,),
        in_specs=[pl.BlockSpec((PROJ_ROWS, D_MODEL), lambda i: (i, 0)),
                  _const_spec((1, D_MODEL)), _const_spec((D_MODEL, PROJ_WIDTH)),
                  _const_spec((GROUP, D_MODEL))],
        out_specs=[pl.BlockSpec((PROJ_ROWS, w_), lambda i: (i, 0)) for _, w_, _ in PROJ_OUTS]
        + [pl.BlockSpec((tiles, GROUP, TILE), lambda i: (i, 0, 0))],
        out_shape=[jax.ShapeDtypeStruct((n, w_), dt) for _, w_, dt in PROJ_OUTS]
        + [jax.ShapeDtypeStruct((n // TILE, GROUP, TILE), BF16)],
        compiler_params=_params("parallel"),
        name="in_proj",
    )(h, g, w, wvt)


ML_ROWS = 512
ML_HALO = 8


def _chunk_scan(x, rin, op, fill):
    s = 1
    while s < MLSTM_CHUNK:
        x = op(x, jnp.where(rin >= s, pltpu.roll(x, s, 0), fill))
        s *= 2
    return x


def _mlstm_body(za_ref, zg_ref, cw_ref, cb_ref, gb_ref, ng_ref, y_ref,
                buf_ref, tail_ref, hh_ref, c_ref, n_ref, m_ref):
    L = MLSTM_CHUNK

    @pl.when(pl.program_id(1) == 0)
    def _():
        tail_ref[...] = jnp.zeros_like(tail_ref)
        c_ref[...] = jnp.zeros_like(c_ref)
        n_ref[...] = jnp.zeros_like(n_ref)
        m_ref[...] = jnp.zeros_like(m_ref)

    buf_ref[0:ML_HALO, :] = tail_ref[...]
    buf_ref[ML_HALO:, :] = za_ref[:, 0:2 * GROUP]
    tail_ref[...] = za_ref[ML_ROWS - ML_HALO:, 0:2 * GROUP]
    conv = cb_ref[...]
    for j in range(CONV_WIDTH):
        conv = conv + buf_ref[pl.ds(ML_HALO - (CONV_WIDTH - 1) + j, ML_ROWS), :] * cw_ref[j:j + 1, :]
    qk = conv * jax.nn.sigmoid(conv)
    q = qk[:, :GROUP]
    k = qk[:, GROUP:] * (HEAD_DIM ** -0.5)
    v = za_ref[:, 2 * GROUP:3 * GROUP].astype(BF16)

    ii = zg_ref[:, :GROUP] + gb_ref[:, :GROUP]
    fx = zg_ref[:, GROUP:] + gb_ref[:, GROUP:]
    lf = jnp.minimum(fx, 0.0) - jnp.log1p(jnp.exp(-jnp.abs(fx)))
    rin = lax.broadcasted_iota(jnp.int32, (ML_ROWS, GROUP), 0) % L
    b = _chunk_scan(lf, rin, jnp.add, 0.0)
    a = ii - b
    ca = _chunk_scan(a, rin, jnp.maximum, NEG)

    bd = _group_mask(GROUP, GROUP, HEAD_DIM, HEAD_DIM)
    ones_bd = bd.astype(F32)
    row = lax.broadcasted_iota(jnp.int32, (L, GROUP), 0)
    key = lax.broadcasted_iota(jnp.int32, (L, GROUP), 1) % L
    causal = key <= row
    diag = key == row

    m_prev = m_ref[...]
    for c in range(ML_ROWS // L):
        rs = slice(c * L, (c + 1) * L)
        q_c, k_c, v_c = q[rs], k[rs], v[rs]
        q_b = q_c.astype(BF16)
        a_c, b_c = a[rs], b[rs]
        g = jnp.maximum(m_prev, ca[rs])
        g_last = g[L - 1:L]
        a_row = jnp.sum(jnp.where(diag, a_c, 0.0), axis=0, keepdims=True)
        decay = jnp.exp(jnp.where(causal, a_row - g, NEG))
        sc = _dot_nt(q_b, _tile_rows(k_c.astype(BF16), HEADS, bd)) * decay
        inter = jnp.exp(m_prev - g)
        num = inter * _dot(q_b, c_ref[...].astype(BF16)) + _dot(sc.astype(BF16), _tile_rows(v_c, HEADS, bd))
        den = inter * _group_sum(q_c * n_ref[...], ones_bd) + _group_sum(sc, ones_bd)
        hh_ref[rs, :] = num / jnp.maximum(jnp.abs(den), jnp.exp(-(b_c + g)))
        kw = k_c * jnp.exp(a_c - g_last)
        carry = jnp.exp(m_prev - g_last)
        c_ref[...] = carry * c_ref[...] + jnp.where(bd, _dot_tn(kw.astype(BF16), v_c), 0.0)
        n_ref[...] = carry * n_ref[...] + jnp.sum(kw, axis=0, keepdims=True)
        m_prev = b_c[L - 1:L] + g_last
    m_ref[...] = m_prev

    hh = hh_ref[...]
    mu = _group_sum(hh, ones_bd) * (1.0 / HEAD_DIM)
    dev = hh - mu
    var = _group_sum(dev * dev, ones_bd) * (1.0 / HEAD_DIM)
    o_gate = jax.nn.sigmoid(za_ref[:, 3 * GROUP:])
    y_ref[...] = (dev * lax.rsqrt(var + RMS_EPS) * ng_ref[...] * o_gate).astype(BF16)


def _mlstm(za, zg, conv_w, conv_b, gate_b, norm_g, *, batch, seq):
    za = za.reshape(batch, seq, 4 * GROUP)
    zg = zg.reshape(batch, seq, 2 * GROUP)
    y = pl.pallas_call(
        _mlstm_body,
        grid=(batch, seq // ML_ROWS),
        in_specs=[pl.BlockSpec((None, ML_ROWS, 4 * GROUP), lambda b, j: (b, j, 0)),
                  pl.BlockSpec((None, ML_ROWS, 2 * GROUP), lambda b, j: (b, j, 0)),
                  _const_spec((CONV_WIDTH, 2 * GROUP)), _const_spec((1, 2 * GROUP)),
                  _const_spec((1, 2 * GROUP)), _const_spec((1, GROUP))],
        out_specs=pl.BlockSpec((None, ML_ROWS, GROUP), lambda b, j: (b, j, 0)),
        out_shape=jax.ShapeDtypeStruct((batch, seq, GROUP), BF16),
        scratch_shapes=[pltpu.VMEM((ML_ROWS + ML_HALO, 2 * GROUP), F32),
                        pltpu.VMEM((ML_HALO, 2 * GROUP), F32),
                        pltpu.VMEM((ML_ROWS, GROUP), F32),
                        pltpu.VMEM((GROUP, GROUP), F32),
                        pltpu.VMEM((1, GROUP), F32),
                        pltpu.VMEM((1, GROUP), F32)],
        compiler_params=_params("parallel", "arbitrary"),
        name="mlstm",
    )(za, zg, conv_w, conv_b, gate_b, norm_g)
    return y.reshape(batch * seq, GROUP)


POOL_ROWS = 512
POOL_HALO = 16


def _pool_body(u_ref, w_ref, s_ref, y_ref, buf_ref, tail_ref):
    j = pl.program_id(1)

    @pl.when(j == 0)
    def _():
        tail_ref[...] = jnp.zeros_like(tail_ref)

    buf_ref[0:POOL_HALO, :] = tail_ref[...]
    buf_ref[POOL_HALO:, :] = u_ref[...]
    tail_ref[...] = u_ref[POOL_ROWS - POOL_HALO:, :]
    sums, s = [], buf_ref[...]
    for shift in (1, 2, 4, 8):
        s = s + pltpu.roll(s, shift, 0)
        sums.append(s[POOL_HALO:])
    u = u_ref[...]
    lane_group = lax.broadcasted_iota(jnp.int32, (POOL_ROWS, GROUP), 1) // HEAD_DIM
    t = j * POOL_ROWS + lax.broadcasted_iota(jnp.int32, (POOL_ROWS, GROUP), 0)
    total, win = sums[3], jnp.full((POOL_ROWS, GROUP), POOL_WINDOWS[3], jnp.int32)
    for gi in (2, 1, 0):
        total = jnp.where(lane_group == gi, sums[gi], total)
        win = jnp.where(lane_group == gi, POOL_WINDOWS[gi], win)
    mean = total / jnp.minimum(t + 1, win).astype(F32)
    y = _dot((mean - u).astype(BF16), w_ref[...]) * s_ref[...]
    y_ref[...] = y.astype(BF16)


def _pool(zp, w_bd, scale, *, batch, seq):
    zp = zp.reshape(batch, seq, GROUP)
    y = pl.pallas_call(
        _pool_body,
        grid=(batch, seq // POOL_ROWS),
        in_specs=[pl.BlockSpec((None, POOL_ROWS, GROUP), lambda b, j: (b, j, 0)),
                  _const_spec((GROUP, GROUP)), _const_spec((1, GROUP))],
        out_specs=pl.BlockSpec((None, POOL_ROWS, GROUP), lambda b, j: (b, j, 0)),
        out_shape=jax.ShapeDtypeStruct((batch, seq, GROUP), BF16),
        scratch_shapes=[pltpu.VMEM((POOL_ROWS + POOL_HALO, GROUP), F32),
                        pltpu.VMEM((POOL_HALO, GROUP), F32)],
        compiler_params=_params("parallel", "arbitrary"),
        name="pool",
    )(zp, w_bd, scale)
    return y.reshape(batch * seq, GROUP)


def _toeplitz_body(w_ref, o_ref):
    x = jnp.broadcast_to(w_ref[...], (TILE, 2 * TILE))
    o_ref[...] = pltpu.roll(x, 0, 1, stride=1, stride_axis=0)[:, :TILE]


def _toeplitz(rows):
    n = rows.shape[0]
    return pl.pallas_call(
        _toeplitz_body,
        grid=(n,),
        in_specs=[pl.BlockSpec((None, 1, 2 * TILE), lambda i: (i, 0, 0))],
        out_specs=pl.BlockSpec((None, TILE, TILE), lambda i: (i, 0, 0)),
        out_shape=jax.ShapeDtypeStruct((n, TILE, TILE), F32),
        compiler_params=_params("parallel"),
        name="toeplitz",
    )(rows.reshape(n, 1, 2 * TILE))


_TOEPLITZ_X = np.where(np.arange(2 * TILE) <= TILE, -np.arange(2 * TILE), 2 * TILE - np.arange(2 * TILE))


def _t5_bucket(dist):
    max_exact = T5_BUCKETS // 2
    d = jnp.maximum(dist, 1).astype(F32)
    large = max_exact + (jnp.log(d / max_exact) / math.log(T5_MAX_DIST / max_exact)
                         * (T5_BUCKETS - max_exact)).astype(jnp.int32)
    large = jnp.minimum(large, T5_BUCKETS - 1)
    return jnp.where(dist < max_exact, dist, large)


def _bias_tiles(t5_bias, seq):
    x = jnp.asarray(_TOEPLITZ_X, jnp.int32)
    rows = []
    for (w, d) in DIL_PATTERNS:
        tab = t5_bias[_t5_bucket(jnp.arange(DIL_BACK + 1) * d), :HEADS].T
        prev = jnp.where(x <= 0, tab[:, jnp.clip(x + DIL_BACK, 0, DIL_BACK)], NEG)
        cur = jnp.where(x >= 0, tab[:, jnp.clip(x, 0, DIL_BACK)], NEG)
        rows.append(jnp.stack([prev, cur], axis=1))
    dil = _toeplitz(jnp.stack(rows).reshape(-1, 2 * TILE))
    dil = dil.reshape(len(DIL_PATTERNS), HEADS, 2, TILE, TILE).transpose(0, 3, 1, 2, 4)
    dil = dil.reshape(len(DIL_PATTERNS), TILE, HEADS * 2 * TILE)
    tab = t5_bias[_t5_bucket(jnp.arange(seq)), HEADS:].T * LOG2E
    noff = seq // TILE - DIFF_MIN_OFFSET
    dist = (jnp.arange(noff)[:, None] + DIFF_MIN_OFFSET) * TILE - x[None, :]
    rows = jnp.where(dist >= 0, tab[:, jnp.clip(dist, 0, seq - 1)], NEG)
    diff = _toeplitz(rows.transpose(1, 0, 2).reshape(-1, 2 * TILE))
    return dil, diff.reshape(noff, HEADS * TILE, TILE)


def _dil_body(*refs, subs, merge):
    if merge:
        (q_ref, k_ref, v_ref, kp_ref, vp_ref, bias_ref,
         o1_ref, l1_ref, o2_ref, l2_ref, y_ref) = refs
    else:
        q_ref, k_ref, v_ref, kp_ref, vp_ref, bias_ref, o_ref, lse_ref = refs
    first = pl.program_id(2) == 0
    kmask = _group_mask(HEADS * 2 * TILE, GROUP, 2 * TILE, HEAD_DIM)
    head_of_lane = lax.broadcasted_iota(jnp.int32, (TILE, GROUP), 1) // HEAD_DIM
    ones_exp = kmask.astype(BF16)
    lane = lax.broadcasted_iota(jnp.int32, (TILE, HEADS * 2 * TILE), 1)
    before_start = jnp.where((lane % (2 * TILE)) < TILE, NEG, 0.0)
    for sb in range(subs):
        rs = slice(sb * TILE, (sb + 1) * TILE)
        if sb == 0:
            k2 = jnp.concatenate([kp_ref[...], k_ref[rs, :]], axis=0)
            v2 = jnp.concatenate([vp_ref[...], v_ref[rs, :]], axis=0)
        else:
            k2 = k_ref[(sb - 1) * TILE:(sb + 1) * TILE, :]
            v2 = v_ref[(sb - 1) * TILE:(sb + 1) * TILE, :]
        s = _dot_nt(q_ref[rs, :], _tile_rows(k2, HEADS, kmask)) + bias_ref[...]
        if sb == 0:
            s = s + jnp.where(first, before_start, 0.0)
        p, m_x = [], jnp.zeros((TILE, GROUP), F32)
        for h in range(HEADS):
            s_h = s[:, h * 2 * TILE:(h + 1) * 2 * TILE]
            m_h = jnp.max(s_h, axis=-1, keepdims=True)
            p.append(jnp.exp(s_h - m_h))
            m_x = jnp.where(head_of_lane == h, m_h, m_x)
        p = jnp.concatenate(p, axis=-1).astype(BF16)
        acc = _dot(p, _tile_rows(v2, HEADS, kmask))
        l_x = _dot(p, ones_exp)
        o = acc / l_x
        lse = m_x + jnp.log(l_x)
        if merge:
            o1, l1, o2, l2 = o1_ref[rs, :], l1_ref[rs, :], o2_ref[rs, :], l2_ref[rs, :]
            top = jnp.maximum(jnp.maximum(l1, l2), lse)
            w1, w2, w3 = jnp.exp(l1 - top), jnp.exp(l2 - top), jnp.exp(lse - top)
            y_ref[rs, :] = ((w1 * o1 + w2 * o2 + w3 * o) / (w1 + w2 + w3)).astype(BF16)
        else:
            o_ref[rs, :] = o
            lse_ref[rs, :] = lse


def _dilated_pattern(zc, bias, dil, *, batch, seq, merge_with=None):
    length = seq // dil
    rows = min(length, 512)
    subs = rows // TILE
    zc = zc.reshape(batch, length, dil * 3 * GROUP)
    blk = lambda which: pl.BlockSpec((None, rows, GROUP), lambda b, r, n: (b, n, 3 * r + which))
    prev = lambda which: pl.BlockSpec(
        (None, TILE, GROUP), lambda b, r, n: (b, jnp.maximum(n * subs - 1, 0), 3 * r + which))
    nat = pl.BlockSpec((None, rows, GROUP), lambda b, r, n: (b, n, r))
    in_specs = [blk(0), blk(1), blk(2), prev(1), prev(2), _const_spec((TILE, HEADS * 2 * TILE))]
    args = [zc, zc, zc, zc, zc, bias]
    if merge_with is None:
        out_specs = [nat, nat]
        out_shape = [jax.ShapeDtypeStruct((batch, length, dil * GROUP), F32)] * 2
    else:
        in_specs += [nat] * len(merge_with)
        args += [t.reshape(batch, length, dil * GROUP) for t in merge_with]
        out_specs = nat
        out_shape = jax.ShapeDtypeStruct((batch, length, dil * GROUP), BF16)
    out = pl.pallas_call(
        functools.partial(_dil_body, subs=subs, merge=merge_with is not None),
        grid=(batch, dil, length // rows),
        in_specs=in_specs, out_specs=out_specs, out_shape=out_shape,
        compiler_params=_params("parallel", "parallel", "parallel"),
        name=f"dilated_d{dil}",
    )(*args)
    if merge_with is None:
        return [t.reshape(batch * seq, GROUP) for t in out]
    return out.reshape(batch * seq, GROUP)


def _dilated(zc, dil_bias, *, batch, seq):
    o1, l1 = _dilated_pattern(zc, dil_bias[0], DIL_PATTERNS[0][1], batch=batch, seq=seq)
    o2, l2 = _dilated_pattern(zc, dil_bias[1], DIL_PATTERNS[1][1], batch=batch, seq=seq)
    return _dilated_pattern(zc, dil_bias[2], DIL_PATTERNS[2][1], batch=batch, seq=seq,
                            merge_with=(o1, l1, o2, l2))


DIFF_Q = 256
DIFF_K = 256
DIFF_GROUPS = 2 * HEADS
DIFF_AUG = HEAD_DIM + 16


def _diff_body(q_ref, k_ref, vt_ref, bias_ref, lam_ref, sg_ref, y_ref,
               kexp_ref, vaug_ref, acc_ref, sta_ref, stb_ref, *, lam_init, key_steps):
    qi = pl.program_id(1)

    @pl.when(qi == 0)
    def _():
        grp = lax.broadcasted_iota(jnp.int32, (DIFF_GROUPS * DIFF_K, GROUP), 0) // DIFF_K
        slot = lax.broadcasted_iota(jnp.int32, (DIFF_GROUPS * DIFF_K, GROUP), 1) // DIFF_QK_HALF
        kmask = slot == 2 * (grp % HEADS) + grp // HEADS
        ones_rows = (lax.broadcasted_iota(jnp.int32, (DIFF_AUG - HEAD_DIM, DIFF_K), 0) == 0).astype(BF16)

        def build(j, carry):
            k_t = k_ref[pl.ds(pl.multiple_of(j * DIFF_K, DIFF_K), DIFF_K), :]
            kexp_ref[j] = jnp.where(kmask, jnp.concatenate([k_t] * DIFF_GROUPS, axis=0),
                                    jnp.zeros((), BF16))
            vt = jnp.concatenate([vt_ref[2 * j], vt_ref[2 * j + 1]], axis=1)
            for h in range(HEADS):
                vaug_ref[j, h] = jnp.concatenate([vt[h * HEAD_DIM:(h + 1) * HEAD_DIM], ones_rows], axis=0)
            return carry

        lax.fori_loop(0, key_steps, build, 0)

    acc_ref[...] = jnp.zeros_like(acc_ref)
    q = q_ref[...]
    c = (DIFF_QK_HALF ** -0.5) * LOG2E
    last = key_steps - 1

    def scores(j):
        return _dot_nt(kexp_ref[jnp.minimum(j, last)], q)

    def consume(st_ref, j, carry):
        ms, ls = carry
        off = 2 * (qi - j) + 2
        jv = jnp.minimum(j, last)
        b_m, b_0, b_p = bias_ref[off], bias_ref[off + 1], bias_ref[off + 2]
        new_ms, new_ls = [], []
        for g in range(DIFF_GROUPS):
            mp, h = divmod(g, HEADS)
            hs = slice(h * TILE, (h + 1) * TILE)
            bias = jnp.concatenate([jnp.concatenate([b_0[hs], b_p[hs]], axis=1),
                                    jnp.concatenate([b_m[hs], b_0[hs]], axis=1)], axis=0)
            s = st_ref[g * DIFF_K:(g + 1) * DIFF_K, :] * c + bias
            m_new = jnp.maximum(ms[g], jnp.max(s, axis=0, keepdims=True))
            p = jnp.exp2(s - m_new).astype(BF16)
            alpha = jnp.exp2(ms[g] - m_new)
            r = _dot(vaug_ref[jv, h], p)
            acc_ref[mp, h] = alpha * acc_ref[mp, h] + r[:HEAD_DIM]
            new_ls.append(alpha * ls[g] + r[HEAD_DIM:HEAD_DIM + 1])
            new_ms.append(m_new)
        return tuple(new_ms), tuple(new_ls)

    sta_ref[...] = scores(0)

    def pair(jj, carry):
        j = 2 * jj
        stb_ref[...] = scores(j + 1)
        carry = consume(sta_ref, j, carry)
        sta_ref[...] = scores(j + 2)
        return consume(stb_ref, j + 1, carry)

    init = (tuple(jnp.full((1, DIFF_Q), NEG, F32) for _ in range(DIFF_GROUPS)),
            tuple(jnp.zeros((1, DIFF_Q), F32) for _ in range(DIFF_GROUPS)))
    _, ls = lax.fori_loop(0, (qi + 2) // 2, pair, init)

    lv = lam_ref[...]
    lam = (jnp.exp(jnp.sum(lv[0:1] * lv[1:2], axis=-1, keepdims=True))
           - jnp.exp(jnp.sum(lv[2:3] * lv[3:4], axis=-1, keepdims=True)) + lam_init)
    outs = []
    for h in range(HEADS):
        o = acc_ref[0, h] / ls[h] - lam * (acc_ref[1, h] / ls[HEADS + h])
        ms_o = jnp.mean(o * o, axis=0, keepdims=True)
        outs.append(o * lax.rsqrt(ms_o + SUBLN_EPS) * sg_ref[...] * (1.0 - lam_init))
    y_ref[...] = jnp.concatenate(outs, axis=0).T.astype(BF16)


def _diff_attention(zd, vt, bias, lam_vecs, subln_cols, *, lam_init, batch, seq):
    zd = zd.reshape(batch, seq, 2 * GROUP)
    key_tiles = seq // TILE
    key_steps = seq // DIFF_K
    vt = vt.reshape(batch, key_tiles, GROUP, TILE)
    y = pl.pallas_call(
        functools.partial(_diff_body, lam_init=lam_init, key_steps=key_steps),
        grid=(batch, seq // DIFF_Q),
        in_specs=[pl.BlockSpec((None, DIFF_Q, GROUP), lambda b, i: (b, i, 0)),
                  pl.BlockSpec((None, seq, GROUP), lambda b, i: (b, 0, 1), pipeline_mode=pl.Buffered(1)),
                  pl.BlockSpec((None, key_tiles, GROUP, TILE), lambda b, i: (b, 0, 0, 0),
                               pipeline_mode=pl.Buffered(1)),
                  _const_spec((key_tiles - DIFF_MIN_OFFSET, HEADS * TILE, TILE)),
                  _const_spec((4, DIFF_QK_HALF)), _const_spec((HEAD_DIM, DIFF_Q))],
        out_specs=pl.BlockSpec((None, DIFF_Q, GROUP), lambda b, i: (b, i, 0)),
        out_shape=jax.ShapeDtypeStruct((batch, seq, GROUP), BF16),
        scratch_shapes=[pltpu.VMEM((key_steps, DIFF_GROUPS * DIFF_K, GROUP), BF16),
                        pltpu.VMEM((key_steps, HEADS, DIFF_AUG, DIFF_K), BF16),
                        pltpu.VMEM((2, HEADS, HEAD_DIM, DIFF_Q), F32),
                        pltpu.VMEM((DIFF_GROUPS * DIFF_K, DIFF_Q), F32),
                        pltpu.VMEM((DIFF_GROUPS * DIFF_K, DIFF_Q), F32)],
        compiler_params=_params("parallel", "arbitrary"),
        name="diff_attn",
    )(zd, zd, vt, bias, lam_vecs, subln_cols)
    return y.reshape(batch * seq, GROUP)


OUT_ROWS = 512


def _out_proj_body(h_ref, ya_ref, yb_ref, yc_ref, yd_ref, w_ref, o_ref):
    y = h_ref[...]
    for gi, y_ref in enumerate((ya_ref, yb_ref, yc_ref, yd_ref)):
        y = y + _dot(y_ref[...], w_ref[gi * GROUP:(gi + 1) * GROUP, :])
    o_ref[...] = y


def _out_proj(h, ya, yb, yc, yd, w):
    n = h.shape[0]
    row = pl.BlockSpec((OUT_ROWS, D_MODEL), lambda i: (i, 0))
    grp = pl.BlockSpec((OUT_ROWS, GROUP), lambda i: (i, 0))
    return pl.pallas_call(
        _out_proj_body,
        grid=(n // OUT_ROWS,),
        in_specs=[row, grp, grp, grp, grp, _const_spec((D_MODEL, D_MODEL))],
        out_specs=row,
        out_shape=jax.ShapeDtypeStruct((n, D_MODEL), F32),
        compiler_params=_params("parallel"),
        name="out_proj",
    )(h, ya, yb, yc, yd, w)


KV_ROWS = 512


def _mem_kv_body(m_ref, g_ref, w_ref, k_ref, v_ref):
    u = _rms(m_ref[...], g_ref[...]).astype(BF16)
    for c in range(D_MODEL // GROUP):
        sl = slice(c * GROUP, (c + 1) * GROUP)
        k_ref[:, sl] = _dot(u, w_ref[:, sl]).astype(BF16)
        v_ref[:, sl] = _dot(u, w_ref[:, D_MODEL + c * GROUP: D_MODEL + (c + 1) * GROUP]).astype(BF16)


def _mem_kv(mem, g, w):
    n = mem.shape[0]
    row = pl.BlockSpec((KV_ROWS, D_MODEL), lambda i: (i, 0))
    return pl.pallas_call(
        _mem_kv_body,
        grid=(n // KV_ROWS,),
        in_specs=[row, _const_spec((1, D_MODEL)), _const_spec((D_MODEL, 2 * D_MODEL))],
        out_specs=[row, row],
        out_shape=[jax.ShapeDtypeStruct((n, D_MODEL), BF16)] * 2,
        compiler_params=_params("parallel"),
        name="mem_kv",
    )(mem, g, w)


XATTN_ROWS = 512


def _xattn_body(x_ref, g_ref, wq_ref, k_ref, v_ref, wo_ref, o_ref, q_scr, a_scr):
    x = x_ref[...]
    u = _rms(x, g_ref[...]).astype(BF16)
    for c in range(D_MODEL // GROUP):
        sl = slice(c * GROUP, (c + 1) * GROUP)
        q_scr[:, sl] = _dot(u, wq_ref[:, sl]).astype(BF16)
    for h in range(MEM_HEADS):
        sl = slice(h * MEM_HEAD_DIM, (h + 1) * MEM_HEAD_DIM)
        s = _dot_nt(q_scr[:, sl], k_ref[:, sl]) * (MEM_HEAD_DIM ** -0.5)
        e = jnp.exp(s - jnp.max(s, axis=-1, keepdims=True))
        l = jnp.sum(e, axis=-1, keepdims=True)
        a_scr[:, sl] = (_dot(e.astype(BF16), v_ref[:, sl]) / l).astype(BF16)
    o_ref[...] = x + _dot(a_scr[...], wo_ref[...])


def _xattn(h, g, wq, k, v, wo, *, batch, seq):
    h3 = h.reshape(batch, seq, D_MODEL)
    k3 = k.reshape(batch, MEM_LEN, D_MODEL)
    v3 = v.reshape(batch, MEM_LEN, D_MODEL)
    row = pl.BlockSpec((None, XATTN_ROWS, D_MODEL), lambda b, i: (b, i, 0))
    mem = pl.BlockSpec((None, MEM_LEN, D_MODEL), lambda b, i: (b, 0, 0))
    out = pl.pallas_call(
        _xattn_body,
        grid=(batch, seq // XATTN_ROWS),
        in_specs=[row, _const_spec((1, D_MODEL)), _const_spec((D_MODEL, D_MODEL)), mem, mem,
                  _const_spec((D_MODEL, D_MODEL))],
        out_specs=row,
        out_shape=jax.ShapeDtypeStruct((batch, seq, D_MODEL), F32),
        scratch_shapes=[pltpu.VMEM((XATTN_ROWS, D_MODEL), BF16), pltpu.VMEM((XATTN_ROWS, D_MODEL), BF16)],
        compiler_params=_params("parallel", "parallel"),
        name="xattn",
    )(h3, g, wq, k3, v3, wo)
    return out.reshape(batch * seq, D_MODEL)


def _per_head_lanes(x):
    return jnp.repeat(x, HEAD_DIM, axis=-1)


def _in_proj_weight(w_in):
    g = GROUP
    q_a, k_a, v_a, o_a = (w_in[:, i * g:(i + 1) * g] for i in range(4))
    ig = w_in[:, 4 * g:4 * g + HEADS]
    fg = w_in[:, 4 * g + HEADS:4 * g + 2 * HEADS]
    rest = w_in[:, 4 * g + 2 * HEADS:]
    pool, q_c, k_c, v_c, q_d, k_d, v_d = (rest[:, i * g:(i + 1) * g] for i in range(7))
    q_c = q_c * (HEAD_DIM ** -0.5)
    cols = [q_a, k_a, v_a, o_a, _per_head_lanes(ig), _per_head_lanes(fg), pool,
            q_c, k_c, v_c, q_d, k_d]
    return jnp.concatenate(cols, axis=1).astype(BF16), v_d.T.astype(BF16)


def _block_diag(w):
    g, c, _ = w.shape
    eye = jnp.eye(g, dtype=w.dtype)
    return (eye[:, None, :, None] * w[:, :, None, :]).reshape(g * c, g * c)


def kernel(x, mem, t5_bias, ffn1_norm, ffn1_w_gate, ffn1_w_up, ffn1_w_down, mix_norm, w_in,
           mlstm_conv_w, mlstm_conv_b, mlstm_gate_b, mlstm_norm, pool_w, pool_scale,
           diff_lambda, diff_subln, w_out, xattn_norm, mem_norm, xattn_wq, xattn_wkv, xattn_wo,
           ffn2_norm, ffn2_w_gate, ffn2_w_up, ffn2_w_down, final_norm):
    batch, seq, _ = x.shape
    n = batch * seq
    dil_bias, diff_bias = _bias_tiles(t5_bias, seq)
    h = x.reshape(n, D_MODEL)
    mem2 = mem.reshape(batch * MEM_LEN, D_MODEL)
    row = lambda v: v.reshape(1, -1)
    for l in range(DEPTH):
        lam_init = 0.8 - 0.6 * math.exp(-0.3 * l)
        h = _ffn(h, row(ffn1_norm[l]), ffn1_w_gate[l].astype(BF16), ffn1_w_up[l].astype(BF16),
                 ffn1_w_down[l].astype(BF16), row(final_norm), final=False)
        za, zg, zp, zc, zd, vt = _in_proj(h, row(mix_norm[l]), *_in_proj_weight(w_in[l]))
        ya = _mlstm(za, zg, mlstm_conv_w[l], row(mlstm_conv_b[l]),
                    row(_per_head_lanes(mlstm_gate_b[l].reshape(2, HEADS))), row(mlstm_norm[l]),
                    batch=batch, seq=seq)
        yb = _pool(zp, _block_diag(pool_w[l]).astype(BF16), row(pool_scale[l]), batch=batch, seq=seq)
        yc = _dilated(zc, dil_bias, batch=batch, seq=seq)
        yd = _diff_attention(zd, vt, diff_bias, diff_lambda[l],
                             jnp.broadcast_to(diff_subln[l][:, None], (HEAD_DIM, DIFF_Q)),
                             lam_init=lam_init, batch=batch, seq=seq)
        h = _out_proj(h, ya, yb, yc, yd, w_out[l].astype(BF16))
        k_mem, v_mem = _mem_kv(mem2, row(mem_norm[l]), xattn_wkv[l].astype(BF16))
        h = _xattn(h, row(xattn_norm[l]), xattn_wq[l].astype(BF16), k_mem, v_mem,
                   xattn_wo[l].astype(BF16), batch=batch, seq=seq)
        h = _ffn(h, row(ffn2_norm[l]), ffn2_w_gate[l].astype(BF16), ffn2_w_up[l].astype(BF16),
                 ffn2_w_down[l].astype(BF16), row(final_norm), final=(l == DEPTH - 1))
    return h.reshape(batch, seq, D_MODEL)
```

```python
import functools
import math

import jax
import jax.numpy as jnp
import numpy as np
from jax import lax
from jax.experimental import pallas as pl
from jax.experimental.pallas import tpu as pltpu

F32 = jnp.float32
BF16 = jnp.bfloat16

D_MODEL = 1024
D_FF = 2816
DEPTH = 4
GROUP = 256
HEADS = 4
HEAD_DIM = GROUP // HEADS
MEM_LEN = 256
MEM_HEADS = 4
MEM_HEAD_DIM = D_MODEL // MEM_HEADS
MLSTM_CHUNK = 64
CONV_WIDTH = 4
POOL_WINDOWS = (2, 4, 8, 16)
DIL_PATTERNS = ((128, 1), (512, 4), (2048, 16))
DIL_BACK = 128
DIFF_QK_HALF = HEAD_DIM // 2
T5_BUCKETS = 32
T5_MAX_DIST = 2048
RMS_EPS = 1e-6
SUBLN_EPS = 1e-5
NEG = -1e30
LOG2E = math.log2(math.e)
DIFF_SCORE_SCALE = (DIFF_QK_HALF ** -0.5) * LOG2E
DIFF_MIN_OFFSET = -3
TILE = 128

VMEM_LIMIT_BYTES = 56 * 1024 * 1024


def _rms(xf, g, eps=RMS_EPS):
    return xf * lax.rsqrt(jnp.mean(xf * xf, axis=-1, keepdims=True) + eps) * g


def _const_spec(shape):
    zeros = (0,) * len(shape)
    return pl.BlockSpec(shape, lambda *_: zeros, pipeline_mode=pl.Buffered(1))


def _params(*sem):
    return pltpu.CompilerParams(dimension_semantics=sem, vmem_limit_bytes=VMEM_LIMIT_BYTES)


def _group_mask(rows, cols, row_group, col_group):
    r = lax.broadcasted_iota(jnp.int32, (rows, cols), 0) // row_group
    c = lax.broadcasted_iota(jnp.int32, (rows, cols), 1) // col_group
    return r == c


def _tile_rows(x, reps, mask):
    return jnp.where(mask, jnp.concatenate([x] * reps, axis=0), jnp.zeros((), x.dtype))


def _dot(a, b):
    return jnp.dot(a, b, preferred_element_type=F32)


def _dot_nt(a, b):
    return lax.dot_general(a, b, (((1,), (1,)), ((), ())), preferred_element_type=F32)


def _dot_tn(a, b):
    return lax.dot_general(a, b, (((0,), (0,)), ((), ())), preferred_element_type=F32)


def _group_sum(x, ones_bd):
    hi = x.astype(BF16)
    lo = (x - hi.astype(F32)).astype(BF16)
    return _dot(hi, ones_bd) + _dot(lo, ones_bd)


FFN_ROWS = 512
FFN_COLS = 256


def _ffn_body(x_ref, g_ref, wg_ref, wu_ref, wd_ref, fg_ref, o_ref, act_ref, *, final):
    x = x_ref[...]
    u = _rms(x, g_ref[...]).astype(BF16)
    for c in range(D_FF // FFN_COLS):
        sl = slice(c * FFN_COLS, (c + 1) * FFN_COLS)
        gate = _dot(u, wg_ref[:, sl])
        up = _dot(u, wu_ref[:, sl])
        act_ref[:, sl] = (gate * jax.nn.sigmoid(gate) * up).astype(BF16)
    y = x + 0.5 * _dot(act_ref[...], wd_ref[...])
    if final:
        y = _rms(y, fg_ref[...])
    o_ref[...] = y


def _ffn(h, g, wg, wu, wd, fg, *, final):
    n = h.shape[0]
    row = pl.BlockSpec((FFN_ROWS, D_MODEL), lambda i: (i, 0))
    return pl.pallas_call(
        functools.partial(_ffn_body, final=final),
        grid=(n // FFN_ROWS,),
        in_specs=[row, _const_spec((1, D_MODEL)), _const_spec((D_MODEL, D_FF)),
                  _const_spec((D_MODEL, D_FF)), _const_spec((D_FF, D_MODEL)),
                  _const_spec((1, D_MODEL))],
        out_specs=row,
        out_shape=jax.ShapeDtypeStruct((n, D_MODEL), F32),
        scratch_shapes=[pltpu.VMEM((FFN_ROWS, D_FF), BF16)],
        compiler_params=_params("parallel"),
        name="ffn_final" if final else "ffn",
    )(h, g, wg, wu, wd, fg)


PROJ_ROWS = 512
PROJ_OUTS = (("a", 4 * GROUP, F32), ("g", 2 * GROUP, F32), ("p", GROUP, F32),
             ("c", 3 * GROUP, BF16), ("d", 2 * GROUP, BF16))
PROJ_WIDTH = sum(w for _, w, _ in PROJ_OUTS)


def _in_proj_body(x_ref, g_ref, w_ref, wvt_ref, *o_refs):
    u = _rms(x_ref[...], g_ref[...]).astype(BF16)
    off = 0
    for o_ref, (_, width, dtype) in zip(o_refs, PROJ_OUTS):
        for c in range(width // GROUP):
            z = _dot(u, w_ref[:, off + c * GROUP: off + (c + 1) * GROUP])
            o_ref[:, c * GROUP:(c + 1) * GROUP] = z.astype(dtype)
        off += width
    vt_ref = o_refs[-1]
    for t in range(PROJ_ROWS // TILE):
        vt_ref[t] = _dot_nt(wvt_ref[...], u[t * TILE:(t + 1) * TILE]).astype(BF16)


def _in_proj(h, g, w, wvt):
    n = h.shape[0]
    tiles = PROJ_ROWS // TILE
    return pl.pallas_call(
        _in_proj_body,
        grid=(n // PROJ_ROWS,),
        in_specs=[pl.BlockSpec((PROJ_ROWS, D_MODEL), lambda i: (i, 0)),
                  _const_spec((1, D_MODEL)), _const_spec((D_MODEL, PROJ_WIDTH)),
                  _const_spec((GROUP, D_MODEL))],
        out_specs=[pl.BlockSpec((PROJ_ROWS, w_), lambda i: (i, 0)) for _, w_, _ in PROJ_OUTS]
        + [pl.BlockSpec((tiles, GROUP, TILE), lambda i: (i, 0, 0))],
        out_shape=[jax.ShapeDtypeStruct((n, w_), dt) for _, w_, dt in PROJ_OUTS]
        + [jax.ShapeDtypeStruct((n // TILE, GROUP, TILE), BF16)],
        compiler_params=_params("parallel"),
        name="in_proj",
    )(h, g, w, wvt)


ML_ROWS = 512
ML_HALO = 8


def _chunk_scan(x, rin, op, fill):
    s = 1
    while s < MLSTM_CHUNK:
        x = op(x, jnp.where(rin >= s, pltpu.roll(x, s, 0), fill))
        s *= 2
    return x


def _mlstm_body(za_ref, zg_ref, cw_ref, cb_ref, gb_ref, ng_ref, y_ref,
                buf_ref, tail_ref, hh_ref, c_ref, n_ref, m_ref):
    L = MLSTM_CHUNK

    @pl.when(pl.program_id(1) == 0)
    def _():
        tail_ref[...] = jnp.zeros_like(tail_ref)
        c_ref[...] = jnp.zeros_like(c_ref)
        n_ref[...] = jnp.zeros_like(n_ref)
        m_ref[...] = jnp.zeros_like(m_ref)

    buf_ref[0:ML_HALO, :] = tail_ref[...]
    buf_ref[ML_HALO:, :] = za_ref[:, 0:2 * GROUP]
    tail_ref[...] = za_ref[ML_ROWS - ML_HALO:, 0:2 * GROUP]
    conv = cb_ref[...]
    for j in range(CONV_WIDTH):
        conv = conv + buf_ref[pl.ds(ML_HALO - (CONV_WIDTH - 1) + j, ML_ROWS), :] * cw_ref[j:j + 1, :]
    qk = conv * jax.nn.sigmoid(conv)
    q = qk[:, :GROUP]
    k = qk[:, GROUP:] * (HEAD_DIM ** -0.5)
    v = za_ref[:, 2 * GROUP:3 * GROUP].astype(BF16)
    q_bf, k_bf = q.astype(BF16), k.astype(BF16)

    ii = zg_ref[:, :GROUP] + gb_ref[:, :GROUP]
    fx = zg_ref[:, GROUP:] + gb_ref[:, GROUP:]
    lf = jnp.minimum(fx, 0.0) - jnp.log1p(jnp.exp(-jnp.abs(fx)))
    rin = lax.broadcasted_iota(jnp.int32, (ML_ROWS, GROUP), 0) % L
    b = _chunk_scan(lf, rin, jnp.add, 0.0)
    a = ii - b
    ca = _chunk_scan(a, rin, jnp.maximum, NEG)

    bd = _group_mask(GROUP, GROUP, HEAD_DIM, HEAD_DIM)
    ones_bd = bd.astype(BF16)
    row = lax.broadcasted_iota(jnp.int32, (L, GROUP), 0)
    key = lax.broadcasted_iota(jnp.int32, (L, GROUP), 1) % L
    causal = key <= row
    diag = key == row

    m_prev = m_ref[...]
    for c in range(ML_ROWS // L):
        rs = slice(c * L, (c + 1) * L)
        q_c, k_c, v_c = q[rs], k[rs], v[rs]
        q_b = q_bf[rs]
        a_c, b_c = a[rs], b[rs]
        g = jnp.maximum(m_prev, ca[rs])
        g_last = g[L - 1:L]
        a_row = jnp.sum(jnp.where(diag, a_c, 0.0), axis=0, keepdims=True)
        decay = jnp.exp(jnp.where(causal, a_row - g, NEG))
        sc = _dot_nt(q_b, _tile_rows(k_bf[rs], HEADS, bd)) * decay
        inter = jnp.exp(m_prev - g)
        num = inter * _dot(q_b, c_ref[...].astype(BF16)) + _dot(sc.astype(BF16), _tile_rows(v_c, HEADS, bd))
        den = inter * _group_sum(q_c * n_ref[...], ones_bd) + _group_sum(sc, ones_bd)
        hh_ref[rs, :] = num / jnp.maximum(jnp.abs(den), jnp.exp(-(b_c + g)))
        kw = k_c * jnp.exp(a_c - g_last)
        carry = jnp.exp(m_prev - g_last)
        c_ref[...] = carry * c_ref[...] + jnp.where(bd, _dot_tn(kw.astype(BF16), v_c), 0.0)
        n_ref[...] = carry * n_ref[...] + jnp.sum(kw, axis=0, keepdims=True)
        m_prev = b_c[L - 1:L] + g_last
    m_ref[...] = m_prev

    hh = hh_ref[...]
    mu = _group_sum(hh, ones_bd) * (1.0 / HEAD_DIM)
    dev = hh - mu
    var = _group_sum(dev * dev, ones_bd) * (1.0 / HEAD_DIM)
    o_gate = jax.nn.sigmoid(za_ref[:, 3 * GROUP:])
    y_ref[...] = (dev * lax.rsqrt(var + RMS_EPS) * ng_ref[...] * o_gate).astype(BF16)


def _mlstm(za, zg, conv_w, conv_b, gate_b, norm_g, *, batch, seq):
    za = za.reshape(batch, seq, 4 * GROUP)
    zg = zg.reshape(batch, seq, 2 * GROUP)
    y = pl.pallas_call(
        _mlstm_body,
        grid=(batch, seq // ML_ROWS),
        in_specs=[pl.BlockSpec((None, ML_ROWS, 4 * GROUP), lambda b, j: (b, j, 0)),
                  pl.BlockSpec((None, ML_ROWS, 2 * GROUP), lambda b, j: (b, j, 0)),
                  _const_spec((CONV_WIDTH, 2 * GROUP)), _const_spec((1, 2 * GROUP)),
                  _const_spec((1, 2 * GROUP)), _const_spec((1, GROUP))],
        out_specs=pl.BlockSpec((None, ML_ROWS, GROUP), lambda b, j: (b, j, 0)),
        out_shape=jax.ShapeDtypeStruct((batch, seq, GROUP), BF16),
        scratch_shapes=[pltpu.VMEM((ML_ROWS + ML_HALO, 2 * GROUP), F32),
                        pltpu.VMEM((ML_HALO, 2 * GROUP), F32),
                        pltpu.VMEM((ML_ROWS, GROUP), F32),
                        pltpu.VMEM((GROUP, GROUP), F32),
                        pltpu.VMEM((1, GROUP), F32),
                        pltpu.VMEM((1, GROUP), F32)],
        compiler_params=_params("parallel", "arbitrary"),
        name="mlstm",
    )(za, zg, conv_w, conv_b, gate_b, norm_g)
    return y.reshape(batch * seq, GROUP)


POOL_ROWS = 512
POOL_HALO = 16


def _pool_body(u_ref, w_ref, s_ref, y_ref, buf_ref, tail_ref):
    j = pl.program_id(1)

    @pl.when(j == 0)
    def _():
        tail_ref[...] = jnp.zeros_like(tail_ref)

    buf_ref[0:POOL_HALO, :] = tail_ref[...]
    buf_ref[POOL_HALO:, :] = u_ref[...]
    tail_ref[...] = u_ref[POOL_ROWS - POOL_HALO:, :]
    sums, s = [], buf_ref[...]
    for shift in (1, 2, 4, 8):
        s = s + pltpu.roll(s, shift, 0)
        sums.append(s[POOL_HALO:])
    u = u_ref[...]
    lane_group = lax.broadcasted_iota(jnp.int32, (POOL_ROWS, GROUP), 1) // HEAD_DIM
    t = j * POOL_ROWS + lax.broadcasted_iota(jnp.int32, (POOL_ROWS, GROUP), 0)
    total, win = sums[3], jnp.full((POOL_ROWS, GROUP), POOL_WINDOWS[3], jnp.int32)
    for gi in (2, 1, 0):
        total = jnp.where(lane_group == gi, sums[gi], total)
        win = jnp.where(lane_group == gi, POOL_WINDOWS[gi], win)
    mean = total / jnp.minimum(t + 1, win).astype(F32)
    y = _dot((mean - u).astype(BF16), w_ref[...]) * s_ref[...]
    y_ref[...] = y.astype(BF16)


def _pool(zp, w_bd, scale, *, batch, seq):
    zp = zp.reshape(batch, seq, GROUP)
    y = pl.pallas_call(
        _pool_body,
        grid=(batch, seq // POOL_ROWS),
        in_specs=[pl.BlockSpec((None, POOL_ROWS, GROUP), lambda b, j: (b, j, 0)),
                  _const_spec((GROUP, GROUP)), _const_spec((1, GROUP))],
        out_specs=pl.BlockSpec((None, POOL_ROWS, GROUP), lambda b, j: (b, j, 0)),
        out_shape=jax.ShapeDtypeStruct((batch, seq, GROUP), BF16),
        scratch_shapes=[pltpu.VMEM((POOL_ROWS + POOL_HALO, GROUP), F32),
                        pltpu.VMEM((POOL_HALO, GROUP), F32)],
        compiler_params=_params("parallel", "arbitrary"),
        name="pool",
    )(zp, w_bd, scale)
    return y.reshape(batch * seq, GROUP)


def _toeplitz_body(w_ref, o_ref):
    x = jnp.broadcast_to(w_ref[...], (TILE, 2 * TILE))
    o_ref[...] = pltpu.roll(x, 0, 1, stride=1, stride_axis=0)[:, :TILE]


def _toeplitz(rows):
    n = rows.shape[0]
    return pl.pallas_call(
        _toeplitz_body,
        grid=(n,),
        in_specs=[pl.BlockSpec((None, 1, 2 * TILE), lambda i: (i, 0, 0))],
        out_specs=pl.BlockSpec((None, TILE, TILE), lambda i: (i, 0, 0)),
        out_shape=jax.ShapeDtypeStruct((n, TILE, TILE), F32),
        compiler_params=_params("parallel"),
        name="toeplitz",
    )(rows.reshape(n, 1, 2 * TILE))


_TOEPLITZ_X = np.where(np.arange(2 * TILE) <= TILE, -np.arange(2 * TILE), 2 * TILE - np.arange(2 * TILE))


def _t5_bucket(dist):
    max_exact = T5_BUCKETS // 2
    d = jnp.maximum(dist, 1).astype(F32)
    large = max_exact + (jnp.log(d / max_exact) / math.log(T5_MAX_DIST / max_exact)
                         * (T5_BUCKETS - max_exact)).astype(jnp.int32)
    large = jnp.minimum(large, T5_BUCKETS - 1)
    return jnp.where(dist < max_exact, dist, large)


def _bias_tiles(t5_bias, seq):
    x = jnp.asarray(_TOEPLITZ_X, jnp.int32)
    rows = []
    for (w, d) in DIL_PATTERNS:
        tab = t5_bias[_t5_bucket(jnp.arange(DIL_BACK + 1) * d), :HEADS].T
        prev = jnp.where(x <= 0, tab[:, jnp.clip(x + DIL_BACK, 0, DIL_BACK)], NEG)
        cur = jnp.where(x >= 0, tab[:, jnp.clip(x, 0, DIL_BACK)], NEG)
        rows.append(jnp.stack([prev, cur], axis=1))
    dil = _toeplitz(jnp.stack(rows).reshape(-1, 2 * TILE))
    dil = dil.reshape(len(DIL_PATTERNS), HEADS, 2, TILE, TILE).transpose(0, 3, 1, 2, 4)
    dil = dil.reshape(len(DIL_PATTERNS), TILE, HEADS * 2 * TILE)
    tab = t5_bias[_t5_bucket(jnp.arange(seq)), HEADS:].T * LOG2E
    noff = seq // TILE - DIFF_MIN_OFFSET
    dist = (jnp.arange(noff)[:, None] + DIFF_MIN_OFFSET) * TILE - x[None, :]
    rows = jnp.where(dist >= 0, tab[:, jnp.clip(dist, 0, seq - 1)], NEG)
    diff = _toeplitz(rows.transpose(1, 0, 2).reshape(-1, 2 * TILE))
    return dil, diff.reshape(noff, HEADS * TILE, TILE)


def _dil_body(*refs, subs, merge):
    if merge:
        (q_ref, k_ref, v_ref, kp_ref, vp_ref, bias_ref,
         o1_ref, l1_ref, o2_ref, l2_ref, y_ref) = refs
    else:
        q_ref, k_ref, v_ref, kp_ref, vp_ref, bias_ref, o_ref, lse_ref = refs
    first = pl.program_id(2) == 0
    kmask = _group_mask(HEADS * 2 * TILE, GROUP, 2 * TILE, HEAD_DIM)
    head_of_lane = lax.broadcasted_iota(jnp.int32, (TILE, GROUP), 1) // HEAD_DIM
    ones_exp = kmask.astype(BF16)
    lane = lax.broadcasted_iota(jnp.int32, (TILE, HEADS * 2 * TILE), 1)
    before_start = jnp.where((lane % (2 * TILE)) < TILE, NEG, 0.0)
    for sb in range(subs):
        rs = slice(sb * TILE, (sb + 1) * TILE)
        if sb == 0:
            k2 = jnp.concatenate([kp_ref[...], k_ref[rs, :]], axis=0)
            v2 = jnp.concatenate([vp_ref[...], v_ref[rs, :]], axis=0)
        else:
            k2 = k_ref[(sb - 1) * TILE:(sb + 1) * TILE, :]
            v2 = v_ref[(sb - 1) * TILE:(sb + 1) * TILE, :]
        s = _dot_nt(q_ref[rs, :], _tile_rows(k2, HEADS, kmask)) + bias_ref[...]
        if sb == 0:
            s = s + jnp.where(first, before_start, 0.0)
        p, m_x = [], jnp.zeros((TILE, GROUP), F32)
        for h in range(HEADS):
            s_h = s[:, h * 2 * TILE:(h + 1) * 2 * TILE]
            m_h = jnp.max(s_h, axis=-1, keepdims=True)
            p.append(jnp.exp(s_h - m_h))
            m_x = jnp.where(head_of_lane == h, m_h, m_x)
        p = jnp.concatenate(p, axis=-1).astype(BF16)
        acc = _dot(p, _tile_rows(v2, HEADS, kmask))
        l_x = _dot(p, ones_exp)
        o = acc / l_x
        lse = m_x + jnp.log(l_x)
        if merge:
            o1, l1, o2, l2 = o1_ref[rs, :], l1_ref[rs, :], o2_ref[rs, :], l2_ref[rs, :]
            top = jnp.maximum(jnp.maximum(l1, l2), lse)
            w1, w2, w3 = jnp.exp(l1 - top), jnp.exp(l2 - top), jnp.exp(lse - top)
            y_ref[rs, :] = ((w1 * o1 + w2 * o2 + w3 * o) / (w1 + w2 + w3)).astype(BF16)
        else:
            o_ref[rs, :] = o
            lse_ref[rs, :] = lse


def _dilated_pattern(zc, bias, dil, *, batch, seq, merge_with=None):
    length = seq // dil
    rows = min(length, 512)
    subs = rows // TILE
    zc = zc.reshape(batch, length, dil * 3 * GROUP)
    blk = lambda which: pl.BlockSpec((None, rows, GROUP), lambda b, r, n: (b, n, 3 * r + which))
    prev = lambda which: pl.BlockSpec(
        (None, TILE, GROUP), lambda b, r, n: (b, jnp.maximum(n * subs - 1, 0), 3 * r + which))
    nat = pl.BlockSpec((None, rows, GROUP), lambda b, r, n: (b, n, r))
    in_specs = [blk(0), blk(1), blk(2), prev(1), prev(2), _const_spec((TILE, HEADS * 2 * TILE))]
    args = [zc, zc, zc, zc, zc, bias]
    if merge_with is None:
        out_specs = [nat, nat]
        out_shape = [jax.ShapeDtypeStruct((batch, length, dil * GROUP), F32)] * 2
    else:
        in_specs += [nat] * len(merge_with)
        args += [t.reshape(batch, length, dil * GROUP) for t in merge_with]
        out_specs = nat
        out_shape = jax.ShapeDtypeStruct((batch, length, dil * GROUP), BF16)
    out = pl.pallas_call(
        functools.partial(_dil_body, subs=subs, merge=merge_with is not None),
        grid=(batch, dil, length // rows),
        in_specs=in_specs, out_specs=out_specs, out_shape=out_shape,
        compiler_params=_params("parallel", "parallel", "parallel"),
        name=f"dilated_d{dil}",
    )(*args)
    if merge_with is None:
        return [t.reshape(batch * seq, GROUP) for t in out]
    return out.reshape(batch * seq, GROUP)


def _dilated(zc, dil_bias, *, batch, seq):
    o1, l1 = _dilated_pattern(zc, dil_bias[0], DIL_PATTERNS[0][1], batch=batch, seq=seq)
    o2, l2 = _dilated_pattern(zc, dil_bias[1], DIL_PATTERNS[1][1], batch=batch, seq=seq)
    return _dilated_pattern(zc, dil_bias[2], DIL_PATTERNS[2][1], batch=batch, seq=seq,
                            merge_with=(o1, l1, o2, l2))


DIFF_Q = 256
DIFF_K = 256
DIFF_GROUPS = 2 * HEADS
DIFF_AUG = HEAD_DIM + 16


def _diff_body(q_ref, k_ref, vt_ref, bias_ref, lam_ref, sg_ref, y_ref,
               kexp_ref, vaug_ref, acc_ref, sta_ref, stb_ref, *, lam_init, key_steps):
    qi = pl.program_id(1)

    @pl.when(qi == 0)
    def _():
        grp = lax.broadcasted_iota(jnp.int32, (DIFF_GROUPS * DIFF_K, GROUP), 0) // DIFF_K
        slot = lax.broadcasted_iota(jnp.int32, (DIFF_GROUPS * DIFF_K, GROUP), 1) // DIFF_QK_HALF
        kmask = slot == 2 * (grp % HEADS) + grp // HEADS
        ones_rows = (lax.broadcasted_iota(jnp.int32, (DIFF_AUG - HEAD_DIM, DIFF_K), 0) == 0).astype(BF16)

        def build(j, carry):
            k_t = k_ref[pl.ds(pl.multiple_of(j * DIFF_K, DIFF_K), DIFF_K), :]
            kexp_ref[j] = jnp.where(kmask, jnp.concatenate([k_t] * DIFF_GROUPS, axis=0),
                                    jnp.zeros((), BF16))
            vt = jnp.concatenate([vt_ref[2 * j], vt_ref[2 * j + 1]], axis=1)
            for h in range(HEADS):
                vaug_ref[j, h] = jnp.concatenate([vt[h * HEAD_DIM:(h + 1) * HEAD_DIM], ones_rows], axis=0)
            return carry

        lax.fori_loop(0, key_steps, build, 0)

    acc_ref[...] = jnp.zeros_like(acc_ref)
    q = q_ref[...]
    last = key_steps - 1

    def scores(s_ref, j):
        off = jnp.maximum(2 * (qi - j) + 2, 0)
        b_m, b_0, b_p = bias_ref[off], bias_ref[off + 1], bias_ref[off + 2]
        raw = _dot_nt(kexp_ref[jnp.minimum(j, last)], q)
        tops = []
        for g in range(DIFF_GROUPS):
            hs = slice((g % HEADS) * TILE, (g % HEADS + 1) * TILE)
            bias = jnp.concatenate([jnp.concatenate([b_0[hs], b_p[hs]], axis=1),
                                    jnp.concatenate([b_m[hs], b_0[hs]], axis=1)], axis=0)
            s = raw[g * DIFF_K:(g + 1) * DIFF_K] + bias
            s_ref[g * DIFF_K:(g + 1) * DIFF_K, :] = s
            tops.append(jnp.max(s, axis=0, keepdims=True))
        return tuple(tops)

    def consume(s_ref, tops, j, carry):
        ms, ls = carry
        jv = jnp.minimum(j, last)
        new_ms, new_ls = [], []
        for g in range(DIFF_GROUPS):
            mp, h = divmod(g, HEADS)
            m_new = jnp.maximum(ms[g], tops[g])
            p = jnp.exp2(s_ref[g * DIFF_K:(g + 1) * DIFF_K, :] - m_new).astype(BF16)
            alpha = jnp.exp2(ms[g] - m_new)
            r = _dot(vaug_ref[jv, h], p)
            acc_ref[mp, h] = alpha * acc_ref[mp, h] + r[:HEAD_DIM]
            new_ls.append(alpha * ls[g] + r[HEAD_DIM:HEAD_DIM + 1])
            new_ms.append(m_new)
        return tuple(new_ms), tuple(new_ls)

    def pair(jj, carry):
        tops_a, state = carry
        j = 2 * jj
        tops_b = scores(stb_ref, j + 1)
        state = consume(sta_ref, tops_a, j, state)
        tops_a = scores(sta_ref, j + 2)
        return tops_a, consume(stb_ref, tops_b, j + 1, state)

    init = (tuple(jnp.full((1, DIFF_Q), NEG, F32) for _ in range(DIFF_GROUPS)),
            tuple(jnp.zeros((1, DIFF_Q), F32) for _ in range(DIFF_GROUPS)))
    _, (_, ls) = lax.fori_loop(0, (qi + 2) // 2, pair, (scores(sta_ref, 0), init))

    lv = lam_ref[...]
    lam = (jnp.exp(jnp.sum(lv[0:1] * lv[1:2], axis=-1, keepdims=True))
           - jnp.exp(jnp.sum(lv[2:3] * lv[3:4], axis=-1, keepdims=True)) + lam_init)
    outs = []
    for h in range(HEADS):
        o = acc_ref[0, h] / ls[h] - lam * (acc_ref[1, h] / ls[HEADS + h])
        ms_o = jnp.mean(o * o, axis=0, keepdims=True)
        outs.append(o * lax.rsqrt(ms_o + SUBLN_EPS) * sg_ref[...] * (1.0 - lam_init))
    y_ref[...] = jnp.concatenate(outs, axis=0).T.astype(BF16)


def _diff_attention(zd, vt, bias, lam_vecs, subln_cols, *, lam_init, batch, seq):
    zd = zd.reshape(batch, seq, 2 * GROUP)
    key_tiles = seq // TILE
    key_steps = seq // DIFF_K
    vt = vt.reshape(batch, key_tiles, GROUP, TILE)
    y = pl.pallas_call(
        functools.partial(_diff_body, lam_init=lam_init, key_steps=key_steps),
        grid=(batch, seq // DIFF_Q),
        in_specs=[pl.BlockSpec((None, DIFF_Q, GROUP), lambda b, i: (b, i, 0)),
                  pl.BlockSpec((None, seq, GROUP), lambda b, i: (b, 0, 1), pipeline_mode=pl.Buffered(1)),
                  pl.BlockSpec((None, key_tiles, GROUP, TILE), lambda b, i: (b, 0, 0, 0),
                               pipeline_mode=pl.Buffered(1)),
                  _const_spec((key_tiles - DIFF_MIN_OFFSET, HEADS * TILE, TILE)),
                  _const_spec((4, DIFF_QK_HALF)), _const_spec((HEAD_DIM, DIFF_Q))],
        out_specs=pl.BlockSpec((None, DIFF_Q, GROUP), lambda b, i: (b, i, 0)),
        out_shape=jax.ShapeDtypeStruct((batch, seq, GROUP), BF16),
        scratch_shapes=[pltpu.VMEM((key_steps, DIFF_GROUPS * DIFF_K, GROUP), BF16),
                        pltpu.VMEM((key_steps, HEADS, DIFF_AUG, DIFF_K), BF16),
                        pltpu.VMEM((2, HEADS, HEAD_DIM, DIFF_Q), F32),
                        pltpu.VMEM((DIFF_GROUPS * DIFF_K, DIFF_Q), F32),
                        pltpu.VMEM((DIFF_GROUPS * DIFF_K, DIFF_Q), F32)],
        compiler_params=_params("parallel", "arbitrary"),
        name="diff_attn",
    )(zd, zd, vt, bias, lam_vecs, subln_cols)
    return y.reshape(batch * seq, GROUP)


KV_ROWS = 512


def _mem_kv_body(m_ref, g_ref, w_ref, k_ref, v_ref):
    u = _rms(m_ref[...], g_ref[...]).astype(BF16)
    for c in range(D_MODEL // GROUP):
        sl = slice(c * GROUP, (c + 1) * GROUP)
        k_ref[:, sl] = _dot(u, w_ref[:, sl]).astype(BF16)
        v_ref[:, sl] = _dot(u, w_ref[:, D_MODEL + c * GROUP: D_MODEL + (c + 1) * GROUP]).astype(BF16)


def _mem_kv(mem, g, w):
    n = mem.shape[0]
    row = pl.BlockSpec((KV_ROWS, D_MODEL), lambda i: (i, 0))
    return pl.pallas_call(
        _mem_kv_body,
        grid=(n // KV_ROWS,),
        in_specs=[row, _const_spec((1, D_MODEL)), _const_spec((D_MODEL, 2 * D_MODEL))],
        out_specs=[row, row],
        out_shape=[jax.ShapeDtypeStruct((n, D_MODEL), BF16)] * 2,
        compiler_params=_params("parallel"),
        name="mem_kv",
    )(mem, g, w)


XATTN_ROWS = 512


def _xattn_body(x_ref, ya_ref, yb_ref, yc_ref, yd_ref, wout_ref, g_ref, wq_ref, k_ref, v_ref, wo_ref,
                o_ref, q_scr, a_scr):
    x = x_ref[...]
    for gi, y_ref in enumerate((ya_ref, yb_ref, yc_ref, yd_ref)):
        x = x + _dot(y_ref[...], wout_ref[gi * GROUP:(gi + 1) * GROUP, :])
    u = _rms(x, g_ref[...]).astype(BF16)
    for c in range(D_MODEL // GROUP):
        sl = slice(c * GROUP, (c + 1) * GROUP)
        q_scr[:, sl] = _dot(u, wq_ref[:, sl]).astype(BF16)
    for h in range(MEM_HEADS):
        sl = slice(h * MEM_HEAD_DIM, (h + 1) * MEM_HEAD_DIM)
        s = _dot_nt(q_scr[:, sl], k_ref[:, sl]) * (MEM_HEAD_DIM ** -0.5)
        e = jnp.exp(s - jnp.max(s, axis=-1, keepdims=True))
        l = jnp.sum(e, axis=-1, keepdims=True)
        a_scr[:, sl] = (_dot(e.astype(BF16), v_ref[:, sl]) / l).astype(BF16)
    o_ref[...] = x + _dot(a_scr[...], wo_ref[...])


def _xattn(h, ys, w_out, g, wq, k, v, wo, *, batch, seq):
    h3 = h.reshape(batch, seq, D_MODEL)
    ys = [y.reshape(batch, seq, GROUP) for y in ys]
    k3 = k.reshape(batch, MEM_LEN, D_MODEL)
    v3 = v.reshape(batch, MEM_LEN, D_MODEL)
    row = pl.BlockSpec((None, XATTN_ROWS, D_MODEL), lambda b, i: (b, i, 0))
    grp = pl.BlockSpec((None, XATTN_ROWS, GROUP), lambda b, i: (b, i, 0))
    mem = pl.BlockSpec((None, MEM_LEN, D_MODEL), lambda b, i: (b, 0, 0))
    weight = _const_spec((D_MODEL, D_MODEL))
    out = pl.pallas_call(
        _xattn_body,
        grid=(batch, seq // XATTN_ROWS),
        in_specs=[row, grp, grp, grp, grp, weight, _const_spec((1, D_MODEL)), weight, mem, mem, weight],
        out_specs=row,
        out_shape=jax.ShapeDtypeStruct((batch, seq, D_MODEL), F32),
        scratch_shapes=[pltpu.VMEM((XATTN_ROWS, D_MODEL), BF16), pltpu.VMEM((XATTN_ROWS, D_MODEL), BF16)],
        compiler_params=_params("parallel", "parallel"),
        name="xattn",
    )(h3, *ys, w_out, g, wq, k3, v3, wo)
    return out.reshape(batch * seq, D_MODEL)


def _per_head_lanes(x):
    return jnp.repeat(x, HEAD_DIM, axis=-1)


def _in_proj_weight(w_in):
    g = GROUP
    q_a, k_a, v_a, o_a = (w_in[:, i * g:(i + 1) * g] for i in range(4))
    ig = w_in[:, 4 * g:4 * g + HEADS]
    fg = w_in[:, 4 * g + HEADS:4 * g + 2 * HEADS]
    rest = w_in[:, 4 * g + 2 * HEADS:]
    pool, q_c, k_c, v_c, q_d, k_d, v_d = (rest[:, i * g:(i + 1) * g] for i in range(7))
    q_c = q_c * (HEAD_DIM ** -0.5)
    k_d = k_d * DIFF_SCORE_SCALE
    cols = [q_a, k_a, v_a, o_a, _per_head_lanes(ig), _per_head_lanes(fg), pool,
            q_c, k_c, v_c, q_d, k_d]
    return jnp.concatenate(cols, axis=1).astype(BF16), v_d.T.astype(BF16)


def _block_diag(w):
    g, c, _ = w.shape
    eye = jnp.eye(g, dtype=w.dtype)
    return (eye[:, None, :, None] * w[:, :, None, :]).reshape(g * c, g * c)


def kernel(x, mem, t5_bias, ffn1_norm, ffn1_w_gate, ffn1_w_up, ffn1_w_down, mix_norm, w_in,
           mlstm_conv_w, mlstm_conv_b, mlstm_gate_b, mlstm_norm, pool_w, pool_scale,
           diff_lambda, diff_subln, w_out, xattn_norm, mem_norm, xattn_wq, xattn_wkv, xattn_wo,
           ffn2_norm, ffn2_w_gate, ffn2_w_up, ffn2_w_down, final_norm):
    batch, seq, _ = x.shape
    n = batch * seq
    dil_bias, diff_bias = _bias_tiles(t5_bias, seq)
    h = x.reshape(n, D_MODEL)
    mem2 = mem.reshape(batch * MEM_LEN, D_MODEL)
    row = lambda v: v.reshape(1, -1)
    for l in range(DEPTH):
        lam_init = 0.8 - 0.6 * math.exp(-0.3 * l)
        h = _ffn(h, row(ffn1_norm[l]), ffn1_w_gate[l].astype(BF16), ffn1_w_up[l].astype(BF16),
                 ffn1_w_down[l].astype(BF16), row(final_norm), final=False)
        za, zg, zp, zc, zd, vt = _in_proj(h, row(mix_norm[l]), *_in_proj_weight(w_in[l]))
        ya = _mlstm(za, zg, mlstm_conv_w[l], row(mlstm_conv_b[l]),
                    row(_per_head_lanes(mlstm_gate_b[l].reshape(2, HEADS))), row(mlstm_norm[l]),
                    batch=batch, seq=seq)
        yb = _pool(zp, _block_diag(pool_w[l]).astype(BF16), row(pool_scale[l]), batch=batch, seq=seq)
        yc = _dilated(zc, dil_bias, batch=batch, seq=seq)
        yd = _diff_attention(zd, vt, diff_bias, diff_lambda[l],
                             jnp.broadcast_to(diff_subln[l][:, None], (HEAD_DIM, DIFF_Q)),
                             lam_init=lam_init, batch=batch, seq=seq)
        k_mem, v_mem = _mem_kv(mem2, row(mem_norm[l]), xattn_wkv[l].astype(BF16))
        h = _xattn(h, (ya, yb, yc, yd), w_out[l].astype(BF16),
                   row(xattn_norm[l]), xattn_wq[l].astype(BF16), k_mem, v_mem,
                   xattn_wo[l].astype(BF16), batch=batch, seq=seq)
        h = _ffn(h, row(ffn2_norm[l]), ffn2_w_gate[l].astype(BF16), ffn2_w_up[l].astype(BF16),
                 ffn2_w_down[l].astype(BF16), row(final_norm), final=(l == DEPTH - 1))
    return h.reshape(batch, seq, D_MODEL)
```

```python
import functools
import math

import jax
import jax.numpy as jnp
import numpy as np
from jax import lax
from jax.experimental import pallas as pl
from jax.experimental.pallas import tpu as pltpu

F32 = jnp.float32
BF16 = jnp.bfloat16

D_MODEL = 1024
D_FF = 2816
DEPTH = 4
GROUP = 256
HEADS = 4
HEAD_DIM = GROUP // HEADS
MEM_LEN = 256
MEM_HEADS = 4
MEM_HEAD_DIM = D_MODEL // MEM_HEADS
MLSTM_CHUNK = 64
CONV_WIDTH = 4
POOL_WINDOWS = (2, 4, 8, 16)
DIL_PATTERNS = ((128, 1), (512, 4), (2048, 16))
DIL_BACK = 128
DIFF_QK_HALF = HEAD_DIM // 2
T5_BUCKETS = 32
T5_MAX_DIST = 2048
RMS_EPS = 1e-6
SUBLN_EPS = 1e-5
NEG = -1e30
LOG2E = math.log2(math.e)
DIFF_SCORE_SCALE = (DIFF_QK_HALF ** -0.5) * LOG2E
DIL_SCORE_SCALE = (HEAD_DIM ** -0.5) * LOG2E
AUG_ROWS = HEAD_DIM + 16
DIFF_MIN_OFFSET = -3
TILE = 128

VMEM_LIMIT_BYTES = 56 * 1024 * 1024


def _rms(xf, g, eps=RMS_EPS):
    return xf * lax.rsqrt(jnp.mean(xf * xf, axis=-1, keepdims=True) + eps) * g


def _const_spec(shape):
    zeros = (0,) * len(shape)
    return pl.BlockSpec(shape, lambda *_: zeros, pipeline_mode=pl.Buffered(1))


def _params(*sem):
    return pltpu.CompilerParams(dimension_semantics=sem, vmem_limit_bytes=VMEM_LIMIT_BYTES)


def _group_mask(rows, cols, row_group, col_group):
    r = lax.broadcasted_iota(jnp.int32, (rows, cols), 0) // row_group
    c = lax.broadcasted_iota(jnp.int32, (rows, cols), 1) // col_group
    return r == c


def _tile_rows(x, reps, mask):
    return jnp.where(mask, jnp.concatenate([x] * reps, axis=0), jnp.zeros((), x.dtype))


def _dot(a, b):
    return jnp.dot(a, b, preferred_element_type=F32)


def _dot_nt(a, b):
    return lax.dot_general(a, b, (((1,), (1,)), ((), ())), preferred_element_type=F32)


def _dot_tn(a, b):
    return lax.dot_general(a, b, (((0,), (0,)), ((), ())), preferred_element_type=F32)


def _group_sum(x, ones_bd):
    hi = x.astype(BF16)
    lo = (x - hi.astype(F32)).astype(BF16)
    return _dot(hi, ones_bd) + _dot(lo, ones_bd)


FFN_ROWS = 512
FFN_COLS = 256


def _ffn_body(x_ref, g_ref, wg_ref, wu_ref, wd_ref, fg_ref, o_ref, act_ref, *, final):
    x = x_ref[...]
    u = _rms(x, g_ref[...]).astype(BF16)
    for c in range(D_FF // FFN_COLS):
        sl = slice(c * FFN_COLS, (c + 1) * FFN_COLS)
        gate = _dot(u, wg_ref[:, sl])
        up = _dot(u, wu_ref[:, sl])
        act_ref[:, sl] = (gate * jax.nn.sigmoid(gate) * up).astype(BF16)
    y = x + 0.5 * _dot(act_ref[...], wd_ref[...])
    if final:
        y = _rms(y, fg_ref[...])
    o_ref[...] = y


def _ffn(h, g, wg, wu, wd, fg, *, final):
    n = h.shape[0]
    row = pl.BlockSpec((FFN_ROWS, D_MODEL), lambda i: (i, 0))
    return pl.pallas_call(
        functools.partial(_ffn_body, final=final),
        grid=(n // FFN_ROWS,),
        in_specs=[row, _const_spec((1, D_MODEL)), _const_spec((D_MODEL, D_FF)),
                  _const_spec((D_MODEL, D_FF)), _const_spec((D_FF, D_MODEL)),
                  _const_spec((1, D_MODEL))],
        out_specs=row,
        out_shape=jax.ShapeDtypeStruct((n, D_MODEL), F32),
        scratch_shapes=[pltpu.VMEM((FFN_ROWS, D_FF), BF16)],
        compiler_params=_params("parallel"),
        name="ffn_final" if final else "ffn",
    )(h, g, wg, wu, wd, fg)


PROJ_ROWS = 512
PROJ_OUTS = (("a", 4 * GROUP, F32), ("g", 2 * GROUP, F32), ("p", GROUP, F32),
             ("c", 3 * GROUP, BF16), ("d", 2 * GROUP, BF16))
PROJ_WIDTH = sum(w for _, w, _ in PROJ_OUTS)


def _in_proj_body(x_ref, g_ref, w_ref, wvt_ref, *o_refs):
    u = _rms(x_ref[...], g_ref[...]).astype(BF16)
    off = 0
    for o_ref, (_, width, dtype) in zip(o_refs, PROJ_OUTS):
        for c in range(width // GROUP):
            z = _dot(u, w_ref[:, off + c * GROUP: off + (c + 1) * GROUP])
            o_ref[:, c * GROUP:(c + 1) * GROUP] = z.astype(dtype)
        off += width
    vt_ref = o_refs[-1]
    for t in range(PROJ_ROWS // TILE):
        vt_ref[t] = _dot_nt(wvt_ref[...], u[t * TILE:(t + 1) * TILE]).astype(BF16)


def _in_proj(h, g, w, wvt):
    n = h.shape[0]
    tiles = PROJ_ROWS // TILE
    return pl.pallas_call(
        _in_proj_body,
        grid=(n // PROJ_ROWS,),
        in_specs=[pl.BlockSpec((PROJ_ROWS, D_MODEL), lambda i: (i, 0)),
                  _const_spec((1, D_MODEL)), _const_spec((D_MODEL, PROJ_WIDTH)),
                  _const_spec((GROUP, D_MODEL))],
        out_specs=[pl.BlockSpec((PROJ_ROWS, w_), lambda i: (i, 0)) for _, w_, _ in PROJ_OUTS]
        + [pl.BlockSpec((tiles, GROUP, TILE), lambda i: (i, 0, 0))],
        out_shape=[jax.ShapeDtypeStruct((n, w_), dt) for _, w_, dt in PROJ_OUTS]
        + [jax.ShapeDtypeStruct((n // TILE, GROUP, TILE), BF16)],
        compiler_params=_params("parallel"),
        name="in_proj",
    )(h, g, w, wvt)


ML_ROWS = 512
ML_HALO = 8


def _chunk_scan(x, rin, op, fill):
    s = 1
    while s < MLSTM_CHUNK:
        x = op(x, jnp.where(rin >= s, pltpu.roll(x, s, 0), fill))
        s *= 2
    return x


def _mlstm_body(za_ref, zg_ref, cw_ref, cb_ref, gb_ref, ng_ref, y_ref,
                buf_ref, tail_ref, hh_ref, c_ref, n_ref, m_ref):
    L = MLSTM_CHUNK

    @pl.when(pl.program_id(1) == 0)
    def _():
        tail_ref[...] = jnp.zeros_like(tail_ref)
        c_ref[...] = jnp.zeros_like(c_ref)
        n_ref[...] = jnp.zeros_like(n_ref)
        m_ref[...] = jnp.zeros_like(m_ref)

    buf_ref[0:ML_HALO, :] = tail_ref[...]
    buf_ref[ML_HALO:, :] = za_ref[:, 0:2 * GROUP]
    tail_ref[...] = za_ref[ML_ROWS - ML_HALO:, 0:2 * GROUP]
    conv = cb_ref[...]
    for j in range(CONV_WIDTH):
        conv = conv + buf_ref[pl.ds(ML_HALO - (CONV_WIDTH - 1) + j, ML_ROWS), :] * cw_ref[j:j + 1, :]
    qk = conv * jax.nn.sigmoid(conv)
    q = qk[:, :GROUP]
    k = qk[:, GROUP:] * (HEAD_DIM ** -0.5)
    v = za_ref[:, 2 * GROUP:3 * GROUP].astype(BF16)
    q_bf, k_bf = q.astype(BF16), k.astype(BF16)

    ii = zg_ref[:, :GROUP] + gb_ref[:, :GROUP]
    fx = zg_ref[:, GROUP:] + gb_ref[:, GROUP:]
    lf = jnp.minimum(fx, 0.0) - jnp.log1p(jnp.exp(-jnp.abs(fx)))
    rin = lax.broadcasted_iota(jnp.int32, (ML_ROWS, GROUP), 0) % L
    b = _chunk_scan(lf, rin, jnp.add, 0.0)
    a = ii - b
    ca = _chunk_scan(a, rin, jnp.maximum, NEG)

    bd = _group_mask(GROUP, GROUP, HEAD_DIM, HEAD_DIM)
    ones_bd = bd.astype(BF16)
    row = lax.broadcasted_iota(jnp.int32, (L, GROUP), 0)
    key = lax.broadcasted_iota(jnp.int32, (L, GROUP), 1) % L
    causal = key <= row
    diag = key == row

    m_prev = m_ref[...]
    for c in range(ML_ROWS // L):
        rs = slice(c * L, (c + 1) * L)
        q_c, k_c, v_c = q[rs], k[rs], v[rs]
        q_b = q_bf[rs]
        a_c, b_c = a[rs], b[rs]
        g = jnp.maximum(m_prev, ca[rs])
        g_last = g[L - 1:L]
        a_row = jnp.sum(jnp.where(diag, a_c, 0.0), axis=0, keepdims=True)
        decay = jnp.exp(jnp.where(causal, a_row - g, NEG))
        sc = _dot_nt(q_b, _tile_rows(k_bf[rs], HEADS, bd)) * decay
        inter = jnp.exp(m_prev - g)
        num = inter * _dot(q_b, c_ref[...].astype(BF16)) + _dot(sc.astype(BF16), _tile_rows(v_c, HEADS, bd))
        den = inter * _group_sum(q_c * n_ref[...], ones_bd) + _group_sum(sc, ones_bd)
        hh_ref[rs, :] = num / jnp.maximum(jnp.abs(den), jnp.exp(-(b_c + g)))
        kw = k_c * jnp.exp(a_c - g_last)
        carry = jnp.exp(m_prev - g_last)
        c_ref[...] = carry * c_ref[...] + jnp.where(bd, _dot_tn(kw.astype(BF16), v_c), 0.0)
        n_ref[...] = carry * n_ref[...] + jnp.sum(kw, axis=0, keepdims=True)
        m_prev = b_c[L - 1:L] + g_last
    m_ref[...] = m_prev

    hh = hh_ref[...]
    mu = _group_sum(hh, ones_bd) * (1.0 / HEAD_DIM)
    dev = hh - mu
    var = _group_sum(dev * dev, ones_bd) * (1.0 / HEAD_DIM)
    o_gate = jax.nn.sigmoid(za_ref[:, 3 * GROUP:])
    y_ref[...] = (dev * lax.rsqrt(var + RMS_EPS) * ng_ref[...] * o_gate).astype(BF16)


def _mlstm(za, zg, conv_w, conv_b, gate_b, norm_g, *, batch, seq):
    za = za.reshape(batch, seq, 4 * GROUP)
    zg = zg.reshape(batch, seq, 2 * GROUP)
    y = pl.pallas_call(
        _mlstm_body,
        grid=(batch, seq // ML_ROWS),
        in_specs=[pl.BlockSpec((None, ML_ROWS, 4 * GROUP), lambda b, j: (b, j, 0)),
                  pl.BlockSpec((None, ML_ROWS, 2 * GROUP), lambda b, j: (b, j, 0)),
                  _const_spec((CONV_WIDTH, 2 * GROUP)), _const_spec((1, 2 * GROUP)),
                  _const_spec((1, 2 * GROUP)), _const_spec((1, GROUP))],
        out_specs=pl.BlockSpec((None, ML_ROWS, GROUP), lambda b, j: (b, j, 0)),
        out_shape=jax.ShapeDtypeStruct((batch, seq, GROUP), BF16),
        scratch_shapes=[pltpu.VMEM((ML_ROWS + ML_HALO, 2 * GROUP), F32),
                        pltpu.VMEM((ML_HALO, 2 * GROUP), F32),
                        pltpu.VMEM((ML_ROWS, GROUP), F32),
                        pltpu.VMEM((GROUP, GROUP), F32),
                        pltpu.VMEM((1, GROUP), F32),
                        pltpu.VMEM((1, GROUP), F32)],
        compiler_params=_params("parallel", "arbitrary"),
        name="mlstm",
    )(za, zg, conv_w, conv_b, gate_b, norm_g)
    return y.reshape(batch * seq, GROUP)


POOL_ROWS = 512
POOL_HALO = 16


def _pool_body(u_ref, w_ref, s_ref, y_ref, buf_ref, tail_ref):
    j = pl.program_id(1)

    @pl.when(j == 0)
    def _():
        tail_ref[...] = jnp.zeros_like(tail_ref)

    buf_ref[0:POOL_HALO, :] = tail_ref[...]
    buf_ref[POOL_HALO:, :] = u_ref[...]
    tail_ref[...] = u_ref[POOL_ROWS - POOL_HALO:, :]
    sums, s = [], buf_ref[...]
    for shift in (1, 2, 4, 8):
        s = s + pltpu.roll(s, shift, 0)
        sums.append(s[POOL_HALO:])
    u = u_ref[...]
    lane_group = lax.broadcasted_iota(jnp.int32, (POOL_ROWS, GROUP), 1) // HEAD_DIM
    t = j * POOL_ROWS + lax.broadcasted_iota(jnp.int32, (POOL_ROWS, GROUP), 0)
    total, win = sums[3], jnp.full((POOL_ROWS, GROUP), POOL_WINDOWS[3], jnp.int32)
    for gi in (2, 1, 0):
        total = jnp.where(lane_group == gi, sums[gi], total)
        win = jnp.where(lane_group == gi, POOL_WINDOWS[gi], win)
    mean = total / jnp.minimum(t + 1, win).astype(F32)
    y = _dot((mean - u).astype(BF16), w_ref[...]) * s_ref[...]
    y_ref[...] = y.astype(BF16)


def _pool(zp, w_bd, scale, *, batch, seq):
    zp = zp.reshape(batch, seq, GROUP)
    y = pl.pallas_call(
        _pool_body,
        grid=(batch, seq // POOL_ROWS),
        in_specs=[pl.BlockSpec((None, POOL_ROWS, GROUP), lambda b, j: (b, j, 0)),
                  _const_spec((GROUP, GROUP)), _const_spec((1, GROUP))],
        out_specs=pl.BlockSpec((None, POOL_ROWS, GROUP), lambda b, j: (b, j, 0)),
        out_shape=jax.ShapeDtypeStruct((batch, seq, GROUP), BF16),
        scratch_shapes=[pltpu.VMEM((POOL_ROWS + POOL_HALO, GROUP), F32),
                        pltpu.VMEM((POOL_HALO, GROUP), F32)],
        compiler_params=_params("parallel", "arbitrary"),
        name="pool",
    )(zp, w_bd, scale)
    return y.reshape(batch * seq, GROUP)


def _toeplitz_body(w_ref, o_ref):
    x = jnp.broadcast_to(w_ref[...], (TILE, 2 * TILE))
    o_ref[...] = pltpu.roll(x, 0, 1, stride=1, stride_axis=0)[:, :TILE]


def _toeplitz(rows):
    n = rows.shape[0]
    return pl.pallas_call(
        _toeplitz_body,
        grid=(n,),
        in_specs=[pl.BlockSpec((None, 1, 2 * TILE), lambda i: (i, 0, 0))],
        out_specs=pl.BlockSpec((None, TILE, TILE), lambda i: (i, 0, 0)),
        out_shape=jax.ShapeDtypeStruct((n, TILE, TILE), F32),
        compiler_params=_params("parallel"),
        name="toeplitz",
    )(rows.reshape(n, 1, 2 * TILE))


_TOEPLITZ_X = np.where(np.arange(2 * TILE) <= TILE, -np.arange(2 * TILE), 2 * TILE - np.arange(2 * TILE))


def _t5_bucket(dist):
    max_exact = T5_BUCKETS // 2
    d = jnp.maximum(dist, 1).astype(F32)
    large = max_exact + (jnp.log(d / max_exact) / math.log(T5_MAX_DIST / max_exact)
                         * (T5_BUCKETS - max_exact)).astype(jnp.int32)
    large = jnp.minimum(large, T5_BUCKETS - 1)
    return jnp.where(dist < max_exact, dist, large)


def _bias_tiles(t5_bias, seq):
    x = jnp.asarray(_TOEPLITZ_X, jnp.int32)
    rows = []
    for (w, d) in DIL_PATTERNS:
        tab = t5_bias[_t5_bucket(jnp.arange(DIL_BACK + 1) * d), :HEADS].T * LOG2E
        same = jnp.where(x <= 0, tab[:, jnp.clip(-x, 0, DIL_BACK)], NEG)
        nxt = jnp.where(x >= 0, tab[:, jnp.clip(DIL_BACK - x, 0, DIL_BACK)], NEG)
        rows.append(jnp.stack([same, nxt], axis=1))
    dil = _toeplitz(jnp.stack(rows).reshape(-1, 2 * TILE))
    dil = dil.reshape(len(DIL_PATTERNS), HEADS, 2, TILE, TILE).transpose(0, 1, 3, 2, 4)
    dil = dil.reshape(len(DIL_PATTERNS), HEADS * TILE, 2 * TILE)
    tab = t5_bias[_t5_bucket(jnp.arange(seq)), HEADS:].T * LOG2E
    noff = seq // TILE - DIFF_MIN_OFFSET
    dist = (jnp.arange(noff)[:, None] + DIFF_MIN_OFFSET) * TILE - x[None, :]
    rows = jnp.where(dist >= 0, tab[:, jnp.clip(dist, 0, seq - 1)], NEG)
    diff = _toeplitz(rows.transpose(1, 0, 2).reshape(-1, 2 * TILE))
    return dil, diff.reshape(noff, HEADS * TILE, TILE)


def _dil_body(*refs, subs, classes, merge):
    for rc in range(classes):
        _dil_class(*refs, rc=rc, subs=subs, merge=merge)


def _dil_class(*refs, rc, subs, merge):
    if merge:
        x_ref, xp_ref, bias_ref, o1_ref, l1_ref, o2_ref, l2_ref, y_ref = refs
    else:
        x_ref, xp_ref, bias_ref, o_ref, lse_ref = refs
    first = pl.program_id(2) == 0
    kmask = _group_mask(HEADS * TILE, GROUP, TILE, HEAD_DIM)
    ones_rows = (lax.broadcasted_iota(jnp.int32, (AUG_ROWS - HEAD_DIM, TILE), 0) == 0).astype(BF16)
    rows_of = lambda sb: slice(sb * TILE, (sb + 1) * TILE)
    col_q, col_k, col_v = (slice((3 * rc + w) * GROUP, (3 * rc + w + 1) * GROUP) for w in range(3))
    out_cols = slice(rc * GROUP, (rc + 1) * GROUP)

    s_same, s_next, vaug = {}, {}, {}
    for j in range(-1, subs):
        k_j = xp_ref[:, col_k] if j < 0 else x_ref[rows_of(j), col_k]
        v_j = xp_ref[:, col_v] if j < 0 else x_ref[rows_of(j), col_v]
        parts = ([0] if j >= 0 else []) + ([1] if j + 1 < subs else [])
        q_cat = jnp.concatenate([x_ref[rows_of(j + e), col_q] for e in parts], axis=0)
        bias = bias_ref[:, parts[0] * TILE:(parts[-1] + 1) * TILE]
        st = _dot_nt(_tile_rows(k_j, HEADS, kmask), q_cat) + bias
        if j < 0:
            st = st + jnp.where(first, NEG, 0.0)
        for pos, e in enumerate(parts):
            (s_same if e == 0 else s_next)[j + e] = st[:, pos * TILE:(pos + 1) * TILE]
        v_t = v_j.astype(F32).T.astype(BF16)
        vaug[j] = [jnp.concatenate([v_t[h * HEAD_DIM:(h + 1) * HEAD_DIM], ones_rows], axis=0)
                   for h in range(HEADS)]

    p_same, p_next, tops = {}, {}, {}
    for i in range(subs):
        ps, pn, tp = [], [], []
        for h in range(HEADS):
            hs = slice(h * TILE, (h + 1) * TILE)
            a, b = s_same[i][hs], s_next[i][hs]
            m = jnp.maximum(jnp.max(a, axis=0, keepdims=True), jnp.max(b, axis=0, keepdims=True))
            ps.append(jnp.exp2(a - m).astype(BF16))
            pn.append(jnp.exp2(b - m).astype(BF16))
            tp.append(m)
        p_same[i], p_next[i], tops[i] = ps, pn, tp

    acc = {i: [None] * HEADS for i in range(subs)}
    for j in range(-1, subs):
        for h in range(HEADS):
            cols = ([p_same[j][h]] if j >= 0 else []) + ([p_next[j + 1][h]] if j + 1 < subs else [])
            r = _dot(vaug[j][h], jnp.concatenate(cols, axis=1))
            targets = ([j] if j >= 0 else []) + ([j + 1] if j + 1 < subs else [])
            for pos, i in enumerate(targets):
                part = r[:, pos * TILE:(pos + 1) * TILE]
                acc[i][h] = part if acc[i][h] is None else acc[i][h] + part

    for i in range(subs):
        o_t, lse_t = [], []
        for h in range(HEADS):
            l = acc[i][h][HEAD_DIM:HEAD_DIM + 1]
            o_t.append(acc[i][h][:HEAD_DIM] / l)
            lse_t.append(jnp.broadcast_to(tops[i][h] + jnp.log2(l), (HEAD_DIM, TILE)))
        o = jnp.concatenate(o_t, axis=0).T
        lse = jnp.concatenate(lse_t, axis=0).T
        rs = rows_of(i)
        if merge:
            o1, l1 = o1_ref[rs, out_cols].astype(F32), l1_ref[rs, out_cols]
            o2, l2 = o2_ref[rs, out_cols].astype(F32), l2_ref[rs, out_cols]
            top = jnp.maximum(jnp.maximum(l1, l2), lse)
            w1, w2, w3 = jnp.exp2(l1 - top), jnp.exp2(l2 - top), jnp.exp2(lse - top)
            y_ref[rs, out_cols] = ((w1 * o1 + w2 * o2 + w3 * o) / (w1 + w2 + w3)).astype(BF16)
        else:
            o_ref[rs, out_cols] = o.astype(BF16)
            lse_ref[rs, out_cols] = lse


DIL_SUBBLOCKS = 8


def _dilated_pattern(zc, bias, dil, *, batch, seq, merge_with=None):
    length = seq // dil
    rows = min(length, DIL_SUBBLOCKS * TILE)
    subs = rows // TILE
    classes = min(dil, DIL_SUBBLOCKS // subs)
    zc = zc.reshape(batch, length, dil * 3 * GROUP)
    blk = pl.BlockSpec((None, rows, classes * 3 * GROUP), lambda b, r, n: (b, n, r))
    prev = pl.BlockSpec((None, TILE, classes * 3 * GROUP),
                        lambda b, r, n: (b, jnp.maximum(n * subs - 1, 0), r))
    nat = pl.BlockSpec((None, rows, classes * GROUP), lambda b, r, n: (b, n, r))
    in_specs = [blk, prev, _const_spec((HEADS * TILE, 2 * TILE))]
    args = [zc, zc, bias]
    if merge_with is None:
        out_specs = [nat, nat]
        out_shape = [jax.ShapeDtypeStruct((batch, length, dil * GROUP), BF16),
                     jax.ShapeDtypeStruct((batch, length, dil * GROUP), F32)]
    else:
        in_specs += [nat] * len(merge_with)
        args += [t.reshape(batch, length, dil * GROUP) for t in merge_with]
        out_specs = nat
        out_shape = jax.ShapeDtypeStruct((batch, length, dil * GROUP), BF16)
    out = pl.pallas_call(
        functools.partial(_dil_body, subs=subs, classes=classes, merge=merge_with is not None),
        grid=(batch, dil // classes, length // rows),
        in_specs=in_specs, out_specs=out_specs, out_shape=out_shape,
        compiler_params=_params("parallel", "parallel", "parallel"),
        name=f"dilated_d{dil}",
    )(*args)
    if merge_with is None:
        return out
    return out.reshape(batch * seq, GROUP)


def _dilated(zc, dil_bias, *, batch, seq):
    o1, l1 = _dilated_pattern(zc, dil_bias[0], DIL_PATTERNS[0][1], batch=batch, seq=seq)
    o2, l2 = _dilated_pattern(zc, dil_bias[1], DIL_PATTERNS[1][1], batch=batch, seq=seq)
    return _dilated_pattern(zc, dil_bias[2], DIL_PATTERNS[2][1], batch=batch, seq=seq,
                            merge_with=(o1, l1, o2, l2))


DIFF_Q = 256
DIFF_K = 256
DIFF_GROUPS = 2 * HEADS


def _diff_body(q_ref, k_ref, vt_ref, bias_ref, lam_ref, sg_ref, y_ref,
               kexp_ref, vaug_ref, acc_ref, sta_ref, stb_ref, *, lam_init, key_steps):
    qi = pl.program_id(1)

    @pl.when(qi == 0)
    def _():
        grp = lax.broadcasted_iota(jnp.int32, (DIFF_GROUPS * DIFF_K, GROUP), 0) // DIFF_K
        slot = lax.broadcasted_iota(jnp.int32, (DIFF_GROUPS * DIFF_K, GROUP), 1) // DIFF_QK_HALF
        kmask = slot == 2 * (grp % HEADS) + grp // HEADS
        ones_rows = (lax.broadcasted_iota(jnp.int32, (AUG_ROWS - HEAD_DIM, DIFF_K), 0) == 0).astype(BF16)

        def build(j, carry):
            k_t = k_ref[pl.ds(pl.multiple_of(j * DIFF_K, DIFF_K), DIFF_K), :]
            kexp_ref[j] = jnp.where(kmask, jnp.concatenate([k_t] * DIFF_GROUPS, axis=0),
                                    jnp.zeros((), BF16))
            vt = jnp.concatenate([vt_ref[2 * j], vt_ref[2 * j + 1]], axis=1)
            for h in range(HEADS):
                vaug_ref[j, h] = jnp.concatenate([vt[h * HEAD_DIM:(h + 1) * HEAD_DIM], ones_rows], axis=0)
            return carry

        lax.fori_loop(0, key_steps, build, 0)

    acc_ref[...] = jnp.zeros_like(acc_ref)
    q = q_ref[...]
    last = key_steps - 1

    def scores(s_ref, j):
        off = jnp.maximum(2 * (qi - j) + 2, 0)
        b_m, b_0, b_p = bias_ref[off], bias_ref[off + 1], bias_ref[off + 2]
        raw = _dot_nt(kexp_ref[jnp.minimum(j, last)], q)
        tops = []
        for g in range(DIFF_GROUPS):
            hs = slice((g % HEADS) * TILE, (g % HEADS + 1) * TILE)
            bias = jnp.concatenate([jnp.concatenate([b_0[hs], b_p[hs]], axis=1),
                                    jnp.concatenate([b_m[hs], b_0[hs]], axis=1)], axis=0)
            s = raw[g * DIFF_K:(g + 1) * DIFF_K] + bias
            s_ref[g * DIFF_K:(g + 1) * DIFF_K, :] = s
            tops.append(jnp.max(s, axis=0, keepdims=True))
        return tuple(tops)

    def consume(s_ref, tops, j, carry):
        ms, ls = carry
        jv = jnp.minimum(j, last)
        new_ms, new_ls = [], []
        for g in range(DIFF_GROUPS):
            mp, h = divmod(g, HEADS)
            m_new = jnp.maximum(ms[g], tops[g])
            p = jnp.exp2(s_ref[g * DIFF_K:(g + 1) * DIFF_K, :] - m_new).astype(BF16)
            alpha = jnp.exp2(ms[g] - m_new)
            r = _dot(vaug_ref[jv, h], p)
            acc_ref[mp, h] = alpha * acc_ref[mp, h] + r[:HEAD_DIM]
            new_ls.append(alpha * ls[g] + r[HEAD_DIM:HEAD_DIM + 1])
            new_ms.append(m_new)
        return tuple(new_ms), tuple(new_ls)

    def pair(jj, carry):
        tops_a, state = carry
        j = 2 * jj
        tops_b = scores(stb_ref, j + 1)
        state = consume(sta_ref, tops_a, j, state)
        tops_a = scores(sta_ref, j + 2)
        return tops_a, consume(stb_ref, tops_b, j + 1, state)

    init = (tuple(jnp.full((1, DIFF_Q), NEG, F32) for _ in range(DIFF_GROUPS)),
            tuple(jnp.zeros((1, DIFF_Q), F32) for _ in range(DIFF_GROUPS)))
    _, (_, ls) = lax.fori_loop(0, (qi + 2) // 2, pair, (scores(sta_ref, 0), init))

    lv = lam_ref[...]
    lam = (jnp.exp(jnp.sum(lv[0:1] * lv[1:2], axis=-1, keepdims=True))
           - jnp.exp(jnp.sum(lv[2:3] * lv[3:4], axis=-1, keepdims=True)) + lam_init)
    outs = []
    for h in range(HEADS):
        o = acc_ref[0, h] / ls[h] - lam * (acc_ref[1, h] / ls[HEADS + h])
        ms_o = jnp.mean(o * o, axis=0, keepdims=True)
        outs.append(o * lax.rsqrt(ms_o + SUBLN_EPS) * sg_ref[...] * (1.0 - lam_init))
    y_ref[...] = jnp.concatenate(outs, axis=0).T.astype(BF16)


def _diff_attention(zd, vt, bias, lam_vecs, subln_cols, *, lam_init, batch, seq):
    zd = zd.reshape(batch, seq, 2 * GROUP)
    key_tiles = seq // TILE
    key_steps = seq // DIFF_K
    vt = vt.reshape(batch, key_tiles, GROUP, TILE)
    y = pl.pallas_call(
        functools.partial(_diff_body, lam_init=lam_init, key_steps=key_steps),
        grid=(batch, seq // DIFF_Q),
        in_specs=[pl.BlockSpec((None, DIFF_Q, GROUP), lambda b, i: (b, i, 0)),
                  pl.BlockSpec((None, seq, GROUP), lambda b, i: (b, 0, 1), pipeline_mode=pl.Buffered(1)),
                  pl.BlockSpec((None, key_tiles, GROUP, TILE), lambda b, i: (b, 0, 0, 0),
                               pipeline_mode=pl.Buffered(1)),
                  _const_spec((key_tiles - DIFF_MIN_OFFSET, HEADS * TILE, TILE)),
                  _const_spec((4, DIFF_QK_HALF)), _const_spec((HEAD_DIM, DIFF_Q))],
        out_specs=pl.BlockSpec((None, DIFF_Q, GROUP), lambda b, i: (b, i, 0)),
        out_shape=jax.ShapeDtypeStruct((batch, seq, GROUP), BF16),
        scratch_shapes=[pltpu.VMEM((key_steps, DIFF_GROUPS * DIFF_K, GROUP), BF16),
                        pltpu.VMEM((key_steps, HEADS, AUG_ROWS, DIFF_K), BF16),
                        pltpu.VMEM((2, HEADS, HEAD_DIM, DIFF_Q), F32),
                        pltpu.VMEM((DIFF_GROUPS * DIFF_K, DIFF_Q), F32),
                        pltpu.VMEM((DIFF_GROUPS * DIFF_K, DIFF_Q), F32)],
        compiler_params=_params("parallel", "arbitrary"),
        name="diff_attn",
    )(zd, zd, vt, bias, lam_vecs, subln_cols)
    return y.reshape(batch * seq, GROUP)


KV_ROWS = 512


def _mem_kv_body(m_ref, g_ref, w_ref, k_ref, v_ref):
    u = _rms(m_ref[...], g_ref[...]).astype(BF16)
    for c in range(D_MODEL // GROUP):
        sl = slice(c * GROUP, (c + 1) * GROUP)
        k_ref[:, sl] = _dot(u, w_ref[:, sl]).astype(BF16)
        v_ref[:, sl] = _dot(u, w_ref[:, D_MODEL + c * GROUP: D_MODEL + (c + 1) * GROUP]).astype(BF16)


def _mem_kv(mem, g, w):
    n = mem.shape[0]
    row = pl.BlockSpec((KV_ROWS, D_MODEL), lambda i: (i, 0))
    return pl.pallas_call(
        _mem_kv_body,
        grid=(n // KV_ROWS,),
        in_specs=[row, _const_spec((1, D_MODEL)), _const_spec((D_MODEL, 2 * D_MODEL))],
        out_specs=[row, row],
        out_shape=[jax.ShapeDtypeStruct((n, D_MODEL), BF16)] * 2,
        compiler_params=_params("parallel"),
        name="mem_kv",
    )(mem, g, w)


XATTN_ROWS = 512


def _xattn_body(x_ref, ya_ref, yb_ref, yc_ref, yd_ref, wout_ref, g_ref, wq_ref, k_ref, v_ref, wo_ref,
                o_ref, q_scr, a_scr):
    x = x_ref[...]
    for gi, y_ref in enumerate((ya_ref, yb_ref, yc_ref, yd_ref)):
        x = x + _dot(y_ref[...], wout_ref[gi * GROUP:(gi + 1) * GROUP, :])
    u = _rms(x, g_ref[...]).astype(BF16)
    for c in range(D_MODEL // GROUP):
        sl = slice(c * GROUP, (c + 1) * GROUP)
        q_scr[:, sl] = _dot(u, wq_ref[:, sl]).astype(BF16)
    for h in range(MEM_HEADS):
        sl = slice(h * MEM_HEAD_DIM, (h + 1) * MEM_HEAD_DIM)
        s = _dot_nt(q_scr[:, sl], k_ref[:, sl]) * (MEM_HEAD_DIM ** -0.5)
        e = jnp.exp(s - jnp.max(s, axis=-1, keepdims=True))
        l = jnp.sum(e, axis=-1, keepdims=True)
        a_scr[:, sl] = (_dot(e.astype(BF16), v_ref[:, sl]) / l).astype(BF16)
    o_ref[...] = x + _dot(a_scr[...], wo_ref[...])


def _xattn(h, ys, w_out, g, wq, k, v, wo, *, batch, seq):
    h3 = h.reshape(batch, seq, D_MODEL)
    ys = [y.reshape(batch, seq, GROUP) for y in ys]
    k3 = k.reshape(batch, MEM_LEN, D_MODEL)
    v3 = v.reshape(batch, MEM_LEN, D_MODEL)
    row = pl.BlockSpec((None, XATTN_ROWS, D_MODEL), lambda b, i: (b, i, 0))
    grp = pl.BlockSpec((None, XATTN_ROWS, GROUP), lambda b, i: (b, i, 0))
    mem = pl.BlockSpec((None, MEM_LEN, D_MODEL), lambda b, i: (b, 0, 0))
    weight = _const_spec((D_MODEL, D_MODEL))
    out = pl.pallas_call(
        _xattn_body,
        grid=(batch, seq // XATTN_ROWS),
        in_specs=[row, grp, grp, grp, grp, weight, _const_spec((1, D_MODEL)), weight, mem, mem, weight],
        out_specs=row,
        out_shape=jax.ShapeDtypeStruct((batch, seq, D_MODEL), F32),
        scratch_shapes=[pltpu.VMEM((XATTN_ROWS, D_MODEL), BF16), pltpu.VMEM((XATTN_ROWS, D_MODEL), BF16)],
        compiler_params=_params("parallel", "parallel"),
        name="xattn",
    )(h3, *ys, w_out, g, wq, k3, v3, wo)
    return out.reshape(batch * seq, D_MODEL)


def _per_head_lanes(x):
    return jnp.repeat(x, HEAD_DIM, axis=-1)


def _in_proj_weight(w_in):
    g = GROUP
    q_a, k_a, v_a, o_a = (w_in[:, i * g:(i + 1) * g] for i in range(4))
    ig = w_in[:, 4 * g:4 * g + HEADS]
    fg = w_in[:, 4 * g + HEADS:4 * g + 2 * HEADS]
    rest = w_in[:, 4 * g + 2 * HEADS:]
    pool, q_c, k_c, v_c, q_d, k_d, v_d = (rest[:, i * g:(i + 1) * g] for i in range(7))
    q_c = q_c * DIL_SCORE_SCALE
    k_d = k_d * DIFF_SCORE_SCALE
    cols = [q_a, k_a, v_a, o_a, _per_head_lanes(ig), _per_head_lanes(fg), pool,
            q_c, k_c, v_c, q_d, k_d]
    return jnp.concatenate(cols, axis=1).astype(BF16), v_d.T.astype(BF16)


def _block_diag(w):
    g, c, _ = w.shape
    eye = jnp.eye(g, dtype=w.dtype)
    return (eye[:, None, :, None] * w[:, :, None, :]).reshape(g * c, g * c)


def kernel(x, mem, t5_bias, ffn1_norm, ffn1_w_gate, ffn1_w_up, ffn1_w_down, mix_norm, w_in,
           mlstm_conv_w, mlstm_conv_b, mlstm_gate_b, mlstm_norm, pool_w, pool_scale,
           diff_lambda, diff_subln, w_out, xattn_norm, mem_norm, xattn_wq, xattn_wkv, xattn_wo,
           ffn2_norm, ffn2_w_gate, ffn2_w_up, ffn2_w_down, final_norm):
    batch, seq, _ = x.shape
    n = batch * seq
    dil_bias, diff_bias = _bias_tiles(t5_bias, seq)
    h = x.reshape(n, D_MODEL)
    mem2 = mem.reshape(batch * MEM_LEN, D_MODEL)
    row = lambda v: v.reshape(1, -1)
    for l in range(DEPTH):
        lam_init = 0.8 - 0.6 * math.exp(-0.3 * l)
        h = _ffn(h, row(ffn1_norm[l]), ffn1_w_gate[l].astype(BF16), ffn1_w_up[l].astype(BF16),
                 ffn1_w_down[l].astype(BF16), row(final_norm), final=False)
        za, zg, zp, zc, zd, vt = _in_proj(h, row(mix_norm[l]), *_in_proj_weight(w_in[l]))
        ya = _mlstm(za, zg, mlstm_conv_w[l], row(mlstm_conv_b[l]),
                    row(_per_head_lanes(mlstm_gate_b[l].reshape(2, HEADS))), row(mlstm_norm[l]),
                    batch=batch, seq=seq)
        yb = _pool(zp, _block_diag(pool_w[l]).astype(BF16), row(pool_scale[l]), batch=batch, seq=seq)
        yc = _dilated(zc, dil_bias, batch=batch, seq=seq)
        yd = _diff_attention(zd, vt, diff_bias, diff_lambda[l],
                             jnp.broadcast_to(diff_subln[l][:, None], (HEAD_DIM, DIFF_Q)),
                             lam_init=lam_init, batch=batch, seq=seq)
        k_mem, v_mem = _mem_kv(mem2, row(mem_norm[l]), xattn_wkv[l].astype(BF16))
        h = _xattn(h, (ya, yb, yc, yd), w_out[l].astype(BF16),
                   row(xattn_norm[l]), xattn_wq[l].astype(BF16), k_mem, v_mem,
                   xattn_wo[l].astype(BF16), batch=batch, seq=seq)
        h = _ffn(h, row(ffn2_norm[l]), ffn2_w_gate[l].astype(BF16), ffn2_w_up[l].astype(BF16),
                 ffn2_w_down[l].astype(BF16), row(final_norm), final=(l == DEPTH - 1))
    return h.reshape(batch, seq, D_MODEL)
```

```python
import functools
import math

import jax
import jax.numpy as jnp
import numpy as np
from jax import lax
from jax.experimental import pallas as pl
from jax.experimental.pallas import tpu as pltpu

F32 = jnp.float32
BF16 = jnp.bfloat16

D_MODEL = 1024
D_FF = 2816
DEPTH = 4
GROUP = 256
HEADS = 4
HEAD_DIM = GROUP // HEADS
MEM_LEN = 256
MEM_HEADS = 4
MEM_HEAD_DIM = D_MODEL // MEM_HEADS
MLSTM_CHUNK = 64
CONV_WIDTH = 4
POOL_WINDOWS = (2, 4, 8, 16)
DIL_PATTERNS = ((128, 1), (512, 4), (2048, 16))
DIL_BACK = 128
DIFF_QK_HALF = HEAD_DIM // 2
T5_BUCKETS = 32
T5_MAX_DIST = 2048
RMS_EPS = 1e-6
SUBLN_EPS = 1e-5
NEG = -1e30
LOG2E = math.log2(math.e)
DIFF_SCORE_SCALE = (DIFF_QK_HALF ** -0.5) * LOG2E
DIL_SCORE_SCALE = (HEAD_DIM ** -0.5) * LOG2E
AUG_ROWS = HEAD_DIM + 16
DIFF_MIN_OFFSET = -3
TILE = 128

VMEM_LIMIT_BYTES = 56 * 1024 * 1024


def _rms(xf, g, eps=RMS_EPS):
    return xf * lax.rsqrt(jnp.mean(xf * xf, axis=-1, keepdims=True) + eps) * g


def _const_spec(shape):
    zeros = (0,) * len(shape)
    return pl.BlockSpec(shape, lambda *_: zeros, pipeline_mode=pl.Buffered(1))


def _params(*sem):
    return pltpu.CompilerParams(dimension_semantics=sem, vmem_limit_bytes=VMEM_LIMIT_BYTES)


def _group_mask(rows, cols, row_group, col_group):
    r = lax.broadcasted_iota(jnp.int32, (rows, cols), 0) // row_group
    c = lax.broadcasted_iota(jnp.int32, (rows, cols), 1) // col_group
    return r == c


def _tile_rows(x, reps, mask):
    return jnp.where(mask, jnp.concatenate([x] * reps, axis=0), jnp.zeros((), x.dtype))


def _dot(a, b):
    return jnp.dot(a, b, preferred_element_type=F32)


def _dot_nt(a, b):
    return lax.dot_general(a, b, (((1,), (1,)), ((), ())), preferred_element_type=F32)


def _dot_tn(a, b):
    return lax.dot_general(a, b, (((0,), (0,)), ((), ())), preferred_element_type=F32)


def _group_sum(x, ones_bd):
    hi = x.astype(BF16)
    lo = (x - hi.astype(F32)).astype(BF16)
    return _dot(hi, ones_bd) + _dot(lo, ones_bd)


FFN_ROWS = 512
FFN_COLS = 256


def _ffn_body(x_ref, g_ref, wg_ref, wu_ref, wd_ref, fg_ref, o_ref, act_ref, *, final):
    x = x_ref[...]
    u = _rms(x, g_ref[...]).astype(BF16)
    for c in range(D_FF // FFN_COLS):
        sl = slice(c * FFN_COLS, (c + 1) * FFN_COLS)
        gate = _dot(u, wg_ref[:, sl])
        up = _dot(u, wu_ref[:, sl])
        act_ref[:, sl] = (gate * jax.nn.sigmoid(gate) * up).astype(BF16)
    y = x + 0.5 * _dot(act_ref[...], wd_ref[...])
    if final:
        y = _rms(y, fg_ref[...])
    o_ref[...] = y


def _ffn(h, g, wg, wu, wd, fg, *, final):
    n = h.shape[0]
    row = pl.BlockSpec((FFN_ROWS, D_MODEL), lambda i: (i, 0))
    return pl.pallas_call(
        functools.partial(_ffn_body, final=final),
        grid=(n // FFN_ROWS,),
        in_specs=[row, _const_spec((1, D_MODEL)), _const_spec((D_MODEL, D_FF)),
                  _const_spec((D_MODEL, D_FF)), _const_spec((D_FF, D_MODEL)),
                  _const_spec((1, D_MODEL))],
        out_specs=row,
        out_shape=jax.ShapeDtypeStruct((n, D_MODEL), F32),
        scratch_shapes=[pltpu.VMEM((FFN_ROWS, D_FF), BF16)],
        compiler_params=_params("parallel"),
        name="ffn_final" if final else "ffn",
    )(h, g, wg, wu, wd, fg)


PROJ_ROWS = 512
PROJ_OUTS = (("a", 4 * GROUP, F32), ("g", 2 * GROUP, F32), ("p", GROUP, F32),
             ("c", 3 * GROUP, BF16), ("d", 2 * GROUP, BF16))
PROJ_WIDTH = sum(w for _, w, _ in PROJ_OUTS)


PROJ_DILATIONS = tuple(d for _, d in DIL_PATTERNS if d > 1)


def _in_proj_body(x_ref, g_ref, w_ref, wvt_ref, perm_ref, *o_refs):
    u = _rms(x_ref[...], g_ref[...]).astype(BF16)
    off = 0
    for o_ref, (_, width, dtype) in zip(o_refs, PROJ_OUTS):
        for c in range(width // GROUP):
            z = _dot(u, w_ref[:, off + c * GROUP: off + (c + 1) * GROUP])
            o_ref[:, c * GROUP:(c + 1) * GROUP] = z.astype(dtype)
        off += width
    vt_ref = o_refs[len(PROJ_OUTS)]
    for t in range(PROJ_ROWS // TILE):
        vt_ref[t] = _dot_nt(wvt_ref[...], u[t * TILE:(t + 1) * TILE]).astype(BF16)
    zc = o_refs[3][...]
    for pi, d in enumerate(PROJ_DILATIONS):
        per_class = PROJ_ROWS // d
        zp = _dot(perm_ref[pi], zc).astype(BF16)
        for r in range(d):
            o_refs[len(PROJ_OUTS) + 1 + pi][:, r * 3 * GROUP:(r + 1) * 3 * GROUP] = \
                zp[r * per_class:(r + 1) * per_class]


def _class_permutations():
    mats = np.zeros((len(PROJ_DILATIONS), PROJ_ROWS, PROJ_ROWS), np.float32)
    for pi, d in enumerate(PROJ_DILATIONS):
        t = np.arange(PROJ_ROWS)
        mats[pi, (t % d) * (PROJ_ROWS // d) + t // d, t] = 1.0
    return jnp.asarray(mats, BF16)


def _in_proj(h, g, w, wvt):
    n = h.shape[0]
    tiles = PROJ_ROWS // TILE
    return pl.pallas_call(
        _in_proj_body,
        grid=(n // PROJ_ROWS,),
        in_specs=[pl.BlockSpec((PROJ_ROWS, D_MODEL), lambda i: (i, 0)),
                  _const_spec((1, D_MODEL)), _const_spec((D_MODEL, PROJ_WIDTH)),
                  _const_spec((GROUP, D_MODEL)),
                  _const_spec((len(PROJ_DILATIONS), PROJ_ROWS, PROJ_ROWS))],
        out_specs=[pl.BlockSpec((PROJ_ROWS, w_), lambda i: (i, 0)) for _, w_, _ in PROJ_OUTS]
        + [pl.BlockSpec((tiles, GROUP, TILE), lambda i: (i, 0, 0))]
        + [pl.BlockSpec((PROJ_ROWS // d, d * 3 * GROUP), lambda i: (i, 0)) for d in PROJ_DILATIONS],
        out_shape=[jax.ShapeDtypeStruct((n, w_), dt) for _, w_, dt in PROJ_OUTS]
        + [jax.ShapeDtypeStruct((n // TILE, GROUP, TILE), BF16)]
        + [jax.ShapeDtypeStruct((n // d, d * 3 * GROUP), BF16) for d in PROJ_DILATIONS],
        compiler_params=_params("parallel"),
        name="in_proj",
    )(h, g, w, wvt, _class_permutations())


ML_ROWS = 512
ML_HALO = 8


def _chunk_scan(x, rin, op, fill):
    s = 1
    while s < MLSTM_CHUNK:
        x = op(x, jnp.where(rin >= s, pltpu.roll(x, s, 0), fill))
        s *= 2
    return x


def _mlstm_body(za_ref, zg_ref, cw_ref, cb_ref, gb_ref, ng_ref, y_ref,
                buf_ref, tail_ref, hh_ref, c_ref, n_ref, m_ref):
    L = MLSTM_CHUNK

    @pl.when(pl.program_id(1) == 0)
    def _():
        tail_ref[...] = jnp.zeros_like(tail_ref)
        c_ref[...] = jnp.zeros_like(c_ref)
        n_ref[...] = jnp.zeros_like(n_ref)
        m_ref[...] = jnp.zeros_like(m_ref)

    buf_ref[0:ML_HALO, :] = tail_ref[...]
    buf_ref[ML_HALO:, :] = za_ref[:, 0:2 * GROUP]
    tail_ref[...] = za_ref[ML_ROWS - ML_HALO:, 0:2 * GROUP]
    conv = cb_ref[...]
    for j in range(CONV_WIDTH):
        conv = conv + buf_ref[pl.ds(ML_HALO - (CONV_WIDTH - 1) + j, ML_ROWS), :] * cw_ref[j:j + 1, :]
    qk = conv * jax.nn.sigmoid(conv)
    q = qk[:, :GROUP]
    k = qk[:, GROUP:] * (HEAD_DIM ** -0.5)
    v = za_ref[:, 2 * GROUP:3 * GROUP].astype(BF16)
    q_bf, k_bf = q.astype(BF16), k.astype(BF16)

    ii = zg_ref[:, :GROUP] + gb_ref[:, :GROUP]
    fx = zg_ref[:, GROUP:] + gb_ref[:, GROUP:]
    lf = jnp.minimum(fx, 0.0) - jnp.log1p(jnp.exp(-jnp.abs(fx)))
    rin = lax.broadcasted_iota(jnp.int32, (ML_ROWS, GROUP), 0) % L
    b = _chunk_scan(lf, rin, jnp.add, 0.0)
    a = ii - b
    ca = _chunk_scan(a, rin, jnp.maximum, NEG)

    bd = _group_mask(GROUP, GROUP, HEAD_DIM, HEAD_DIM)
    ones_bd = bd.astype(BF16)
    row = lax.broadcasted_iota(jnp.int32, (L, GROUP), 0)
    key = lax.broadcasted_iota(jnp.int32, (L, GROUP), 1) % L
    causal = key <= row
    diag = key == row

    m_prev = m_ref[...]
    for c in range(ML_ROWS // L):
        rs = slice(c * L, (c + 1) * L)
        q_c, k_c, v_c = q[rs], k[rs], v[rs]
        q_b = q_bf[rs]
        a_c, b_c = a[rs], b[rs]
        g = jnp.maximum(m_prev, ca[rs])
        g_last = g[L - 1:L]
        a_row = jnp.sum(jnp.where(diag, a_c, 0.0), axis=0, keepdims=True)
        decay = jnp.exp(jnp.where(causal, a_row - g, NEG))
        sc = _dot_nt(q_b, _tile_rows(k_bf[rs], HEADS, bd)) * decay
        inter = jnp.exp(m_prev - g)
        num = inter * _dot(q_b, c_ref[...].astype(BF16)) + _dot(sc.astype(BF16), _tile_rows(v_c, HEADS, bd))
        den = inter * _group_sum(q_c * n_ref[...], ones_bd) + _group_sum(sc, ones_bd)
        hh_ref[rs, :] = num / jnp.maximum(jnp.abs(den), jnp.exp(-(b_c + g)))
        kw = k_c * jnp.exp(a_c - g_last)
        carry = jnp.exp(m_prev - g_last)
        c_ref[...] = carry * c_ref[...] + jnp.where(bd, _dot_tn(kw.astype(BF16), v_c), 0.0)
        n_ref[...] = carry * n_ref[...] + jnp.sum(kw, axis=0, keepdims=True)
        m_prev = b_c[L - 1:L] + g_last
    m_ref[...] = m_prev

    hh = hh_ref[...]
    mu = _group_sum(hh, ones_bd) * (1.0 / HEAD_DIM)
    dev = hh - mu
    var = _group_sum(dev * dev, ones_bd) * (1.0 / HEAD_DIM)
    o_gate = jax.nn.sigmoid(za_ref[:, 3 * GROUP:])
    y_ref[...] = (dev * lax.rsqrt(var + RMS_EPS) * ng_ref[...] * o_gate).astype(BF16)


def _mlstm(za, zg, conv_w, conv_b, gate_b, norm_g, *, batch, seq):
    za = za.reshape(batch, seq, 4 * GROUP)
    zg = zg.reshape(batch, seq, 2 * GROUP)
    y = pl.pallas_call(
        _mlstm_body,
        grid=(batch, seq // ML_ROWS),
        in_specs=[pl.BlockSpec((None, ML_ROWS, 4 * GROUP), lambda b, j: (b, j, 0)),
                  pl.BlockSpec((None, ML_ROWS, 2 * GROUP), lambda b, j: (b, j, 0)),
                  _const_spec((CONV_WIDTH, 2 * GROUP)), _const_spec((1, 2 * GROUP)),
                  _const_spec((1, 2 * GROUP)), _const_spec((1, GROUP))],
        out_specs=pl.BlockSpec((None, ML_ROWS, GROUP), lambda b, j: (b, j, 0)),
        out_shape=jax.ShapeDtypeStruct((batch, seq, GROUP), BF16),
        scratch_shapes=[pltpu.VMEM((ML_ROWS + ML_HALO, 2 * GROUP), F32),
                        pltpu.VMEM((ML_HALO, 2 * GROUP), F32),
                        pltpu.VMEM((ML_ROWS, GROUP), F32),
                        pltpu.VMEM((GROUP, GROUP), F32),
                        pltpu.VMEM((1, GROUP), F32),
                        pltpu.VMEM((1, GROUP), F32)],
        compiler_params=_params("parallel", "arbitrary"),
        name="mlstm",
    )(za, zg, conv_w, conv_b, gate_b, norm_g)
    return y.reshape(batch * seq, GROUP)


POOL_ROWS = 512
POOL_HALO = 16


def _pool_body(u_ref, w_ref, s_ref, y_ref, buf_ref, tail_ref):
    j = pl.program_id(1)

    @pl.when(j == 0)
    def _():
        tail_ref[...] = jnp.zeros_like(tail_ref)

    buf_ref[0:POOL_HALO, :] = tail_ref[...]
    buf_ref[POOL_HALO:, :] = u_ref[...]
    tail_ref[...] = u_ref[POOL_ROWS - POOL_HALO:, :]
    sums, s = [], buf_ref[...]
    for shift in (1, 2, 4, 8):
        s = s + pltpu.roll(s, shift, 0)
        sums.append(s[POOL_HALO:])
    u = u_ref[...]
    lane_group = lax.broadcasted_iota(jnp.int32, (POOL_ROWS, GROUP), 1) // HEAD_DIM
    t = j * POOL_ROWS + lax.broadcasted_iota(jnp.int32, (POOL_ROWS, GROUP), 0)
    total, win = sums[3], jnp.full((POOL_ROWS, GROUP), POOL_WINDOWS[3], jnp.int32)
    for gi in (2, 1, 0):
        total = jnp.where(lane_group == gi, sums[gi], total)
        win = jnp.where(lane_group == gi, POOL_WINDOWS[gi], win)
    mean = total / jnp.minimum(t + 1, win).astype(F32)
    y = _dot((mean - u).astype(BF16), w_ref[...]) * s_ref[...]
    y_ref[...] = y.astype(BF16)


def _pool(zp, w_bd, scale, *, batch, seq):
    zp = zp.reshape(batch, seq, GROUP)
    y = pl.pallas_call(
        _pool_body,
        grid=(batch, seq // POOL_ROWS),
        in_specs=[pl.BlockSpec((None, POOL_ROWS, GROUP), lambda b, j: (b, j, 0)),
                  _const_spec((GROUP, GROUP)), _const_spec((1, GROUP))],
        out_specs=pl.BlockSpec((None, POOL_ROWS, GROUP), lambda b, j: (b, j, 0)),
        out_shape=jax.ShapeDtypeStruct((batch, seq, GROUP), BF16),
        scratch_shapes=[pltpu.VMEM((POOL_ROWS + POOL_HALO, GROUP), F32),
                        pltpu.VMEM((POOL_HALO, GROUP), F32)],
        compiler_params=_params("parallel", "arbitrary"),
        name="pool",
    )(zp, w_bd, scale)
    return y.reshape(batch * seq, GROUP)


def _toeplitz_body(w_ref, o_ref):
    x = jnp.broadcast_to(w_ref[...], (TILE, 2 * TILE))
    o_ref[...] = pltpu.roll(x, 0, 1, stride=1, stride_axis=0)[:, :TILE]


def _toeplitz(rows):
    n = rows.shape[0]
    return pl.pallas_call(
        _toeplitz_body,
        grid=(n,),
        in_specs=[pl.BlockSpec((None, 1, 2 * TILE), lambda i: (i, 0, 0))],
        out_specs=pl.BlockSpec((None, TILE, TILE), lambda i: (i, 0, 0)),
        out_shape=jax.ShapeDtypeStruct((n, TILE, TILE), F32),
        compiler_params=_params("parallel"),
        name="toeplitz",
    )(rows.reshape(n, 1, 2 * TILE))


_TOEPLITZ_X = np.where(np.arange(2 * TILE) <= TILE, -np.arange(2 * TILE), 2 * TILE - np.arange(2 * TILE))


def _t5_bucket(dist):
    max_exact = T5_BUCKETS // 2
    d = jnp.maximum(dist, 1).astype(F32)
    large = max_exact + (jnp.log(d / max_exact) / math.log(T5_MAX_DIST / max_exact)
                         * (T5_BUCKETS - max_exact)).astype(jnp.int32)
    large = jnp.minimum(large, T5_BUCKETS - 1)
    return jnp.where(dist < max_exact, dist, large)


def _bias_tiles(t5_bias, seq):
    x = jnp.asarray(_TOEPLITZ_X, jnp.int32)
    rows = []
    for (w, d) in DIL_PATTERNS:
        tab = t5_bias[_t5_bucket(jnp.arange(DIL_BACK + 1) * d), :HEADS].T * LOG2E
        same = jnp.where(x <= 0, tab[:, jnp.clip(-x, 0, DIL_BACK)], NEG)
        nxt = jnp.where(x >= 0, tab[:, jnp.clip(DIL_BACK - x, 0, DIL_BACK)], NEG)
        rows.append(jnp.stack([same, nxt], axis=1))
    dil = _toeplitz(jnp.stack(rows).reshape(-1, 2 * TILE))
    dil = dil.reshape(len(DIL_PATTERNS), HEADS, 2, TILE, TILE).transpose(0, 1, 3, 2, 4)
    dil = dil.reshape(len(DIL_PATTERNS), HEADS * TILE, 2 * TILE)
    tab = t5_bias[_t5_bucket(jnp.arange(seq)), HEADS:].T * LOG2E
    noff = seq // TILE - DIFF_MIN_OFFSET
    dist = (jnp.arange(noff)[:, None] + DIFF_MIN_OFFSET) * TILE - x[None, :]
    rows = jnp.where(dist >= 0, tab[:, jnp.clip(dist, 0, seq - 1)], NEG)
    diff = _toeplitz(rows.transpose(1, 0, 2).reshape(-1, 2 * TILE))
    return dil, diff.reshape(noff, HEADS * TILE, TILE)


def _dil_body(*refs, subs, classes, merge):
    for rc in range(classes):
        _dil_class(*refs, rc=rc, subs=subs, merge=merge)


def _dil_class(*refs, rc, subs, merge):
    if merge:
        x_ref, xp_ref, bias_ref, o1_ref, l1_ref, o2_ref, l2_ref, y_ref = refs
    else:
        x_ref, xp_ref, bias_ref, o_ref, lse_ref = refs
    first = pl.program_id(2) == 0
    kmask = _group_mask(HEADS * TILE, GROUP, TILE, HEAD_DIM)
    ones_rows = (lax.broadcasted_iota(jnp.int32, (AUG_ROWS - HEAD_DIM, TILE), 0) == 0).astype(BF16)
    rows_of = lambda sb: slice(sb * TILE, (sb + 1) * TILE)
    col_q, col_k, col_v = (slice((3 * rc + w) * GROUP, (3 * rc + w + 1) * GROUP) for w in range(3))
    out_cols = slice(rc * GROUP, (rc + 1) * GROUP)

    s_same, s_next, vaug = {}, {}, {}
    for j in range(-1, subs):
        k_j = xp_ref[:, col_k] if j < 0 else x_ref[rows_of(j), col_k]
        v_j = xp_ref[:, col_v] if j < 0 else x_ref[rows_of(j), col_v]
        parts = ([0] if j >= 0 else []) + ([1] if j + 1 < subs else [])
        q_cat = jnp.concatenate([x_ref[rows_of(j + e), col_q] for e in parts], axis=0)
        bias = bias_ref[:, parts[0] * TILE:(parts[-1] + 1) * TILE]
        st = _dot_nt(_tile_rows(k_j, HEADS, kmask), q_cat) + bias
        if j < 0:
            st = st + jnp.where(first, NEG, 0.0)
        for pos, e in enumerate(parts):
            (s_same if e == 0 else s_next)[j + e] = st[:, pos * TILE:(pos + 1) * TILE]
        v_t = v_j.astype(F32).T.astype(BF16)
        vaug[j] = [jnp.concatenate([v_t[h * HEAD_DIM:(h + 1) * HEAD_DIM], ones_rows], axis=0)
                   for h in range(HEADS)]

    p_same, p_next, tops = {}, {}, {}
    for i in range(subs):
        ps, pn, tp = [], [], []
        for h in range(HEADS):
            hs = slice(h * TILE, (h + 1) * TILE)
            a, b = s_same[i][hs], s_next[i][hs]
            m = jnp.maximum(jnp.max(a, axis=0, keepdims=True), jnp.max(b, axis=0, keepdims=True))
            ps.append(jnp.exp2(a - m).astype(BF16))
            pn.append(jnp.exp2(b - m).astype(BF16))
            tp.append(m)
        p_same[i], p_next[i], tops[i] = ps, pn, tp

    acc = {i: [None] * HEADS for i in range(subs)}
    for j in range(-1, subs):
        for h in range(HEADS):
            cols = ([p_same[j][h]] if j >= 0 else []) + ([p_next[j + 1][h]] if j + 1 < subs else [])
            r = _dot(vaug[j][h], jnp.concatenate(cols, axis=1))
            targets = ([j] if j >= 0 else []) + ([j + 1] if j + 1 < subs else [])
            for pos, i in enumerate(targets):
                part = r[:, pos * TILE:(pos + 1) * TILE]
                acc[i][h] = part if acc[i][h] is None else acc[i][h] + part

    for i in range(subs):
        o_t, lse_t = [], []
        for h in range(HEADS):
            l = acc[i][h][HEAD_DIM:HEAD_DIM + 1]
            o_t.append(acc[i][h][:HEAD_DIM] / l)
            lse_t.append(jnp.broadcast_to(tops[i][h] + jnp.log2(l), (HEAD_DIM, TILE)))
        o = jnp.concatenate(o_t, axis=0).T
        lse = jnp.concatenate(lse_t, axis=0).T
        rs = rows_of(i)
        if merge:
            o1, l1 = o1_ref[rs, out_cols].astype(F32), l1_ref[rs, out_cols]
            o2, l2 = o2_ref[rs, out_cols].astype(F32), l2_ref[rs, out_cols]
            top = jnp.maximum(jnp.maximum(l1, l2), lse)
            w1, w2, w3 = jnp.exp2(l1 - top), jnp.exp2(l2 - top), jnp.exp2(lse - top)
            y_ref[rs, out_cols] = ((w1 * o1 + w2 * o2 + w3 * o) / (w1 + w2 + w3)).astype(BF16)
        else:
            o_ref[rs, out_cols] = o.astype(BF16)
            lse_ref[rs, out_cols] = lse


DIL_SUBBLOCKS = 8


def _dilated_pattern(zc, bias, dil, *, batch, seq, merge_with=None):
    length = seq // dil
    rows = min(length, DIL_SUBBLOCKS * TILE)
    subs = rows // TILE
    classes = min(dil, DIL_SUBBLOCKS // subs)
    zc = zc.reshape(batch, length, dil * 3 * GROUP)
    blk = pl.BlockSpec((None, rows, classes * 3 * GROUP), lambda b, r, n: (b, n, r))
    prev = pl.BlockSpec((None, TILE, classes * 3 * GROUP),
                        lambda b, r, n: (b, jnp.maximum(n * subs - 1, 0), r))
    nat = pl.BlockSpec((None, rows, classes * GROUP), lambda b, r, n: (b, n, r))
    in_specs = [blk, prev, _const_spec((HEADS * TILE, 2 * TILE))]
    args = [zc, zc, bias]
    if merge_with is None:
        out_specs = [nat, nat]
        out_shape = [jax.ShapeDtypeStruct((batch, length, dil * GROUP), BF16),
                     jax.ShapeDtypeStruct((batch, length, dil * GROUP), F32)]
    else:
        in_specs += [nat] * len(merge_with)
        args += [t.reshape(batch, length, dil * GROUP) for t in merge_with]
        out_specs = nat
        out_shape = jax.ShapeDtypeStruct((batch, length, dil * GROUP), BF16)
    out = pl.pallas_call(
        functools.partial(_dil_body, subs=subs, classes=classes, merge=merge_with is not None),
        grid=(batch, dil // classes, length // rows),
        in_specs=in_specs, out_specs=out_specs, out_shape=out_shape,
        compiler_params=_params("parallel", "parallel", "parallel"),
        name=f"dilated_d{dil}",
    )(*args)
    if merge_with is None:
        return out
    return out.reshape(batch * seq, GROUP)


def _dilated(zc_views, dil_bias, *, batch, seq):
    o1, l1 = _dilated_pattern(zc_views[0], dil_bias[0], DIL_PATTERNS[0][1], batch=batch, seq=seq)
    o2, l2 = _dilated_pattern(zc_views[1], dil_bias[1], DIL_PATTERNS[1][1], batch=batch, seq=seq)
    return _dilated_pattern(zc_views[2], dil_bias[2], DIL_PATTERNS[2][1], batch=batch, seq=seq,
                            merge_with=(o1, l1, o2, l2))


DIFF_Q = 256
DIFF_K = 256
DIFF_GROUPS = 2 * HEADS


def _diff_body(q_ref, k_ref, vt_ref, bias_ref, lam_ref, sg_ref, y_ref,
               kexp_ref, vaug_ref, acc_ref, sta_ref, stb_ref, *, lam_init, key_steps):
    qi = pl.program_id(1)

    @pl.when(qi == 0)
    def _():
        grp = lax.broadcasted_iota(jnp.int32, (DIFF_GROUPS * DIFF_K, GROUP), 0) // DIFF_K
        slot = lax.broadcasted_iota(jnp.int32, (DIFF_GROUPS * DIFF_K, GROUP), 1) // DIFF_QK_HALF
        kmask = slot == 2 * (grp % HEADS) + grp // HEADS
        ones_rows = (lax.broadcasted_iota(jnp.int32, (AUG_ROWS - HEAD_DIM, DIFF_K), 0) == 0).astype(BF16)

        def build(j, carry):
            k_t = k_ref[pl.ds(pl.multiple_of(j * DIFF_K, DIFF_K), DIFF_K), :]
            kexp_ref[j] = jnp.where(kmask, jnp.concatenate([k_t] * DIFF_GROUPS, axis=0),
                                    jnp.zeros((), BF16))
            vt = jnp.concatenate([vt_ref[2 * j], vt_ref[2 * j + 1]], axis=1)
            for h in range(HEADS):
                vaug_ref[j, h] = jnp.concatenate([vt[h * HEAD_DIM:(h + 1) * HEAD_DIM], ones_rows], axis=0)
            return carry

        lax.fori_loop(0, key_steps, build, 0)

    acc_ref[...] = jnp.zeros_like(acc_ref)
    q = q_ref[...]
    last = key_steps - 1

    def scores(s_ref, j):
        off = jnp.maximum(2 * (qi - j) + 2, 0)
        b_m, b_0, b_p = bias_ref[off], bias_ref[off + 1], bias_ref[off + 2]
        raw = _dot_nt(kexp_ref[jnp.minimum(j, last)], q)
        tops = []
        for g in range(DIFF_GROUPS):
            hs = slice((g % HEADS) * TILE, (g % HEADS + 1) * TILE)
            bias = jnp.concatenate([jnp.concatenate([b_0[hs], b_p[hs]], axis=1),
                                    jnp.concatenate([b_m[hs], b_0[hs]], axis=1)], axis=0)
            s = raw[g * DIFF_K:(g + 1) * DIFF_K] + bias
            s_ref[g * DIFF_K:(g + 1) * DIFF_K, :] = s
            tops.append(jnp.max(s, axis=0, keepdims=True))
        return tuple(tops)

    def consume(s_ref, tops, j, carry):
        ms, ls = carry
        jv = jnp.minimum(j, last)
        new_ms, new_ls = [], []
        for g in range(DIFF_GROUPS):
            mp, h = divmod(g, HEADS)
            m_new = jnp.maximum(ms[g], tops[g])
            p = jnp.exp2(s_ref[g * DIFF_K:(g + 1) * DIFF_K, :] - m_new).astype(BF16)
            alpha = jnp.exp2(ms[g] - m_new)
            r = _dot(vaug_ref[jv, h], p)
            acc_ref[mp, h] = alpha * acc_ref[mp, h] + r[:HEAD_DIM]
            new_ls.append(alpha * ls[g] + r[HEAD_DIM:HEAD_DIM + 1])
            new_ms.append(m_new)
        return tuple(new_ms), tuple(new_ls)

    def pair(jj, carry):
        tops_a, state = carry
        j = 2 * jj
        tops_b = scores(stb_ref, j + 1)
        state = consume(sta_ref, tops_a, j, state)
        tops_a = scores(sta_ref, j + 2)
        return tops_a, consume(stb_ref, tops_b, j + 1, state)

    init = (tuple(jnp.full((1, DIFF_Q), NEG, F32) for _ in range(DIFF_GROUPS)),
            tuple(jnp.zeros((1, DIFF_Q), F32) for _ in range(DIFF_GROUPS)))
    _, (_, ls) = lax.fori_loop(0, (qi + 2) // 2, pair, (scores(sta_ref, 0), init))

    lv = lam_ref[...]
    lam = (jnp.exp(jnp.sum(lv[0:1] * lv[1:2], axis=-1, keepdims=True))
           - jnp.exp(jnp.sum(lv[2:3] * lv[3:4], axis=-1, keepdims=True)) + lam_init)
    outs = []
    for h in range(HEADS):
        o = acc_ref[0, h] / ls[h] - lam * (acc_ref[1, h] / ls[HEADS + h])
        ms_o = jnp.mean(o * o, axis=0, keepdims=True)
        outs.append(o * lax.rsqrt(ms_o + SUBLN_EPS) * sg_ref[...] * (1.0 - lam_init))
    y_ref[...] = jnp.concatenate(outs, axis=0).T.astype(BF16)


def _diff_attention(zd, vt, bias, lam_vecs, subln_cols, *, lam_init, batch, seq):
    zd = zd.reshape(batch, seq, 2 * GROUP)
    key_tiles = seq // TILE
    key_steps = seq // DIFF_K
    vt = vt.reshape(batch, key_tiles, GROUP, TILE)
    y = pl.pallas_call(
        functools.partial(_diff_body, lam_init=lam_init, key_steps=key_steps),
        grid=(batch, seq // DIFF_Q),
        in_specs=[pl.BlockSpec((None, DIFF_Q, GROUP), lambda b, i: (b, i, 0)),
                  pl.BlockSpec((None, seq, GROUP), lambda b, i: (b, 0, 1), pipeline_mode=pl.Buffered(1)),
                  pl.BlockSpec((None, key_tiles, GROUP, TILE), lambda b, i: (b, 0, 0, 0),
                               pipeline_mode=pl.Buffered(1)),
                  _const_spec((key_tiles - DIFF_MIN_OFFSET, HEADS * TILE, TILE)),
                  _const_spec((4, DIFF_QK_HALF)), _const_spec((HEAD_DIM, DIFF_Q))],
        out_specs=pl.BlockSpec((None, DIFF_Q, GROUP), lambda b, i: (b, i, 0)),
        out_shape=jax.ShapeDtypeStruct((batch, seq, GROUP), BF16),
        scratch_shapes=[pltpu.VMEM((key_steps, DIFF_GROUPS * DIFF_K, GROUP), BF16),
                        pltpu.VMEM((key_steps, HEADS, AUG_ROWS, DIFF_K), BF16),
                        pltpu.VMEM((2, HEADS, HEAD_DIM, DIFF_Q), F32),
                        pltpu.VMEM((DIFF_GROUPS * DIFF_K, DIFF_Q), F32),
                        pltpu.VMEM((DIFF_GROUPS * DIFF_K, DIFF_Q), F32)],
        compiler_params=_params("parallel", "arbitrary"),
        name="diff_attn",
    )(zd, zd, vt, bias, lam_vecs, subln_cols)
    return y.reshape(batch * seq, GROUP)


KV_ROWS = 512


def _mem_kv_body(m_ref, g_ref, w_ref, k_ref, v_ref):
    u = _rms(m_ref[...], g_ref[...]).astype(BF16)
    for c in range(D_MODEL // GROUP):
        sl = slice(c * GROUP, (c + 1) * GROUP)
        k_ref[:, sl] = _dot(u, w_ref[:, sl]).astype(BF16)
        v_ref[:, sl] = _dot(u, w_ref[:, D_MODEL + c * GROUP: D_MODEL + (c + 1) * GROUP]).astype(BF16)


def _mem_kv(mem, g, w):
    n = mem.shape[0]
    row = pl.BlockSpec((KV_ROWS, D_MODEL), lambda i: (i, 0))
    return pl.pallas_call(
        _mem_kv_body,
        grid=(n // KV_ROWS,),
        in_specs=[row, _const_spec((1, D_MODEL)), _const_spec((D_MODEL, 2 * D_MODEL))],
        out_specs=[row, row],
        out_shape=[jax.ShapeDtypeStruct((n, D_MODEL), BF16)] * 2,
        compiler_params=_params("parallel"),
        name="mem_kv",
    )(mem, g, w)


XATTN_ROWS = 512


def _xattn_body(x_ref, ya_ref, yb_ref, yc_ref, yd_ref, wout_ref, g_ref, wq_ref, k_ref, v_ref, wo_ref,
                o_ref, q_scr, a_scr):
    x = x_ref[...]
    for gi, y_ref in enumerate((ya_ref, yb_ref, yc_ref, yd_ref)):
        x = x + _dot(y_ref[...], wout_ref[gi * GROUP:(gi + 1) * GROUP, :])
    u = _rms(x, g_ref[...]).astype(BF16)
    for c in range(D_MODEL // GROUP):
        sl = slice(c * GROUP, (c + 1) * GROUP)
        q_scr[:, sl] = _dot(u, wq_ref[:, sl]).astype(BF16)
    for h in range(MEM_HEADS):
        sl = slice(h * MEM_HEAD_DIM, (h + 1) * MEM_HEAD_DIM)
        s = _dot_nt(q_scr[:, sl], k_ref[:, sl]) * (MEM_HEAD_DIM ** -0.5)
        e = jnp.exp(s - jnp.max(s, axis=-1, keepdims=True))
        l = jnp.sum(e, axis=-1, keepdims=True)
        a_scr[:, sl] = (_dot(e.astype(BF16), v_ref[:, sl]) / l).astype(BF16)
    o_ref[...] = x + _dot(a_scr[...], wo_ref[...])


def _xattn(h, ys, w_out, g, wq, k, v, wo, *, batch, seq):
    h3 = h.reshape(batch, seq, D_MODEL)
    ys = [y.reshape(batch, seq, GROUP) for y in ys]
    k3 = k.reshape(batch, MEM_LEN, D_MODEL)
    v3 = v.reshape(batch, MEM_LEN, D_MODEL)
    row = pl.BlockSpec((None, XATTN_ROWS, D_MODEL), lambda b, i: (b, i, 0))
    grp = pl.BlockSpec((None, XATTN_ROWS, GROUP), lambda b, i: (b, i, 0))
    mem = pl.BlockSpec((None, MEM_LEN, D_MODEL), lambda b, i: (b, 0, 0))
    weight = _const_spec((D_MODEL, D_MODEL))
    out = pl.pallas_call(
        _xattn_body,
        grid=(batch, seq // XATTN_ROWS),
        in_specs=[row, grp, grp, grp, grp, weight, _const_spec((1, D_MODEL)), weight, mem, mem, weight],
        out_specs=row,
        out_shape=jax.ShapeDtypeStruct((batch, seq, D_MODEL), F32),
        scratch_shapes=[pltpu.VMEM((XATTN_ROWS, D_MODEL), BF16), pltpu.VMEM((XATTN_ROWS, D_MODEL), BF16)],
        compiler_params=_params("parallel", "parallel"),
        name="xattn",
    )(h3, *ys, w_out, g, wq, k3, v3, wo)
    return out.reshape(batch * seq, D_MODEL)


def _per_head_lanes(x):
    return jnp.repeat(x, HEAD_DIM, axis=-1)


def _in_proj_weight(w_in):
    g = GROUP
    q_a, k_a, v_a, o_a = (w_in[:, i * g:(i + 1) * g] for i in range(4))
    ig = w_in[:, 4 * g:4 * g + HEADS]
    fg = w_in[:, 4 * g + HEADS:4 * g + 2 * HEADS]
    rest = w_in[:, 4 * g + 2 * HEADS:]
    pool, q_c, k_c, v_c, q_d, k_d, v_d = (rest[:, i * g:(i + 1) * g] for i in range(7))
    q_c = q_c * DIL_SCORE_SCALE
    k_d = k_d * DIFF_SCORE_SCALE
    cols = [q_a, k_a, v_a, o_a, _per_head_lanes(ig), _per_head_lanes(fg), pool,
            q_c, k_c, v_c, q_d, k_d]
    return jnp.concatenate(cols, axis=1).astype(BF16), v_d.T.astype(BF16)


def _block_diag(w):
    g, c, _ = w.shape
    eye = jnp.eye(g, dtype=w.dtype)
    return (eye[:, None, :, None] * w[:, :, None, :]).reshape(g * c, g * c)


def kernel(x, mem, t5_bias, ffn1_norm, ffn1_w_gate, ffn1_w_up, ffn1_w_down, mix_norm, w_in,
           mlstm_conv_w, mlstm_conv_b, mlstm_gate_b, mlstm_norm, pool_w, pool_scale,
           diff_lambda, diff_subln, w_out, xattn_norm, mem_norm, xattn_wq, xattn_wkv, xattn_wo,
           ffn2_norm, ffn2_w_gate, ffn2_w_up, ffn2_w_down, final_norm):
    batch, seq, _ = x.shape
    n = batch * seq
    dil_bias, diff_bias = _bias_tiles(t5_bias, seq)
    h = x.reshape(n, D_MODEL)
    mem2 = mem.reshape(batch * MEM_LEN, D_MODEL)
    row = lambda v: v.reshape(1, -1)
    for l in range(DEPTH):
        lam_init = 0.8 - 0.6 * math.exp(-0.3 * l)
        h = _ffn(h, row(ffn1_norm[l]), ffn1_w_gate[l].astype(BF16), ffn1_w_up[l].astype(BF16),
                 ffn1_w_down[l].astype(BF16), row(final_norm), final=False)
        za, zg, zp, zc, zd, vt, zc4, zc16 = _in_proj(h, row(mix_norm[l]), *_in_proj_weight(w_in[l]))
        ya = _mlstm(za, zg, mlstm_conv_w[l], row(mlstm_conv_b[l]),
                    row(_per_head_lanes(mlstm_gate_b[l].reshape(2, HEADS))), row(mlstm_norm[l]),
                    batch=batch, seq=seq)
        yb = _pool(zp, _block_diag(pool_w[l]).astype(BF16), row(pool_scale[l]), batch=batch, seq=seq)
        yc = _dilated((zc, zc4, zc16), dil_bias, batch=batch, seq=seq)
        yd = _diff_attention(zd, vt, diff_bias, diff_lambda[l],
                             jnp.broadcast_to(diff_subln[l][:, None], (HEAD_DIM, DIFF_Q)),
                             lam_init=lam_init, batch=batch, seq=seq)
        k_mem, v_mem = _mem_kv(mem2, row(mem_norm[l]), xattn_wkv[l].astype(BF16))
        h = _xattn(h, (ya, yb, yc, yd), w_out[l].astype(BF16),
                   row(xattn_norm[l]), xattn_wq[l].astype(BF16), k_mem, v_mem,
                   xattn_wo[l].astype(BF16), batch=batch, seq=seq)
        h = _ffn(h, row(ffn2_norm[l]), ffn2_w_gate[l].astype(BF16), ffn2_w_up[l].astype(BF16),
                 ffn2_w_down[l].astype(BF16), row(final_norm), final=(l == DEPTH - 1))
    return h.reshape(batch, seq, D_MODEL)
```

```python
import functools
import math

import jax
import jax.numpy as jnp
import numpy as np
from jax import lax
from jax.experimental import pallas as pl
from jax.experimental.pallas import tpu as pltpu

F32 = jnp.float32
BF16 = jnp.bfloat16

D_MODEL = 1024
D_FF = 2816
DEPTH = 4
GROUP = 256
HEADS = 4
HEAD_DIM = GROUP // HEADS
MEM_LEN = 256
MEM_HEADS = 4
MEM_HEAD_DIM = D_MODEL // MEM_HEADS
MLSTM_CHUNK = 64
CONV_WIDTH = 4
POOL_WINDOWS = (2, 4, 8, 16)
DIL_PATTERNS = ((128, 1), (512, 4), (2048, 16))
DIL_BACK = 128
DIFF_QK_HALF = HEAD_DIM // 2
T5_BUCKETS = 32
T5_MAX_DIST = 2048
RMS_EPS = 1e-6
SUBLN_EPS = 1e-5
NEG = -1e30
LOG2E = math.log2(math.e)
DIFF_SCORE_SCALE = (DIFF_QK_HALF ** -0.5) * LOG2E
DIL_SCORE_SCALE = (HEAD_DIM ** -0.5) * LOG2E
AUG_ROWS = HEAD_DIM + 16
DIFF_MIN_OFFSET = -3
TILE = 128

VMEM_LIMIT_BYTES = 56 * 1024 * 1024


def _rms(xf, g, eps=RMS_EPS):
    return xf * lax.rsqrt(jnp.mean(xf * xf, axis=-1, keepdims=True) + eps) * g


def _const_spec(shape):
    zeros = (0,) * len(shape)
    return pl.BlockSpec(shape, lambda *_: zeros, pipeline_mode=pl.Buffered(1))


def _params(*sem):
    return pltpu.CompilerParams(dimension_semantics=sem, vmem_limit_bytes=VMEM_LIMIT_BYTES)


def _group_mask(rows, cols, row_group, col_group):
    r = lax.broadcasted_iota(jnp.int32, (rows, cols), 0) // row_group
    c = lax.broadcasted_iota(jnp.int32, (rows, cols), 1) // col_group
    return r == c


def _tile_rows(x, reps, mask):
    return jnp.where(mask, jnp.concatenate([x] * reps, axis=0), jnp.zeros((), x.dtype))


def _dot(a, b):
    return jnp.dot(a, b, preferred_element_type=F32)


def _dot_nt(a, b):
    return lax.dot_general(a, b, (((1,), (1,)), ((), ())), preferred_element_type=F32)


def _dot_tn(a, b):
    return lax.dot_general(a, b, (((0,), (0,)), ((), ())), preferred_element_type=F32)


def _group_sum(x, ones_bd):
    hi = x.astype(BF16)
    lo = (x - hi.astype(F32)).astype(BF16)
    return _dot(hi, ones_bd) + _dot(lo, ones_bd)


FFN_ROWS = 512
FFN_COLS = 256


def _ffn_body(x_ref, g_ref, wg_ref, wu_ref, wd_ref, fg_ref, o_ref, act_ref, *, final):
    x = x_ref[...]
    u = _rms(x, g_ref[...]).astype(BF16)
    for c in range(D_FF // FFN_COLS):
        sl = slice(c * FFN_COLS, (c + 1) * FFN_COLS)
        gate = _dot(u, wg_ref[:, sl])
        up = _dot(u, wu_ref[:, sl])
        act_ref[:, sl] = (gate * jax.nn.sigmoid(gate) * up).astype(BF16)
    y = x + 0.5 * _dot(act_ref[...], wd_ref[...])
    if final:
        y = _rms(y, fg_ref[...])
    o_ref[...] = y


def _ffn(h, g, wg, wu, wd, fg, *, final):
    n = h.shape[0]
    row = pl.BlockSpec((FFN_ROWS, D_MODEL), lambda i: (i, 0))
    return pl.pallas_call(
        functools.partial(_ffn_body, final=final),
        grid=(n // FFN_ROWS,),
        in_specs=[row, _const_spec((1, D_MODEL)), _const_spec((D_MODEL, D_FF)),
                  _const_spec((D_MODEL, D_FF)), _const_spec((D_FF, D_MODEL)),
                  _const_spec((1, D_MODEL))],
        out_specs=row,
        out_shape=jax.ShapeDtypeStruct((n, D_MODEL), F32),
        scratch_shapes=[pltpu.VMEM((FFN_ROWS, D_FF), BF16)],
        compiler_params=_params("parallel"),
        name="ffn_final" if final else "ffn",
    )(h, g, wg, wu, wd, fg)


PROJ_ROWS = 512
PROJ_OUTS = (("a", 4 * GROUP, F32), ("g", 2 * GROUP, F32), ("p", GROUP, F32),
             ("c", 3 * GROUP, BF16), ("d", 2 * GROUP, BF16))
PROJ_WIDTH = sum(w for _, w, _ in PROJ_OUTS)


PROJ_DILATIONS = tuple(d for _, d in DIL_PATTERNS if d > 1)


def _in_proj_body(x_ref, g_ref, w_ref, wvt_ref, perm_ref, *o_refs):
    u = _rms(x_ref[...], g_ref[...]).astype(BF16)
    off = 0
    for o_ref, (_, width, dtype) in zip(o_refs, PROJ_OUTS):
        for c in range(width // GROUP):
            z = _dot(u, w_ref[:, off + c * GROUP: off + (c + 1) * GROUP])
            o_ref[:, c * GROUP:(c + 1) * GROUP] = z.astype(dtype)
        off += width
    vt_ref = o_refs[len(PROJ_OUTS)]
    for t in range(PROJ_ROWS // TILE):
        vt_ref[t] = _dot_nt(wvt_ref[...], u[t * TILE:(t + 1) * TILE]).astype(BF16)
    zc = o_refs[3][...]
    for pi, d in enumerate(PROJ_DILATIONS):
        per_class = PROJ_ROWS // d
        zp = _dot(perm_ref[pi], zc).astype(BF16)
        for r in range(d):
            o_refs[len(PROJ_OUTS) + 1 + pi][:, r * 3 * GROUP:(r + 1) * 3 * GROUP] = \
                zp[r * per_class:(r + 1) * per_class]


def _class_permutations():
    mats = np.zeros((len(PROJ_DILATIONS), PROJ_ROWS, PROJ_ROWS), np.float32)
    for pi, d in enumerate(PROJ_DILATIONS):
        t = np.arange(PROJ_ROWS)
        mats[pi, (t % d) * (PROJ_ROWS // d) + t // d, t] = 1.0
    return jnp.asarray(mats, BF16)


def _in_proj(h, g, w, wvt):
    n = h.shape[0]
    tiles = PROJ_ROWS // TILE
    return pl.pallas_call(
        _in_proj_body,
        grid=(n // PROJ_ROWS,),
        in_specs=[pl.BlockSpec((PROJ_ROWS, D_MODEL), lambda i: (i, 0)),
                  _const_spec((1, D_MODEL)), _const_spec((D_MODEL, PROJ_WIDTH)),
                  _const_spec((GROUP, D_MODEL)),
                  _const_spec((len(PROJ_DILATIONS), PROJ_ROWS, PROJ_ROWS))],
        out_specs=[pl.BlockSpec((PROJ_ROWS, w_), lambda i: (i, 0)) for _, w_, _ in PROJ_OUTS]
        + [pl.BlockSpec((tiles, GROUP, TILE), lambda i: (i, 0, 0))]
        + [pl.BlockSpec((PROJ_ROWS // d, d * 3 * GROUP), lambda i: (i, 0)) for d in PROJ_DILATIONS],
        out_shape=[jax.ShapeDtypeStruct((n, w_), dt) for _, w_, dt in PROJ_OUTS]
        + [jax.ShapeDtypeStruct((n // TILE, GROUP, TILE), BF16)]
        + [jax.ShapeDtypeStruct((n // d, d * 3 * GROUP), BF16) for d in PROJ_DILATIONS],
        compiler_params=_params("parallel"),
        name="in_proj",
    )(h, g, w, wvt, _class_permutations())


ML_ROWS = 512
ML_HALO = 8


def _chunk_scan(x, rin, op, fill):
    s = 1
    while s < MLSTM_CHUNK:
        x = op(x, jnp.where(rin >= s, pltpu.roll(x, s, 0), fill))
        s *= 2
    return x


def _mlstm_body(za_ref, zg_ref, cw_ref, cb_ref, gb_ref, ng_ref, y_ref,
                buf_ref, tail_ref, hh_ref, c_ref, n_ref, m_ref):
    L = MLSTM_CHUNK

    @pl.when(pl.program_id(1) == 0)
    def _():
        tail_ref[...] = jnp.zeros_like(tail_ref)
        c_ref[...] = jnp.zeros_like(c_ref)
        n_ref[...] = jnp.zeros_like(n_ref)
        m_ref[...] = jnp.zeros_like(m_ref)

    buf_ref[0:ML_HALO, :] = tail_ref[...]
    buf_ref[ML_HALO:, :] = za_ref[:, 0:2 * GROUP]
    tail_ref[...] = za_ref[ML_ROWS - ML_HALO:, 0:2 * GROUP]
    conv = cb_ref[...]
    for j in range(CONV_WIDTH):
        conv = conv + buf_ref[pl.ds(ML_HALO - (CONV_WIDTH - 1) + j, ML_ROWS), :] * cw_ref[j:j + 1, :]
    qk = conv * jax.nn.sigmoid(conv)
    q = qk[:, :GROUP]
    k = qk[:, GROUP:] * (HEAD_DIM ** -0.5)
    v = za_ref[:, 2 * GROUP:3 * GROUP].astype(BF16)
    q_bf, k_bf = q.astype(BF16), k.astype(BF16)

    ii = zg_ref[:, :GROUP] + gb_ref[:, :GROUP]
    fx = zg_ref[:, GROUP:] + gb_ref[:, GROUP:]
    lf = jnp.minimum(fx, 0.0) - jnp.log1p(jnp.exp(-jnp.abs(fx)))
    rin = lax.broadcasted_iota(jnp.int32, (ML_ROWS, GROUP), 0) % L
    b = _chunk_scan(lf, rin, jnp.add, 0.0)
    a = ii - b
    ca = _chunk_scan(a, rin, jnp.maximum, NEG)

    bd = _group_mask(GROUP, GROUP, HEAD_DIM, HEAD_DIM)
    ones_bd = bd.astype(BF16)
    row = lax.broadcasted_iota(jnp.int32, (L, GROUP), 0)
    key = lax.broadcasted_iota(jnp.int32, (L, GROUP), 1) % L
    causal = key <= row
    diag = key == row

    m_prev = m_ref[...]
    for c in range(ML_ROWS // L):
        rs = slice(c * L, (c + 1) * L)
        q_c, k_c, v_c = q[rs], k[rs], v[rs]
        q_b = q_bf[rs]
        a_c, b_c = a[rs], b[rs]
        g = jnp.maximum(m_prev, ca[rs])
        g_last = g[L - 1:L]
        a_row = jnp.sum(jnp.where(diag, a_c, 0.0), axis=0, keepdims=True)
        decay = jnp.exp(jnp.where(causal, a_row - g, NEG))
        sc = _dot_nt(q_b, _tile_rows(k_bf[rs], HEADS, bd)) * decay
        inter = jnp.exp(m_prev - g)
        num = inter * _dot(q_b, c_ref[...].astype(BF16)) + _dot(sc.astype(BF16), _tile_rows(v_c, HEADS, bd))
        den = inter * _group_sum(q_c * n_ref[...], ones_bd) + _group_sum(sc, ones_bd)
        hh_ref[rs, :] = num / jnp.maximum(jnp.abs(den), jnp.exp(-(b_c + g)))
        kw = k_c * jnp.exp(a_c - g_last)
        carry = jnp.exp(m_prev - g_last)
        c_ref[...] = carry * c_ref[...] + jnp.where(bd, _dot_tn(kw.astype(BF16), v_c), 0.0)
        n_ref[...] = carry * n_ref[...] + jnp.sum(kw, axis=0, keepdims=True)
        m_prev = b_c[L - 1:L] + g_last
    m_ref[...] = m_prev

    hh = hh_ref[...]
    mu = _group_sum(hh, ones_bd) * (1.0 / HEAD_DIM)
    dev = hh - mu
    var = _group_sum(dev * dev, ones_bd) * (1.0 / HEAD_DIM)
    o_gate = jax.nn.sigmoid(za_ref[:, 3 * GROUP:])
    y_ref[...] = (dev * lax.rsqrt(var + RMS_EPS) * ng_ref[...] * o_gate).astype(BF16)


def _mlstm(za, zg, conv_w, conv_b, gate_b, norm_g, *, batch, seq):
    za = za.reshape(batch, seq, 4 * GROUP)
    zg = zg.reshape(batch, seq, 2 * GROUP)
    y = pl.pallas_call(
        _mlstm_body,
        grid=(batch, seq // ML_ROWS),
        in_specs=[pl.BlockSpec((None, ML_ROWS, 4 * GROUP), lambda b, j: (b, j, 0)),
                  pl.BlockSpec((None, ML_ROWS, 2 * GROUP), lambda b, j: (b, j, 0)),
                  _const_spec((CONV_WIDTH, 2 * GROUP)), _const_spec((1, 2 * GROUP)),
                  _const_spec((1, 2 * GROUP)), _const_spec((1, GROUP))],
        out_specs=pl.BlockSpec((None, ML_ROWS, GROUP), lambda b, j: (b, j, 0)),
        out_shape=jax.ShapeDtypeStruct((batch, seq, GROUP), BF16),
        scratch_shapes=[pltpu.VMEM((ML_ROWS + ML_HALO, 2 * GROUP), F32),
                        pltpu.VMEM((ML_HALO, 2 * GROUP), F32),
                        pltpu.VMEM((ML_ROWS, GROUP), F32),
                        pltpu.VMEM((GROUP, GROUP), F32),
                        pltpu.VMEM((1, GROUP), F32),
                        pltpu.VMEM((1, GROUP), F32)],
        compiler_params=_params("parallel", "arbitrary"),
        name="mlstm",
    )(za, zg, conv_w, conv_b, gate_b, norm_g)
    return y.reshape(batch * seq, GROUP)


POOL_ROWS = 1024
POOL_HALO = 16


def _pool_body(u_ref, w_ref, s_ref, y_ref, buf_ref, tail_ref):
    j = pl.program_id(1)

    @pl.when(j == 0)
    def _():
        tail_ref[...] = jnp.zeros_like(tail_ref)

    buf_ref[0:POOL_HALO, :] = tail_ref[...]
    buf_ref[POOL_HALO:, :] = u_ref[...]
    tail_ref[...] = u_ref[POOL_ROWS - POOL_HALO:, :]
    sums, s = [], buf_ref[...]
    for shift in (1, 2, 4, 8):
        s = s + pltpu.roll(s, shift, 0)
        sums.append(s[POOL_HALO:])
    u = u_ref[...]
    lane_group = lax.broadcasted_iota(jnp.int32, (POOL_ROWS, GROUP), 1) // HEAD_DIM
    t = j * POOL_ROWS + lax.broadcasted_iota(jnp.int32, (POOL_ROWS, GROUP), 0)
    total, win = sums[3], jnp.full((POOL_ROWS, GROUP), POOL_WINDOWS[3], jnp.int32)
    for gi in (2, 1, 0):
        total = jnp.where(lane_group == gi, sums[gi], total)
        win = jnp.where(lane_group == gi, POOL_WINDOWS[gi], win)
    mean = total / jnp.minimum(t + 1, win).astype(F32)
    y = _dot((mean - u).astype(BF16), w_ref[...]) * s_ref[...]
    y_ref[...] = y.astype(BF16)


def _pool(zp, w_bd, scale, *, batch, seq):
    zp = zp.reshape(batch, seq, GROUP)
    y = pl.pallas_call(
        _pool_body,
        grid=(batch, seq // POOL_ROWS),
        in_specs=[pl.BlockSpec((None, POOL_ROWS, GROUP), lambda b, j: (b, j, 0)),
                  _const_spec((GROUP, GROUP)), _const_spec((1, GROUP))],
        out_specs=pl.BlockSpec((None, POOL_ROWS, GROUP), lambda b, j: (b, j, 0)),
        out_shape=jax.ShapeDtypeStruct((batch, seq, GROUP), BF16),
        scratch_shapes=[pltpu.VMEM((POOL_ROWS + POOL_HALO, GROUP), F32),
                        pltpu.VMEM((POOL_HALO, GROUP), F32)],
        compiler_params=_params("parallel", "arbitrary"),
        name="pool",
    )(zp, w_bd, scale)
    return y.reshape(batch * seq, GROUP)


def _toeplitz_body(w_ref, o_ref):
    x = jnp.broadcast_to(w_ref[...], (TILE, 2 * TILE))
    o_ref[...] = pltpu.roll(x, 0, 1, stride=1, stride_axis=0)[:, :TILE]


def _toeplitz(rows):
    n = rows.shape[0]
    return pl.pallas_call(
        _toeplitz_body,
        grid=(n,),
        in_specs=[pl.BlockSpec((None, 1, 2 * TILE), lambda i: (i, 0, 0))],
        out_specs=pl.BlockSpec((None, TILE, TILE), lambda i: (i, 0, 0)),
        out_shape=jax.ShapeDtypeStruct((n, TILE, TILE), F32),
        compiler_params=_params("parallel"),
        name="toeplitz",
    )(rows.reshape(n, 1, 2 * TILE))


_TOEPLITZ_X = np.where(np.arange(2 * TILE) <= TILE, -np.arange(2 * TILE), 2 * TILE - np.arange(2 * TILE))


def _t5_bucket(dist):
    max_exact = T5_BUCKETS // 2
    d = jnp.maximum(dist, 1).astype(F32)
    large = max_exact + (jnp.log(d / max_exact) / math.log(T5_MAX_DIST / max_exact)
                         * (T5_BUCKETS - max_exact)).astype(jnp.int32)
    large = jnp.minimum(large, T5_BUCKETS - 1)
    return jnp.where(dist < max_exact, dist, large)


def _bias_tiles(t5_bias, seq):
    x = jnp.asarray(_TOEPLITZ_X, jnp.int32)
    rows = []
    for (w, d) in DIL_PATTERNS:
        tab = t5_bias[_t5_bucket(jnp.arange(DIL_BACK + 1) * d), :HEADS].T * LOG2E
        same = jnp.where(x <= 0, tab[:, jnp.clip(-x, 0, DIL_BACK)], NEG)
        nxt = jnp.where(x >= 0, tab[:, jnp.clip(DIL_BACK - x, 0, DIL_BACK)], NEG)
        rows.append(jnp.stack([same, nxt], axis=1))
    dil = _toeplitz(jnp.stack(rows).reshape(-1, 2 * TILE))
    dil = dil.reshape(len(DIL_PATTERNS), HEADS, 2, TILE, TILE).transpose(0, 1, 3, 2, 4)
    dil = dil.reshape(len(DIL_PATTERNS), HEADS * TILE, 2 * TILE)
    tab = t5_bias[_t5_bucket(jnp.arange(seq)), HEADS:].T * LOG2E
    noff = seq // TILE - DIFF_MIN_OFFSET
    dist = (jnp.arange(noff)[:, None] + DIFF_MIN_OFFSET) * TILE - x[None, :]
    rows = jnp.where(dist >= 0, tab[:, jnp.clip(dist, 0, seq - 1)], NEG)
    diff = _toeplitz(rows.transpose(1, 0, 2).reshape(-1, 2 * TILE))
    return dil, diff.reshape(noff, HEADS * TILE, TILE)


DIL_UNIT = 2 * TILE


def _dil_body(*refs, subs, dil, merge):
    if merge:
        x_ref, xp_ref, bias_ref, unperm_ref, o1_ref, l1_ref, o2_ref, l2_ref, y_ref = refs
    else:
        x_ref, xp_ref, bias_ref, unperm_ref, o_ref, lse_ref = refs
    per_class = [_dil_class(x_ref, xp_ref, bias_ref, rc=rc, subs=subs) for rc in range(dil)]
    piece = DIL_UNIT // dil
    class_o = [jnp.concatenate(pc[0], axis=0) for pc in per_class]
    class_l = [jnp.concatenate(pc[1], axis=0) for pc in per_class]
    for u in range(dil * subs * TILE // DIL_UNIT):
        take = lambda arrs: jnp.concatenate([a[u * piece:(u + 1) * piece] for a in arrs], axis=0)
        if dil == 1:
            o, lse = take(class_o), take(class_l)
        else:
            o_c = take(class_o).astype(BF16)
            l_c = take(class_l)
            l_hi = l_c.astype(BF16)
            l_lo = (l_c - l_hi.astype(F32)).astype(BF16)
            o = _dot(unperm_ref[...], o_c)
            lse = _dot(unperm_ref[...], l_hi) + _dot(unperm_ref[...], l_lo)
        rs = slice(u * DIL_UNIT, (u + 1) * DIL_UNIT)
        if merge:
            o1, l1 = o1_ref[rs, :].astype(F32), l1_ref[rs, :]
            o2, l2 = o2_ref[rs, :].astype(F32), l2_ref[rs, :]
            top = jnp.maximum(jnp.maximum(l1, l2), lse)
            w1, w2, w3 = jnp.exp2(l1 - top), jnp.exp2(l2 - top), jnp.exp2(lse - top)
            y_ref[rs, :] = ((w1 * o1 + w2 * o2 + w3 * o) / (w1 + w2 + w3)).astype(BF16)
        else:
            o_ref[rs, :] = o.astype(BF16)
            lse_ref[rs, :] = lse


def _dil_class(x_ref, xp_ref, bias_ref, *, rc, subs):
    first = pl.program_id(1) == 0
    kmask = _group_mask(HEADS * TILE, GROUP, TILE, HEAD_DIM)
    ones_rows = (lax.broadcasted_iota(jnp.int32, (AUG_ROWS - HEAD_DIM, TILE), 0) == 0).astype(BF16)
    rows_of = lambda sb: slice(sb * TILE, (sb + 1) * TILE)
    col_q, col_k, col_v = (slice((3 * rc + w) * GROUP, (3 * rc + w + 1) * GROUP) for w in range(3))

    s_same, s_next, vaug = {}, {}, {}
    for j in range(-1, subs):
        k_j = xp_ref[:, col_k] if j < 0 else x_ref[rows_of(j), col_k]
        v_j = xp_ref[:, col_v] if j < 0 else x_ref[rows_of(j), col_v]
        parts = ([0] if j >= 0 else []) + ([1] if j + 1 < subs else [])
        q_cat = jnp.concatenate([x_ref[rows_of(j + e), col_q] for e in parts], axis=0)
        bias = bias_ref[:, parts[0] * TILE:(parts[-1] + 1) * TILE]
        st = _dot_nt(_tile_rows(k_j, HEADS, kmask), q_cat) + bias
        if j < 0:
            st = st + jnp.where(first, NEG, 0.0)
        for pos, e in enumerate(parts):
            (s_same if e == 0 else s_next)[j + e] = st[:, pos * TILE:(pos + 1) * TILE]
        v_t = v_j.astype(F32).T.astype(BF16)
        vaug[j] = [jnp.concatenate([v_t[h * HEAD_DIM:(h + 1) * HEAD_DIM], ones_rows], axis=0)
                   for h in range(HEADS)]

    p_same, p_next, tops = {}, {}, {}
    for i in range(subs):
        ps, pn, tp = [], [], []
        for h in range(HEADS):
            hs = slice(h * TILE, (h + 1) * TILE)
            a, b = s_same[i][hs], s_next[i][hs]
            m = jnp.maximum(jnp.max(a, axis=0, keepdims=True), jnp.max(b, axis=0, keepdims=True))
            ps.append(jnp.exp2(a - m).astype(BF16))
            pn.append(jnp.exp2(b - m).astype(BF16))
            tp.append(m)
        p_same[i], p_next[i], tops[i] = ps, pn, tp

    acc = {i: [None] * HEADS for i in range(subs)}
    for j in range(-1, subs):
        for h in range(HEADS):
            cols = ([p_same[j][h]] if j >= 0 else []) + ([p_next[j + 1][h]] if j + 1 < subs else [])
            r = _dot(vaug[j][h], jnp.concatenate(cols, axis=1))
            targets = ([j] if j >= 0 else []) + ([j + 1] if j + 1 < subs else [])
            for pos, i in enumerate(targets):
                part = r[:, pos * TILE:(pos + 1) * TILE]
                acc[i][h] = part if acc[i][h] is None else acc[i][h] + part

    outs, lses = [], []
    for i in range(subs):
        o_t, lse_t = [], []
        for h in range(HEADS):
            l = acc[i][h][HEAD_DIM:HEAD_DIM + 1]
            o_t.append(acc[i][h][:HEAD_DIM] / l)
            lse_t.append(jnp.broadcast_to(tops[i][h] + jnp.log2(l), (HEAD_DIM, TILE)))
        outs.append(jnp.concatenate(o_t, axis=0).T)
        lses.append(jnp.concatenate(lse_t, axis=0).T)
    return outs, lses


DIL_SUBBLOCKS = 8


def _unpermutation(dil):
    t = np.arange(DIL_UNIT)
    mat = np.zeros((DIL_UNIT, DIL_UNIT), np.float32)
    mat[t, (t % dil) * (DIL_UNIT // dil) + t // dil] = 1.0
    return jnp.asarray(mat, BF16)


def _dilated_pattern(zc, bias, dil, *, batch, seq, merge_with=None):
    length = seq // dil
    subs = min(max(DIL_SUBBLOCKS // max(dil // 2, 1), 2), length // TILE)
    rows = subs * TILE
    tokens = rows * dil
    zc = zc.reshape(batch, length, dil * 3 * GROUP)
    blk = pl.BlockSpec((None, rows, dil * 3 * GROUP), lambda b, n: (b, n, 0))
    prev = pl.BlockSpec((None, TILE, dil * 3 * GROUP), lambda b, n: (b, jnp.maximum(n * subs - 1, 0), 0))
    nat = pl.BlockSpec((None, tokens, GROUP), lambda b, n: (b, n, 0))
    in_specs = [blk, prev, _const_spec((HEADS * TILE, 2 * TILE)), _const_spec((DIL_UNIT, DIL_UNIT))]
    args = [zc, zc, bias, _unpermutation(dil)]
    if merge_with is None:
        out_specs = [nat, nat]
        out_shape = [jax.ShapeDtypeStruct((batch, seq, GROUP), BF16),
                     jax.ShapeDtypeStruct((batch, seq, GROUP), F32)]
    else:
        in_specs += [nat] * len(merge_with)
        args += list(merge_with)
        out_specs = nat
        out_shape = jax.ShapeDtypeStruct((batch, seq, GROUP), BF16)
    out = pl.pallas_call(
        functools.partial(_dil_body, subs=subs, dil=dil, merge=merge_with is not None),
        grid=(batch, length // rows),
        in_specs=in_specs, out_specs=out_specs, out_shape=out_shape,
        compiler_params=_params("parallel", "parallel"),
        name=f"dilated_d{dil}",
    )(*args)
    if merge_with is None:
        return out
    return out.reshape(batch * seq, GROUP)


def _dilated(zc_views, dil_bias, *, batch, seq):
    o1, l1 = _dilated_pattern(zc_views[0], dil_bias[0], DIL_PATTERNS[0][1], batch=batch, seq=seq)
    o2, l2 = _dilated_pattern(zc_views[1], dil_bias[1], DIL_PATTERNS[1][1], batch=batch, seq=seq)
    return _dilated_pattern(zc_views[2], dil_bias[2], DIL_PATTERNS[2][1], batch=batch, seq=seq,
                            merge_with=(o1, l1, o2, l2))


DIFF_Q = 256
DIFF_K = 256
DIFF_GROUPS = 2 * HEADS


def _diff_body(q_ref, k_ref, vt_ref, bias_ref, lam_ref, sg_ref, y_ref,
               kexp_ref, vaug_ref, acc_ref, sta_ref, stb_ref, *, lam_init, key_steps):
    qi = pl.program_id(1)

    @pl.when(qi == 0)
    def _():
        grp = lax.broadcasted_iota(jnp.int32, (DIFF_GROUPS * DIFF_K, GROUP), 0) // DIFF_K
        slot = lax.broadcasted_iota(jnp.int32, (DIFF_GROUPS * DIFF_K, GROUP), 1) // DIFF_QK_HALF
        kmask = slot == 2 * (grp % HEADS) + grp // HEADS
        ones_rows = (lax.broadcasted_iota(jnp.int32, (AUG_ROWS - HEAD_DIM, DIFF_K), 0) == 0).astype(BF16)

        def build(j, carry):
            k_t = k_ref[pl.ds(pl.multiple_of(j * DIFF_K, DIFF_K), DIFF_K), :]
            kexp_ref[j] = jnp.where(kmask, jnp.concatenate([k_t] * DIFF_GROUPS, axis=0),
                                    jnp.zeros((), BF16))
            vt = jnp.concatenate([vt_ref[2 * j], vt_ref[2 * j + 1]], axis=1)
            for h in range(HEADS):
                vaug_ref[j, h] = jnp.concatenate([vt[h * HEAD_DIM:(h + 1) * HEAD_DIM], ones_rows], axis=0)
            return carry

        lax.fori_loop(0, key_steps, build, 0)

    acc_ref[...] = jnp.zeros_like(acc_ref)
    q = q_ref[...]
    last = key_steps - 1

    def scores(s_ref, j):
        base = (DIFF_Q // TILE) * qi - (DIFF_K // TILE) * j - DIFF_MIN_OFFSET
        tiles = {d: bias_ref[jnp.maximum(base + d, 0)]
                 for d in range(1 - DIFF_K // TILE, DIFF_Q // TILE)}
        raw = _dot_nt(kexp_ref[jnp.minimum(j, last)], q)
        tops = []
        for g in range(DIFF_GROUPS):
            hs = slice((g % HEADS) * TILE, (g % HEADS + 1) * TILE)
            bias = jnp.concatenate(
                [jnp.concatenate([tiles[a - b][hs] for a in range(DIFF_Q // TILE)], axis=1)
                 for b in range(DIFF_K // TILE)], axis=0)
            s = raw[g * DIFF_K:(g + 1) * DIFF_K] + bias
            s_ref[g * DIFF_K:(g + 1) * DIFF_K, :] = s
            tops.append(jnp.max(s, axis=0, keepdims=True))
        return tuple(tops)

    def consume(s_ref, tops, j, carry):
        ms, ls = carry
        jv = jnp.minimum(j, last)
        new_ms, new_ls = [], []
        for g in range(DIFF_GROUPS):
            mp, h = divmod(g, HEADS)
            m_new = jnp.maximum(ms[g], tops[g])
            p = jnp.exp2(s_ref[g * DIFF_K:(g + 1) * DIFF_K, :] - m_new).astype(BF16)
            alpha = jnp.exp2(ms[g] - m_new)
            r = _dot(vaug_ref[jv, h], p)
            acc_ref[mp, h] = alpha * acc_ref[mp, h] + r[:HEAD_DIM]
            new_ls.append(alpha * ls[g] + r[HEAD_DIM:HEAD_DIM + 1])
            new_ms.append(m_new)
        return tuple(new_ms), tuple(new_ls)

    def pair(jj, carry):
        tops_a, state = carry
        j = 2 * jj
        tops_b = scores(stb_ref, j + 1)
        state = consume(sta_ref, tops_a, j, state)
        tops_a = scores(sta_ref, j + 2)
        return tops_a, consume(stb_ref, tops_b, j + 1, state)

    init = (tuple(jnp.full((1, DIFF_Q), NEG, F32) for _ in range(DIFF_GROUPS)),
            tuple(jnp.zeros((1, DIFF_Q), F32) for _ in range(DIFF_GROUPS)))
    key_steps_needed = (qi + 1) * (DIFF_Q // DIFF_K)
    _, (_, ls) = lax.fori_loop(0, (key_steps_needed + 1) // 2, pair, (scores(sta_ref, 0), init))

    lv = lam_ref[...]
    lam = (jnp.exp(jnp.sum(lv[0:1] * lv[1:2], axis=-1, keepdims=True))
           - jnp.exp(jnp.sum(lv[2:3] * lv[3:4], axis=-1, keepdims=True)) + lam_init)
    outs = []
    for h in range(HEADS):
        o = acc_ref[0, h] / ls[h] - lam * (acc_ref[1, h] / ls[HEADS + h])
        ms_o = jnp.mean(o * o, axis=0, keepdims=True)
        outs.append(o * lax.rsqrt(ms_o + SUBLN_EPS) * sg_ref[...] * (1.0 - lam_init))
    y_ref[...] = jnp.concatenate(outs, axis=0).T.astype(BF16)


def _diff_attention(zd, vt, bias, lam_vecs, subln_cols, *, lam_init, batch, seq):
    zd = zd.reshape(batch, seq, 2 * GROUP)
    key_tiles = seq // TILE
    key_steps = seq // DIFF_K
    vt = vt.reshape(batch, key_tiles, GROUP, TILE)
    y = pl.pallas_call(
        functools.partial(_diff_body, lam_init=lam_init, key_steps=key_steps),
        grid=(batch, seq // DIFF_Q),
        in_specs=[pl.BlockSpec((None, DIFF_Q, GROUP), lambda b, i: (b, i, 0)),
                  pl.BlockSpec((None, seq, GROUP), lambda b, i: (b, 0, 1), pipeline_mode=pl.Buffered(1)),
                  pl.BlockSpec((None, key_tiles, GROUP, TILE), lambda b, i: (b, 0, 0, 0),
                               pipeline_mode=pl.Buffered(1)),
                  _const_spec((key_tiles - DIFF_MIN_OFFSET, HEADS * TILE, TILE)),
                  _const_spec((4, DIFF_QK_HALF)), _const_spec((HEAD_DIM, DIFF_Q))],
        out_specs=pl.BlockSpec((None, DIFF_Q, GROUP), lambda b, i: (b, i, 0)),
        out_shape=jax.ShapeDtypeStruct((batch, seq, GROUP), BF16),
        scratch_shapes=[pltpu.VMEM((key_steps, DIFF_GROUPS * DIFF_K, GROUP), BF16),
                        pltpu.VMEM((key_steps, HEADS, AUG_ROWS, DIFF_K), BF16),
                        pltpu.VMEM((2, HEADS, HEAD_DIM, DIFF_Q), F32),
                        pltpu.VMEM((DIFF_GROUPS * DIFF_K, DIFF_Q), F32),
                        pltpu.VMEM((DIFF_GROUPS * DIFF_K, DIFF_Q), F32)],
        compiler_params=_params("parallel", "arbitrary"),
        name="diff_attn",
    )(zd, zd, vt, bias, lam_vecs, subln_cols)
    return y.reshape(batch * seq, GROUP)


KV_ROWS = 512


def _mem_kv_body(m_ref, g_ref, w_ref, k_ref, v_ref):
    u = _rms(m_ref[...], g_ref[...]).astype(BF16)
    for c in range(D_MODEL // GROUP):
        sl = slice(c * GROUP, (c + 1) * GROUP)
        k_ref[:, sl] = _dot(u, w_ref[:, sl]).astype(BF16)
        v_ref[:, sl] = _dot(u, w_ref[:, D_MODEL + c * GROUP: D_MODEL + (c + 1) * GROUP]).astype(BF16)


def _mem_kv(mem, g, w):
    n = mem.shape[0]
    row = pl.BlockSpec((KV_ROWS, D_MODEL), lambda i: (i, 0))
    return pl.pallas_call(
        _mem_kv_body,
        grid=(n // KV_ROWS,),
        in_specs=[row, _const_spec((1, D_MODEL)), _const_spec((D_MODEL, 2 * D_MODEL))],
        out_specs=[row, row],
        out_shape=[jax.ShapeDtypeStruct((n, D_MODEL), BF16)] * 2,
        compiler_params=_params("parallel"),
        name="mem_kv",
    )(mem, g, w)


XATTN_ROWS = 512


def _xattn_body(x_ref, ya_ref, yb_ref, yc_ref, yd_ref, wout_ref, g_ref, wq_ref, k_ref, v_ref, wo_ref,
                o_ref, q_scr, a_scr):
    x = x_ref[...]
    for gi, y_ref in enumerate((ya_ref, yb_ref, yc_ref, yd_ref)):
        x = x + _dot(y_ref[...], wout_ref[gi * GROUP:(gi + 1) * GROUP, :])
    u = _rms(x, g_ref[...]).astype(BF16)
    for c in range(D_MODEL // GROUP):
        sl = slice(c * GROUP, (c + 1) * GROUP)
        q_scr[:, sl] = _dot(u, wq_ref[:, sl]).astype(BF16)
    for h in range(MEM_HEADS):
        sl = slice(h * MEM_HEAD_DIM, (h + 1) * MEM_HEAD_DIM)
        s = _dot_nt(q_scr[:, sl], k_ref[:, sl]) * (MEM_HEAD_DIM ** -0.5)
        e = jnp.exp(s - jnp.max(s, axis=-1, keepdims=True))
        l = jnp.sum(e, axis=-1, keepdims=True)
        a_scr[:, sl] = (_dot(e.astype(BF16), v_ref[:, sl]) / l).astype(BF16)
    o_ref[...] = x + _dot(a_scr[...], wo_ref[...])


def _xattn(h, ys, w_out, g, wq, k, v, wo, *, batch, seq):
    h3 = h.reshape(batch, seq, D_MODEL)
    ys = [y.reshape(batch, seq, GROUP) for y in ys]
    k3 = k.reshape(batch, MEM_LEN, D_MODEL)
    v3 = v.reshape(batch, MEM_LEN, D_MODEL)
    row = pl.BlockSpec((None, XATTN_ROWS, D_MODEL), lambda b, i: (b, i, 0))
    grp = pl.BlockSpec((None, XATTN_ROWS, GROUP), lambda b, i: (b, i, 0))
    mem = pl.BlockSpec((None, MEM_LEN, D_MODEL), lambda b, i: (b, 0, 0))
    weight = _const_spec((D_MODEL, D_MODEL))
    out = pl.pallas_call(
        _xattn_body,
        grid=(batch, seq // XATTN_ROWS),
        in_specs=[row, grp, grp, grp, grp, weight, _const_spec((1, D_MODEL)), weight, mem, mem, weight],
        out_specs=row,
        out_shape=jax.ShapeDtypeStruct((batch, seq, D_MODEL), F32),
        scratch_shapes=[pltpu.VMEM((XATTN_ROWS, D_MODEL), BF16), pltpu.VMEM((XATTN_ROWS, D_MODEL), BF16)],
        compiler_params=_params("parallel", "parallel"),
        name="xattn",
    )(h3, *ys, w_out, g, wq, k3, v3, wo)
    return out.reshape(batch * seq, D_MODEL)


def _per_head_lanes(x):
    return jnp.repeat(x, HEAD_DIM, axis=-1)


def _in_proj_weight(w_in):
    g = GROUP
    q_a, k_a, v_a, o_a = (w_in[:, i * g:(i + 1) * g] for i in range(4))
    ig = w_in[:, 4 * g:4 * g + HEADS]
    fg = w_in[:, 4 * g + HEADS:4 * g + 2 * HEADS]
    rest = w_in[:, 4 * g + 2 * HEADS:]
    pool, q_c, k_c, v_c, q_d, k_d, v_d = (rest[:, i * g:(i + 1) * g] for i in range(7))
    q_c = q_c * DIL_SCORE_SCALE
    k_d = k_d * DIFF_SCORE_SCALE
    cols = [q_a, k_a, v_a, o_a, _per_head_lanes(ig), _per_head_lanes(fg), pool,
            q_c, k_c, v_c, q_d, k_d]
    return jnp.concatenate(cols, axis=1).astype(BF16), v_d.T.astype(BF16)


def _block_diag(w):
    g, c, _ = w.shape
    eye = jnp.eye(g, dtype=w.dtype)
    return (eye[:, None, :, None] * w[:, :, None, :]).reshape(g * c, g * c)


def kernel(x, mem, t5_bias, ffn1_norm, ffn1_w_gate, ffn1_w_up, ffn1_w_down, mix_norm, w_in,
           mlstm_conv_w, mlstm_conv_b, mlstm_gate_b, mlstm_norm, pool_w, pool_scale,
           diff_lambda, diff_subln, w_out, xattn_norm, mem_norm, xattn_wq, xattn_wkv, xattn_wo,
           ffn2_norm, ffn2_w_gate, ffn2_w_up, ffn2_w_down, final_norm):
    batch, seq, _ = x.shape
    n = batch * seq
    dil_bias, diff_bias = _bias_tiles(t5_bias, seq)
    h = x.reshape(n, D_MODEL)
    mem2 = mem.reshape(batch * MEM_LEN, D_MODEL)
    row = lambda v: v.reshape(1, -1)
    for l in range(DEPTH):
        lam_init = 0.8 - 0.6 * math.exp(-0.3 * l)
        h = _ffn(h, row(ffn1_norm[l]), ffn1_w_gate[l].astype(BF16), ffn1_w_up[l].astype(BF16),
                 ffn1_w_down[l].astype(BF16), row(final_norm), final=False)
        za, zg, zp, zc, zd, vt, zc4, zc16 = _in_proj(h, row(mix_norm[l]), *_in_proj_weight(w_in[l]))
        ya = _mlstm(za, zg, mlstm_conv_w[l], row(mlstm_conv_b[l]),
                    row(_per_head_lanes(mlstm_gate_b[l].reshape(2, HEADS))), row(mlstm_norm[l]),
                    batch=batch, seq=seq)
        yb = _pool(zp, _block_diag(pool_w[l]).astype(BF16), row(pool_scale[l]), batch=batch, seq=seq)
        yc = _dilated((zc, zc4, zc16), dil_bias, batch=batch, seq=seq)
        yd = _diff_attention(zd, vt, diff_bias, diff_lambda[l],
                             jnp.broadcast_to(diff_subln[l][:, None], (HEAD_DIM, DIFF_Q)),
                             lam_init=lam_init, batch=batch, seq=seq)
        k_mem, v_mem = _mem_kv(mem2, row(mem_norm[l]), xattn_wkv[l].astype(BF16))
        h = _xattn(h, (ya, yb, yc, yd), w_out[l].astype(BF16),
                   row(xattn_norm[l]), xattn_wq[l].astype(BF16), k_mem, v_mem,
                   xattn_wo[l].astype(BF16), batch=batch, seq=seq)
        h = _ffn(h, row(ffn2_norm[l]), ffn2_w_gate[l].astype(BF16), ffn2_w_up[l].astype(BF16),
                 ffn2_w_down[l].astype(BF16), row(final_norm), final=(l == DEPTH - 1))
    return h.reshape(batch, seq, D_MODEL)
```

```python
import functools
import math

import jax
import jax.numpy as jnp
import numpy as np
from jax import lax
from jax.experimental import pallas as pl
from jax.experimental.pallas import tpu as pltpu

F32 = jnp.float32
BF16 = jnp.bfloat16

D_MODEL = 1024
D_FF = 2816
DEPTH = 4
GROUP = 256
HEADS = 4
HEAD_DIM = GROUP // HEADS
MEM_LEN = 256
MEM_HEADS = 4
MEM_HEAD_DIM = D_MODEL // MEM_HEADS
MLSTM_CHUNK = 64
CONV_WIDTH = 4
POOL_WINDOWS = (2, 4, 8, 16)
DIL_PATTERNS = ((128, 1), (512, 4), (2048, 16))
DIL_BACK = 128
DIFF_QK_HALF = HEAD_DIM // 2
T5_BUCKETS = 32
T5_MAX_DIST = 2048
RMS_EPS = 1e-6
SUBLN_EPS = 1e-5
NEG = -1e30
LOG2E = math.log2(math.e)
DIFF_SCORE_SCALE = (DIFF_QK_HALF ** -0.5) * LOG2E
DIL_SCORE_SCALE = (HEAD_DIM ** -0.5) * LOG2E
AUG_ROWS = HEAD_DIM + 16
DIFF_MIN_OFFSET = -3
TILE = 128

VMEM_LIMIT_BYTES = 56 * 1024 * 1024


def _rms(xf, g, eps=RMS_EPS):
    return xf * lax.rsqrt(jnp.mean(xf * xf, axis=-1, keepdims=True) + eps) * g


def _const_spec(shape):
    zeros = (0,) * len(shape)
    return pl.BlockSpec(shape, lambda *_: zeros, pipeline_mode=pl.Buffered(1))


def _params(*sem):
    return pltpu.CompilerParams(dimension_semantics=sem, vmem_limit_bytes=VMEM_LIMIT_BYTES)


def _group_mask(rows, cols, row_group, col_group):
    r = lax.broadcasted_iota(jnp.int32, (rows, cols), 0) // row_group
    c = lax.broadcasted_iota(jnp.int32, (rows, cols), 1) // col_group
    return r == c


def _tile_rows(x, reps, mask):
    return jnp.where(mask, jnp.concatenate([x] * reps, axis=0), jnp.zeros((), x.dtype))


def _dot(a, b):
    return jnp.dot(a, b, preferred_element_type=F32)


def _dot_nt(a, b):
    return lax.dot_general(a, b, (((1,), (1,)), ((), ())), preferred_element_type=F32)


def _dot_tn(a, b):
    return lax.dot_general(a, b, (((0,), (0,)), ((), ())), preferred_element_type=F32)


def _group_sum(x, ones_bd):
    hi = x.astype(BF16)
    lo = (x - hi.astype(F32)).astype(BF16)
    return _dot(hi, ones_bd) + _dot(lo, ones_bd)


FFN_ROWS = 512
FFN_COLS = 256


def _ffn_body(x_ref, g_ref, wg_ref, wu_ref, wd_ref, fg_ref, o_ref, act_ref, *, final):
    x = x_ref[...]
    u = _rms(x, g_ref[...]).astype(BF16)
    for c in range(D_FF // FFN_COLS):
        sl = slice(c * FFN_COLS, (c + 1) * FFN_COLS)
        gate = _dot(u, wg_ref[:, sl])
        up = _dot(u, wu_ref[:, sl])
        act_ref[:, sl] = (gate * jax.nn.sigmoid(gate) * up).astype(BF16)
    y = x + 0.5 * _dot(act_ref[...], wd_ref[...])
    if final:
        y = _rms(y, fg_ref[...])
    o_ref[...] = y


def _ffn(h, g, wg, wu, wd, fg, *, final):
    n = h.shape[0]
    row = pl.BlockSpec((FFN_ROWS, D_MODEL), lambda i: (i, 0))
    return pl.pallas_call(
        functools.partial(_ffn_body, final=final),
        grid=(n // FFN_ROWS,),
        in_specs=[row, _const_spec((1, D_MODEL)), _const_spec((D_MODEL, D_FF)),
                  _const_spec((D_MODEL, D_FF)), _const_spec((D_FF, D_MODEL)),
                  _const_spec((1, D_MODEL))],
        out_specs=row,
        out_shape=jax.ShapeDtypeStruct((n, D_MODEL), F32),
        scratch_shapes=[pltpu.VMEM((FFN_ROWS, D_FF), BF16)],
        compiler_params=_params("parallel"),
        name="ffn_final" if final else "ffn",
    )(h, g, wg, wu, wd, fg)


PROJ_ROWS = 512
PROJ_OUTS = (("a", 4 * GROUP, F32), ("g", 2 * GROUP, F32), ("p", GROUP, F32),
             ("c", 3 * GROUP, BF16), ("d", 2 * GROUP, BF16))
PROJ_WIDTH = sum(w for _, w, _ in PROJ_OUTS)


PROJ_DILATIONS = tuple(d for _, d in DIL_PATTERNS if d > 1)


def _in_proj_body(x_ref, g_ref, w_ref, wv_ref, perm_ref, *o_refs):
    u = _rms(x_ref[...], g_ref[...]).astype(BF16)
    off = 0
    for o_ref, (_, width, dtype) in zip(o_refs, PROJ_OUTS):
        for c in range(width // GROUP):
            z = _dot(u, w_ref[:, off + c * GROUP: off + (c + 1) * GROUP])
            o_ref[:, c * GROUP:(c + 1) * GROUP] = z.astype(dtype)
        off += width
    vt_ref = o_refs[len(PROJ_OUTS)]
    v = _dot(u, wv_ref[...])
    for t in range(PROJ_ROWS // TILE):
        vt_ref[t] = v[t * TILE:(t + 1) * TILE].T.astype(BF16)
    for pi, d in enumerate(PROJ_DILATIONS):
        per_class = PROJ_UNIT // d
        for un in range(PROJ_ROWS // PROJ_UNIT):
            zp = _dot(perm_ref[pi], o_refs[3][un * PROJ_UNIT:(un + 1) * PROJ_UNIT, :]).astype(BF16)
            for r in range(d):
                o_refs[len(PROJ_OUTS) + 1 + pi][un * per_class:(un + 1) * per_class,
                                                r * 3 * GROUP:(r + 1) * 3 * GROUP] = \
                    zp[r * per_class:(r + 1) * per_class]


PROJ_UNIT = 2 * TILE


def _class_permutations():
    mats = np.zeros((len(PROJ_DILATIONS), PROJ_UNIT, PROJ_UNIT), np.float32)
    for pi, d in enumerate(PROJ_DILATIONS):
        t = np.arange(PROJ_UNIT)
        mats[pi, (t % d) * (PROJ_UNIT // d) + t // d, t] = 1.0
    return jnp.asarray(mats, BF16)


def _in_proj(h, g, w, wv):
    n = h.shape[0]
    tiles = PROJ_ROWS // TILE
    return pl.pallas_call(
        _in_proj_body,
        grid=(n // PROJ_ROWS,),
        in_specs=[pl.BlockSpec((PROJ_ROWS, D_MODEL), lambda i: (i, 0)),
                  _const_spec((1, D_MODEL)), _const_spec((D_MODEL, PROJ_WIDTH)),
                  _const_spec((D_MODEL, GROUP)),
                  _const_spec((len(PROJ_DILATIONS), PROJ_UNIT, PROJ_UNIT))],
        out_specs=[pl.BlockSpec((PROJ_ROWS, w_), lambda i: (i, 0)) for _, w_, _ in PROJ_OUTS]
        + [pl.BlockSpec((tiles, GROUP, TILE), lambda i: (i, 0, 0))]
        + [pl.BlockSpec((PROJ_ROWS // d, d * 3 * GROUP), lambda i: (i, 0)) for d in PROJ_DILATIONS],
        out_shape=[jax.ShapeDtypeStruct((n, w_), dt) for _, w_, dt in PROJ_OUTS]
        + [jax.ShapeDtypeStruct((n // TILE, GROUP, TILE), BF16)]
        + [jax.ShapeDtypeStruct((n // d, d * 3 * GROUP), BF16) for d in PROJ_DILATIONS],
        compiler_params=_params("parallel"),
        name="in_proj",
    )(h, g, w, wv, _class_permutations())


ML_ROWS = 512
ML_HALO = 8


def _chunk_scan(x, rin, op, fill):
    s = 1
    while s < MLSTM_CHUNK:
        x = op(x, jnp.where(rin >= s, pltpu.roll(x, s, 0), fill))
        s *= 2
    return x


def _mlstm_body(za_ref, zg_ref, cw_ref, cb_ref, gb_ref, ng_ref, y_ref,
                buf_ref, tail_ref, hh_ref, c_ref, n_ref, m_ref):
    L = MLSTM_CHUNK

    @pl.when(pl.program_id(1) == 0)
    def _():
        tail_ref[...] = jnp.zeros_like(tail_ref)
        c_ref[...] = jnp.zeros_like(c_ref)
        n_ref[...] = jnp.zeros_like(n_ref)
        m_ref[...] = jnp.zeros_like(m_ref)

    buf_ref[0:ML_HALO, :] = tail_ref[...]
    buf_ref[ML_HALO:, :] = za_ref[:, 0:2 * GROUP]
    tail_ref[...] = za_ref[ML_ROWS - ML_HALO:, 0:2 * GROUP]
    conv = cb_ref[...]
    for j in range(CONV_WIDTH):
        conv = conv + buf_ref[pl.ds(ML_HALO - (CONV_WIDTH - 1) + j, ML_ROWS), :] * cw_ref[j:j + 1, :]
    qk = conv * jax.nn.sigmoid(conv)
    q = qk[:, :GROUP]
    k = qk[:, GROUP:] * (HEAD_DIM ** -0.5)
    v = za_ref[:, 2 * GROUP:3 * GROUP].astype(BF16)
    q_bf, k_bf = q.astype(BF16), k.astype(BF16)

    ii = zg_ref[:, :GROUP] + gb_ref[:, :GROUP]
    fx = zg_ref[:, GROUP:] + gb_ref[:, GROUP:]
    lf = jnp.minimum(fx, 0.0) - jnp.log1p(jnp.exp(-jnp.abs(fx)))
    rin = lax.broadcasted_iota(jnp.int32, (ML_ROWS, GROUP), 0) % L
    b = _chunk_scan(lf, rin, jnp.add, 0.0)
    a = ii - b
    ca = _chunk_scan(a, rin, jnp.maximum, NEG)

    bd = _group_mask(GROUP, GROUP, HEAD_DIM, HEAD_DIM)
    ones_bd = bd.astype(BF16)
    row = lax.broadcasted_iota(jnp.int32, (L, GROUP), 0)
    key = lax.broadcasted_iota(jnp.int32, (L, GROUP), 1) % L
    causal = key <= row
    diag = key == row

    m_prev = m_ref[...]
    for c in range(ML_ROWS // L):
        rs = slice(c * L, (c + 1) * L)
        q_c, k_c, v_c = q[rs], k[rs], v[rs]
        q_b = q_bf[rs]
        a_c, b_c = a[rs], b[rs]
        g = jnp.maximum(m_prev, ca[rs])
        g_last = g[L - 1:L]
        a_row = jnp.sum(jnp.where(diag, a_c, 0.0), axis=0, keepdims=True)
        decay = jnp.exp(jnp.where(causal, a_row - g, NEG))
        sc = _dot_nt(q_b, _tile_rows(k_bf[rs], HEADS, bd)) * decay
        inter = jnp.exp(m_prev - g)
        num = inter * _dot(q_b, c_ref[...].astype(BF16)) + _dot(sc.astype(BF16), _tile_rows(v_c, HEADS, bd))
        den = inter * _group_sum(q_c * n_ref[...], ones_bd) + _group_sum(sc, ones_bd)
        hh_ref[rs, :] = num / jnp.maximum(jnp.abs(den), jnp.exp(-(b_c + g)))
        kw = k_c * jnp.exp(a_c - g_last)
        carry = jnp.exp(m_prev - g_last)
        c_ref[...] = carry * c_ref[...] + jnp.where(bd, _dot_tn(kw.astype(BF16), v_c), 0.0)
        n_ref[...] = carry * n_ref[...] + jnp.sum(kw, axis=0, keepdims=True)
        m_prev = b_c[L - 1:L] + g_last
    m_ref[...] = m_prev

    hh = hh_ref[...]
    mu = _group_sum(hh, ones_bd) * (1.0 / HEAD_DIM)
    dev = hh - mu
    var = _group_sum(dev * dev, ones_bd) * (1.0 / HEAD_DIM)
    o_gate = jax.nn.sigmoid(za_ref[:, 3 * GROUP:])
    y_ref[...] = (dev * lax.rsqrt(var + RMS_EPS) * ng_ref[...] * o_gate).astype(BF16)


def _mlstm(za, zg, conv_w, conv_b, gate_b, norm_g, *, batch, seq):
    za = za.reshape(batch, seq, 4 * GROUP)
    zg = zg.reshape(batch, seq, 2 * GROUP)
    y = pl.pallas_call(
        _mlstm_body,
        grid=(batch, seq // ML_ROWS),
        in_specs=[pl.BlockSpec((None, ML_ROWS, 4 * GROUP), lambda b, j: (b, j, 0)),
                  pl.BlockSpec((None, ML_ROWS, 2 * GROUP), lambda b, j: (b, j, 0)),
                  _const_spec((CONV_WIDTH, 2 * GROUP)), _const_spec((1, 2 * GROUP)),
                  _const_spec((1, 2 * GROUP)), _const_spec((1, GROUP))],
        out_specs=pl.BlockSpec((None, ML_ROWS, GROUP), lambda b, j: (b, j, 0)),
        out_shape=jax.ShapeDtypeStruct((batch, seq, GROUP), BF16),
        scratch_shapes=[pltpu.VMEM((ML_ROWS + ML_HALO, 2 * GROUP), F32),
                        pltpu.VMEM((ML_HALO, 2 * GROUP), F32),
                        pltpu.VMEM((ML_ROWS, GROUP), F32),
                        pltpu.VMEM((GROUP, GROUP), F32),
                        pltpu.VMEM((1, GROUP), F32),
                        pltpu.VMEM((1, GROUP), F32)],
        compiler_params=_params("parallel", "arbitrary"),
        name="mlstm",
    )(za, zg, conv_w, conv_b, gate_b, norm_g)
    return y.reshape(batch * seq, GROUP)


POOL_ROWS = 1024
POOL_HALO = 16


def _pool_body(u_ref, w_ref, s_ref, y_ref, buf_ref, tail_ref):
    j = pl.program_id(1)

    @pl.when(j == 0)
    def _():
        tail_ref[...] = jnp.zeros_like(tail_ref)

    buf_ref[0:POOL_HALO, :] = tail_ref[...]
    buf_ref[POOL_HALO:, :] = u_ref[...]
    tail_ref[...] = u_ref[POOL_ROWS - POOL_HALO:, :]
    sums, s = [], buf_ref[...]
    for shift in (1, 2, 4, 8):
        s = s + pltpu.roll(s, shift, 0)
        sums.append(s[POOL_HALO:])
    u = u_ref[...]
    lane_group = lax.broadcasted_iota(jnp.int32, (POOL_ROWS, GROUP), 1) // HEAD_DIM
    t = j * POOL_ROWS + lax.broadcasted_iota(jnp.int32, (POOL_ROWS, GROUP), 0)
    total, win = sums[3], jnp.full((POOL_ROWS, GROUP), POOL_WINDOWS[3], jnp.int32)
    for gi in (2, 1, 0):
        total = jnp.where(lane_group == gi, sums[gi], total)
        win = jnp.where(lane_group == gi, POOL_WINDOWS[gi], win)
    mean = total / jnp.minimum(t + 1, win).astype(F32)
    y = _dot((mean - u).astype(BF16), w_ref[...]) * s_ref[...]
    y_ref[...] = y.astype(BF16)


def _pool(zp, w_bd, scale, *, batch, seq):
    zp = zp.reshape(batch, seq, GROUP)
    y = pl.pallas_call(
        _pool_body,
        grid=(batch, seq // POOL_ROWS),
        in_specs=[pl.BlockSpec((None, POOL_ROWS, GROUP), lambda b, j: (b, j, 0)),
                  _const_spec((GROUP, GROUP)), _const_spec((1, GROUP))],
        out_specs=pl.BlockSpec((None, POOL_ROWS, GROUP), lambda b, j: (b, j, 0)),
        out_shape=jax.ShapeDtypeStruct((batch, seq, GROUP), BF16),
        scratch_shapes=[pltpu.VMEM((POOL_ROWS + POOL_HALO, GROUP), F32),
                        pltpu.VMEM((POOL_HALO, GROUP), F32)],
        compiler_params=_params("parallel", "arbitrary"),
        name="pool",
    )(zp, w_bd, scale)
    return y.reshape(batch * seq, GROUP)


def _toeplitz_body(w_ref, o_ref):
    x = jnp.broadcast_to(w_ref[...], (TILE, 2 * TILE))
    o_ref[...] = pltpu.roll(x, 0, 1, stride=1, stride_axis=0)[:, :TILE]


def _toeplitz(rows):
    n = rows.shape[0]
    return pl.pallas_call(
        _toeplitz_body,
        grid=(n,),
        in_specs=[pl.BlockSpec((None, 1, 2 * TILE), lambda i: (i, 0, 0))],
        out_specs=pl.BlockSpec((None, TILE, TILE), lambda i: (i, 0, 0)),
        out_shape=jax.ShapeDtypeStruct((n, TILE, TILE), F32),
        compiler_params=_params("parallel"),
        name="toeplitz",
    )(rows.reshape(n, 1, 2 * TILE))


_TOEPLITZ_X = np.where(np.arange(2 * TILE) <= TILE, -np.arange(2 * TILE), 2 * TILE - np.arange(2 * TILE))


def _t5_bucket(dist):
    max_exact = T5_BUCKETS // 2
    d = jnp.maximum(dist, 1).astype(F32)
    large = max_exact + (jnp.log(d / max_exact) / math.log(T5_MAX_DIST / max_exact)
                         * (T5_BUCKETS - max_exact)).astype(jnp.int32)
    large = jnp.minimum(large, T5_BUCKETS - 1)
    return jnp.where(dist < max_exact, dist, large)


def _bias_tiles(t5_bias, seq):
    x = jnp.asarray(_TOEPLITZ_X, jnp.int32)
    rows = []
    for (w, d) in DIL_PATTERNS:
        tab = t5_bias[_t5_bucket(jnp.arange(DIL_BACK + 1) * d), :HEADS].T * LOG2E
        same = jnp.where(x <= 0, tab[:, jnp.clip(-x, 0, DIL_BACK)], NEG)
        nxt = jnp.where(x >= 0, tab[:, jnp.clip(DIL_BACK - x, 0, DIL_BACK)], NEG)
        rows.append(jnp.stack([same, nxt], axis=1))
    dil = _toeplitz(jnp.stack(rows).reshape(-1, 2 * TILE))
    dil = dil.reshape(len(DIL_PATTERNS), HEADS, 2, TILE, TILE).transpose(0, 1, 3, 2, 4)
    dil = dil.reshape(len(DIL_PATTERNS), HEADS * TILE, 2 * TILE)
    tab = t5_bias[_t5_bucket(jnp.arange(seq)), HEADS:].T * LOG2E
    noff = seq // TILE - DIFF_MIN_OFFSET
    dist = (jnp.arange(noff)[:, None] + DIFF_MIN_OFFSET) * TILE - x[None, :]
    rows = jnp.where(dist >= 0, tab[:, jnp.clip(dist, 0, seq - 1)], NEG)
    diff = _toeplitz(rows.transpose(1, 0, 2).reshape(-1, 2 * TILE))
    return dil, diff.reshape(noff, HEADS * TILE, TILE)


DIL_UNIT = 2 * TILE


def _dil_body(*refs, subs, dil, merge):
    if merge:
        x_ref, xp_ref, bias_ref, unperm_ref, o1_ref, l1_ref, o2_ref, l2_ref, y_ref = refs
    else:
        x_ref, xp_ref, bias_ref, unperm_ref, o_ref, lse_ref = refs
    per_class = [_dil_class(x_ref, xp_ref, bias_ref, rc=rc, subs=subs) for rc in range(dil)]
    piece = DIL_UNIT // dil
    class_o = [jnp.concatenate(pc[0], axis=0) for pc in per_class]
    class_l = [jnp.concatenate(pc[1], axis=0) for pc in per_class]
    for u in range(dil * subs * TILE // DIL_UNIT):
        take = lambda arrs: jnp.concatenate([a[u * piece:(u + 1) * piece] for a in arrs], axis=0)
        if dil == 1:
            o, lse = take(class_o), take(class_l)
        else:
            o_c = take(class_o).astype(BF16)
            l_c = take(class_l)
            l_hi = l_c.astype(BF16)
            l_lo = (l_c - l_hi.astype(F32)).astype(BF16)
            o = _dot(unperm_ref[...], o_c)
            lse = _dot(unperm_ref[...], l_hi) + _dot(unperm_ref[...], l_lo)
        rs = slice(u * DIL_UNIT, (u + 1) * DIL_UNIT)
        if merge:
            o1, l1 = o1_ref[rs, :].astype(F32), l1_ref[rs, :]
            o2, l2 = o2_ref[rs, :].astype(F32), l2_ref[rs, :]
            top = jnp.maximum(jnp.maximum(l1, l2), lse)
            w1, w2, w3 = jnp.exp2(l1 - top), jnp.exp2(l2 - top), jnp.exp2(lse - top)
            y_ref[rs, :] = ((w1 * o1 + w2 * o2 + w3 * o) / (w1 + w2 + w3)).astype(BF16)
        else:
            o_ref[rs, :] = o.astype(BF16)
            lse_ref[rs, :] = lse


def _dil_class(x_ref, xp_ref, bias_ref, *, rc, subs):
    first = pl.program_id(1) == 0
    kmask = _group_mask(HEADS * TILE, GROUP, TILE, HEAD_DIM)
    ones_rows = (lax.broadcasted_iota(jnp.int32, (AUG_ROWS - HEAD_DIM, TILE), 0) == 0).astype(BF16)
    rows_of = lambda sb: slice(sb * TILE, (sb + 1) * TILE)
    col_q, col_k, col_v = (slice((3 * rc + w) * GROUP, (3 * rc + w + 1) * GROUP) for w in range(3))

    s_same, s_next, vaug = {}, {}, {}
    for j in range(-1, subs):
        k_j = xp_ref[:, col_k] if j < 0 else x_ref[rows_of(j), col_k]
        v_j = xp_ref[:, col_v] if j < 0 else x_ref[rows_of(j), col_v]
        parts = ([0] if j >= 0 else []) + ([1] if j + 1 < subs else [])
        q_cat = jnp.concatenate([x_ref[rows_of(j + e), col_q] for e in parts], axis=0)
        bias = bias_ref[:, parts[0] * TILE:(parts[-1] + 1) * TILE]
        st = _dot_nt(_tile_rows(k_j, HEADS, kmask), q_cat) + bias
        if j < 0:
            st = st + jnp.where(first, NEG, 0.0)
        for pos, e in enumerate(parts):
            (s_same if e == 0 else s_next)[j + e] = st[:, pos * TILE:(pos + 1) * TILE]
        v_t = v_j.astype(F32).T.astype(BF16)
        vaug[j] = [jnp.concatenate([v_t[h * HEAD_DIM:(h + 1) * HEAD_DIM], ones_rows], axis=0)
                   for h in range(HEADS)]

    p_same, p_next, tops = {}, {}, {}
    for i in range(subs):
        ps, pn, tp = [], [], []
        for h in range(HEADS):
            hs = slice(h * TILE, (h + 1) * TILE)
            a, b = s_same[i][hs], s_next[i][hs]
            m = jnp.maximum(jnp.max(a, axis=0, keepdims=True), jnp.max(b, axis=0, keepdims=True))
            ps.append(jnp.exp2(a - m).astype(BF16))
            pn.append(jnp.exp2(b - m).astype(BF16))
            tp.append(m)
        p_same[i], p_next[i], tops[i] = ps, pn, tp

    acc = {i: [None] * HEADS for i in range(subs)}
    for j in range(-1, subs):
        for h in range(HEADS):
            cols = ([p_same[j][h]] if j >= 0 else []) + ([p_next[j + 1][h]] if j + 1 < subs else [])
            r = _dot(vaug[j][h], jnp.concatenate(cols, axis=1))
            targets = ([j] if j >= 0 else []) + ([j + 1] if j + 1 < subs else [])
            for pos, i in enumerate(targets):
                part = r[:, pos * TILE:(pos + 1) * TILE]
                acc[i][h] = part if acc[i][h] is None else acc[i][h] + part

    outs, lses = [], []
    for i in range(subs):
        o_t, lse_t = [], []
        for h in range(HEADS):
            l = acc[i][h][HEAD_DIM:HEAD_DIM + 1]
            o_t.append(acc[i][h][:HEAD_DIM] / l)
            lse_t.append(jnp.broadcast_to(tops[i][h] + jnp.log2(l), (HEAD_DIM, TILE)))
        outs.append(jnp.concatenate(o_t, axis=0).T)
        lses.append(jnp.concatenate(lse_t, axis=0).T)
    return outs, lses


DIL_SUBBLOCKS = 8


def _unpermutation(dil):
    t = np.arange(DIL_UNIT)
    mat = np.zeros((DIL_UNIT, DIL_UNIT), np.float32)
    mat[t, (t % dil) * (DIL_UNIT // dil) + t // dil] = 1.0
    return jnp.asarray(mat, BF16)


def _dilated_pattern(zc, bias, dil, *, batch, seq, merge_with=None):
    length = seq // dil
    subs = min(max(DIL_SUBBLOCKS // max(dil // 2, 1), 2), length // TILE)
    rows = subs * TILE
    tokens = rows * dil
    zc = zc.reshape(batch, length, dil * 3 * GROUP)
    blk = pl.BlockSpec((None, rows, dil * 3 * GROUP), lambda b, n: (b, n, 0))
    prev = pl.BlockSpec((None, TILE, dil * 3 * GROUP), lambda b, n: (b, jnp.maximum(n * subs - 1, 0), 0))
    nat = pl.BlockSpec((None, tokens, GROUP), lambda b, n: (b, n, 0))
    in_specs = [blk, prev, _const_spec((HEADS * TILE, 2 * TILE)), _const_spec((DIL_UNIT, DIL_UNIT))]
    args = [zc, zc, bias, _unpermutation(dil)]
    if merge_with is None:
        out_specs = [nat, nat]
        out_shape = [jax.ShapeDtypeStruct((batch, seq, GROUP), BF16),
                     jax.ShapeDtypeStruct((batch, seq, GROUP), F32)]
    else:
        in_specs += [nat] * len(merge_with)
        args += list(merge_with)
        out_specs = nat
        out_shape = jax.ShapeDtypeStruct((batch, seq, GROUP), BF16)
    out = pl.pallas_call(
        functools.partial(_dil_body, subs=subs, dil=dil, merge=merge_with is not None),
        grid=(batch, length // rows),
        in_specs=in_specs, out_specs=out_specs, out_shape=out_shape,
        compiler_params=_params("parallel", "parallel"),
        name=f"dilated_d{dil}",
    )(*args)
    if merge_with is None:
        return out
    return out.reshape(batch * seq, GROUP)


def _dilated(zc_views, dil_bias, *, batch, seq):
    o1, l1 = _dilated_pattern(zc_views[0], dil_bias[0], DIL_PATTERNS[0][1], batch=batch, seq=seq)
    o2, l2 = _dilated_pattern(zc_views[1], dil_bias[1], DIL_PATTERNS[1][1], batch=batch, seq=seq)
    return _dilated_pattern(zc_views[2], dil_bias[2], DIL_PATTERNS[2][1], batch=batch, seq=seq,
                            merge_with=(o1, l1, o2, l2))


DIFF_Q = 256
DIFF_K = 256
DIFF_GROUPS = 2 * HEADS


def _diff_body(q_ref, k_ref, vt_ref, bias_ref, lam_ref, sg_ref, y_ref,
               kexp_ref, vaug_ref, acc_ref, sta_ref, stb_ref, *, lam_init, key_steps):
    qi = pl.program_id(1)

    @pl.when(qi == 0)
    def _():
        grp = lax.broadcasted_iota(jnp.int32, (DIFF_GROUPS * DIFF_K, GROUP), 0) // DIFF_K
        slot = lax.broadcasted_iota(jnp.int32, (DIFF_GROUPS * DIFF_K, GROUP), 1) // DIFF_QK_HALF
        kmask = slot == 2 * (grp % HEADS) + grp // HEADS
        ones_rows = (lax.broadcasted_iota(jnp.int32, (AUG_ROWS - HEAD_DIM, DIFF_K), 0) == 0).astype(BF16)

        def build(j, carry):
            k_t = k_ref[pl.ds(pl.multiple_of(j * DIFF_K, DIFF_K), DIFF_K), :]
            kexp_ref[j] = jnp.where(kmask, jnp.concatenate([k_t] * DIFF_GROUPS, axis=0),
                                    jnp.zeros((), BF16))
            vt = jnp.concatenate([vt_ref[2 * j], vt_ref[2 * j + 1]], axis=1)
            for h in range(HEADS):
                vaug_ref[j, h] = jnp.concatenate([vt[h * HEAD_DIM:(h + 1) * HEAD_DIM], ones_rows], axis=0)
            return carry

        lax.fori_loop(0, key_steps, build, 0)

    acc_ref[...] = jnp.zeros_like(acc_ref)
    q = q_ref[...]
    last = key_steps - 1

    def scores(s_ref, j):
        base = (DIFF_Q // TILE) * qi - (DIFF_K // TILE) * j - DIFF_MIN_OFFSET
        tiles = {d: bias_ref[jnp.maximum(base + d, 0)]
                 for d in range(1 - DIFF_K // TILE, DIFF_Q // TILE)}
        raw = _dot_nt(kexp_ref[jnp.minimum(j, last)], q)
        tops = [None] * DIFF_GROUPS
        for h in range(HEADS):
            hs = slice(h * TILE, (h + 1) * TILE)
            bias = jnp.concatenate(
                [jnp.concatenate([tiles[a - b][hs] for a in range(DIFF_Q // TILE)], axis=1)
                 for b in range(DIFF_K // TILE)], axis=0)
            for g in (h, HEADS + h):
                s = raw[g * DIFF_K:(g + 1) * DIFF_K] + bias
                s_ref[g * DIFF_K:(g + 1) * DIFF_K, :] = s
                tops[g] = jnp.max(s, axis=0, keepdims=True)
        return tuple(tops)

    def consume(s_ref, tops, j, carry):
        ms, ls = carry
        jv = jnp.minimum(j, last)
        new_ms, new_ls = [], []
        for g in range(DIFF_GROUPS):
            mp, h = divmod(g, HEADS)
            m_new = jnp.maximum(ms[g], tops[g])
            p = jnp.exp2(s_ref[g * DIFF_K:(g + 1) * DIFF_K, :] - m_new).astype(BF16)
            alpha = jnp.exp2(ms[g] - m_new)
            r = _dot(vaug_ref[jv, h], p)
            acc_ref[mp, h] = alpha * acc_ref[mp, h] + r[:HEAD_DIM]
            new_ls.append(alpha * ls[g] + r[HEAD_DIM:HEAD_DIM + 1])
            new_ms.append(m_new)
        return tuple(new_ms), tuple(new_ls)

    def pair(jj, carry):
        tops_a, state = carry
        j = 2 * jj
        tops_b = scores(stb_ref, j + 1)
        state = consume(sta_ref, tops_a, j, state)
        tops_a = scores(sta_ref, j + 2)
        return tops_a, consume(stb_ref, tops_b, j + 1, state)

    init = (tuple(jnp.full((1, DIFF_Q), NEG, F32) for _ in range(DIFF_GROUPS)),
            tuple(jnp.zeros((1, DIFF_Q), F32) for _ in range(DIFF_GROUPS)))
    key_steps_needed = (qi + 1) * (DIFF_Q // DIFF_K)
    _, (_, ls) = lax.fori_loop(0, (key_steps_needed + 1) // 2, pair, (scores(sta_ref, 0), init))

    lv = lam_ref[...]
    lam = (jnp.exp(jnp.sum(lv[0:1] * lv[1:2], axis=-1, keepdims=True))
           - jnp.exp(jnp.sum(lv[2:3] * lv[3:4], axis=-1, keepdims=True)) + lam_init)
    outs = []
    for h in range(HEADS):
        o = acc_ref[0, h] / ls[h] - lam * (acc_ref[1, h] / ls[HEADS + h])
        ms_o = jnp.mean(o * o, axis=0, keepdims=True)
        outs.append(o * lax.rsqrt(ms_o + SUBLN_EPS) * sg_ref[...] * (1.0 - lam_init))
    y_ref[...] = jnp.concatenate(outs, axis=0).T.astype(BF16)


def _diff_attention(zd, vt, bias, lam_vecs, subln_cols, *, lam_init, batch, seq):
    zd = zd.reshape(batch, seq, 2 * GROUP)
    key_tiles = seq // TILE
    key_steps = seq // DIFF_K
    vt = vt.reshape(batch, key_tiles, GROUP, TILE)
    y = pl.pallas_call(
        functools.partial(_diff_body, lam_init=lam_init, key_steps=key_steps),
        grid=(batch, seq // DIFF_Q),
        in_specs=[pl.BlockSpec((None, DIFF_Q, GROUP), lambda b, i: (b, i, 0)),
                  pl.BlockSpec((None, seq, GROUP), lambda b, i: (b, 0, 1), pipeline_mode=pl.Buffered(1)),
                  pl.BlockSpec((None, key_tiles, GROUP, TILE), lambda b, i: (b, 0, 0, 0),
                               pipeline_mode=pl.Buffered(1)),
                  _const_spec((key_tiles - DIFF_MIN_OFFSET, HEADS * TILE, TILE)),
                  _const_spec((4, DIFF_QK_HALF)), _const_spec((HEAD_DIM, DIFF_Q))],
        out_specs=pl.BlockSpec((None, DIFF_Q, GROUP), lambda b, i: (b, i, 0)),
        out_shape=jax.ShapeDtypeStruct((batch, seq, GROUP), BF16),
        scratch_shapes=[pltpu.VMEM((key_steps, DIFF_GROUPS * DIFF_K, GROUP), BF16),
                        pltpu.VMEM((key_steps, HEADS, AUG_ROWS, DIFF_K), BF16),
                        pltpu.VMEM((2, HEADS, HEAD_DIM, DIFF_Q), F32),
                        pltpu.VMEM((DIFF_GROUPS * DIFF_K, DIFF_Q), F32),
                        pltpu.VMEM((DIFF_GROUPS * DIFF_K, DIFF_Q), F32)],
        compiler_params=_params("parallel", "arbitrary"),
        name="diff_attn",
    )(zd, zd, vt, bias, lam_vecs, subln_cols)
    return y.reshape(batch * seq, GROUP)


KV_ROWS = 512


def _mem_kv_body(m_ref, g_ref, w_ref, k_ref, v_ref):
    u = _rms(m_ref[...], g_ref[...]).astype(BF16)
    for c in range(D_MODEL // GROUP):
        sl = slice(c * GROUP, (c + 1) * GROUP)
        k_ref[:, sl] = _dot(u, w_ref[:, sl]).astype(BF16)
        v_ref[:, sl] = _dot(u, w_ref[:, D_MODEL + c * GROUP: D_MODEL + (c + 1) * GROUP]).astype(BF16)


def _mem_kv(mem, g, w):
    n = mem.shape[0]
    row = pl.BlockSpec((KV_ROWS, D_MODEL), lambda i: (i, 0))
    return pl.pallas_call(
        _mem_kv_body,
        grid=(n // KV_ROWS,),
        in_specs=[row, _const_spec((1, D_MODEL)), _const_spec((D_MODEL, 2 * D_MODEL))],
        out_specs=[row, row],
        out_shape=[jax.ShapeDtypeStruct((n, D_MODEL), BF16)] * 2,
        compiler_params=_params("parallel"),
        name="mem_kv",
    )(mem, g, w)


XATTN_ROWS = 512


def _xattn_body(x_ref, ya_ref, yb_ref, yc_ref, yd_ref, wout_ref, g_ref, wq_ref, k_ref, v_ref, wo_ref,
                o_ref, q_scr, a_scr):
    x = x_ref[...]
    for gi, y_ref in enumerate((ya_ref, yb_ref, yc_ref, yd_ref)):
        x = x + _dot(y_ref[...], wout_ref[gi * GROUP:(gi + 1) * GROUP, :])
    u = _rms(x, g_ref[...]).astype(BF16)
    for c in range(D_MODEL // GROUP):
        sl = slice(c * GROUP, (c + 1) * GROUP)
        q_scr[:, sl] = _dot(u, wq_ref[:, sl]).astype(BF16)
    for h in range(MEM_HEADS):
        sl = slice(h * MEM_HEAD_DIM, (h + 1) * MEM_HEAD_DIM)
        s = _dot_nt(q_scr[:, sl], k_ref[:, sl]) * (MEM_HEAD_DIM ** -0.5)
        e = jnp.exp(s - jnp.max(s, axis=-1, keepdims=True))
        l = jnp.sum(e, axis=-1, keepdims=True)
        a_scr[:, sl] = (_dot(e.astype(BF16), v_ref[:, sl]) / l).astype(BF16)
    o_ref[...] = x + _dot(a_scr[...], wo_ref[...])


def _xattn(h, ys, w_out, g, wq, k, v, wo, *, batch, seq):
    h3 = h.reshape(batch, seq, D_MODEL)
    ys = [y.reshape(batch, seq, GROUP) for y in ys]
    k3 = k.reshape(batch, MEM_LEN, D_MODEL)
    v3 = v.reshape(batch, MEM_LEN, D_MODEL)
    row = pl.BlockSpec((None, XATTN_ROWS, D_MODEL), lambda b, i: (b, i, 0))
    grp = pl.BlockSpec((None, XATTN_ROWS, GROUP), lambda b, i: (b, i, 0))
    mem = pl.BlockSpec((None, MEM_LEN, D_MODEL), lambda b, i: (b, 0, 0))
    weight = _const_spec((D_MODEL, D_MODEL))
    out = pl.pallas_call(
        _xattn_body,
        grid=(batch, seq // XATTN_ROWS),
        in_specs=[row, grp, grp, grp, grp, weight, _const_spec((1, D_MODEL)), weight, mem, mem, weight],
        out_specs=row,
        out_shape=jax.ShapeDtypeStruct((batch, seq, D_MODEL), F32),
        scratch_shapes=[pltpu.VMEM((XATTN_ROWS, D_MODEL), BF16), pltpu.VMEM((XATTN_ROWS, D_MODEL), BF16)],
        compiler_params=_params("parallel", "parallel"),
        name="xattn",
    )(h3, *ys, w_out, g, wq, k3, v3, wo)
    return out.reshape(batch * seq, D_MODEL)


def _per_head_lanes(x):
    return jnp.repeat(x, HEAD_DIM, axis=-1)


def _in_proj_weight(w_in):
    g = GROUP
    q_a, k_a, v_a, o_a = (w_in[:, i * g:(i + 1) * g] for i in range(4))
    ig = w_in[:, 4 * g:4 * g + HEADS]
    fg = w_in[:, 4 * g + HEADS:4 * g + 2 * HEADS]
    rest = w_in[:, 4 * g + 2 * HEADS:]
    pool, q_c, k_c, v_c, q_d, k_d, v_d = (rest[:, i * g:(i + 1) * g] for i in range(7))
    q_c = q_c * DIL_SCORE_SCALE
    k_d = k_d * DIFF_SCORE_SCALE
    cols = [q_a, k_a, v_a, o_a, _per_head_lanes(ig), _per_head_lanes(fg), pool,
            q_c, k_c, v_c, q_d, k_d]
    return jnp.concatenate(cols, axis=1).astype(BF16), v_d.astype(BF16)


def _block_diag(w):
    g, c, _ = w.shape
    eye = jnp.eye(g, dtype=w.dtype)
    return (eye[:, None, :, None] * w[:, :, None, :]).reshape(g * c, g * c)


def kernel(x, mem, t5_bias, ffn1_norm, ffn1_w_gate, ffn1_w_up, ffn1_w_down, mix_norm, w_in,
           mlstm_conv_w, mlstm_conv_b, mlstm_gate_b, mlstm_norm, pool_w, pool_scale,
           diff_lambda, diff_subln, w_out, xattn_norm, mem_norm, xattn_wq, xattn_wkv, xattn_wo,
           ffn2_norm, ffn2_w_gate, ffn2_w_up, ffn2_w_down, final_norm):
    batch, seq, _ = x.shape
    n = batch * seq
    dil_bias, diff_bias = _bias_tiles(t5_bias, seq)
    h = x.reshape(n, D_MODEL)
    mem2 = mem.reshape(batch * MEM_LEN, D_MODEL)
    row = lambda v: v.reshape(1, -1)
    for l in range(DEPTH):
        lam_init = 0.8 - 0.6 * math.exp(-0.3 * l)
        h = _ffn(h, row(ffn1_norm[l]), ffn1_w_gate[l].astype(BF16), ffn1_w_up[l].astype(BF16),
                 ffn1_w_down[l].astype(BF16), row(final_norm), final=False)
        za, zg, zp, zc, zd, vt, zc4, zc16 = _in_proj(h, row(mix_norm[l]), *_in_proj_weight(w_in[l]))
        ya = _mlstm(za, zg, mlstm_conv_w[l], row(mlstm_conv_b[l]),
                    row(_per_head_lanes(mlstm_gate_b[l].reshape(2, HEADS))), row(mlstm_norm[l]),
                    batch=batch, seq=seq)
        yb = _pool(zp, _block_diag(pool_w[l]).astype(BF16), row(pool_scale[l]), batch=batch, seq=seq)
        yc = _dilated((zc, zc4, zc16), dil_bias, batch=batch, seq=seq)
        yd = _diff_attention(zd, vt, diff_bias, diff_lambda[l],
                             jnp.broadcast_to(diff_subln[l][:, None], (HEAD_DIM, DIFF_Q)),
                             lam_init=lam_init, batch=batch, seq=seq)
        k_mem, v_mem = _mem_kv(mem2, row(mem_norm[l]), xattn_wkv[l].astype(BF16))
        h = _xattn(h, (ya, yb, yc, yd), w_out[l].astype(BF16),
                   row(xattn_norm[l]), xattn_wq[l].astype(BF16), k_mem, v_mem,
                   xattn_wo[l].astype(BF16), batch=batch, seq=seq)
        h = _ffn(h, row(ffn2_norm[l]), ffn2_w_gate[l].astype(BF16), ffn2_w_up[l].astype(BF16),
                 ffn2_w_down[l].astype(BF16), row(final_norm), final=(l == DEPTH - 1))
    return h.reshape(batch, seq, D_MODEL)
```

```python
import functools
import math

import jax
import jax.numpy as jnp
import numpy as np
from jax import lax
from jax.experimental import pallas as pl
from jax.experimental.pallas import tpu as pltpu

F32 = jnp.float32
BF16 = jnp.bfloat16

D_MODEL = 1024
D_FF = 2816
DEPTH = 4
GROUP = 256
HEADS = 4
HEAD_DIM = GROUP // HEADS
MEM_LEN = 256
MEM_HEADS = 4
MEM_HEAD_DIM = D_MODEL // MEM_HEADS
MLSTM_CHUNK = 64
CONV_WIDTH = 4
POOL_WINDOWS = (2, 4, 8, 16)
DIL_PATTERNS = ((128, 1), (512, 4), (2048, 16))
DIL_BACK = 128
DIFF_QK_HALF = HEAD_DIM // 2
T5_BUCKETS = 32
T5_MAX_DIST = 2048
RMS_EPS = 1e-6
SUBLN_EPS = 1e-5
NEG = -1e30
LOG2E = math.log2(math.e)
DIFF_SCORE_SCALE = (DIFF_QK_HALF ** -0.5) * LOG2E
DIL_SCORE_SCALE = (HEAD_DIM ** -0.5) * LOG2E
AUG_ROWS = HEAD_DIM + 16
DIFF_MIN_OFFSET = -3
TILE = 128

VMEM_LIMIT_BYTES = 56 * 1024 * 1024


def _rms(xf, g, eps=RMS_EPS):
    return xf * lax.rsqrt(jnp.mean(xf * xf, axis=-1, keepdims=True) + eps) * g


def _const_spec(shape):
    zeros = (0,) * len(shape)
    return pl.BlockSpec(shape, lambda *_: zeros, pipeline_mode=pl.Buffered(1))


def _params(*sem):
    return pltpu.CompilerParams(dimension_semantics=sem, vmem_limit_bytes=VMEM_LIMIT_BYTES)


def _group_mask(rows, cols, row_group, col_group):
    r = lax.broadcasted_iota(jnp.int32, (rows, cols), 0) // row_group
    c = lax.broadcasted_iota(jnp.int32, (rows, cols), 1) // col_group
    return r == c


def _tile_rows(x, reps, mask):
    return jnp.where(mask, jnp.concatenate([x] * reps, axis=0), jnp.zeros((), x.dtype))


def _dot(a, b):
    return jnp.dot(a, b, preferred_element_type=F32)


def _dot_nt(a, b):
    return lax.dot_general(a, b, (((1,), (1,)), ((), ())), preferred_element_type=F32)


def _dot_tn(a, b):
    return lax.dot_general(a, b, (((0,), (0,)), ((), ())), preferred_element_type=F32)


def _group_sum(x, ones_bd):
    hi = x.astype(BF16)
    lo = (x - hi.astype(F32)).astype(BF16)
    return _dot(hi, ones_bd) + _dot(lo, ones_bd)


FFN_ROWS = 512
FFN_COLS = 256


def _ffn_body(x_ref, g_ref, wg_ref, wu_ref, wd_ref, fg_ref, o_ref, act_ref, *, final):
    x = x_ref[...]
    u = _rms(x, g_ref[...]).astype(BF16)
    for c in range(D_FF // FFN_COLS):
        sl = slice(c * FFN_COLS, (c + 1) * FFN_COLS)
        gate = _dot(u, wg_ref[:, sl])
        up = _dot(u, wu_ref[:, sl])
        act_ref[:, sl] = (gate * jax.nn.sigmoid(gate) * up).astype(BF16)
    y = x + 0.5 * _dot(act_ref[...], wd_ref[...])
    if final:
        y = _rms(y, fg_ref[...])
    o_ref[...] = y


def _ffn(h, g, wg, wu, wd, fg, *, final):
    n = h.shape[0]
    row = pl.BlockSpec((FFN_ROWS, D_MODEL), lambda i: (i, 0))
    return pl.pallas_call(
        functools.partial(_ffn_body, final=final),
        grid=(n // FFN_ROWS,),
        in_specs=[row, _const_spec((1, D_MODEL)), _const_spec((D_MODEL, D_FF)),
                  _const_spec((D_MODEL, D_FF)), _const_spec((D_FF, D_MODEL)),
                  _const_spec((1, D_MODEL))],
        out_specs=row,
        out_shape=jax.ShapeDtypeStruct((n, D_MODEL), F32),
        scratch_shapes=[pltpu.VMEM((FFN_ROWS, D_FF), BF16)],
        compiler_params=_params("parallel"),
        name="ffn_final" if final else "ffn",
    )(h, g, wg, wu, wd, fg)


PROJ_ROWS = 512
PROJ_OUTS = (("a", 4 * GROUP, F32), ("g", 2 * GROUP, F32), ("p", GROUP, F32),
             ("c", 3 * GROUP, BF16), ("d", 2 * GROUP, BF16))
PROJ_WIDTH = sum(w for _, w, _ in PROJ_OUTS)


PROJ_DILATIONS = tuple(d for _, d in DIL_PATTERNS if d > 1)


def _in_proj_body(x_ref, g_ref, w_ref, wv_ref, perm_ref, *o_refs):
    u = _rms(x_ref[...], g_ref[...]).astype(BF16)
    off = 0
    for o_ref, (_, width, dtype) in zip(o_refs, PROJ_OUTS):
        for c in range(width // GROUP):
            z = _dot(u, w_ref[:, off + c * GROUP: off + (c + 1) * GROUP])
            o_ref[:, c * GROUP:(c + 1) * GROUP] = z.astype(dtype)
        off += width
    vt_ref = o_refs[len(PROJ_OUTS)]
    v = _dot(u, wv_ref[...])
    for t in range(PROJ_ROWS // TILE):
        vt_ref[t] = v[t * TILE:(t + 1) * TILE].T.astype(BF16)
    for pi, d in enumerate(PROJ_DILATIONS):
        per_class = PROJ_UNIT // d
        for un in range(PROJ_ROWS // PROJ_UNIT):
            zp = _dot(perm_ref[pi], o_refs[3][un * PROJ_UNIT:(un + 1) * PROJ_UNIT, :]).astype(BF16)
            for r in range(d):
                o_refs[len(PROJ_OUTS) + 1 + pi][un * per_class:(un + 1) * per_class,
                                                r * 3 * GROUP:(r + 1) * 3 * GROUP] = \
                    zp[r * per_class:(r + 1) * per_class]


PROJ_UNIT = 2 * TILE


def _class_permutations():
    mats = np.zeros((len(PROJ_DILATIONS), PROJ_UNIT, PROJ_UNIT), np.float32)
    for pi, d in enumerate(PROJ_DILATIONS):
        t = np.arange(PROJ_UNIT)
        mats[pi, (t % d) * (PROJ_UNIT // d) + t // d, t] = 1.0
    return jnp.asarray(mats, BF16)


def _in_proj(h, g, w, wv):
    n = h.shape[0]
    tiles = PROJ_ROWS // TILE
    return pl.pallas_call(
        _in_proj_body,
        grid=(n // PROJ_ROWS,),
        in_specs=[pl.BlockSpec((PROJ_ROWS, D_MODEL), lambda i: (i, 0)),
                  _const_spec((1, D_MODEL)), _const_spec((D_MODEL, PROJ_WIDTH)),
                  _const_spec((D_MODEL, GROUP)),
                  _const_spec((len(PROJ_DILATIONS), PROJ_UNIT, PROJ_UNIT))],
        out_specs=[pl.BlockSpec((PROJ_ROWS, w_), lambda i: (i, 0)) for _, w_, _ in PROJ_OUTS]
        + [pl.BlockSpec((tiles, GROUP, TILE), lambda i: (i, 0, 0))]
        + [pl.BlockSpec((PROJ_ROWS // d, d * 3 * GROUP), lambda i: (i, 0)) for d in PROJ_DILATIONS],
        out_shape=[jax.ShapeDtypeStruct((n, w_), dt) for _, w_, dt in PROJ_OUTS]
        + [jax.ShapeDtypeStruct((n // TILE, GROUP, TILE), BF16)]
        + [jax.ShapeDtypeStruct((n // d, d * 3 * GROUP), BF16) for d in PROJ_DILATIONS],
        compiler_params=_params("parallel"),
        name="in_proj",
    )(h, g, w, wv, _class_permutations())


ML_ROWS = 512
ML_HALO = 8


def _chunk_scan(x, rin, op, fill):
    s = 1
    while s < MLSTM_CHUNK:
        x = op(x, jnp.where(rin >= s, pltpu.roll(x, s, 0), fill))
        s *= 2
    return x


def _mlstm_body(za_ref, zg_ref, cw_ref, cb_ref, gb_ref, ng_ref, y_ref,
                buf_ref, tail_ref, hh_ref, c_ref, n_ref, m_ref):
    L = MLSTM_CHUNK

    @pl.when(pl.program_id(1) == 0)
    def _():
        tail_ref[...] = jnp.zeros_like(tail_ref)
        c_ref[...] = jnp.zeros_like(c_ref)
        n_ref[...] = jnp.zeros_like(n_ref)
        m_ref[...] = jnp.zeros_like(m_ref)

    buf_ref[0:ML_HALO, :] = tail_ref[...]
    buf_ref[ML_HALO:, :] = za_ref[:, 0:2 * GROUP]
    tail_ref[...] = za_ref[ML_ROWS - ML_HALO:, 0:2 * GROUP]
    conv = cb_ref[...]
    for j in range(CONV_WIDTH):
        conv = conv + buf_ref[pl.ds(ML_HALO - (CONV_WIDTH - 1) + j, ML_ROWS), :] * cw_ref[j:j + 1, :]
    qk = conv * jax.nn.sigmoid(conv)
    q = qk[:, :GROUP]
    k = qk[:, GROUP:] * (HEAD_DIM ** -0.5)
    v = za_ref[:, 2 * GROUP:3 * GROUP].astype(BF16)
    q_bf, k_bf = q.astype(BF16), k.astype(BF16)

    ii = zg_ref[:, :GROUP] + gb_ref[:, :GROUP]
    fx = zg_ref[:, GROUP:] + gb_ref[:, GROUP:]
    lf = jnp.minimum(fx, 0.0) - jnp.log1p(jnp.exp(-jnp.abs(fx)))
    rin = lax.broadcasted_iota(jnp.int32, (ML_ROWS, GROUP), 0) % L
    b = _chunk_scan(lf, rin, jnp.add, 0.0)
    a = ii - b
    ca = _chunk_scan(a, rin, jnp.maximum, NEG)

    bd = _group_mask(GROUP, GROUP, HEAD_DIM, HEAD_DIM)
    ones_bd = bd.astype(BF16)
    row = lax.broadcasted_iota(jnp.int32, (L, GROUP), 0)
    key = lax.broadcasted_iota(jnp.int32, (L, GROUP), 1) % L
    causal = key <= row
    diag = key == row

    m_prev = m_ref[...]
    for c in range(ML_ROWS // L):
        rs = slice(c * L, (c + 1) * L)
        q_c, k_c, v_c = q[rs], k[rs], v[rs]
        q_b = q_bf[rs]
        a_c, b_c = a[rs], b[rs]
        g = jnp.maximum(m_prev, ca[rs])
        g_last = g[L - 1:L]
        a_row = jnp.sum(jnp.where(diag, a_c, 0.0), axis=0, keepdims=True)
        decay = jnp.exp(jnp.where(causal, a_row - g, NEG))
        sc = _dot_nt(q_b, _tile_rows(k_bf[rs], HEADS, bd)) * decay
        inter = jnp.exp(m_prev - g)
        num = inter * _dot(q_b, c_ref[...].astype(BF16)) + _dot(sc.astype(BF16), _tile_rows(v_c, HEADS, bd))
        den = inter * _group_sum(q_c * n_ref[...], ones_bd) + _group_sum(sc, ones_bd)
        hh_ref[rs, :] = num / jnp.maximum(jnp.abs(den), jnp.exp(-(b_c + g)))
        kw = k_c * jnp.exp(a_c - g_last)
        carry = jnp.exp(m_prev - g_last)
        c_ref[...] = carry * c_ref[...] + jnp.where(bd, _dot_tn(kw.astype(BF16), v_c), 0.0)
        n_ref[...] = carry * n_ref[...] + jnp.sum(kw, axis=0, keepdims=True)
        m_prev = b_c[L - 1:L] + g_last
    m_ref[...] = m_prev

    hh = hh_ref[...]
    mu = _group_sum(hh, ones_bd) * (1.0 / HEAD_DIM)
    dev = hh - mu
    var = _group_sum(dev * dev, ones_bd) * (1.0 / HEAD_DIM)
    o_gate = jax.nn.sigmoid(za_ref[:, 3 * GROUP:])
    y_ref[...] = (dev * lax.rsqrt(var + RMS_EPS) * ng_ref[...] * o_gate).astype(BF16)


def _mlstm(za, zg, conv_w, conv_b, gate_b, norm_g, *, batch, seq):
    za = za.reshape(batch, seq, 4 * GROUP)
    zg = zg.reshape(batch, seq, 2 * GROUP)
    y = pl.pallas_call(
        _mlstm_body,
        grid=(batch, seq // ML_ROWS),
        in_specs=[pl.BlockSpec((None, ML_ROWS, 4 * GROUP), lambda b, j: (b, j, 0)),
                  pl.BlockSpec((None, ML_ROWS, 2 * GROUP), lambda b, j: (b, j, 0)),
                  _const_spec((CONV_WIDTH, 2 * GROUP)), _const_spec((1, 2 * GROUP)),
                  _const_spec((1, 2 * GROUP)), _const_spec((1, GROUP))],
        out_specs=pl.BlockSpec((None, ML_ROWS, GROUP), lambda b, j: (b, j, 0)),
        out_shape=jax.ShapeDtypeStruct((batch, seq, GROUP), BF16),
        scratch_shapes=[pltpu.VMEM((ML_ROWS + ML_HALO, 2 * GROUP), F32),
                        pltpu.VMEM((ML_HALO, 2 * GROUP), F32),
                        pltpu.VMEM((ML_ROWS, GROUP), F32),
                        pltpu.VMEM((GROUP, GROUP), F32),
                        pltpu.VMEM((1, GROUP), F32),
                        pltpu.VMEM((1, GROUP), F32)],
        compiler_params=_params("parallel", "arbitrary"),
        name="mlstm",
    )(za, zg, conv_w, conv_b, gate_b, norm_g)
    return y.reshape(batch * seq, GROUP)


POOL_ROWS = 1024
POOL_HALO = 16


def _pool_body(u_ref, w_ref, s_ref, y_ref, buf_ref, tail_ref):
    j = pl.program_id(1)

    @pl.when(j == 0)
    def _():
        tail_ref[...] = jnp.zeros_like(tail_ref)

    buf_ref[0:POOL_HALO, :] = tail_ref[...]
    buf_ref[POOL_HALO:, :] = u_ref[...]
    tail_ref[...] = u_ref[POOL_ROWS - POOL_HALO:, :]
    sums, s = [], buf_ref[...]
    for shift in (1, 2, 4, 8):
        s = s + pltpu.roll(s, shift, 0)
        sums.append(s[POOL_HALO:])
    u = u_ref[...]
    lane_group = lax.broadcasted_iota(jnp.int32, (POOL_ROWS, GROUP), 1) // HEAD_DIM
    t = j * POOL_ROWS + lax.broadcasted_iota(jnp.int32, (POOL_ROWS, GROUP), 0)
    total, win = sums[3], jnp.full((POOL_ROWS, GROUP), POOL_WINDOWS[3], jnp.int32)
    for gi in (2, 1, 0):
        total = jnp.where(lane_group == gi, sums[gi], total)
        win = jnp.where(lane_group == gi, POOL_WINDOWS[gi], win)
    mean = total / jnp.minimum(t + 1, win).astype(F32)
    y = _dot((mean - u).astype(BF16), w_ref[...]) * s_ref[...]
    y_ref[...] = y.astype(BF16)


def _pool(zp, w_bd, scale, *, batch, seq):
    zp = zp.reshape(batch, seq, GROUP)
    y = pl.pallas_call(
        _pool_body,
        grid=(batch, seq // POOL_ROWS),
        in_specs=[pl.BlockSpec((None, POOL_ROWS, GROUP), lambda b, j: (b, j, 0)),
                  _const_spec((GROUP, GROUP)), _const_spec((1, GROUP))],
        out_specs=pl.BlockSpec((None, POOL_ROWS, GROUP), lambda b, j: (b, j, 0)),
        out_shape=jax.ShapeDtypeStruct((batch, seq, GROUP), BF16),
        scratch_shapes=[pltpu.VMEM((POOL_ROWS + POOL_HALO, GROUP), F32),
                        pltpu.VMEM((POOL_HALO, GROUP), F32)],
        compiler_params=_params("parallel", "arbitrary"),
        name="pool",
    )(zp, w_bd, scale)
    return y.reshape(batch * seq, GROUP)


def _toeplitz_body(w_ref, o_ref):
    x = jnp.broadcast_to(w_ref[...], (TILE, 2 * TILE))
    o_ref[...] = pltpu.roll(x, 0, 1, stride=1, stride_axis=0)[:, :TILE]


def _toeplitz(rows):
    n = rows.shape[0]
    return pl.pallas_call(
        _toeplitz_body,
        grid=(n,),
        in_specs=[pl.BlockSpec((None, 1, 2 * TILE), lambda i: (i, 0, 0))],
        out_specs=pl.BlockSpec((None, TILE, TILE), lambda i: (i, 0, 0)),
        out_shape=jax.ShapeDtypeStruct((n, TILE, TILE), F32),
        compiler_params=_params("parallel"),
        name="toeplitz",
    )(rows.reshape(n, 1, 2 * TILE))


_TOEPLITZ_X = np.where(np.arange(2 * TILE) <= TILE, -np.arange(2 * TILE), 2 * TILE - np.arange(2 * TILE))


def _t5_bucket(dist):
    max_exact = T5_BUCKETS // 2
    d = jnp.maximum(dist, 1).astype(F32)
    large = max_exact + (jnp.log(d / max_exact) / math.log(T5_MAX_DIST / max_exact)
                         * (T5_BUCKETS - max_exact)).astype(jnp.int32)
    large = jnp.minimum(large, T5_BUCKETS - 1)
    return jnp.where(dist < max_exact, dist, large)


def _bias_tiles(t5_bias, seq):
    x = jnp.asarray(_TOEPLITZ_X, jnp.int32)
    rows = []
    for (w, d) in DIL_PATTERNS:
        tab = t5_bias[_t5_bucket(jnp.arange(DIL_BACK + 1) * d), :HEADS].T * LOG2E
        same = jnp.where(x <= 0, tab[:, jnp.clip(-x, 0, DIL_BACK)], NEG)
        nxt = jnp.where(x >= 0, tab[:, jnp.clip(DIL_BACK - x, 0, DIL_BACK)], NEG)
        rows.append(jnp.stack([same, nxt], axis=1))
    dil = _toeplitz(jnp.stack(rows).reshape(-1, 2 * TILE))
    dil = dil.reshape(len(DIL_PATTERNS), HEADS, 2, TILE, TILE).transpose(0, 1, 3, 2, 4)
    dil = dil.reshape(len(DIL_PATTERNS), HEADS * TILE, 2 * TILE)
    tab = t5_bias[_t5_bucket(jnp.arange(seq)), HEADS:].T * LOG2E
    noff = seq // TILE - DIFF_MIN_OFFSET
    dist = (jnp.arange(noff)[:, None] + DIFF_MIN_OFFSET) * TILE - x[None, :]
    rows = jnp.where(dist >= 0, tab[:, jnp.clip(dist, 0, seq - 1)], NEG)
    diff = _toeplitz(rows.transpose(1, 0, 2).reshape(-1, 2 * TILE))
    room = jnp.maximum(DIFF_SAFE_LOG2 - jnp.max(jnp.abs(tab)), 0.0)
    return dil, diff.reshape(noff, HEADS * TILE, TILE), (room * room).reshape(1, 1)


DIL_UNIT = 2 * TILE


def _dil_body(*refs, subs, dil, merge):
    if merge:
        x_ref, xp_ref, bias_ref, unperm_ref, o1_ref, l1_ref, o2_ref, l2_ref, y_ref = refs
    else:
        x_ref, xp_ref, bias_ref, unperm_ref, o_ref, lse_ref = refs
    per_class = [_dil_class(x_ref, xp_ref, bias_ref, rc=rc, subs=subs) for rc in range(dil)]
    piece = DIL_UNIT // dil
    class_o = [jnp.concatenate(pc[0], axis=0) for pc in per_class]
    class_l = [jnp.concatenate(pc[1], axis=0) for pc in per_class]
    for u in range(dil * subs * TILE // DIL_UNIT):
        take = lambda arrs: jnp.concatenate([a[u * piece:(u + 1) * piece] for a in arrs], axis=0)
        if dil == 1:
            o, lse = take(class_o), take(class_l)
        else:
            o_c = take(class_o).astype(BF16)
            l_c = take(class_l)
            l_hi = l_c.astype(BF16)
            l_lo = (l_c - l_hi.astype(F32)).astype(BF16)
            o = _dot(unperm_ref[...], o_c)
            lse = _dot(unperm_ref[...], l_hi) + _dot(unperm_ref[...], l_lo)
        rs = slice(u * DIL_UNIT, (u + 1) * DIL_UNIT)
        if merge:
            o1, l1 = o1_ref[rs, :].astype(F32), l1_ref[rs, :]
            o2, l2 = o2_ref[rs, :].astype(F32), l2_ref[rs, :]
            top = jnp.maximum(jnp.maximum(l1, l2), lse)
            w1, w2, w3 = jnp.exp2(l1 - top), jnp.exp2(l2 - top), jnp.exp2(lse - top)
            y_ref[rs, :] = ((w1 * o1 + w2 * o2 + w3 * o) / (w1 + w2 + w3)).astype(BF16)
        else:
            o_ref[rs, :] = o.astype(BF16)
            lse_ref[rs, :] = lse


def _dil_class(x_ref, xp_ref, bias_ref, *, rc, subs):
    first = pl.program_id(1) == 0
    kmask = _group_mask(HEADS * TILE, GROUP, TILE, HEAD_DIM)
    ones_rows = (lax.broadcasted_iota(jnp.int32, (AUG_ROWS - HEAD_DIM, TILE), 0) == 0).astype(BF16)
    rows_of = lambda sb: slice(sb * TILE, (sb + 1) * TILE)
    col_q, col_k, col_v = (slice((3 * rc + w) * GROUP, (3 * rc + w + 1) * GROUP) for w in range(3))

    s_same, s_next, vaug = {}, {}, {}
    for j in range(-1, subs):
        k_j = xp_ref[:, col_k] if j < 0 else x_ref[rows_of(j), col_k]
        v_j = xp_ref[:, col_v] if j < 0 else x_ref[rows_of(j), col_v]
        parts = ([0] if j >= 0 else []) + ([1] if j + 1 < subs else [])
        q_cat = jnp.concatenate([x_ref[rows_of(j + e), col_q] for e in parts], axis=0)
        bias = bias_ref[:, parts[0] * TILE:(parts[-1] + 1) * TILE]
        st = _dot_nt(_tile_rows(k_j, HEADS, kmask), q_cat) + bias
        if j < 0:
            st = st + jnp.where(first, NEG, 0.0)
        for pos, e in enumerate(parts):
            (s_same if e == 0 else s_next)[j + e] = st[:, pos * TILE:(pos + 1) * TILE]
        v_t = v_j.astype(F32).T.astype(BF16)
        vaug[j] = [jnp.concatenate([v_t[h * HEAD_DIM:(h + 1) * HEAD_DIM], ones_rows], axis=0)
                   for h in range(HEADS)]

    p_same, p_next, tops = {}, {}, {}
    for i in range(subs):
        ps, pn, tp = [], [], []
        for h in range(HEADS):
            hs = slice(h * TILE, (h + 1) * TILE)
            a, b = s_same[i][hs], s_next[i][hs]
            m = jnp.maximum(jnp.max(a, axis=0, keepdims=True), jnp.max(b, axis=0, keepdims=True))
            ps.append(jnp.exp2(a - m).astype(BF16))
            pn.append(jnp.exp2(b - m).astype(BF16))
            tp.append(m)
        p_same[i], p_next[i], tops[i] = ps, pn, tp

    acc = {i: [None] * HEADS for i in range(subs)}
    for j in range(-1, subs):
        for h in range(HEADS):
            cols = ([p_same[j][h]] if j >= 0 else []) + ([p_next[j + 1][h]] if j + 1 < subs else [])
            r = _dot(vaug[j][h], jnp.concatenate(cols, axis=1))
            targets = ([j] if j >= 0 else []) + ([j + 1] if j + 1 < subs else [])
            for pos, i in enumerate(targets):
                part = r[:, pos * TILE:(pos + 1) * TILE]
                acc[i][h] = part if acc[i][h] is None else acc[i][h] + part

    outs, lses = [], []
    for i in range(subs):
        o_t, lse_t = [], []
        for h in range(HEADS):
            l = acc[i][h][HEAD_DIM:HEAD_DIM + 1]
            o_t.append(acc[i][h][:HEAD_DIM] / l)
            lse_t.append(jnp.broadcast_to(tops[i][h] + jnp.log2(l), (HEAD_DIM, TILE)))
        outs.append(jnp.concatenate(o_t, axis=0).T)
        lses.append(jnp.concatenate(lse_t, axis=0).T)
    return outs, lses


DIL_SUBBLOCKS = 8


def _unpermutation(dil):
    t = np.arange(DIL_UNIT)
    mat = np.zeros((DIL_UNIT, DIL_UNIT), np.float32)
    mat[t, (t % dil) * (DIL_UNIT // dil) + t // dil] = 1.0
    return jnp.asarray(mat, BF16)


def _dilated_pattern(zc, bias, dil, *, batch, seq, merge_with=None):
    length = seq // dil
    subs = min(max(DIL_SUBBLOCKS // max(dil // 2, 1), 2), length // TILE)
    rows = subs * TILE
    tokens = rows * dil
    zc = zc.reshape(batch, length, dil * 3 * GROUP)
    blk = pl.BlockSpec((None, rows, dil * 3 * GROUP), lambda b, n: (b, n, 0))
    prev = pl.BlockSpec((None, TILE, dil * 3 * GROUP), lambda b, n: (b, jnp.maximum(n * subs - 1, 0), 0))
    nat = pl.BlockSpec((None, tokens, GROUP), lambda b, n: (b, n, 0))
    in_specs = [blk, prev, _const_spec((HEADS * TILE, 2 * TILE)), _const_spec((DIL_UNIT, DIL_UNIT))]
    args = [zc, zc, bias, _unpermutation(dil)]
    if merge_with is None:
        out_specs = [nat, nat]
        out_shape = [jax.ShapeDtypeStruct((batch, seq, GROUP), BF16),
                     jax.ShapeDtypeStruct((batch, seq, GROUP), F32)]
    else:
        in_specs += [nat] * len(merge_with)
        args += list(merge_with)
        out_specs = nat
        out_shape = jax.ShapeDtypeStruct((batch, seq, GROUP), BF16)
    out = pl.pallas_call(
        functools.partial(_dil_body, subs=subs, dil=dil, merge=merge_with is not None),
        grid=(batch, length // rows),
        in_specs=in_specs, out_specs=out_specs, out_shape=out_shape,
        compiler_params=_params("parallel", "parallel"),
        name=f"dilated_d{dil}",
    )(*args)
    if merge_with is None:
        return out
    return out.reshape(batch * seq, GROUP)


def _dilated(zc_views, dil_bias, *, batch, seq):
    o1, l1 = _dilated_pattern(zc_views[0], dil_bias[0], DIL_PATTERNS[0][1], batch=batch, seq=seq)
    o2, l2 = _dilated_pattern(zc_views[1], dil_bias[1], DIL_PATTERNS[1][1], batch=batch, seq=seq)
    return _dilated_pattern(zc_views[2], dil_bias[2], DIL_PATTERNS[2][1], batch=batch, seq=seq,
                            merge_with=(o1, l1, o2, l2))


DIFF_Q = 256
DIFF_K = 256
DIFF_GROUPS = 2 * HEADS


DIFF_SAFE_LOG2 = 60.0
DIFF_BOUND_SLACK = 1.01


def _diff_running_max(q, qi, last, pairs, bias_tiles, kexp_ref, vaug_ref, acc_ref, sta_ref, stb_ref):
    def scores(s_ref, j):
        bias = bias_tiles(j)
        raw = _dot_nt(kexp_ref[jnp.minimum(j, last)], q)
        tops = []
        for g in range(DIFF_GROUPS):
            s = raw[g * DIFF_K:(g + 1) * DIFF_K] + bias[g % HEADS]
            s_ref[g * DIFF_K:(g + 1) * DIFF_K, :] = s
            tops.append(jnp.max(s, axis=0, keepdims=True))
        return tuple(tops)

    def consume(s_ref, tops, j, carry):
        ms, ls = carry
        jv = jnp.minimum(j, last)
        new_ms, new_ls = [], []
        for g in range(DIFF_GROUPS):
            mp, h = divmod(g, HEADS)
            m_new = jnp.maximum(ms[g], tops[g])
            p = jnp.exp2(s_ref[g * DIFF_K:(g + 1) * DIFF_K, :] - m_new).astype(BF16)
            alpha = jnp.exp2(ms[g] - m_new)
            r = _dot(vaug_ref[jv, h], p)
            acc_ref[mp, h] = alpha * acc_ref[mp, h] + r[:HEAD_DIM]
            new_ls.append(alpha * ls[g] + r[HEAD_DIM:HEAD_DIM + 1])
            new_ms.append(m_new)
        return tuple(new_ms), tuple(new_ls)

    def pair(jj, carry):
        tops_a, state = carry
        j = 2 * jj
        tops_b = scores(stb_ref, j + 1)
        state = consume(sta_ref, tops_a, j, state)
        tops_a = scores(sta_ref, j + 2)
        return tops_a, consume(stb_ref, tops_b, j + 1, state)

    init = (tuple(jnp.full((1, DIFF_Q), NEG, F32) for _ in range(DIFF_GROUPS)),
            tuple(jnp.zeros((1, DIFF_Q), F32) for _ in range(DIFF_GROUPS)))
    _, (_, ls) = lax.fori_loop(0, pairs, pair, (scores(sta_ref, 0), init))
    return ls


def _diff_body(q_ref, k_ref, vt_ref, bias_ref, lam_ref, sg_ref, lim_ref, y_ref,
               kexp_ref, vaug_ref, acc_ref, sta_ref, stb_ref, pa_ref, pb_ref, knorm_ref,
               *, lam_init, key_steps):
    qi = pl.program_id(1)
    qk_group = _group_mask(GROUP, GROUP, DIFF_QK_HALF, DIFF_QK_HALF).astype(BF16)

    @pl.when(qi == 0)
    def _():
        grp = lax.broadcasted_iota(jnp.int32, (DIFF_GROUPS * DIFF_K, GROUP), 0) // DIFF_K
        slot = lax.broadcasted_iota(jnp.int32, (DIFF_GROUPS * DIFF_K, GROUP), 1) // DIFF_QK_HALF
        kmask = slot == 2 * (grp % HEADS) + grp // HEADS
        ones_rows = (lax.broadcasted_iota(jnp.int32, (AUG_ROWS - HEAD_DIM, DIFF_K), 0) == 0).astype(BF16)

        def build(j, kmax):
            k_t = k_ref[pl.ds(pl.multiple_of(j * DIFF_K, DIFF_K), DIFF_K), :]
            kexp_ref[j] = jnp.where(kmask, jnp.concatenate([k_t] * DIFF_GROUPS, axis=0),
                                    jnp.zeros((), BF16))
            vt = jnp.concatenate([vt_ref[2 * j], vt_ref[2 * j + 1]], axis=1)
            for h in range(HEADS):
                vaug_ref[j, h] = jnp.concatenate([vt[h * HEAD_DIM:(h + 1) * HEAD_DIM], ones_rows], axis=0)
            k_f = k_t.astype(F32)
            return jnp.maximum(kmax, jnp.max(_group_sum(k_f * k_f, qk_group), axis=0, keepdims=True))

        knorm_ref[...] = lax.fori_loop(0, key_steps, build, jnp.zeros((1, GROUP), F32))

    acc_ref[...] = jnp.zeros_like(acc_ref)
    q = q_ref[...]
    last = key_steps - 1
    key_steps_needed = (qi + 1) * (DIFF_Q // DIFF_K)
    pairs = (key_steps_needed + 1) // 2

    def bias_tiles(j):
        base = (DIFF_Q // TILE) * qi - (DIFF_K // TILE) * j - DIFF_MIN_OFFSET
        tiles = {d: bias_ref[jnp.maximum(base + d, 0)]
                 for d in range(1 - DIFF_K // TILE, DIFF_Q // TILE)}
        return [jnp.concatenate(
            [jnp.concatenate([tiles[a - b][h * TILE:(h + 1) * TILE] for a in range(DIFF_Q // TILE)], axis=1)
             for b in range(DIFF_K // TILE)], axis=0) for h in range(HEADS)]

    q_f = q.astype(F32)
    bound2 = jnp.max(_group_sum(q_f * q_f, qk_group) * knorm_ref[...])
    no_overflow = bound2 * DIFF_BOUND_SLACK <= lim_ref[0, 0]

    def unshifted():
        def weights(p_ref, j):
            bias = bias_tiles(j)
            raw = _dot_nt(kexp_ref[jnp.minimum(j, last)], q)
            for g in range(DIFF_GROUPS):
                rows = slice(g * DIFF_K, (g + 1) * DIFF_K)
                p_ref[rows, :] = jnp.exp2(raw[rows] + bias[g % HEADS]).astype(BF16)

        def accumulate(p_ref, j, ls):
            jv = jnp.minimum(j, last)
            new_ls = []
            for g in range(DIFF_GROUPS):
                mp, h = divmod(g, HEADS)
                r = _dot(vaug_ref[jv, h], p_ref[g * DIFF_K:(g + 1) * DIFF_K, :])
                acc_ref[mp, h] = acc_ref[mp, h] + r[:HEAD_DIM]
                new_ls.append(ls[g] + r[HEAD_DIM:HEAD_DIM + 1])
            return tuple(new_ls)

        def pair(jj, ls):
            j = 2 * jj
            weights(pb_ref, j + 1)
            ls = accumulate(pa_ref, j, ls)
            weights(pa_ref, j + 2)
            return accumulate(pb_ref, j + 1, ls)

        weights(pa_ref, 0)
        return lax.fori_loop(0, pairs, pair,
                             tuple(jnp.zeros((1, DIFF_Q), F32) for _ in range(DIFF_GROUPS)))

    def running_max():
        return _diff_running_max(q, qi, last, pairs, bias_tiles, kexp_ref, vaug_ref, acc_ref,
                                 sta_ref, stb_ref)

    ls = lax.cond(no_overflow, unshifted, running_max)

    lv = lam_ref[...]
    lam = (jnp.exp(jnp.sum(lv[0:1] * lv[1:2], axis=-1, keepdims=True))
           - jnp.exp(jnp.sum(lv[2:3] * lv[3:4], axis=-1, keepdims=True)) + lam_init)
    outs = []
    for h in range(HEADS):
        o = acc_ref[0, h] / ls[h] - lam * (acc_ref[1, h] / ls[HEADS + h])
        ms_o = jnp.mean(o * o, axis=0, keepdims=True)
        outs.append(o * lax.rsqrt(ms_o + SUBLN_EPS) * sg_ref[...] * (1.0 - lam_init))
    y_ref[...] = jnp.concatenate(outs, axis=0).T.astype(BF16)


def _diff_attention(zd, vt, bias, score_limit2, lam_vecs, subln_cols, *, lam_init, batch, seq):
    zd = zd.reshape(batch, seq, 2 * GROUP)
    key_tiles = seq // TILE
    key_steps = seq // DIFF_K
    vt = vt.reshape(batch, key_tiles, GROUP, TILE)
    y = pl.pallas_call(
        functools.partial(_diff_body, lam_init=lam_init, key_steps=key_steps),
        grid=(batch, seq // DIFF_Q),
        in_specs=[pl.BlockSpec((None, DIFF_Q, GROUP), lambda b, i: (b, i, 0)),
                  pl.BlockSpec((None, seq, GROUP), lambda b, i: (b, 0, 1), pipeline_mode=pl.Buffered(1)),
                  pl.BlockSpec((None, key_tiles, GROUP, TILE), lambda b, i: (b, 0, 0, 0),
                               pipeline_mode=pl.Buffered(1)),
                  _const_spec((key_tiles - DIFF_MIN_OFFSET, HEADS * TILE, TILE)),
                  _const_spec((4, DIFF_QK_HALF)), _const_spec((HEAD_DIM, DIFF_Q)),
                  pl.BlockSpec(memory_space=pltpu.SMEM)],
        out_specs=pl.BlockSpec((None, DIFF_Q, GROUP), lambda b, i: (b, i, 0)),
        out_shape=jax.ShapeDtypeStruct((batch, seq, GROUP), BF16),
        scratch_shapes=[pltpu.VMEM((key_steps, DIFF_GROUPS * DIFF_K, GROUP), BF16),
                        pltpu.VMEM((key_steps, HEADS, AUG_ROWS, DIFF_K), BF16),
                        pltpu.VMEM((2, HEADS, HEAD_DIM, DIFF_Q), F32),
                        pltpu.VMEM((DIFF_GROUPS * DIFF_K, DIFF_Q), F32),
                        pltpu.VMEM((DIFF_GROUPS * DIFF_K, DIFF_Q), F32),
                        pltpu.VMEM((DIFF_GROUPS * DIFF_K, DIFF_Q), BF16),
                        pltpu.VMEM((DIFF_GROUPS * DIFF_K, DIFF_Q), BF16),
                        pltpu.VMEM((1, GROUP), F32)],
        compiler_params=_params("parallel", "arbitrary"),
        name="diff_attn",
    )(zd, zd, vt, bias, lam_vecs, subln_cols, score_limit2)
    return y.reshape(batch * seq, GROUP)


KV_ROWS = 512


def _mem_kv_body(m_ref, g_ref, w_ref, k_ref, v_ref):
    u = _rms(m_ref[...], g_ref[...]).astype(BF16)
    for c in range(D_MODEL // GROUP):
        sl = slice(c * GROUP, (c + 1) * GROUP)
        k_ref[:, sl] = _dot(u, w_ref[:, sl]).astype(BF16)
        v_ref[:, sl] = _dot(u, w_ref[:, D_MODEL + c * GROUP: D_MODEL + (c + 1) * GROUP]).astype(BF16)


def _mem_kv(mem, g, w):
    n = mem.shape[0]
    row = pl.BlockSpec((KV_ROWS, D_MODEL), lambda i: (i, 0))
    return pl.pallas_call(
        _mem_kv_body,
        grid=(n // KV_ROWS,),
        in_specs=[row, _const_spec((1, D_MODEL)), _const_spec((D_MODEL, 2 * D_MODEL))],
        out_specs=[row, row],
        out_shape=[jax.ShapeDtypeStruct((n, D_MODEL), BF16)] * 2,
        compiler_params=_params("parallel"),
        name="mem_kv",
    )(mem, g, w)


XATTN_ROWS = 512


def _xattn_body(x_ref, ya_ref, yb_ref, yc_ref, yd_ref, wout_ref, g_ref, wq_ref, k_ref, v_ref, wo_ref,
                o_ref, q_scr, a_scr):
    x = x_ref[...]
    for gi, y_ref in enumerate((ya_ref, yb_ref, yc_ref, yd_ref)):
        x = x + _dot(y_ref[...], wout_ref[gi * GROUP:(gi + 1) * GROUP, :])
    u = _rms(x, g_ref[...]).astype(BF16)
    for c in range(D_MODEL // GROUP):
        sl = slice(c * GROUP, (c + 1) * GROUP)
        q_scr[:, sl] = _dot(u, wq_ref[:, sl]).astype(BF16)
    for h in range(MEM_HEADS):
        sl = slice(h * MEM_HEAD_DIM, (h + 1) * MEM_HEAD_DIM)
        s = _dot_nt(q_scr[:, sl], k_ref[:, sl]) * (MEM_HEAD_DIM ** -0.5)
        e = jnp.exp(s - jnp.max(s, axis=-1, keepdims=True))
        l = jnp.sum(e, axis=-1, keepdims=True)
        a_scr[:, sl] = (_dot(e.astype(BF16), v_ref[:, sl]) / l).astype(BF16)
    o_ref[...] = x + _dot(a_scr[...], wo_ref[...])


def _xattn(h, ys, w_out, g, wq, k, v, wo, *, batch, seq):
    h3 = h.reshape(batch, seq, D_MODEL)
    ys = [y.reshape(batch, seq, GROUP) for y in ys]
    k3 = k.reshape(batch, MEM_LEN, D_MODEL)
    v3 = v.reshape(batch, MEM_LEN, D_MODEL)
    row = pl.BlockSpec((None, XATTN_ROWS, D_MODEL), lambda b, i: (b, i, 0))
    grp = pl.BlockSpec((None, XATTN_ROWS, GROUP), lambda b, i: (b, i, 0))
    mem = pl.BlockSpec((None, MEM_LEN, D_MODEL), lambda b, i: (b, 0, 0))
    weight = _const_spec((D_MODEL, D_MODEL))
    out = pl.pallas_call(
        _xattn_body,
        grid=(batch, seq // XATTN_ROWS),
        in_specs=[row, grp, grp, grp, grp, weight, _const_spec((1, D_MODEL)), weight, mem, mem, weight],
        out_specs=row,
        out_shape=jax.ShapeDtypeStruct((batch, seq, D_MODEL), F32),
        scratch_shapes=[pltpu.VMEM((XATTN_ROWS, D_MODEL), BF16), pltpu.VMEM((XATTN_ROWS, D_MODEL), BF16)],
        compiler_params=_params("parallel", "parallel"),
        name="xattn",
    )(h3, *ys, w_out, g, wq, k3, v3, wo)
    return out.reshape(batch * seq, D_MODEL)


def _per_head_lanes(x):
    return jnp.repeat(x, HEAD_DIM, axis=-1)


def _in_proj_weight(w_in):
    g = GROUP
    q_a, k_a, v_a, o_a = (w_in[:, i * g:(i + 1) * g] for i in range(4))
    ig = w_in[:, 4 * g:4 * g + HEADS]
    fg = w_in[:, 4 * g + HEADS:4 * g + 2 * HEADS]
    rest = w_in[:, 4 * g + 2 * HEADS:]
    pool, q_c, k_c, v_c, q_d, k_d, v_d = (rest[:, i * g:(i + 1) * g] for i in range(7))
    q_c = q_c * DIL_SCORE_SCALE
    k_d = k_d * DIFF_SCORE_SCALE
    cols = [q_a, k_a, v_a, o_a, _per_head_lanes(ig), _per_head_lanes(fg), pool,
            q_c, k_c, v_c, q_d, k_d]
    return jnp.concatenate(cols, axis=1).astype(BF16), v_d.astype(BF16)


def _block_diag(w):
    g, c, _ = w.shape
    eye = jnp.eye(g, dtype=w.dtype)
    return (eye[:, None, :, None] * w[:, :, None, :]).reshape(g * c, g * c)


def kernel(x, mem, t5_bias, ffn1_norm, ffn1_w_gate, ffn1_w_up, ffn1_w_down, mix_norm, w_in,
           mlstm_conv_w, mlstm_conv_b, mlstm_gate_b, mlstm_norm, pool_w, pool_scale,
           diff_lambda, diff_subln, w_out, xattn_norm, mem_norm, xattn_wq, xattn_wkv, xattn_wo,
           ffn2_norm, ffn2_w_gate, ffn2_w_up, ffn2_w_down, final_norm):
    batch, seq, _ = x.shape
    n = batch * seq
    dil_bias, diff_bias, diff_limit2 = _bias_tiles(t5_bias, seq)
    h = x.reshape(n, D_MODEL)
    mem2 = mem.reshape(batch * MEM_LEN, D_MODEL)
    row = lambda v: v.reshape(1, -1)
    for l in range(DEPTH):
        lam_init = 0.8 - 0.6 * math.exp(-0.3 * l)
        h = _ffn(h, row(ffn1_norm[l]), ffn1_w_gate[l].astype(BF16), ffn1_w_up[l].astype(BF16),
                 ffn1_w_down[l].astype(BF16), row(final_norm), final=False)
        za, zg, zp, zc, zd, vt, zc4, zc16 = _in_proj(h, row(mix_norm[l]), *_in_proj_weight(w_in[l]))
        ya = _mlstm(za, zg, mlstm_conv_w[l], row(mlstm_conv_b[l]),
                    row(_per_head_lanes(mlstm_gate_b[l].reshape(2, HEADS))), row(mlstm_norm[l]),
                    batch=batch, seq=seq)
        yb = _pool(zp, _block_diag(pool_w[l]).astype(BF16), row(pool_scale[l]), batch=batch, seq=seq)
        yc = _dilated((zc, zc4, zc16), dil_bias, batch=batch, seq=seq)
        yd = _diff_attention(zd, vt, diff_bias, diff_limit2, diff_lambda[l],
                             jnp.broadcast_to(diff_subln[l][:, None], (HEAD_DIM, DIFF_Q)),
                             lam_init=lam_init, batch=batch, seq=seq)
        k_mem, v_mem = _mem_kv(mem2, row(mem_norm[l]), xattn_wkv[l].astype(BF16))
        h = _xattn(h, (ya, yb, yc, yd), w_out[l].astype(BF16),
                   row(xattn_norm[l]), xattn_wq[l].astype(BF16), k_mem, v_mem,
                   xattn_wo[l].astype(BF16), batch=batch, seq=seq)
        h = _ffn(h, row(ffn2_norm[l]), ffn2_w_gate[l].astype(BF16), ffn2_w_up[l].astype(BF16),
                 ffn2_w_down[l].astype(BF16), row(final_norm), final=(l == DEPTH - 1))
    return h.reshape(batch, seq, D_MODEL)
```

```python
import functools
import math

import jax
import jax.numpy as jnp
import numpy as np
from jax import lax
from jax.experimental import pallas as pl
from jax.experimental.pallas import tpu as pltpu

F32 = jnp.float32
BF16 = jnp.bfloat16

D_MODEL = 1024
D_FF = 2816
DEPTH = 4
GROUP = 256
HEADS = 4
HEAD_DIM = GROUP // HEADS
MEM_LEN = 256
MEM_HEADS = 4
MEM_HEAD_DIM = D_MODEL // MEM_HEADS
MLSTM_CHUNK = 64
CONV_WIDTH = 4
POOL_WINDOWS = (2, 4, 8, 16)
DIL_PATTERNS = ((128, 1), (512, 4), (2048, 16))
DIL_BACK = 128
DIFF_QK_HALF = HEAD_DIM // 2
T5_BUCKETS = 32
T5_MAX_DIST = 2048
RMS_EPS = 1e-6
SUBLN_EPS = 1e-5
NEG = -1e30
LOG2E = math.log2(math.e)
DIFF_SCORE_SCALE = (DIFF_QK_HALF ** -0.5) * LOG2E
DIL_SCORE_SCALE = (HEAD_DIM ** -0.5) * LOG2E
AUG_ROWS = HEAD_DIM + 16
DIFF_MIN_OFFSET = -3
TILE = 128

VMEM_LIMIT_BYTES = 56 * 1024 * 1024


def _rms(xf, g, eps=RMS_EPS):
    return xf * lax.rsqrt(jnp.mean(xf * xf, axis=-1, keepdims=True) + eps) * g


def _const_spec(shape):
    zeros = (0,) * len(shape)
    return pl.BlockSpec(shape, lambda *_: zeros, pipeline_mode=pl.Buffered(1))


def _params(*sem):
    return pltpu.CompilerParams(dimension_semantics=sem, vmem_limit_bytes=VMEM_LIMIT_BYTES)


def _group_mask(rows, cols, row_group, col_group):
    r = lax.broadcasted_iota(jnp.int32, (rows, cols), 0) // row_group
    c = lax.broadcasted_iota(jnp.int32, (rows, cols), 1) // col_group
    return r == c


def _tile_rows(x, reps, mask):
    return jnp.where(mask, jnp.concatenate([x] * reps, axis=0), jnp.zeros((), x.dtype))


def _dot(a, b):
    return jnp.dot(a, b, preferred_element_type=F32)


def _dot_nt(a, b):
    return lax.dot_general(a, b, (((1,), (1,)), ((), ())), preferred_element_type=F32)


def _dot_tn(a, b):
    return lax.dot_general(a, b, (((0,), (0,)), ((), ())), preferred_element_type=F32)


def _group_sum(x, ones_bd):
    hi = x.astype(BF16)
    lo = (x - hi.astype(F32)).astype(BF16)
    return _dot(hi, ones_bd) + _dot(lo, ones_bd)


FFN_ROWS = 1024
FFN_COLS = 256


def _ffn_body(x_ref, g_ref, wg_ref, wu_ref, wd_ref, fg_ref, o_ref, act_ref, *, final):
    x = x_ref[...]
    u = _rms(x, g_ref[...]).astype(BF16)
    for c in range(D_FF // FFN_COLS):
        sl = slice(c * FFN_COLS, (c + 1) * FFN_COLS)
        gate = _dot(u, wg_ref[:, sl])
        up = _dot(u, wu_ref[:, sl])
        act_ref[:, sl] = (gate * jax.nn.sigmoid(gate) * up).astype(BF16)
    y = x + 0.5 * _dot(act_ref[...], wd_ref[...])
    if final:
        y = _rms(y, fg_ref[...])
    o_ref[...] = y


def _ffn(h, g, wg, wu, wd, fg, *, final):
    n = h.shape[0]
    row = pl.BlockSpec((FFN_ROWS, D_MODEL), lambda i: (i, 0))
    return pl.pallas_call(
        functools.partial(_ffn_body, final=final),
        grid=(n // FFN_ROWS,),
        in_specs=[row, _const_spec((1, D_MODEL)), _const_spec((D_MODEL, D_FF)),
                  _const_spec((D_MODEL, D_FF)), _const_spec((D_FF, D_MODEL)),
                  _const_spec((1, D_MODEL))],
        out_specs=row,
        out_shape=jax.ShapeDtypeStruct((n, D_MODEL), F32),
        scratch_shapes=[pltpu.VMEM((FFN_ROWS, D_FF), BF16)],
        compiler_params=_params("parallel"),
        name="ffn_final" if final else "ffn",
    )(h, g, wg, wu, wd, fg)


PROJ_ROWS = 512
PROJ_OUTS = (("a", 4 * GROUP, F32), ("g", 2 * GROUP, F32), ("p", GROUP, F32),
             ("c", 3 * GROUP, BF16), ("d", 2 * GROUP, BF16))
PROJ_WIDTH = sum(w for _, w, _ in PROJ_OUTS)


PROJ_DILATIONS = tuple(d for _, d in DIL_PATTERNS if d > 1)


def _in_proj_body(x_ref, g_ref, w_ref, wv_ref, perm_ref, *o_refs):
    u = _rms(x_ref[...], g_ref[...]).astype(BF16)
    off = 0
    for o_ref, (_, width, dtype) in zip(o_refs, PROJ_OUTS):
        for c in range(width // GROUP):
            z = _dot(u, w_ref[:, off + c * GROUP: off + (c + 1) * GROUP])
            o_ref[:, c * GROUP:(c + 1) * GROUP] = z.astype(dtype)
        off += width
    vt_ref = o_refs[len(PROJ_OUTS)]
    v = _dot(u, wv_ref[...])
    for t in range(PROJ_ROWS // TILE):
        vt_ref[t] = v[t * TILE:(t + 1) * TILE].T.astype(BF16)
    for pi, d in enumerate(PROJ_DILATIONS):
        per_class = PROJ_UNIT // d
        for un in range(PROJ_ROWS // PROJ_UNIT):
            zp = _dot(perm_ref[pi], o_refs[3][un * PROJ_UNIT:(un + 1) * PROJ_UNIT, :]).astype(BF16)
            for r in range(d):
                o_refs[len(PROJ_OUTS) + 1 + pi][un * per_class:(un + 1) * per_class,
                                                r * 3 * GROUP:(r + 1) * 3 * GROUP] = \
                    zp[r * per_class:(r + 1) * per_class]


PROJ_UNIT = 2 * TILE


def _class_permutations():
    mats = np.zeros((len(PROJ_DILATIONS), PROJ_UNIT, PROJ_UNIT), np.float32)
    for pi, d in enumerate(PROJ_DILATIONS):
        t = np.arange(PROJ_UNIT)
        mats[pi, (t % d) * (PROJ_UNIT // d) + t // d, t] = 1.0
    return jnp.asarray(mats, BF16)


def _in_proj(h, g, w, wv):
    n = h.shape[0]
    tiles = PROJ_ROWS // TILE
    return pl.pallas_call(
        _in_proj_body,
        grid=(n // PROJ_ROWS,),
        in_specs=[pl.BlockSpec((PROJ_ROWS, D_MODEL), lambda i: (i, 0)),
                  _const_spec((1, D_MODEL)), _const_spec((D_MODEL, PROJ_WIDTH)),
                  _const_spec((D_MODEL, GROUP)),
                  _const_spec((len(PROJ_DILATIONS), PROJ_UNIT, PROJ_UNIT))],
        out_specs=[pl.BlockSpec((PROJ_ROWS, w_), lambda i: (i, 0)) for _, w_, _ in PROJ_OUTS]
        + [pl.BlockSpec((tiles, GROUP, TILE), lambda i: (i, 0, 0))]
        + [pl.BlockSpec((PROJ_ROWS // d, d * 3 * GROUP), lambda i: (i, 0)) for d in PROJ_DILATIONS],
        out_shape=[jax.ShapeDtypeStruct((n, w_), dt) for _, w_, dt in PROJ_OUTS]
        + [jax.ShapeDtypeStruct((n // TILE, GROUP, TILE), BF16)]
        + [jax.ShapeDtypeStruct((n // d, d * 3 * GROUP), BF16) for d in PROJ_DILATIONS],
        compiler_params=_params("parallel"),
        name="in_proj",
    )(h, g, w, wv, _class_permutations())


ML_ROWS = 512
ML_HALO = 8


def _chunk_scan(x, rin, op, fill):
    s = 1
    while s < MLSTM_CHUNK:
        x = op(x, jnp.where(rin >= s, pltpu.roll(x, s, 0), fill))
        s *= 2
    return x


def _mlstm_body(za_ref, zg_ref, cw_ref, cb_ref, gb_ref, ng_ref, y_ref,
                buf_ref, tail_ref, hh_ref, c_ref, n_ref, m_ref):
    L = MLSTM_CHUNK

    @pl.when(pl.program_id(1) == 0)
    def _():
        tail_ref[...] = jnp.zeros_like(tail_ref)
        c_ref[...] = jnp.zeros_like(c_ref)
        n_ref[...] = jnp.zeros_like(n_ref)
        m_ref[...] = jnp.zeros_like(m_ref)

    buf_ref[0:ML_HALO, :] = tail_ref[...]
    buf_ref[ML_HALO:, :] = za_ref[:, 0:2 * GROUP]
    tail_ref[...] = za_ref[ML_ROWS - ML_HALO:, 0:2 * GROUP]
    conv = cb_ref[...]
    for j in range(CONV_WIDTH):
        conv = conv + buf_ref[pl.ds(ML_HALO - (CONV_WIDTH - 1) + j, ML_ROWS), :] * cw_ref[j:j + 1, :]
    qk = conv * jax.nn.sigmoid(conv)
    q = qk[:, :GROUP]
    k = qk[:, GROUP:] * (HEAD_DIM ** -0.5)
    v = za_ref[:, 2 * GROUP:3 * GROUP].astype(BF16)
    q_bf, k_bf = q.astype(BF16), k.astype(BF16)

    ii = zg_ref[:, :GROUP] + gb_ref[:, :GROUP]
    fx = zg_ref[:, GROUP:] + gb_ref[:, GROUP:]
    lf = jnp.minimum(fx, 0.0) - jnp.log1p(jnp.exp(-jnp.abs(fx)))
    rin = lax.broadcasted_iota(jnp.int32, (ML_ROWS, GROUP), 0) % L
    b = _chunk_scan(lf, rin, jnp.add, 0.0)
    a = ii - b
    ca = _chunk_scan(a, rin, jnp.maximum, NEG)

    bd = _group_mask(GROUP, GROUP, HEAD_DIM, HEAD_DIM)
    ones_bd = bd.astype(BF16)
    row = lax.broadcasted_iota(jnp.int32, (L, GROUP), 0)
    key = lax.broadcasted_iota(jnp.int32, (L, GROUP), 1) % L
    causal = key <= row
    diag = key == row

    m_prev = m_ref[...]
    for c in range(ML_ROWS // L):
        rs = slice(c * L, (c + 1) * L)
        q_c, k_c, v_c = q[rs], k[rs], v[rs]
        q_b = q_bf[rs]
        a_c, b_c = a[rs], b[rs]
        g = jnp.maximum(m_prev, ca[rs])
        g_last = g[L - 1:L]
        a_row = jnp.sum(jnp.where(diag, a_c, 0.0), axis=0, keepdims=True)
        decay = jnp.exp(jnp.where(causal, a_row - g, NEG))
        sc = _dot_nt(q_b, _tile_rows(k_bf[rs], HEADS, bd)) * decay
        inter = jnp.exp(m_prev - g)
        num = inter * _dot(q_b, c_ref[...].astype(BF16)) + _dot(sc.astype(BF16), _tile_rows(v_c, HEADS, bd))
        den = inter * _group_sum(q_c * n_ref[...], ones_bd) + _group_sum(sc, ones_bd)
        hh_ref[rs, :] = num / jnp.maximum(jnp.abs(den), jnp.exp(-(b_c + g)))
        kw = k_c * jnp.exp(a_c - g_last)
        carry = jnp.exp(m_prev - g_last)
        c_ref[...] = carry * c_ref[...] + jnp.where(bd, _dot_tn(kw.astype(BF16), v_c), 0.0)
        n_ref[...] = carry * n_ref[...] + jnp.sum(kw, axis=0, keepdims=True)
        m_prev = b_c[L - 1:L] + g_last
    m_ref[...] = m_prev

    hh = hh_ref[...]
    mu = _group_sum(hh, ones_bd) * (1.0 / HEAD_DIM)
    dev = hh - mu
    var = _group_sum(dev * dev, ones_bd) * (1.0 / HEAD_DIM)
    o_gate = jax.nn.sigmoid(za_ref[:, 3 * GROUP:])
    y_ref[...] = (dev * lax.rsqrt(var + RMS_EPS) * ng_ref[...] * o_gate).astype(BF16)


def _mlstm(za, zg, conv_w, conv_b, gate_b, norm_g, *, batch, seq):
    za = za.reshape(batch, seq, 4 * GROUP)
    zg = zg.reshape(batch, seq, 2 * GROUP)
    y = pl.pallas_call(
        _mlstm_body,
        grid=(batch, seq // ML_ROWS),
        in_specs=[pl.BlockSpec((None, ML_ROWS, 4 * GROUP), lambda b, j: (b, j, 0)),
                  pl.BlockSpec((None, ML_ROWS, 2 * GROUP), lambda b, j: (b, j, 0)),
                  _const_spec((CONV_WIDTH, 2 * GROUP)), _const_spec((1, 2 * GROUP)),
                  _const_spec((1, 2 * GROUP)), _const_spec((1, GROUP))],
        out_specs=pl.BlockSpec((None, ML_ROWS, GROUP), lambda b, j: (b, j, 0)),
        out_shape=jax.ShapeDtypeStruct((batch, seq, GROUP), BF16),
        scratch_shapes=[pltpu.VMEM((ML_ROWS + ML_HALO, 2 * GROUP), F32),
                        pltpu.VMEM((ML_HALO, 2 * GROUP), F32),
                        pltpu.VMEM((ML_ROWS, GROUP), F32),
                        pltpu.VMEM((GROUP, GROUP), F32),
                        pltpu.VMEM((1, GROUP), F32),
                        pltpu.VMEM((1, GROUP), F32)],
        compiler_params=_params("parallel", "arbitrary"),
        name="mlstm",
    )(za, zg, conv_w, conv_b, gate_b, norm_g)
    return y.reshape(batch * seq, GROUP)


POOL_ROWS = 1024
POOL_HALO = 16


def _pool_body(u_ref, w_ref, s_ref, y_ref, buf_ref, tail_ref):
    j = pl.program_id(1)

    @pl.when(j == 0)
    def _():
        tail_ref[...] = jnp.zeros_like(tail_ref)

    buf_ref[0:POOL_HALO, :] = tail_ref[...]
    buf_ref[POOL_HALO:, :] = u_ref[...]
    tail_ref[...] = u_ref[POOL_ROWS - POOL_HALO:, :]
    sums, s = [], buf_ref[...]
    for shift in (1, 2, 4, 8):
        s = s + pltpu.roll(s, shift, 0)
        sums.append(s[POOL_HALO:])
    u = u_ref[...]
    lane_group = lax.broadcasted_iota(jnp.int32, (POOL_ROWS, GROUP), 1) // HEAD_DIM
    t = j * POOL_ROWS + lax.broadcasted_iota(jnp.int32, (POOL_ROWS, GROUP), 0)
    total, win = sums[3], jnp.full((POOL_ROWS, GROUP), POOL_WINDOWS[3], jnp.int32)
    for gi in (2, 1, 0):
        total = jnp.where(lane_group == gi, sums[gi], total)
        win = jnp.where(lane_group == gi, POOL_WINDOWS[gi], win)
    mean = total / jnp.minimum(t + 1, win).astype(F32)
    y = _dot((mean - u).astype(BF16), w_ref[...]) * s_ref[...]
    y_ref[...] = y.astype(BF16)


def _pool(zp, w_bd, scale, *, batch, seq):
    zp = zp.reshape(batch, seq, GROUP)
    y = pl.pallas_call(
        _pool_body,
        grid=(batch, seq // POOL_ROWS),
        in_specs=[pl.BlockSpec((None, POOL_ROWS, GROUP), lambda b, j: (b, j, 0)),
                  _const_spec((GROUP, GROUP)), _const_spec((1, GROUP))],
        out_specs=pl.BlockSpec((None, POOL_ROWS, GROUP), lambda b, j: (b, j, 0)),
        out_shape=jax.ShapeDtypeStruct((batch, seq, GROUP), BF16),
        scratch_shapes=[pltpu.VMEM((POOL_ROWS + POOL_HALO, GROUP), F32),
                        pltpu.VMEM((POOL_HALO, GROUP), F32)],
        compiler_params=_params("parallel", "arbitrary"),
        name="pool",
    )(zp, w_bd, scale)
    return y.reshape(batch * seq, GROUP)


def _toeplitz_body(w_ref, o_ref):
    x = jnp.broadcast_to(w_ref[...], (TILE, 2 * TILE))
    o_ref[...] = pltpu.roll(x, 0, 1, stride=1, stride_axis=0)[:, :TILE]


def _toeplitz(rows):
    n = rows.shape[0]
    return pl.pallas_call(
        _toeplitz_body,
        grid=(n,),
        in_specs=[pl.BlockSpec((None, 1, 2 * TILE), lambda i: (i, 0, 0))],
        out_specs=pl.BlockSpec((None, TILE, TILE), lambda i: (i, 0, 0)),
        out_shape=jax.ShapeDtypeStruct((n, TILE, TILE), F32),
        compiler_params=_params("parallel"),
        name="toeplitz",
    )(rows.reshape(n, 1, 2 * TILE))


_TOEPLITZ_X = np.where(np.arange(2 * TILE) <= TILE, -np.arange(2 * TILE), 2 * TILE - np.arange(2 * TILE))


def _t5_bucket(dist):
    max_exact = T5_BUCKETS // 2
    d = jnp.maximum(dist, 1).astype(F32)
    large = max_exact + (jnp.log(d / max_exact) / math.log(T5_MAX_DIST / max_exact)
                         * (T5_BUCKETS - max_exact)).astype(jnp.int32)
    large = jnp.minimum(large, T5_BUCKETS - 1)
    return jnp.where(dist < max_exact, dist, large)


def _bias_tiles(t5_bias, seq):
    x = jnp.asarray(_TOEPLITZ_X, jnp.int32)
    rows = []
    for (w, d) in DIL_PATTERNS:
        tab = t5_bias[_t5_bucket(jnp.arange(DIL_BACK + 1) * d), :HEADS].T * LOG2E
        same = jnp.where(x <= 0, tab[:, jnp.clip(-x, 0, DIL_BACK)], NEG)
        nxt = jnp.where(x >= 0, tab[:, jnp.clip(DIL_BACK - x, 0, DIL_BACK)], NEG)
        rows.append(jnp.stack([same, nxt], axis=1))
    dil = _toeplitz(jnp.stack(rows).reshape(-1, 2 * TILE))
    dil = dil.reshape(len(DIL_PATTERNS), HEADS, 2, TILE, TILE).transpose(0, 1, 3, 2, 4)
    dil = dil.reshape(len(DIL_PATTERNS), HEADS * TILE, 2 * TILE)
    tab = t5_bias[_t5_bucket(jnp.arange(seq)), HEADS:].T * LOG2E
    noff = seq // TILE - DIFF_MIN_OFFSET
    dist = (jnp.arange(noff)[:, None] + DIFF_MIN_OFFSET) * TILE - x[None, :]
    rows = jnp.where(dist >= 0, tab[:, jnp.clip(dist, 0, seq - 1)], NEG)
    diff = _toeplitz(rows.transpose(1, 0, 2).reshape(-1, 2 * TILE))
    room = jnp.maximum(DIFF_SAFE_LOG2 - jnp.max(jnp.abs(tab)), 0.0)
    return dil, diff.reshape(noff, HEADS * TILE, TILE), (room * room).reshape(1, 1)


DIL_UNIT = 2 * TILE


def _dil_body(*refs, subs, dil, merge, has_prev):
    if merge:
        x_ref, xp_ref, bias_ref, unperm_ref, o1_ref, l1_ref, o2_ref, l2_ref, y_ref = refs
    else:
        x_ref, xp_ref, bias_ref, unperm_ref, o_ref, lse_ref = refs
    per_class = [_dil_class(x_ref, xp_ref, bias_ref, rc=rc, subs=subs, has_prev=has_prev)
                 for rc in range(dil)]
    piece = DIL_UNIT // dil
    class_o = [jnp.concatenate(pc[0], axis=0) for pc in per_class]
    class_l = [jnp.concatenate(pc[1], axis=0) for pc in per_class]
    for u in range(dil * subs * TILE // DIL_UNIT):
        take = lambda arrs: jnp.concatenate([a[u * piece:(u + 1) * piece] for a in arrs], axis=0)
        if dil == 1:
            o, lse = take(class_o), take(class_l)
        else:
            o_c = take(class_o).astype(BF16)
            l_c = take(class_l)
            l_hi = l_c.astype(BF16)
            l_lo = (l_c - l_hi.astype(F32)).astype(BF16)
            o = _dot(unperm_ref[...], o_c)
            lse = _dot(unperm_ref[...], l_hi) + _dot(unperm_ref[...], l_lo)
        rs = slice(u * DIL_UNIT, (u + 1) * DIL_UNIT)
        if merge:
            o1, l1 = o1_ref[rs, :].astype(F32), l1_ref[rs, :]
            o2, l2 = o2_ref[rs, :].astype(F32), l2_ref[rs, :]
            top = jnp.maximum(jnp.maximum(l1, l2), lse)
            w1, w2, w3 = jnp.exp2(l1 - top), jnp.exp2(l2 - top), jnp.exp2(lse - top)
            y_ref[rs, :] = ((w1 * o1 + w2 * o2 + w3 * o) / (w1 + w2 + w3)).astype(BF16)
        else:
            o_ref[rs, :] = o.astype(BF16)
            lse_ref[rs, :] = lse


def _dil_class(x_ref, xp_ref, bias_ref, *, rc, subs, has_prev):
    first = pl.program_id(1) == 0
    kmask = _group_mask(HEADS * TILE, GROUP, TILE, HEAD_DIM)
    ones_rows = (lax.broadcasted_iota(jnp.int32, (AUG_ROWS - HEAD_DIM, TILE), 0) == 0).astype(BF16)
    rows_of = lambda sb: slice(sb * TILE, (sb + 1) * TILE)
    col_q, col_k, col_v = (slice((3 * rc + w) * GROUP, (3 * rc + w + 1) * GROUP) for w in range(3))

    s_same, s_next, vaug = {}, {}, {}
    first_block = -1 if has_prev else 0
    for j in range(first_block, subs):
        k_j = xp_ref[:, col_k] if j < 0 else x_ref[rows_of(j), col_k]
        v_j = xp_ref[:, col_v] if j < 0 else x_ref[rows_of(j), col_v]
        parts = ([0] if j >= 0 else []) + ([1] if j + 1 < subs else [])
        q_cat = jnp.concatenate([x_ref[rows_of(j + e), col_q] for e in parts], axis=0)
        bias = bias_ref[:, parts[0] * TILE:(parts[-1] + 1) * TILE]
        st = _dot_nt(_tile_rows(k_j, HEADS, kmask), q_cat) + bias
        if j < 0:
            st = st + jnp.where(first, NEG, 0.0)
        for pos, e in enumerate(parts):
            (s_same if e == 0 else s_next)[j + e] = st[:, pos * TILE:(pos + 1) * TILE]
        v_t = v_j.astype(F32).T.astype(BF16)
        vaug[j] = [jnp.concatenate([v_t[h * HEAD_DIM:(h + 1) * HEAD_DIM], ones_rows], axis=0)
                   for h in range(HEADS)]

    p_same, p_next, tops = {}, {}, {}
    for i in range(subs):
        ps, pn, tp = [], [], []
        for h in range(HEADS):
            hs = slice(h * TILE, (h + 1) * TILE)
            a = s_same[i][hs]
            m = jnp.max(a, axis=0, keepdims=True)
            if i in s_next:
                b = s_next[i][hs]
                m = jnp.maximum(m, jnp.max(b, axis=0, keepdims=True))
                pn.append(jnp.exp2(b - m).astype(BF16))
            ps.append(jnp.exp2(a - m).astype(BF16))
            tp.append(m)
        p_same[i], p_next[i], tops[i] = ps, pn, tp

    acc = {i: [None] * HEADS for i in range(subs)}
    for j in range(first_block, subs):
        for h in range(HEADS):
            cols = ([p_same[j][h]] if j >= 0 else []) + ([p_next[j + 1][h]] if j + 1 < subs else [])
            r = _dot(vaug[j][h], jnp.concatenate(cols, axis=1))
            targets = ([j] if j >= 0 else []) + ([j + 1] if j + 1 < subs else [])
            for pos, i in enumerate(targets):
                part = r[:, pos * TILE:(pos + 1) * TILE]
                acc[i][h] = part if acc[i][h] is None else acc[i][h] + part

    outs, lses = [], []
    for i in range(subs):
        o_t, lse_t = [], []
        for h in range(HEADS):
            l = acc[i][h][HEAD_DIM:HEAD_DIM + 1]
            o_t.append(acc[i][h][:HEAD_DIM] / l)
            lse_t.append(jnp.broadcast_to(tops[i][h] + jnp.log2(l), (HEAD_DIM, TILE)))
        outs.append(jnp.concatenate(o_t, axis=0).T)
        lses.append(jnp.concatenate(lse_t, axis=0).T)
    return outs, lses


DIL_SUBBLOCKS = 8


def _unpermutation(dil):
    t = np.arange(DIL_UNIT)
    mat = np.zeros((DIL_UNIT, DIL_UNIT), np.float32)
    mat[t, (t % dil) * (DIL_UNIT // dil) + t // dil] = 1.0
    return jnp.asarray(mat, BF16)


def _dilated_pattern(zc, bias, dil, *, batch, seq, merge_with=None):
    length = seq // dil
    subs = min(max(DIL_SUBBLOCKS // max(dil // 2, 1), 2), length // TILE)
    rows = subs * TILE
    tokens = rows * dil
    zc = zc.reshape(batch, length, dil * 3 * GROUP)
    blk = pl.BlockSpec((None, rows, dil * 3 * GROUP), lambda b, n: (b, n, 0))
    prev = pl.BlockSpec((None, TILE, dil * 3 * GROUP), lambda b, n: (b, jnp.maximum(n * subs - 1, 0), 0))
    nat = pl.BlockSpec((None, tokens, GROUP), lambda b, n: (b, n, 0))
    in_specs = [blk, prev, _const_spec((HEADS * TILE, 2 * TILE)), _const_spec((DIL_UNIT, DIL_UNIT))]
    args = [zc, zc, bias, _unpermutation(dil)]
    if merge_with is None:
        out_specs = [nat, nat]
        out_shape = [jax.ShapeDtypeStruct((batch, seq, GROUP), BF16),
                     jax.ShapeDtypeStruct((batch, seq, GROUP), F32)]
    else:
        in_specs += [nat] * len(merge_with)
        args += list(merge_with)
        out_specs = nat
        out_shape = jax.ShapeDtypeStruct((batch, seq, GROUP), BF16)
    out = pl.pallas_call(
        functools.partial(_dil_body, subs=subs, dil=dil, merge=merge_with is not None,
                          has_prev=length > rows),
        grid=(batch, length // rows),
        in_specs=in_specs, out_specs=out_specs, out_shape=out_shape,
        compiler_params=_params("parallel", "parallel"),
        name=f"dilated_d{dil}",
    )(*args)
    if merge_with is None:
        return out
    return out.reshape(batch * seq, GROUP)


def _dilated(zc_views, dil_bias, *, batch, seq):
    o1, l1 = _dilated_pattern(zc_views[0], dil_bias[0], DIL_PATTERNS[0][1], batch=batch, seq=seq)
    o2, l2 = _dilated_pattern(zc_views[1], dil_bias[1], DIL_PATTERNS[1][1], batch=batch, seq=seq)
    return _dilated_pattern(zc_views[2], dil_bias[2], DIL_PATTERNS[2][1], batch=batch, seq=seq,
                            merge_with=(o1, l1, o2, l2))


DIFF_Q = 256
DIFF_K = 256
DIFF_GROUPS = 2 * HEADS


DIFF_SAFE_LOG2 = 60.0
DIFF_BOUND_SLACK = 1.01


def _diff_running_max(q, qi, last, pairs, bias_tiles, kexp_ref, vaug_ref, acc_ref, sta_ref, stb_ref):
    def scores(s_ref, j):
        bias = bias_tiles(j)
        raw = _dot_nt(kexp_ref[jnp.minimum(j, last)], q)
        tops = []
        for g in range(DIFF_GROUPS):
            s = raw[g * DIFF_K:(g + 1) * DIFF_K] + bias[g % HEADS]
            s_ref[g * DIFF_K:(g + 1) * DIFF_K, :] = s
            tops.append(jnp.max(s, axis=0, keepdims=True))
        return tuple(tops)

    def consume(s_ref, tops, j, carry):
        ms, ls = carry
        jv = jnp.minimum(j, last)
        new_ms, new_ls = [], []
        for g in range(DIFF_GROUPS):
            mp, h = divmod(g, HEADS)
            m_new = jnp.maximum(ms[g], tops[g])
            p = jnp.exp2(s_ref[g * DIFF_K:(g + 1) * DIFF_K, :] - m_new).astype(BF16)
            alpha = jnp.exp2(ms[g] - m_new)
            r = _dot(vaug_ref[jv, h], p)
            acc_ref[mp, h] = alpha * acc_ref[mp, h] + r[:HEAD_DIM]
            new_ls.append(alpha * ls[g] + r[HEAD_DIM:HEAD_DIM + 1])
            new_ms.append(m_new)
        return tuple(new_ms), tuple(new_ls)

    def pair(jj, carry):
        tops_a, state = carry
        j = 2 * jj
        tops_b = scores(stb_ref, j + 1)
        state = consume(sta_ref, tops_a, j, state)
        tops_a = scores(sta_ref, j + 2)
        return tops_a, consume(stb_ref, tops_b, j + 1, state)

    init = (tuple(jnp.full((1, DIFF_Q), NEG, F32) for _ in range(DIFF_GROUPS)),
            tuple(jnp.zeros((1, DIFF_Q), F32) for _ in range(DIFF_GROUPS)))
    _, (_, ls) = lax.fori_loop(0, pairs, pair, (scores(sta_ref, 0), init))
    return ls


def _diff_body(q_ref, k_ref, vt_ref, bias_ref, lam_ref, sg_ref, lim_ref, y_ref,
               kexp_ref, vaug_ref, acc_ref, sta_ref, stb_ref, pa_ref, pb_ref, knorm_ref,
               *, lam_init, key_steps):
    qi = pl.program_id(1)
    qk_group = _group_mask(GROUP, GROUP, DIFF_QK_HALF, DIFF_QK_HALF).astype(BF16)

    @pl.when(qi == 0)
    def _():
        grp = lax.broadcasted_iota(jnp.int32, (DIFF_GROUPS * DIFF_K, GROUP), 0) // DIFF_K
        slot = lax.broadcasted_iota(jnp.int32, (DIFF_GROUPS * DIFF_K, GROUP), 1) // DIFF_QK_HALF
        kmask = slot == 2 * (grp % HEADS) + grp // HEADS
        ones_rows = (lax.broadcasted_iota(jnp.int32, (AUG_ROWS - HEAD_DIM, DIFF_K), 0) == 0).astype(BF16)

        def build(j, kmax):
            k_t = k_ref[pl.ds(pl.multiple_of(j * DIFF_K, DIFF_K), DIFF_K), :]
            kexp_ref[j] = jnp.where(kmask, jnp.concatenate([k_t] * DIFF_GROUPS, axis=0),
                                    jnp.zeros((), BF16))
            vt = jnp.concatenate([vt_ref[2 * j], vt_ref[2 * j + 1]], axis=1)
            for h in range(HEADS):
                vaug_ref[j, h] = jnp.concatenate([vt[h * HEAD_DIM:(h + 1) * HEAD_DIM], ones_rows], axis=0)
            k_f = k_t.astype(F32)
            return jnp.maximum(kmax, jnp.max(_group_sum(k_f * k_f, qk_group), axis=0, keepdims=True))

        knorm_ref[...] = lax.fori_loop(0, key_steps, build, jnp.zeros((1, GROUP), F32))

    acc_ref[...] = jnp.zeros_like(acc_ref)
    q = q_ref[...]
    last = key_steps - 1
    key_steps_needed = (qi + 1) * (DIFF_Q // DIFF_K)
    pairs = (key_steps_needed + 1) // 2

    def bias_tiles(j):
        base = (DIFF_Q // TILE) * qi - (DIFF_K // TILE) * j - DIFF_MIN_OFFSET
        tiles = {d: bias_ref[jnp.maximum(base + d, 0)]
                 for d in range(1 - DIFF_K // TILE, DIFF_Q // TILE)}
        return [jnp.concatenate(
            [jnp.concatenate([tiles[a - b][h * TILE:(h + 1) * TILE] for a in range(DIFF_Q // TILE)], axis=1)
             for b in range(DIFF_K // TILE)], axis=0) for h in range(HEADS)]

    q_f = q.astype(F32)
    bound2 = jnp.max(_group_sum(q_f * q_f, qk_group) * knorm_ref[...])
    no_overflow = bound2 * DIFF_BOUND_SLACK <= lim_ref[0, 0]

    def unshifted():
        def weights(p_ref, j):
            bias = bias_tiles(j)
            raw = _dot_nt(kexp_ref[jnp.minimum(j, last)], q)
            for g in range(DIFF_GROUPS):
                rows = slice(g * DIFF_K, (g + 1) * DIFF_K)
                p_ref[rows, :] = jnp.exp2(raw[rows] + bias[g % HEADS]).astype(BF16)

        def accumulate(p_ref, j, ls):
            jv = jnp.minimum(j, last)
            new_ls = []
            for g in range(DIFF_GROUPS):
                mp, h = divmod(g, HEADS)
                r = _dot(vaug_ref[jv, h], p_ref[g * DIFF_K:(g + 1) * DIFF_K, :])
                acc_ref[mp, h] = acc_ref[mp, h] + r[:HEAD_DIM]
                new_ls.append(ls[g] + r[HEAD_DIM:HEAD_DIM + 1])
            return tuple(new_ls)

        def pair(jj, ls):
            j = 2 * jj
            weights(pb_ref, j + 1)
            ls = accumulate(pa_ref, j, ls)
            weights(pa_ref, j + 2)
            return accumulate(pb_ref, j + 1, ls)

        weights(pa_ref, 0)
        return lax.fori_loop(0, pairs, pair,
                             tuple(jnp.zeros((1, DIFF_Q), F32) for _ in range(DIFF_GROUPS)))

    def running_max():
        return _diff_running_max(q, qi, last, pairs, bias_tiles, kexp_ref, vaug_ref, acc_ref,
                                 sta_ref, stb_ref)

    ls = lax.cond(no_overflow, unshifted, running_max)

    lv = lam_ref[...]
    lam = (jnp.exp(jnp.sum(lv[0:1] * lv[1:2], axis=-1, keepdims=True))
           - jnp.exp(jnp.sum(lv[2:3] * lv[3:4], axis=-1, keepdims=True)) + lam_init)
    outs = []
    for h in range(HEADS):
        o = acc_ref[0, h] / ls[h] - lam * (acc_ref[1, h] / ls[HEADS + h])
        ms_o = jnp.mean(o * o, axis=0, keepdims=True)
        outs.append(o * lax.rsqrt(ms_o + SUBLN_EPS) * sg_ref[...] * (1.0 - lam_init))
    y_ref[...] = jnp.concatenate(outs, axis=0).T.astype(BF16)


def _diff_attention(zd, vt, bias, score_limit2, lam_vecs, subln_cols, *, lam_init, batch, seq):
    zd = zd.reshape(batch, seq, 2 * GROUP)
    key_tiles = seq // TILE
    key_steps = seq // DIFF_K
    vt = vt.reshape(batch, key_tiles, GROUP, TILE)
    y = pl.pallas_call(
        functools.partial(_diff_body, lam_init=lam_init, key_steps=key_steps),
        grid=(batch, seq // DIFF_Q),
        in_specs=[pl.BlockSpec((None, DIFF_Q, GROUP), lambda b, i: (b, i, 0)),
                  pl.BlockSpec((None, seq, GROUP), lambda b, i: (b, 0, 1), pipeline_mode=pl.Buffered(1)),
                  pl.BlockSpec((None, key_tiles, GROUP, TILE), lambda b, i: (b, 0, 0, 0),
                               pipeline_mode=pl.Buffered(1)),
                  _const_spec((key_tiles - DIFF_MIN_OFFSET, HEADS * TILE, TILE)),
                  _const_spec((4, DIFF_QK_HALF)), _const_spec((HEAD_DIM, DIFF_Q)),
                  pl.BlockSpec(memory_space=pltpu.SMEM)],
        out_specs=pl.BlockSpec((None, DIFF_Q, GROUP), lambda b, i: (b, i, 0)),
        out_shape=jax.ShapeDtypeStruct((batch, seq, GROUP), BF16),
        scratch_shapes=[pltpu.VMEM((key_steps, DIFF_GROUPS * DIFF_K, GROUP), BF16),
                        pltpu.VMEM((key_steps, HEADS, AUG_ROWS, DIFF_K), BF16),
                        pltpu.VMEM((2, HEADS, HEAD_DIM, DIFF_Q), F32),
                        pltpu.VMEM((DIFF_GROUPS * DIFF_K, DIFF_Q), F32),
                        pltpu.VMEM((DIFF_GROUPS * DIFF_K, DIFF_Q), F32),
                        pltpu.VMEM((DIFF_GROUPS * DIFF_K, DIFF_Q), BF16),
                        pltpu.VMEM((DIFF_GROUPS * DIFF_K, DIFF_Q), BF16),
                        pltpu.VMEM((1, GROUP), F32)],
        compiler_params=_params("parallel", "arbitrary"),
        name="diff_attn",
    )(zd, zd, vt, bias, lam_vecs, subln_cols, score_limit2)
    return y.reshape(batch * seq, GROUP)


KV_ROWS = 512


def _mem_kv_body(m_ref, g_ref, w_ref, k_ref, v_ref):
    u = _rms(m_ref[...], g_ref[...]).astype(BF16)
    for c in range(D_MODEL // GROUP):
        sl = slice(c * GROUP, (c + 1) * GROUP)
        k_ref[:, sl] = _dot(u, w_ref[:, sl]).astype(BF16)
        v_ref[:, sl] = _dot(u, w_ref[:, D_MODEL + c * GROUP: D_MODEL + (c + 1) * GROUP]).astype(BF16)


def _mem_kv(mem, g, w):
    n = mem.shape[0]
    row = pl.BlockSpec((KV_ROWS, D_MODEL), lambda i: (i, 0))
    return pl.pallas_call(
        _mem_kv_body,
        grid=(n // KV_ROWS,),
        in_specs=[row, _const_spec((1, D_MODEL)), _const_spec((D_MODEL, 2 * D_MODEL))],
        out_specs=[row, row],
        out_shape=[jax.ShapeDtypeStruct((n, D_MODEL), BF16)] * 2,
        compiler_params=_params("parallel"),
        name="mem_kv",
    )(mem, g, w)


XATTN_ROWS = 512


def _xattn_body(x_ref, ya_ref, yb_ref, yc_ref, yd_ref, wout_ref, g_ref, wq_ref, k_ref, v_ref, wo_ref,
                o_ref, q_scr, a_scr):
    x = x_ref[...]
    for gi, y_ref in enumerate((ya_ref, yb_ref, yc_ref, yd_ref)):
        x = x + _dot(y_ref[...], wout_ref[gi * GROUP:(gi + 1) * GROUP, :])
    u = _rms(x, g_ref[...]).astype(BF16)
    for c in range(D_MODEL // GROUP):
        sl = slice(c * GROUP, (c + 1) * GROUP)
        q_scr[:, sl] = _dot(u, wq_ref[:, sl]).astype(BF16)
    for h in range(MEM_HEADS):
        sl = slice(h * MEM_HEAD_DIM, (h + 1) * MEM_HEAD_DIM)
        s = _dot_nt(q_scr[:, sl], k_ref[:, sl]) * (MEM_HEAD_DIM ** -0.5)
        e = jnp.exp(s - jnp.max(s, axis=-1, keepdims=True))
        l = jnp.sum(e, axis=-1, keepdims=True)
        a_scr[:, sl] = (_dot(e.astype(BF16), v_ref[:, sl]) / l).astype(BF16)
    o_ref[...] = x + _dot(a_scr[...], wo_ref[...])


def _xattn(h, ys, w_out, g, wq, k, v, wo, *, batch, seq):
    h3 = h.reshape(batch, seq, D_MODEL)
    ys = [y.reshape(batch, seq, GROUP) for y in ys]
    k3 = k.reshape(batch, MEM_LEN, D_MODEL)
    v3 = v.reshape(batch, MEM_LEN, D_MODEL)
    row = pl.BlockSpec((None, XATTN_ROWS, D_MODEL), lambda b, i: (b, i, 0))
    grp = pl.BlockSpec((None, XATTN_ROWS, GROUP), lambda b, i: (b, i, 0))
    mem = pl.BlockSpec((None, MEM_LEN, D_MODEL), lambda b, i: (b, 0, 0))
    weight = _const_spec((D_MODEL, D_MODEL))
    out = pl.pallas_call(
        _xattn_body,
        grid=(batch, seq // XATTN_ROWS),
        in_specs=[row, grp, grp, grp, grp, weight, _const_spec((1, D_MODEL)), weight, mem, mem, weight],
        out_specs=row,
        out_shape=jax.ShapeDtypeStruct((batch, seq, D_MODEL), F32),
        scratch_shapes=[pltpu.VMEM((XATTN_ROWS, D_MODEL), BF16), pltpu.VMEM((XATTN_ROWS, D_MODEL), BF16)],
        compiler_params=_params("parallel", "parallel"),
        name="xattn",
    )(h3, *ys, w_out, g, wq, k3, v3, wo)
    return out.reshape(batch * seq, D_MODEL)


def _per_head_lanes(x):
    return jnp.repeat(x, HEAD_DIM, axis=-1)


def _in_proj_weight(w_in):
    g = GROUP
    q_a, k_a, v_a, o_a = (w_in[:, i * g:(i + 1) * g] for i in range(4))
    ig = w_in[:, 4 * g:4 * g + HEADS]
    fg = w_in[:, 4 * g + HEADS:4 * g + 2 * HEADS]
    rest = w_in[:, 4 * g + 2 * HEADS:]
    pool, q_c, k_c, v_c, q_d, k_d, v_d = (rest[:, i * g:(i + 1) * g] for i in range(7))
    q_c = q_c * DIL_SCORE_SCALE
    k_d = k_d * DIFF_SCORE_SCALE
    cols = [q_a, k_a, v_a, o_a, _per_head_lanes(ig), _per_head_lanes(fg), pool,
            q_c, k_c, v_c, q_d, k_d]
    return jnp.concatenate(cols, axis=1).astype(BF16), v_d.astype(BF16)


def _block_diag(w):
    g, c, _ = w.shape
    eye = jnp.eye(g, dtype=w.dtype)
    return (eye[:, None, :, None] * w[:, :, None, :]).reshape(g * c, g * c)


def kernel(x, mem, t5_bias, ffn1_norm, ffn1_w_gate, ffn1_w_up, ffn1_w_down, mix_norm, w_in,
           mlstm_conv_w, mlstm_conv_b, mlstm_gate_b, mlstm_norm, pool_w, pool_scale,
           diff_lambda, diff_subln, w_out, xattn_norm, mem_norm, xattn_wq, xattn_wkv, xattn_wo,
           ffn2_norm, ffn2_w_gate, ffn2_w_up, ffn2_w_down, final_norm):
    batch, seq, _ = x.shape
    n = batch * seq
    dil_bias, diff_bias, diff_limit2 = _bias_tiles(t5_bias, seq)
    h = x.reshape(n, D_MODEL)
    mem2 = mem.reshape(batch * MEM_LEN, D_MODEL)
    row = lambda v: v.reshape(1, -1)
    for l in range(DEPTH):
        lam_init = 0.8 - 0.6 * math.exp(-0.3 * l)
        h = _ffn(h, row(ffn1_norm[l]), ffn1_w_gate[l].astype(BF16), ffn1_w_up[l].astype(BF16),
                 ffn1_w_down[l].astype(BF16), row(final_norm), final=False)
        za, zg, zp, zc, zd, vt, zc4, zc16 = _in_proj(h, row(mix_norm[l]), *_in_proj_weight(w_in[l]))
        ya = _mlstm(za, zg, mlstm_conv_w[l], row(mlstm_conv_b[l]),
                    row(_per_head_lanes(mlstm_gate_b[l].reshape(2, HEADS))), row(mlstm_norm[l]),
                    batch=batch, seq=seq)
        yb = _pool(zp, _block_diag(pool_w[l]).astype(BF16), row(pool_scale[l]), batch=batch, seq=seq)
        yc = _dilated((zc, zc4, zc16), dil_bias, batch=batch, seq=seq)
        yd = _diff_attention(zd, vt, diff_bias, diff_limit2, diff_lambda[l],
                             jnp.broadcast_to(diff_subln[l][:, None], (HEAD_DIM, DIFF_Q)),
                             lam_init=lam_init, batch=batch, seq=seq)
        k_mem, v_mem = _mem_kv(mem2, row(mem_norm[l]), xattn_wkv[l].astype(BF16))
        h = _xattn(h, (ya, yb, yc, yd), w_out[l].astype(BF16),
                   row(xattn_norm[l]), xattn_wq[l].astype(BF16), k_mem, v_mem,
                   xattn_wo[l].astype(BF16), batch=batch, seq=seq)
        h = _ffn(h, row(ffn2_norm[l]), ffn2_w_gate[l].astype(BF16), ffn2_w_up[l].astype(BF16),
                 ffn2_w_down[l].astype(BF16), row(final_norm), final=(l == DEPTH - 1))
    return h.reshape(batch, seq, D_MODEL)
```

```python
import functools
import math

import jax
import jax.numpy as jnp
import numpy as np
from jax import lax
from jax.experimental import pallas as pl
from jax.experimental.pallas import tpu as pltpu

F32 = jnp.float32
BF16 = jnp.bfloat16

D_MODEL = 1024
D_FF = 2816
DEPTH = 4
GROUP = 256
HEADS = 4
HEAD_DIM = GROUP // HEADS
MEM_LEN = 256
MEM_HEADS = 4
MEM_HEAD_DIM = D_MODEL // MEM_HEADS
MLSTM_CHUNK = 64
CONV_WIDTH = 4
POOL_WINDOWS = (2, 4, 8, 16)
DIL_PATTERNS = ((128, 1), (512, 4), (2048, 16))
DIL_BACK = 128
DIFF_QK_HALF = HEAD_DIM // 2
T5_BUCKETS = 32
T5_MAX_DIST = 2048
RMS_EPS = 1e-6
SUBLN_EPS = 1e-5
NEG = -1e30
LOG2E = math.log2(math.e)
DIFF_SCORE_SCALE = (DIFF_QK_HALF ** -0.5) * LOG2E
DIL_SCORE_SCALE = (HEAD_DIM ** -0.5) * LOG2E
AUG_ROWS = HEAD_DIM + 16
DIFF_MIN_OFFSET = -3
TILE = 128

VMEM_LIMIT_BYTES = 56 * 1024 * 1024


def _rms(xf, g, eps=RMS_EPS):
    return xf * lax.rsqrt(jnp.mean(xf * xf, axis=-1, keepdims=True) + eps) * g


def _const_spec(shape):
    zeros = (0,) * len(shape)
    return pl.BlockSpec(shape, lambda *_: zeros, pipeline_mode=pl.Buffered(1))


def _params(*sem):
    return pltpu.CompilerParams(dimension_semantics=sem, vmem_limit_bytes=VMEM_LIMIT_BYTES)


def _group_mask(rows, cols, row_group, col_group):
    r = lax.broadcasted_iota(jnp.int32, (rows, cols), 0) // row_group
    c = lax.broadcasted_iota(jnp.int32, (rows, cols), 1) // col_group
    return r == c


def _tile_rows(x, reps, mask):
    return jnp.where(mask, jnp.concatenate([x] * reps, axis=0), jnp.zeros((), x.dtype))


def _dot(a, b):
    return jnp.dot(a, b, preferred_element_type=F32)


def _dot_nt(a, b):
    return lax.dot_general(a, b, (((1,), (1,)), ((), ())), preferred_element_type=F32)


def _dot_tn(a, b):
    return lax.dot_general(a, b, (((0,), (0,)), ((), ())), preferred_element_type=F32)


def _group_sum(x, ones_bd):
    hi = x.astype(BF16)
    lo = (x - hi.astype(F32)).astype(BF16)
    return _dot(hi, ones_bd) + _dot(lo, ones_bd)


FFN_ROWS = 1024
FFN_COLS = 256


def _ffn_body(x_ref, g_ref, wg_ref, wu_ref, wd_ref, fg_ref, o_ref, act_ref, *, final):
    x = x_ref[...]
    u = _rms(x, g_ref[...]).astype(BF16)
    for c in range(D_FF // FFN_COLS):
        sl = slice(c * FFN_COLS, (c + 1) * FFN_COLS)
        gate = _dot(u, wg_ref[:, sl])
        up = _dot(u, wu_ref[:, sl])
        act_ref[:, sl] = (gate * jax.nn.sigmoid(gate) * up).astype(BF16)
    y = x + 0.5 * _dot(act_ref[...], wd_ref[...])
    if final:
        y = _rms(y, fg_ref[...])
    o_ref[...] = y


def _ffn(h, g, wg, wu, wd, fg, *, final):
    n = h.shape[0]
    row = pl.BlockSpec((FFN_ROWS, D_MODEL), lambda i: (i, 0))
    return pl.pallas_call(
        functools.partial(_ffn_body, final=final),
        grid=(n // FFN_ROWS,),
        in_specs=[row, _const_spec((1, D_MODEL)), _const_spec((D_MODEL, D_FF)),
                  _const_spec((D_MODEL, D_FF)), _const_spec((D_FF, D_MODEL)),
                  _const_spec((1, D_MODEL))],
        out_specs=row,
        out_shape=jax.ShapeDtypeStruct((n, D_MODEL), F32),
        scratch_shapes=[pltpu.VMEM((FFN_ROWS, D_FF), BF16)],
        compiler_params=_params("parallel"),
        name="ffn_final" if final else "ffn",
    )(h, g, wg, wu, wd, fg)


PROJ_ROWS = 1024
PROJ_OUTS = (("a", 4 * GROUP, F32), ("g", 2 * GROUP, F32), ("p", GROUP, F32),
             ("c", 3 * GROUP, BF16), ("d", 2 * GROUP, BF16))
PROJ_WIDTH = sum(w for _, w, _ in PROJ_OUTS)


PROJ_DILATIONS = tuple(d for _, d in DIL_PATTERNS if d > 1)


def _in_proj_body(x_ref, g_ref, w_ref, wv_ref, perm_ref, *o_refs):
    u = _rms(x_ref[...], g_ref[...]).astype(BF16)
    off = 0
    for o_ref, (_, width, dtype) in zip(o_refs, PROJ_OUTS):
        for c in range(width // GROUP):
            z = _dot(u, w_ref[:, off + c * GROUP: off + (c + 1) * GROUP])
            o_ref[:, c * GROUP:(c + 1) * GROUP] = z.astype(dtype)
        off += width
    vt_ref = o_refs[len(PROJ_OUTS)]
    v = _dot(u, wv_ref[...])
    for t in range(PROJ_ROWS // TILE):
        vt_ref[t] = v[t * TILE:(t + 1) * TILE].T.astype(BF16)
    for pi, d in enumerate(PROJ_DILATIONS):
        per_class = PROJ_UNIT // d
        for un in range(PROJ_ROWS // PROJ_UNIT):
            zp = _dot(perm_ref[pi], o_refs[3][un * PROJ_UNIT:(un + 1) * PROJ_UNIT, :]).astype(BF16)
            for r in range(d):
                o_refs[len(PROJ_OUTS) + 1 + pi][un * per_class:(un + 1) * per_class,
                                                r * 3 * GROUP:(r + 1) * 3 * GROUP] = \
                    zp[r * per_class:(r + 1) * per_class]


PROJ_UNIT = 2 * TILE


def _class_permutations():
    mats = np.zeros((len(PROJ_DILATIONS), PROJ_UNIT, PROJ_UNIT), np.float32)
    for pi, d in enumerate(PROJ_DILATIONS):
        t = np.arange(PROJ_UNIT)
        mats[pi, (t % d) * (PROJ_UNIT // d) + t // d, t] = 1.0
    return jnp.asarray(mats, BF16)


def _in_proj(h, g, w, wv):
    n = h.shape[0]
    tiles = PROJ_ROWS // TILE
    return pl.pallas_call(
        _in_proj_body,
        grid=(n // PROJ_ROWS,),
        in_specs=[pl.BlockSpec((PROJ_ROWS, D_MODEL), lambda i: (i, 0)),
                  _const_spec((1, D_MODEL)), _const_spec((D_MODEL, PROJ_WIDTH)),
                  _const_spec((D_MODEL, GROUP)),
                  _const_spec((len(PROJ_DILATIONS), PROJ_UNIT, PROJ_UNIT))],
        out_specs=[pl.BlockSpec((PROJ_ROWS, w_), lambda i: (i, 0)) for _, w_, _ in PROJ_OUTS]
        + [pl.BlockSpec((tiles, GROUP, TILE), lambda i: (i, 0, 0))]
        + [pl.BlockSpec((PROJ_ROWS // d, d * 3 * GROUP), lambda i: (i, 0)) for d in PROJ_DILATIONS],
        out_shape=[jax.ShapeDtypeStruct((n, w_), dt) for _, w_, dt in PROJ_OUTS]
        + [jax.ShapeDtypeStruct((n // TILE, GROUP, TILE), BF16)]
        + [jax.ShapeDtypeStruct((n // d, d * 3 * GROUP), BF16) for d in PROJ_DILATIONS],
        compiler_params=_params("parallel"),
        name="in_proj",
    )(h, g, w, wv, _class_permutations())


ML_ROWS = 512
ML_HALO = 8


def _chunk_scan(x, rin, op, fill):
    s = 1
    while s < MLSTM_CHUNK:
        x = op(x, jnp.where(rin >= s, pltpu.roll(x, s, 0), fill))
        s *= 2
    return x


def _mlstm_body(za_ref, zg_ref, cw_ref, cb_ref, gb_ref, ng_ref, y_ref,
                buf_ref, tail_ref, hh_ref, c_ref, n_ref, m_ref):
    L = MLSTM_CHUNK

    @pl.when(pl.program_id(1) == 0)
    def _():
        tail_ref[...] = jnp.zeros_like(tail_ref)
        c_ref[...] = jnp.zeros_like(c_ref)
        n_ref[...] = jnp.zeros_like(n_ref)
        m_ref[...] = jnp.zeros_like(m_ref)

    buf_ref[0:ML_HALO, :] = tail_ref[...]
    buf_ref[ML_HALO:, :] = za_ref[:, 0:2 * GROUP]
    tail_ref[...] = za_ref[ML_ROWS - ML_HALO:, 0:2 * GROUP]
    conv = cb_ref[...]
    for j in range(CONV_WIDTH):
        conv = conv + buf_ref[pl.ds(ML_HALO - (CONV_WIDTH - 1) + j, ML_ROWS), :] * cw_ref[j:j + 1, :]
    qk = conv * jax.nn.sigmoid(conv)
    q = qk[:, :GROUP]
    k = qk[:, GROUP:] * (HEAD_DIM ** -0.5)
    v = za_ref[:, 2 * GROUP:3 * GROUP].astype(BF16)
    q_bf, k_bf = q.astype(BF16), k.astype(BF16)

    ii = zg_ref[:, :GROUP] + gb_ref[:, :GROUP]
    fx = zg_ref[:, GROUP:] + gb_ref[:, GROUP:]
    lf = jnp.minimum(fx, 0.0) - jnp.log1p(jnp.exp(-jnp.abs(fx)))
    rin = lax.broadcasted_iota(jnp.int32, (ML_ROWS, GROUP), 0) % L
    b = _chunk_scan(lf, rin, jnp.add, 0.0)
    a = ii - b
    ca = _chunk_scan(a, rin, jnp.maximum, NEG)

    bd = _group_mask(GROUP, GROUP, HEAD_DIM, HEAD_DIM)
    ones_bd = bd.astype(BF16)
    row = lax.broadcasted_iota(jnp.int32, (L, GROUP), 0)
    key = lax.broadcasted_iota(jnp.int32, (L, GROUP), 1) % L
    causal = key <= row
    diag = key == row

    m_prev = m_ref[...]
    for c in range(ML_ROWS // L):
        rs = slice(c * L, (c + 1) * L)
        q_c, k_c, v_c = q[rs], k[rs], v[rs]
        q_b = q_bf[rs]
        a_c, b_c = a[rs], b[rs]
        g = jnp.maximum(m_prev, ca[rs])
        g_last = g[L - 1:L]
        a_row = jnp.sum(jnp.where(diag, a_c, 0.0), axis=0, keepdims=True)
        decay = jnp.exp(jnp.where(causal, a_row - g, NEG))
        sc = _dot_nt(q_b, _tile_rows(k_bf[rs], HEADS, bd)) * decay
        inter = jnp.exp(m_prev - g)
        num = inter * _dot(q_b, c_ref[...].astype(BF16)) + _dot(sc.astype(BF16), _tile_rows(v_c, HEADS, bd))
        den = inter * _group_sum(q_c * n_ref[...], ones_bd) + _group_sum(sc, ones_bd)
        hh_ref[rs, :] = num / jnp.maximum(jnp.abs(den), jnp.exp(-(b_c + g)))
        kw = k_c * jnp.exp(a_c - g_last)
        carry = jnp.exp(m_prev - g_last)
        c_ref[...] = carry * c_ref[...] + jnp.where(bd, _dot_tn(kw.astype(BF16), v_c), 0.0)
        n_ref[...] = carry * n_ref[...] + jnp.sum(kw, axis=0, keepdims=True)
        m_prev = b_c[L - 1:L] + g_last
    m_ref[...] = m_prev

    hh = hh_ref[...]
    mu = _group_sum(hh, ones_bd) * (1.0 / HEAD_DIM)
    dev = hh - mu
    var = _group_sum(dev * dev, ones_bd) * (1.0 / HEAD_DIM)
    o_gate = jax.nn.sigmoid(za_ref[:, 3 * GROUP:])
    y_ref[...] = (dev * lax.rsqrt(var + RMS_EPS) * ng_ref[...] * o_gate).astype(BF16)


def _mlstm(za, zg, conv_w, conv_b, gate_b, norm_g, *, batch, seq):
    za = za.reshape(batch, seq, 4 * GROUP)
    zg = zg.reshape(batch, seq, 2 * GROUP)
    y = pl.pallas_call(
        _mlstm_body,
        grid=(batch, seq // ML_ROWS),
        in_specs=[pl.BlockSpec((None, ML_ROWS, 4 * GROUP), lambda b, j: (b, j, 0)),
                  pl.BlockSpec((None, ML_ROWS, 2 * GROUP), lambda b, j: (b, j, 0)),
                  _const_spec((CONV_WIDTH, 2 * GROUP)), _const_spec((1, 2 * GROUP)),
                  _const_spec((1, 2 * GROUP)), _const_spec((1, GROUP))],
        out_specs=pl.BlockSpec((None, ML_ROWS, GROUP), lambda b, j: (b, j, 0)),
        out_shape=jax.ShapeDtypeStruct((batch, seq, GROUP), BF16),
        scratch_shapes=[pltpu.VMEM((ML_ROWS + ML_HALO, 2 * GROUP), F32),
                        pltpu.VMEM((ML_HALO, 2 * GROUP), F32),
                        pltpu.VMEM((ML_ROWS, GROUP), F32),
                        pltpu.VMEM((GROUP, GROUP), F32),
                        pltpu.VMEM((1, GROUP), F32),
                        pltpu.VMEM((1, GROUP), F32)],
        compiler_params=_params("parallel", "arbitrary"),
        name="mlstm",
    )(za, zg, conv_w, conv_b, gate_b, norm_g)
    return y.reshape(batch * seq, GROUP)


POOL_ROWS = 1024
POOL_HALO = 16


def _pool_body(u_ref, w_ref, s_ref, y_ref, buf_ref, tail_ref):
    j = pl.program_id(1)

    @pl.when(j == 0)
    def _():
        tail_ref[...] = jnp.zeros_like(tail_ref)

    buf_ref[0:POOL_HALO, :] = tail_ref[...]
    buf_ref[POOL_HALO:, :] = u_ref[...]
    tail_ref[...] = u_ref[POOL_ROWS - POOL_HALO:, :]
    sums, s = [], buf_ref[...]
    for shift in (1, 2, 4, 8):
        s = s + pltpu.roll(s, shift, 0)
        sums.append(s[POOL_HALO:])
    u = u_ref[...]
    lane_group = lax.broadcasted_iota(jnp.int32, (POOL_ROWS, GROUP), 1) // HEAD_DIM
    t = j * POOL_ROWS + lax.broadcasted_iota(jnp.int32, (POOL_ROWS, GROUP), 0)
    total, win = sums[3], jnp.full((POOL_ROWS, GROUP), POOL_WINDOWS[3], jnp.int32)
    for gi in (2, 1, 0):
        total = jnp.where(lane_group == gi, sums[gi], total)
        win = jnp.where(lane_group == gi, POOL_WINDOWS[gi], win)
    mean = total / jnp.minimum(t + 1, win).astype(F32)
    y = _dot((mean - u).astype(BF16), w_ref[...]) * s_ref[...]
    y_ref[...] = y.astype(BF16)


def _pool(zp, w_bd, scale, *, batch, seq):
    zp = zp.reshape(batch, seq, GROUP)
    y = pl.pallas_call(
        _pool_body,
        grid=(batch, seq // POOL_ROWS),
        in_specs=[pl.BlockSpec((None, POOL_ROWS, GROUP), lambda b, j: (b, j, 0)),
                  _const_spec((GROUP, GROUP)), _const_spec((1, GROUP))],
        out_specs=pl.BlockSpec((None, POOL_ROWS, GROUP), lambda b, j: (b, j, 0)),
        out_shape=jax.ShapeDtypeStruct((batch, seq, GROUP), BF16),
        scratch_shapes=[pltpu.VMEM((POOL_ROWS + POOL_HALO, GROUP), F32),
                        pltpu.VMEM((POOL_HALO, GROUP), F32)],
        compiler_params=_params("parallel", "arbitrary"),
        name="pool",
    )(zp, w_bd, scale)
    return y.reshape(batch * seq, GROUP)


def _toeplitz_body(w_ref, o_ref):
    x = jnp.broadcast_to(w_ref[...], (TILE, 2 * TILE))
    o_ref[...] = pltpu.roll(x, 0, 1, stride=1, stride_axis=0)[:, :TILE]


def _toeplitz(rows):
    n = rows.shape[0]
    return pl.pallas_call(
        _toeplitz_body,
        grid=(n,),
        in_specs=[pl.BlockSpec((None, 1, 2 * TILE), lambda i: (i, 0, 0))],
        out_specs=pl.BlockSpec((None, TILE, TILE), lambda i: (i, 0, 0)),
        out_shape=jax.ShapeDtypeStruct((n, TILE, TILE), F32),
        compiler_params=_params("parallel"),
        name="toeplitz",
    )(rows.reshape(n, 1, 2 * TILE))


_TOEPLITZ_X = np.where(np.arange(2 * TILE) <= TILE, -np.arange(2 * TILE), 2 * TILE - np.arange(2 * TILE))


def _t5_bucket(dist):
    max_exact = T5_BUCKETS // 2
    d = jnp.maximum(dist, 1).astype(F32)
    large = max_exact + (jnp.log(d / max_exact) / math.log(T5_MAX_DIST / max_exact)
                         * (T5_BUCKETS - max_exact)).astype(jnp.int32)
    large = jnp.minimum(large, T5_BUCKETS - 1)
    return jnp.where(dist < max_exact, dist, large)


def _bias_tiles(t5_bias, seq):
    x = jnp.asarray(_TOEPLITZ_X, jnp.int32)
    rows = []
    for (w, d) in DIL_PATTERNS:
        tab = t5_bias[_t5_bucket(jnp.arange(DIL_BACK + 1) * d), :HEADS].T * LOG2E
        same = jnp.where(x <= 0, tab[:, jnp.clip(-x, 0, DIL_BACK)], NEG)
        nxt = jnp.where(x >= 0, tab[:, jnp.clip(DIL_BACK - x, 0, DIL_BACK)], NEG)
        rows.append(jnp.stack([same, nxt], axis=1))
    dil = _toeplitz(jnp.stack(rows).reshape(-1, 2 * TILE))
    dil = dil.reshape(len(DIL_PATTERNS), HEADS, 2, TILE, TILE).transpose(0, 1, 3, 2, 4)
    dil = dil.reshape(len(DIL_PATTERNS), HEADS * TILE, 2 * TILE)
    tab = t5_bias[_t5_bucket(jnp.arange(seq)), HEADS:].T * LOG2E
    noff = seq // TILE - DIFF_MIN_OFFSET
    dist = (jnp.arange(noff)[:, None] + DIFF_MIN_OFFSET) * TILE - x[None, :]
    rows = jnp.where(dist >= 0, tab[:, jnp.clip(dist, 0, seq - 1)], NEG)
    diff = _toeplitz(rows.transpose(1, 0, 2).reshape(-1, 2 * TILE))
    room = jnp.maximum(DIFF_SAFE_LOG2 - jnp.max(jnp.abs(tab)), 0.0)
    return dil, diff.reshape(noff, HEADS * TILE, TILE), (room * room).reshape(1, 1)


DIL_UNIT = 2 * TILE


def _dil_body(*refs, subs, dil, merge, has_prev):
    if merge:
        x_ref, xp_ref, bias_ref, unperm_ref, o1_ref, l1_ref, o2_ref, l2_ref, y_ref = refs
    else:
        x_ref, xp_ref, bias_ref, unperm_ref, o_ref, lse_ref = refs
    per_class = [_dil_class(x_ref, xp_ref, bias_ref, rc=rc, subs=subs, has_prev=has_prev)
                 for rc in range(dil)]
    piece = DIL_UNIT // dil
    class_o = [jnp.concatenate(pc[0], axis=0) for pc in per_class]
    class_l = [jnp.concatenate(pc[1], axis=0) for pc in per_class]
    for u in range(dil * subs * TILE // DIL_UNIT):
        take = lambda arrs: jnp.concatenate([a[u * piece:(u + 1) * piece] for a in arrs], axis=0)
        if dil == 1:
            o, lse = take(class_o), take(class_l)
        else:
            o_c = take(class_o).astype(BF16)
            l_c = take(class_l)
            l_hi = l_c.astype(BF16)
            l_lo = (l_c - l_hi.astype(F32)).astype(BF16)
            o = _dot(unperm_ref[...], o_c)
            lse = _dot(unperm_ref[...], l_hi) + _dot(unperm_ref[...], l_lo)
        rs = slice(u * DIL_UNIT, (u + 1) * DIL_UNIT)
        if merge:
            o1, l1 = o1_ref[rs, :].astype(F32), l1_ref[rs, :]
            o2, l2 = o2_ref[rs, :].astype(F32), l2_ref[rs, :]
            top = jnp.maximum(jnp.maximum(l1, l2), lse)
            w1, w2, w3 = jnp.exp2(l1 - top), jnp.exp2(l2 - top), jnp.exp2(lse - top)
            y_ref[rs, :] = ((w1 * o1 + w2 * o2 + w3 * o) / (w1 + w2 + w3)).astype(BF16)
        else:
            o_ref[rs, :] = o.astype(BF16)
            lse_ref[rs, :] = lse


def _dil_class(x_ref, xp_ref, bias_ref, *, rc, subs, has_prev):
    first = pl.program_id(1) == 0
    kmask = _group_mask(HEADS * TILE, GROUP, TILE, HEAD_DIM)
    ones_rows = (lax.broadcasted_iota(jnp.int32, (AUG_ROWS - HEAD_DIM, TILE), 0) == 0).astype(BF16)
    rows_of = lambda sb: slice(sb * TILE, (sb + 1) * TILE)
    col_q, col_k, col_v = (slice((3 * rc + w) * GROUP, (3 * rc + w + 1) * GROUP) for w in range(3))

    s_same, s_next, vaug = {}, {}, {}
    first_block = -1 if has_prev else 0
    for j in range(first_block, subs):
        k_j = xp_ref[:, col_k] if j < 0 else x_ref[rows_of(j), col_k]
        v_j = xp_ref[:, col_v] if j < 0 else x_ref[rows_of(j), col_v]
        parts = ([0] if j >= 0 else []) + ([1] if j + 1 < subs else [])
        q_cat = jnp.concatenate([x_ref[rows_of(j + e), col_q] for e in parts], axis=0)
        bias = bias_ref[:, parts[0] * TILE:(parts[-1] + 1) * TILE]
        st = _dot_nt(_tile_rows(k_j, HEADS, kmask), q_cat) + bias
        if j < 0:
            st = st + jnp.where(first, NEG, 0.0)
        for pos, e in enumerate(parts):
            (s_same if e == 0 else s_next)[j + e] = st[:, pos * TILE:(pos + 1) * TILE]
        v_t = v_j.astype(F32).T.astype(BF16)
        vaug[j] = [jnp.concatenate([v_t[h * HEAD_DIM:(h + 1) * HEAD_DIM], ones_rows], axis=0)
                   for h in range(HEADS)]

    p_same, p_next, tops = {}, {}, {}
    for i in range(subs):
        ps, pn, tp = [], [], []
        for h in range(HEADS):
            hs = slice(h * TILE, (h + 1) * TILE)
            a = s_same[i][hs]
            m = jnp.max(a, axis=0, keepdims=True)
            if i in s_next:
                b = s_next[i][hs]
                m = jnp.maximum(m, jnp.max(b, axis=0, keepdims=True))
                pn.append(jnp.exp2(b - m).astype(BF16))
            ps.append(jnp.exp2(a - m).astype(BF16))
            tp.append(m)
        p_same[i], p_next[i], tops[i] = ps, pn, tp

    acc = {i: [None] * HEADS for i in range(subs)}
    for j in range(first_block, subs):
        for h in range(HEADS):
            cols = ([p_same[j][h]] if j >= 0 else []) + ([p_next[j + 1][h]] if j + 1 < subs else [])
            r = _dot(vaug[j][h], jnp.concatenate(cols, axis=1))
            targets = ([j] if j >= 0 else []) + ([j + 1] if j + 1 < subs else [])
            for pos, i in enumerate(targets):
                part = r[:, pos * TILE:(pos + 1) * TILE]
                acc[i][h] = part if acc[i][h] is None else acc[i][h] + part

    outs, lses = [], []
    for i in range(subs):
        o_t, lse_t = [], []
        for h in range(HEADS):
            l = acc[i][h][HEAD_DIM:HEAD_DIM + 1]
            o_t.append(acc[i][h][:HEAD_DIM] / l)
            lse_t.append(jnp.broadcast_to(tops[i][h] + jnp.log2(l), (HEAD_DIM, TILE)))
        outs.append(jnp.concatenate(o_t, axis=0).T)
        lses.append(jnp.concatenate(lse_t, axis=0).T)
    return outs, lses


DIL_SUBBLOCKS = 8


def _unpermutation(dil):
    t = np.arange(DIL_UNIT)
    mat = np.zeros((DIL_UNIT, DIL_UNIT), np.float32)
    mat[t, (t % dil) * (DIL_UNIT // dil) + t // dil] = 1.0
    return jnp.asarray(mat, BF16)


def _dilated_pattern(zc, bias, dil, *, batch, seq, merge_with=None):
    length = seq // dil
    subs = min(max(DIL_SUBBLOCKS // max(dil // 2, 1), 2), length // TILE)
    rows = subs * TILE
    tokens = rows * dil
    zc = zc.reshape(batch, length, dil * 3 * GROUP)
    blk = pl.BlockSpec((None, rows, dil * 3 * GROUP), lambda b, n: (b, n, 0))
    prev = pl.BlockSpec((None, TILE, dil * 3 * GROUP), lambda b, n: (b, jnp.maximum(n * subs - 1, 0), 0))
    nat = pl.BlockSpec((None, tokens, GROUP), lambda b, n: (b, n, 0))
    in_specs = [blk, prev, _const_spec((HEADS * TILE, 2 * TILE)), _const_spec((DIL_UNIT, DIL_UNIT))]
    args = [zc, zc, bias, _unpermutation(dil)]
    if merge_with is None:
        out_specs = [nat, nat]
        out_shape = [jax.ShapeDtypeStruct((batch, seq, GROUP), BF16),
                     jax.ShapeDtypeStruct((batch, seq, GROUP), F32)]
    else:
        in_specs += [nat] * len(merge_with)
        args += list(merge_with)
        out_specs = nat
        out_shape = jax.ShapeDtypeStruct((batch, seq, GROUP), BF16)
    out = pl.pallas_call(
        functools.partial(_dil_body, subs=subs, dil=dil, merge=merge_with is not None,
                          has_prev=length > rows),
        grid=(batch, length // rows),
        in_specs=in_specs, out_specs=out_specs, out_shape=out_shape,
        compiler_params=_params("parallel", "parallel"),
        name=f"dilated_d{dil}",
    )(*args)
    if merge_with is None:
        return out
    return out.reshape(batch * seq, GROUP)


def _dilated(zc_views, dil_bias, *, batch, seq):
    o1, l1 = _dilated_pattern(zc_views[0], dil_bias[0], DIL_PATTERNS[0][1], batch=batch, seq=seq)
    o2, l2 = _dilated_pattern(zc_views[1], dil_bias[1], DIL_PATTERNS[1][1], batch=batch, seq=seq)
    return _dilated_pattern(zc_views[2], dil_bias[2], DIL_PATTERNS[2][1], batch=batch, seq=seq,
                            merge_with=(o1, l1, o2, l2))


DIFF_Q = 256
DIFF_K = 256
DIFF_GROUPS = 2 * HEADS


DIFF_SAFE_LOG2 = 60.0
DIFF_BOUND_SLACK = 1.01


def _diff_running_max(q, qi, last, pairs, bias_tiles, kexp_ref, vaug_ref, acc_ref, sta_ref, stb_ref):
    def scores(s_ref, j):
        bias = bias_tiles(j)
        raw = _dot_nt(kexp_ref[jnp.minimum(j, last)], q)
        tops = []
        for g in range(DIFF_GROUPS):
            s = raw[g * DIFF_K:(g + 1) * DIFF_K] + bias[g % HEADS]
            s_ref[g * DIFF_K:(g + 1) * DIFF_K, :] = s
            tops.append(jnp.max(s, axis=0, keepdims=True))
        return tuple(tops)

    def consume(s_ref, tops, j, carry):
        ms, ls = carry
        jv = jnp.minimum(j, last)
        new_ms, new_ls = [], []
        for g in range(DIFF_GROUPS):
            mp, h = divmod(g, HEADS)
            m_new = jnp.maximum(ms[g], tops[g])
            p = jnp.exp2(s_ref[g * DIFF_K:(g + 1) * DIFF_K, :] - m_new).astype(BF16)
            alpha = jnp.exp2(ms[g] - m_new)
            r = _dot(vaug_ref[jv, h], p)
            acc_ref[mp, h] = alpha * acc_ref[mp, h] + r[:HEAD_DIM]
            new_ls.append(alpha * ls[g] + r[HEAD_DIM:HEAD_DIM + 1])
            new_ms.append(m_new)
        return tuple(new_ms), tuple(new_ls)

    def pair(jj, carry):
        tops_a, state = carry
        j = 2 * jj
        tops_b = scores(stb_ref, j + 1)
        state = consume(sta_ref, tops_a, j, state)
        tops_a = scores(sta_ref, j + 2)
        return tops_a, consume(stb_ref, tops_b, j + 1, state)

    init = (tuple(jnp.full((1, DIFF_Q), NEG, F32) for _ in range(DIFF_GROUPS)),
            tuple(jnp.zeros((1, DIFF_Q), F32) for _ in range(DIFF_GROUPS)))
    _, (_, ls) = lax.fori_loop(0, pairs, pair, (scores(sta_ref, 0), init))
    return ls


def _diff_body(q_ref, k_ref, vt_ref, bias_ref, lam_ref, sg_ref, lim_ref, y_ref,
               kexp_ref, vaug_ref, acc_ref, sta_ref, stb_ref, pa_ref, pb_ref, knorm_ref,
               *, lam_init, key_steps):
    qi = pl.program_id(1)
    qk_group = _group_mask(GROUP, GROUP, DIFF_QK_HALF, DIFF_QK_HALF).astype(BF16)

    @pl.when(qi == 0)
    def _():
        grp = lax.broadcasted_iota(jnp.int32, (DIFF_GROUPS * DIFF_K, GROUP), 0) // DIFF_K
        slot = lax.broadcasted_iota(jnp.int32, (DIFF_GROUPS * DIFF_K, GROUP), 1) // DIFF_QK_HALF
        kmask = slot == 2 * (grp % HEADS) + grp // HEADS
        ones_rows = (lax.broadcasted_iota(jnp.int32, (AUG_ROWS - HEAD_DIM, DIFF_K), 0) == 0).astype(BF16)

        def build(j, kmax):
            k_t = k_ref[pl.ds(pl.multiple_of(j * DIFF_K, DIFF_K), DIFF_K), :]
            kexp_ref[j] = jnp.where(kmask, jnp.concatenate([k_t] * DIFF_GROUPS, axis=0),
                                    jnp.zeros((), BF16))
            vt = jnp.concatenate([vt_ref[2 * j], vt_ref[2 * j + 1]], axis=1)
            for h in range(HEADS):
                vaug_ref[j, h] = jnp.concatenate([vt[h * HEAD_DIM:(h + 1) * HEAD_DIM], ones_rows], axis=0)
            k_f = k_t.astype(F32)
            return jnp.maximum(kmax, jnp.max(_group_sum(k_f * k_f, qk_group), axis=0, keepdims=True))

        knorm_ref[...] = lax.fori_loop(0, key_steps, build, jnp.zeros((1, GROUP), F32))

    acc_ref[...] = jnp.zeros_like(acc_ref)
    q = q_ref[...]
    last = key_steps - 1
    key_steps_needed = (qi + 1) * (DIFF_Q // DIFF_K)
    pairs = (key_steps_needed + 1) // 2

    def bias_tiles(j):
        base = (DIFF_Q // TILE) * qi - (DIFF_K // TILE) * j - DIFF_MIN_OFFSET
        tiles = {d: bias_ref[jnp.maximum(base + d, 0)]
                 for d in range(1 - DIFF_K // TILE, DIFF_Q // TILE)}
        return [jnp.concatenate(
            [jnp.concatenate([tiles[a - b][h * TILE:(h + 1) * TILE] for a in range(DIFF_Q // TILE)], axis=1)
             for b in range(DIFF_K // TILE)], axis=0) for h in range(HEADS)]

    q_f = q.astype(F32)
    bound2 = jnp.max(_group_sum(q_f * q_f, qk_group) * knorm_ref[...])
    no_overflow = bound2 * DIFF_BOUND_SLACK <= lim_ref[0, 0]

    def unshifted():
        def weights(p_ref, j):
            bias = bias_tiles(j)
            raw = _dot_nt(kexp_ref[jnp.minimum(j, last)], q)
            for g in range(DIFF_GROUPS):
                rows = slice(g * DIFF_K, (g + 1) * DIFF_K)
                p_ref[rows, :] = jnp.exp2(raw[rows] + bias[g % HEADS]).astype(BF16)

        def accumulate(p_ref, j, ls):
            jv = jnp.minimum(j, last)
            new_ls = []
            for g in range(DIFF_GROUPS):
                mp, h = divmod(g, HEADS)
                r = _dot(vaug_ref[jv, h], p_ref[g * DIFF_K:(g + 1) * DIFF_K, :])
                acc_ref[mp, h] = acc_ref[mp, h] + r[:HEAD_DIM]
                new_ls.append(ls[g] + r[HEAD_DIM:HEAD_DIM + 1])
            return tuple(new_ls)

        def pair(jj, ls):
            j = 2 * jj
            weights(pb_ref, j + 1)
            ls = accumulate(pa_ref, j, ls)
            weights(pa_ref, j + 2)
            return accumulate(pb_ref, j + 1, ls)

        weights(pa_ref, 0)
        return lax.fori_loop(0, pairs, pair,
                             tuple(jnp.zeros((1, DIFF_Q), F32) for _ in range(DIFF_GROUPS)))

    def running_max():
        return _diff_running_max(q, qi, last, pairs, bias_tiles, kexp_ref, vaug_ref, acc_ref,
                                 sta_ref, stb_ref)

    ls = lax.cond(no_overflow, unshifted, running_max)

    lv = lam_ref[...]
    lam = (jnp.exp(jnp.sum(lv[0:1] * lv[1:2], axis=-1, keepdims=True))
           - jnp.exp(jnp.sum(lv[2:3] * lv[3:4], axis=-1, keepdims=True)) + lam_init)
    outs = []
    for h in range(HEADS):
        o = acc_ref[0, h] / ls[h] - lam * (acc_ref[1, h] / ls[HEADS + h])
        ms_o = jnp.mean(o * o, axis=0, keepdims=True)
        outs.append(o * lax.rsqrt(ms_o + SUBLN_EPS) * sg_ref[...] * (1.0 - lam_init))
    y_ref[...] = jnp.concatenate(outs, axis=0).T.astype(BF16)


def _diff_attention(zd, vt, bias, score_limit2, lam_vecs, subln_cols, *, lam_init, batch, seq):
    zd = zd.reshape(batch, seq, 2 * GROUP)
    key_tiles = seq // TILE
    key_steps = seq // DIFF_K
    vt = vt.reshape(batch, key_tiles, GROUP, TILE)
    y = pl.pallas_call(
        functools.partial(_diff_body, lam_init=lam_init, key_steps=key_steps),
        grid=(batch, seq // DIFF_Q),
        in_specs=[pl.BlockSpec((None, DIFF_Q, GROUP), lambda b, i: (b, i, 0)),
                  pl.BlockSpec((None, seq, GROUP), lambda b, i: (b, 0, 1), pipeline_mode=pl.Buffered(1)),
                  pl.BlockSpec((None, key_tiles, GROUP, TILE), lambda b, i: (b, 0, 0, 0),
                               pipeline_mode=pl.Buffered(1)),
                  _const_spec((key_tiles - DIFF_MIN_OFFSET, HEADS * TILE, TILE)),
                  _const_spec((4, DIFF_QK_HALF)), _const_spec((HEAD_DIM, DIFF_Q)),
                  pl.BlockSpec(memory_space=pltpu.SMEM)],
        out_specs=pl.BlockSpec((None, DIFF_Q, GROUP), lambda b, i: (b, i, 0)),
        out_shape=jax.ShapeDtypeStruct((batch, seq, GROUP), BF16),
        scratch_shapes=[pltpu.VMEM((key_steps, DIFF_GROUPS * DIFF_K, GROUP), BF16),
                        pltpu.VMEM((key_steps, HEADS, AUG_ROWS, DIFF_K), BF16),
                        pltpu.VMEM((2, HEADS, HEAD_DIM, DIFF_Q), F32),
                        pltpu.VMEM((DIFF_GROUPS * DIFF_K, DIFF_Q), F32),
                        pltpu.VMEM((DIFF_GROUPS * DIFF_K, DIFF_Q), F32),
                        pltpu.VMEM((DIFF_GROUPS * DIFF_K, DIFF_Q), BF16),
                        pltpu.VMEM((DIFF_GROUPS * DIFF_K, DIFF_Q), BF16),
                        pltpu.VMEM((1, GROUP), F32)],
        compiler_params=_params("parallel", "arbitrary"),
        name="diff_attn",
    )(zd, zd, vt, bias, lam_vecs, subln_cols, score_limit2)
    return y.reshape(batch * seq, GROUP)


KV_ROWS = 512


def _mem_kv_body(m_ref, g_ref, w_ref, k_ref, v_ref):
    u = _rms(m_ref[...], g_ref[...]).astype(BF16)
    for c in range(D_MODEL // GROUP):
        sl = slice(c * GROUP, (c + 1) * GROUP)
        k_ref[:, sl] = _dot(u, w_ref[:, sl]).astype(BF16)
        v_ref[:, sl] = _dot(u, w_ref[:, D_MODEL + c * GROUP: D_MODEL + (c + 1) * GROUP]).astype(BF16)


def _mem_kv(mem, g, w):
    n = mem.shape[0]
    row = pl.BlockSpec((KV_ROWS, D_MODEL), lambda i: (i, 0))
    return pl.pallas_call(
        _mem_kv_body,
        grid=(n // KV_ROWS,),
        in_specs=[row, _const_spec((1, D_MODEL)), _const_spec((D_MODEL, 2 * D_MODEL))],
        out_specs=[row, row],
        out_shape=[jax.ShapeDtypeStruct((n, D_MODEL), BF16)] * 2,
        compiler_params=_params("parallel"),
        name="mem_kv",
    )(mem, g, w)


XATTN_ROWS = 1024


def _xattn_body(x_ref, ya_ref, yb_ref, yc_ref, yd_ref, wout_ref, g_ref, wq_ref, k_ref, v_ref, wo_ref,
                o_ref, q_scr, a_scr):
    x = x_ref[...]
    for gi, y_ref in enumerate((ya_ref, yb_ref, yc_ref, yd_ref)):
        x = x + _dot(y_ref[...], wout_ref[gi * GROUP:(gi + 1) * GROUP, :])
    u = _rms(x, g_ref[...]).astype(BF16)
    for c in range(D_MODEL // GROUP):
        sl = slice(c * GROUP, (c + 1) * GROUP)
        q_scr[:, sl] = _dot(u, wq_ref[:, sl]).astype(BF16)
    for h in range(MEM_HEADS):
        sl = slice(h * MEM_HEAD_DIM, (h + 1) * MEM_HEAD_DIM)
        s = _dot_nt(q_scr[:, sl], k_ref[:, sl]) * (MEM_HEAD_DIM ** -0.5)
        e = jnp.exp(s - jnp.max(s, axis=-1, keepdims=True))
        l = jnp.sum(e, axis=-1, keepdims=True)
        a_scr[:, sl] = (_dot(e.astype(BF16), v_ref[:, sl]) / l).astype(BF16)
    o_ref[...] = x + _dot(a_scr[...], wo_ref[...])


def _xattn(h, ys, w_out, g, wq, k, v, wo, *, batch, seq):
    h3 = h.reshape(batch, seq, D_MODEL)
    ys = [y.reshape(batch, seq, GROUP) for y in ys]
    k3 = k.reshape(batch, MEM_LEN, D_MODEL)
    v3 = v.reshape(batch, MEM_LEN, D_MODEL)
    row = pl.BlockSpec((None, XATTN_ROWS, D_MODEL), lambda b, i: (b, i, 0))
    grp = pl.BlockSpec((None, XATTN_ROWS, GROUP), lambda b, i: (b, i, 0))
    mem = pl.BlockSpec((None, MEM_LEN, D_MODEL), lambda b, i: (b, 0, 0))
    weight = _const_spec((D_MODEL, D_MODEL))
    out = pl.pallas_call(
        _xattn_body,
        grid=(batch, seq // XATTN_ROWS),
        in_specs=[row, grp, grp, grp, grp, weight, _const_spec((1, D_MODEL)), weight, mem, mem, weight],
        out_specs=row,
        out_shape=jax.ShapeDtypeStruct((batch, seq, D_MODEL), F32),
        scratch_shapes=[pltpu.VMEM((XATTN_ROWS, D_MODEL), BF16), pltpu.VMEM((XATTN_ROWS, D_MODEL), BF16)],
        compiler_params=_params("parallel", "parallel"),
        name="xattn",
    )(h3, *ys, w_out, g, wq, k3, v3, wo)
    return out.reshape(batch * seq, D_MODEL)


def _per_head_lanes(x):
    return jnp.repeat(x, HEAD_DIM, axis=-1)


def _in_proj_weight(w_in):
    g = GROUP
    q_a, k_a, v_a, o_a = (w_in[:, i * g:(i + 1) * g] for i in range(4))
    ig = w_in[:, 4 * g:4 * g + HEADS]
    fg = w_in[:, 4 * g + HEADS:4 * g + 2 * HEADS]
    rest = w_in[:, 4 * g + 2 * HEADS:]
    pool, q_c, k_c, v_c, q_d, k_d, v_d = (rest[:, i * g:(i + 1) * g] for i in range(7))
    q_c = q_c * DIL_SCORE_SCALE
    k_d = k_d * DIFF_SCORE_SCALE
    cols = [q_a, k_a, v_a, o_a, _per_head_lanes(ig), _per_head_lanes(fg), pool,
            q_c, k_c, v_c, q_d, k_d]
    return jnp.concatenate(cols, axis=1).astype(BF16), v_d.astype(BF16)


def _block_diag(w):
    g, c, _ = w.shape
    eye = jnp.eye(g, dtype=w.dtype)
    return (eye[:, None, :, None] * w[:, :, None, :]).reshape(g * c, g * c)


def kernel(x, mem, t5_bias, ffn1_norm, ffn1_w_gate, ffn1_w_up, ffn1_w_down, mix_norm, w_in,
           mlstm_conv_w, mlstm_conv_b, mlstm_gate_b, mlstm_norm, pool_w, pool_scale,
           diff_lambda, diff_subln, w_out, xattn_norm, mem_norm, xattn_wq, xattn_wkv, xattn_wo,
           ffn2_norm, ffn2_w_gate, ffn2_w_up, ffn2_w_down, final_norm):
    batch, seq, _ = x.shape
    n = batch * seq
    dil_bias, diff_bias, diff_limit2 = _bias_tiles(t5_bias, seq)
    h = x.reshape(n, D_MODEL)
    mem2 = mem.reshape(batch * MEM_LEN, D_MODEL)
    row = lambda v: v.reshape(1, -1)
    for l in range(DEPTH):
        lam_init = 0.8 - 0.6 * math.exp(-0.3 * l)
        h = _ffn(h, row(ffn1_norm[l]), ffn1_w_gate[l].astype(BF16), ffn1_w_up[l].astype(BF16),
                 ffn1_w_down[l].astype(BF16), row(final_norm), final=False)
        za, zg, zp, zc, zd, vt, zc4, zc16 = _in_proj(h, row(mix_norm[l]), *_in_proj_weight(w_in[l]))
        ya = _mlstm(za, zg, mlstm_conv_w[l], row(mlstm_conv_b[l]),
                    row(_per_head_lanes(mlstm_gate_b[l].reshape(2, HEADS))), row(mlstm_norm[l]),
                    batch=batch, seq=seq)
        yb = _pool(zp, _block_diag(pool_w[l]).astype(BF16), row(pool_scale[l]), batch=batch, seq=seq)
        yc = _dilated((zc, zc4, zc16), dil_bias, batch=batch, seq=seq)
        yd = _diff_attention(zd, vt, diff_bias, diff_limit2, diff_lambda[l],
                             jnp.broadcast_to(diff_subln[l][:, None], (HEAD_DIM, DIFF_Q)),
                             lam_init=lam_init, batch=batch, seq=seq)
        k_mem, v_mem = _mem_kv(mem2, row(mem_norm[l]), xattn_wkv[l].astype(BF16))
        h = _xattn(h, (ya, yb, yc, yd), w_out[l].astype(BF16),
                   row(xattn_norm[l]), xattn_wq[l].astype(BF16), k_mem, v_mem,
                   xattn_wo[l].astype(BF16), batch=batch, seq=seq)
        h = _ffn(h, row(ffn2_norm[l]), ffn2_w_gate[l].astype(BF16), ffn2_w_up[l].astype(BF16),
                 ffn2_w_down[l].astype(BF16), row(final_norm), final=(l == DEPTH - 1))
    return h.reshape(batch, seq, D_MODEL)
```

```python
import functools
import math

import jax
import jax.numpy as jnp
import numpy as np
from jax import lax
from jax.experimental import pallas as pl
from jax.experimental.pallas import tpu as pltpu

F32 = jnp.float32
BF16 = jnp.bfloat16

D_MODEL = 1024
D_FF = 2816
DEPTH = 4
GROUP = 256
HEADS = 4
HEAD_DIM = GROUP // HEADS
MEM_LEN = 256
MEM_HEADS = 4
MEM_HEAD_DIM = D_MODEL // MEM_HEADS
MLSTM_CHUNK = 64
CONV_WIDTH = 4
POOL_WINDOWS = (2, 4, 8, 16)
DIL_PATTERNS = ((128, 1), (512, 4), (2048, 16))
DIL_BACK = 128
DIFF_QK_HALF = HEAD_DIM // 2
T5_BUCKETS = 32
T5_MAX_DIST = 2048
RMS_EPS = 1e-6
SUBLN_EPS = 1e-5
NEG = -1e30
LOG2E = math.log2(math.e)
DIFF_SCORE_SCALE = (DIFF_QK_HALF ** -0.5) * LOG2E
DIL_SCORE_SCALE = (HEAD_DIM ** -0.5) * LOG2E
AUG_ROWS = HEAD_DIM + 16
DIFF_MIN_OFFSET = -3
TILE = 128

VMEM_LIMIT_BYTES = 56 * 1024 * 1024


def _rms(xf, g, eps=RMS_EPS):
    return xf * lax.rsqrt(jnp.mean(xf * xf, axis=-1, keepdims=True) + eps) * g


def _const_spec(shape):
    zeros = (0,) * len(shape)
    return pl.BlockSpec(shape, lambda *_: zeros, pipeline_mode=pl.Buffered(1))


def _params(*sem):
    return pltpu.CompilerParams(dimension_semantics=sem, vmem_limit_bytes=VMEM_LIMIT_BYTES)


def _group_mask(rows, cols, row_group, col_group):
    r = lax.broadcasted_iota(jnp.int32, (rows, cols), 0) // row_group
    c = lax.broadcasted_iota(jnp.int32, (rows, cols), 1) // col_group
    return r == c


def _tile_rows(x, reps, mask):
    return jnp.where(mask, jnp.concatenate([x] * reps, axis=0), jnp.zeros((), x.dtype))


def _dot(a, b):
    return jnp.dot(a, b, preferred_element_type=F32)


def _dot_nt(a, b):
    return lax.dot_general(a, b, (((1,), (1,)), ((), ())), preferred_element_type=F32)


def _dot_tn(a, b):
    return lax.dot_general(a, b, (((0,), (0,)), ((), ())), preferred_element_type=F32)


def _group_sum(x, ones_bd):
    hi = x.astype(BF16)
    lo = (x - hi.astype(F32)).astype(BF16)
    return _dot(hi, ones_bd) + _dot(lo, ones_bd)


FFN_ROWS = 1024
FFN_COLS = 256


def _ffn_body(x_ref, g_ref, wg_ref, wu_ref, wd_ref, fg_ref, o_ref, act_ref, *, final):
    x = x_ref[...]
    u = _rms(x, g_ref[...]).astype(BF16)
    for c in range(D_FF // FFN_COLS):
        sl = slice(c * FFN_COLS, (c + 1) * FFN_COLS)
        gate = _dot(u, wg_ref[:, sl])
        up = _dot(u, wu_ref[:, sl])
        act_ref[:, sl] = (gate * jax.nn.sigmoid(gate) * up).astype(BF16)
    y = x + 0.5 * _dot(act_ref[...], wd_ref[...])
    if final:
        y = _rms(y, fg_ref[...])
    o_ref[...] = y


def _ffn(h, g, wg, wu, wd, fg, *, final):
    n = h.shape[0]
    row = pl.BlockSpec((FFN_ROWS, D_MODEL), lambda i: (i, 0))
    return pl.pallas_call(
        functools.partial(_ffn_body, final=final),
        grid=(n // FFN_ROWS,),
        in_specs=[row, _const_spec((1, D_MODEL)), _const_spec((D_MODEL, D_FF)),
                  _const_spec((D_MODEL, D_FF)), _const_spec((D_FF, D_MODEL)),
                  _const_spec((1, D_MODEL))],
        out_specs=row,
        out_shape=jax.ShapeDtypeStruct((n, D_MODEL), F32),
        scratch_shapes=[pltpu.VMEM((FFN_ROWS, D_FF), BF16)],
        compiler_params=_params("parallel"),
        name="ffn_final" if final else "ffn",
    )(h, g, wg, wu, wd, fg)


PROJ_ROWS = 1024
PROJ_OUTS = (("a", 4 * GROUP, F32), ("g", 2 * GROUP, F32), ("p", GROUP, F32),
             ("c", 3 * GROUP, BF16), ("d", 2 * GROUP, BF16))
PROJ_WIDTH = sum(w for _, w, _ in PROJ_OUTS)


PROJ_DILATIONS = tuple(d for _, d in DIL_PATTERNS if d > 1)


def _in_proj_body(x_ref, g_ref, w_ref, wv_ref, perm_ref, *o_refs):
    u = _rms(x_ref[...], g_ref[...]).astype(BF16)
    off = 0
    for o_ref, (_, width, dtype) in zip(o_refs, PROJ_OUTS):
        for c in range(width // GROUP):
            z = _dot(u, w_ref[:, off + c * GROUP: off + (c + 1) * GROUP])
            o_ref[:, c * GROUP:(c + 1) * GROUP] = z.astype(dtype)
        off += width
    vt_ref = o_refs[len(PROJ_OUTS)]
    v = _dot(u, wv_ref[...])
    for t in range(PROJ_ROWS // TILE):
        vt_ref[t] = v[t * TILE:(t + 1) * TILE].T.astype(BF16)
    for pi, d in enumerate(PROJ_DILATIONS):
        per_class = PROJ_UNIT // d
        for un in range(PROJ_ROWS // PROJ_UNIT):
            zp = _dot(perm_ref[pi], o_refs[3][un * PROJ_UNIT:(un + 1) * PROJ_UNIT, :]).astype(BF16)
            for r in range(d):
                o_refs[len(PROJ_OUTS) + 1 + pi][un * per_class:(un + 1) * per_class,
                                                r * 3 * GROUP:(r + 1) * 3 * GROUP] = \
                    zp[r * per_class:(r + 1) * per_class]


PROJ_UNIT = 2 * TILE


def _class_permutations():
    mats = np.zeros((len(PROJ_DILATIONS), PROJ_UNIT, PROJ_UNIT), np.float32)
    for pi, d in enumerate(PROJ_DILATIONS):
        t = np.arange(PROJ_UNIT)
        mats[pi, (t % d) * (PROJ_UNIT // d) + t // d, t] = 1.0
    return jnp.asarray(mats, BF16)


def _in_proj(h, g, w, wv):
    n = h.shape[0]
    tiles = PROJ_ROWS // TILE
    return pl.pallas_call(
        _in_proj_body,
        grid=(n // PROJ_ROWS,),
        in_specs=[pl.BlockSpec((PROJ_ROWS, D_MODEL), lambda i: (i, 0)),
                  _const_spec((1, D_MODEL)), _const_spec((D_MODEL, PROJ_WIDTH)),
                  _const_spec((D_MODEL, GROUP)),
                  _const_spec((len(PROJ_DILATIONS), PROJ_UNIT, PROJ_UNIT))],
        out_specs=[pl.BlockSpec((PROJ_ROWS, w_), lambda i: (i, 0)) for _, w_, _ in PROJ_OUTS]
        + [pl.BlockSpec((tiles, GROUP, TILE), lambda i: (i, 0, 0))]
        + [pl.BlockSpec((PROJ_ROWS // d, d * 3 * GROUP), lambda i: (i, 0)) for d in PROJ_DILATIONS],
        out_shape=[jax.ShapeDtypeStruct((n, w_), dt) for _, w_, dt in PROJ_OUTS]
        + [jax.ShapeDtypeStruct((n // TILE, GROUP, TILE), BF16)]
        + [jax.ShapeDtypeStruct((n // d, d * 3 * GROUP), BF16) for d in PROJ_DILATIONS],
        compiler_params=_params("parallel"),
        name="in_proj",
    )(h, g, w, wv, _class_permutations())


ML_ROWS = 512
ML_HALO = 8


def _chunk_scan(x, rin, op, fill):
    s = 1
    while s < MLSTM_CHUNK:
        x = op(x, jnp.where(rin >= s, pltpu.roll(x, s, 0), fill))
        s *= 2
    return x


def _mlstm_body(za_ref, zg_ref, cw_ref, cb_ref, gb_ref, ng_ref, y_ref,
                buf_ref, tail_ref, hh_ref, c_ref, n_ref, m_ref):
    L = MLSTM_CHUNK

    @pl.when(pl.program_id(1) == 0)
    def _():
        tail_ref[...] = jnp.zeros_like(tail_ref)
        c_ref[...] = jnp.zeros_like(c_ref)
        n_ref[...] = jnp.zeros_like(n_ref)
        m_ref[...] = jnp.zeros_like(m_ref)

    buf_ref[0:ML_HALO, :] = tail_ref[...]
    buf_ref[ML_HALO:, :] = za_ref[:, 0:2 * GROUP]
    tail_ref[...] = za_ref[ML_ROWS - ML_HALO:, 0:2 * GROUP]
    conv = cb_ref[...]
    for j in range(CONV_WIDTH):
        conv = conv + buf_ref[pl.ds(ML_HALO - (CONV_WIDTH - 1) + j, ML_ROWS), :] * cw_ref[j:j + 1, :]
    qk = conv * jax.nn.sigmoid(conv)
    q = qk[:, :GROUP]
    k = qk[:, GROUP:] * (HEAD_DIM ** -0.5)
    v = za_ref[:, 2 * GROUP:3 * GROUP].astype(BF16)
    q_bf, k_bf = q.astype(BF16), k.astype(BF16)

    ii = zg_ref[:, :GROUP] + gb_ref[:, :GROUP]
    fx = zg_ref[:, GROUP:] + gb_ref[:, GROUP:]
    lf = jnp.minimum(fx, 0.0) - jnp.log1p(jnp.exp(-jnp.abs(fx)))
    rin = lax.broadcasted_iota(jnp.int32, (ML_ROWS, GROUP), 0) % L
    b = _chunk_scan(lf, rin, jnp.add, 0.0)
    a = ii - b
    ca = _chunk_scan(a, rin, jnp.maximum, NEG)

    bd = _group_mask(GROUP, GROUP, HEAD_DIM, HEAD_DIM)
    ones_bd = bd.astype(BF16)
    row = lax.broadcasted_iota(jnp.int32, (L, GROUP), 0)
    key = lax.broadcasted_iota(jnp.int32, (L, GROUP), 1) % L
    causal = key <= row
    diag = key == row

    m_prev = m_ref[...]
    for c in range(ML_ROWS // L):
        rs = slice(c * L, (c + 1) * L)
        q_c, k_c, v_c = q[rs], k[rs], v[rs]
        q_b = q_bf[rs]
        a_c, b_c = a[rs], b[rs]
        g = jnp.maximum(m_prev, ca[rs])
        g_last = g[L - 1:L]
        a_row = jnp.sum(jnp.where(diag, a_c, 0.0), axis=0, keepdims=True)
        decay = jnp.exp(jnp.where(causal, a_row - g, NEG))
        sc = _dot_nt(q_b, _tile_rows(k_bf[rs], HEADS, bd)) * decay
        inter = jnp.exp(m_prev - g)
        num = inter * _dot(q_b, c_ref[...].astype(BF16)) + _dot(sc.astype(BF16), _tile_rows(v_c, HEADS, bd))
        den = inter * _group_sum(q_c * n_ref[...], ones_bd) + _group_sum(sc, ones_bd)
        hh_ref[rs, :] = num / jnp.maximum(jnp.abs(den), jnp.exp(-(b_c + g)))
        kw = k_c * jnp.exp(a_c - g_last)
        carry = jnp.exp(m_prev - g_last)
        c_ref[...] = carry * c_ref[...] + jnp.where(bd, _dot_tn(kw.astype(BF16), v_c), 0.0)
        n_ref[...] = carry * n_ref[...] + jnp.sum(kw, axis=0, keepdims=True)
        m_prev = b_c[L - 1:L] + g_last
    m_ref[...] = m_prev

    hh = hh_ref[...]
    mu = _group_sum(hh, ones_bd) * (1.0 / HEAD_DIM)
    dev = hh - mu
    var = _group_sum(dev * dev, ones_bd) * (1.0 / HEAD_DIM)
    o_gate = jax.nn.sigmoid(za_ref[:, 3 * GROUP:])
    y_ref[...] = (dev * lax.rsqrt(var + RMS_EPS) * ng_ref[...] * o_gate).astype(BF16)


def _mlstm(za, zg, conv_w, conv_b, gate_b, norm_g, *, batch, seq):
    za = za.reshape(batch, seq, 4 * GROUP)
    zg = zg.reshape(batch, seq, 2 * GROUP)
    y = pl.pallas_call(
        _mlstm_body,
        grid=(batch, seq // ML_ROWS),
        in_specs=[pl.BlockSpec((None, ML_ROWS, 4 * GROUP), lambda b, j: (b, j, 0)),
                  pl.BlockSpec((None, ML_ROWS, 2 * GROUP), lambda b, j: (b, j, 0)),
                  _const_spec((CONV_WIDTH, 2 * GROUP)), _const_spec((1, 2 * GROUP)),
                  _const_spec((1, 2 * GROUP)), _const_spec((1, GROUP))],
        out_specs=pl.BlockSpec((None, ML_ROWS, GROUP), lambda b, j: (b, j, 0)),
        out_shape=jax.ShapeDtypeStruct((batch, seq, GROUP), BF16),
        scratch_shapes=[pltpu.VMEM((ML_ROWS + ML_HALO, 2 * GROUP), F32),
                        pltpu.VMEM((ML_HALO, 2 * GROUP), F32),
                        pltpu.VMEM((ML_ROWS, GROUP), F32),
                        pltpu.VMEM((GROUP, GROUP), F32),
                        pltpu.VMEM((1, GROUP), F32),
                        pltpu.VMEM((1, GROUP), F32)],
        compiler_params=_params("parallel", "arbitrary"),
        name="mlstm",
    )(za, zg, conv_w, conv_b, gate_b, norm_g)
    return y.reshape(batch * seq, GROUP)


POOL_ROWS = 1024
POOL_HALO = 16


def _pool_body(u_ref, w_ref, s_ref, y_ref, buf_ref, tail_ref):
    j = pl.program_id(1)

    @pl.when(j == 0)
    def _():
        tail_ref[...] = jnp.zeros_like(tail_ref)

    buf_ref[0:POOL_HALO, :] = tail_ref[...]
    buf_ref[POOL_HALO:, :] = u_ref[...]
    tail_ref[...] = u_ref[POOL_ROWS - POOL_HALO:, :]
    sums, s = [], buf_ref[...]
    for shift in (1, 2, 4, 8):
        s = s + pltpu.roll(s, shift, 0)
        sums.append(s[POOL_HALO:])
    u = u_ref[...]
    lane_group = lax.broadcasted_iota(jnp.int32, (POOL_ROWS, GROUP), 1) // HEAD_DIM
    t = j * POOL_ROWS + lax.broadcasted_iota(jnp.int32, (POOL_ROWS, GROUP), 0)
    total, win = sums[3], jnp.full((POOL_ROWS, GROUP), POOL_WINDOWS[3], jnp.int32)
    for gi in (2, 1, 0):
        total = jnp.where(lane_group == gi, sums[gi], total)
        win = jnp.where(lane_group == gi, POOL_WINDOWS[gi], win)
    mean = total / jnp.minimum(t + 1, win).astype(F32)
    y = _dot((mean - u).astype(BF16), w_ref[...]) * s_ref[...]
    y_ref[...] = y.astype(BF16)


def _pool(zp, w_bd, scale, *, batch, seq):
    zp = zp.reshape(batch, seq, GROUP)
    y = pl.pallas_call(
        _pool_body,
        grid=(batch, seq // POOL_ROWS),
        in_specs=[pl.BlockSpec((None, POOL_ROWS, GROUP), lambda b, j: (b, j, 0)),
                  _const_spec((GROUP, GROUP)), _const_spec((1, GROUP))],
        out_specs=pl.BlockSpec((None, POOL_ROWS, GROUP), lambda b, j: (b, j, 0)),
        out_shape=jax.ShapeDtypeStruct((batch, seq, GROUP), BF16),
        scratch_shapes=[pltpu.VMEM((POOL_ROWS + POOL_HALO, GROUP), F32),
                        pltpu.VMEM((POOL_HALO, GROUP), F32)],
        compiler_params=_params("parallel", "arbitrary"),
        name="pool",
    )(zp, w_bd, scale)
    return y.reshape(batch * seq, GROUP)


TOEPLITZ_BATCH = 4


def _toeplitz_body(w_ref, o_ref):
    for t in range(TOEPLITZ_BATCH):
        x = jnp.broadcast_to(w_ref[t], (TILE, 2 * TILE))
        o_ref[t] = pltpu.roll(x, 0, 1, stride=1, stride_axis=0)[:, :TILE]


def _toeplitz(rows):
    n = rows.shape[0]
    assert n % TOEPLITZ_BATCH == 0
    return pl.pallas_call(
        _toeplitz_body,
        grid=(n // TOEPLITZ_BATCH,),
        in_specs=[pl.BlockSpec((TOEPLITZ_BATCH, 1, 2 * TILE), lambda i: (i, 0, 0))],
        out_specs=pl.BlockSpec((TOEPLITZ_BATCH, TILE, TILE), lambda i: (i, 0, 0)),
        out_shape=jax.ShapeDtypeStruct((n, TILE, TILE), F32),
        compiler_params=_params("parallel"),
        name="toeplitz",
    )(rows.reshape(n, 1, 2 * TILE))


_TOEPLITZ_X = np.where(np.arange(2 * TILE) <= TILE, -np.arange(2 * TILE), 2 * TILE - np.arange(2 * TILE))


def _t5_bucket(dist):
    max_exact = T5_BUCKETS // 2
    d = jnp.maximum(dist, 1).astype(F32)
    large = max_exact + (jnp.log(d / max_exact) / math.log(T5_MAX_DIST / max_exact)
                         * (T5_BUCKETS - max_exact)).astype(jnp.int32)
    large = jnp.minimum(large, T5_BUCKETS - 1)
    return jnp.where(dist < max_exact, dist, large)


def _bias_tiles(t5_bias, seq):
    x = jnp.asarray(_TOEPLITZ_X, jnp.int32)
    table = t5_bias * LOG2E
    delta = jnp.clip(jnp.stack([-x, DIL_BACK - x]), 0, DIL_BACK)
    valid = jnp.stack([x <= 0, x >= 0])[None, :, :, None]
    rows = table[_t5_bucket(jnp.stack([delta * d for _, d in DIL_PATTERNS])), :HEADS]
    rows = jnp.where(valid, rows, NEG).transpose(0, 3, 1, 2)
    dil = _toeplitz(rows.reshape(-1, 2 * TILE))
    dil = dil.reshape(len(DIL_PATTERNS), HEADS, 2, TILE, TILE).transpose(0, 1, 3, 2, 4)
    dil = dil.reshape(len(DIL_PATTERNS), HEADS * TILE, 2 * TILE)
    noff = seq // TILE - DIFF_MIN_OFFSET
    dist = (jnp.arange(noff)[:, None] + DIFF_MIN_OFFSET) * TILE - x[None, :]
    rows = table[_t5_bucket(jnp.clip(dist, 0, seq - 1)), HEADS:]
    rows = jnp.where((dist >= 0)[:, :, None], rows, NEG).transpose(0, 2, 1)
    diff = _toeplitz(rows.reshape(-1, 2 * TILE))
    room = jnp.maximum(DIFF_SAFE_LOG2 - jnp.max(jnp.abs(table[:, HEADS:])), 0.0)
    return dil, diff.reshape(noff, HEADS * TILE, TILE), (room * room).reshape(1, 1)


DIL_UNIT = 2 * TILE


def _dil_body(*refs, subs, dil, merge, has_prev):
    if merge:
        x_ref, xp_ref, bias_ref, unperm_ref, o1_ref, l1_ref, o2_ref, l2_ref, y_ref = refs
    else:
        x_ref, xp_ref, bias_ref, unperm_ref, o_ref, lse_ref = refs
    per_class = [_dil_class(x_ref, xp_ref, bias_ref, rc=rc, subs=subs, has_prev=has_prev)
                 for rc in range(dil)]
    piece = DIL_UNIT // dil
    class_o = [jnp.concatenate(pc[0], axis=0) for pc in per_class]
    class_l = [jnp.concatenate(pc[1], axis=0) for pc in per_class]
    for u in range(dil * subs * TILE // DIL_UNIT):
        take = lambda arrs: jnp.concatenate([a[u * piece:(u + 1) * piece] for a in arrs], axis=0)
        if dil == 1:
            o, lse = take(class_o), take(class_l)
        else:
            o_c = take(class_o).astype(BF16)
            l_c = take(class_l)
            l_hi = l_c.astype(BF16)
            l_lo = (l_c - l_hi.astype(F32)).astype(BF16)
            o = _dot(unperm_ref[...], o_c)
            lse = _dot(unperm_ref[...], l_hi) + _dot(unperm_ref[...], l_lo)
        rs = slice(u * DIL_UNIT, (u + 1) * DIL_UNIT)
        if merge:
            o1, l1 = o1_ref[rs, :].astype(F32), l1_ref[rs, :]
            o2, l2 = o2_ref[rs, :].astype(F32), l2_ref[rs, :]
            top = jnp.maximum(jnp.maximum(l1, l2), lse)
            w1, w2, w3 = jnp.exp2(l1 - top), jnp.exp2(l2 - top), jnp.exp2(lse - top)
            y_ref[rs, :] = ((w1 * o1 + w2 * o2 + w3 * o) / (w1 + w2 + w3)).astype(BF16)
        else:
            o_ref[rs, :] = o.astype(BF16)
            lse_ref[rs, :] = lse


def _dil_class(x_ref, xp_ref, bias_ref, *, rc, subs, has_prev):
    first = pl.program_id(1) == 0
    kmask = _group_mask(HEADS * TILE, GROUP, TILE, HEAD_DIM)
    ones_rows = (lax.broadcasted_iota(jnp.int32, (AUG_ROWS - HEAD_DIM, TILE), 0) == 0).astype(BF16)
    rows_of = lambda sb: slice(sb * TILE, (sb + 1) * TILE)
    col_q, col_k, col_v = (slice((3 * rc + w) * GROUP, (3 * rc + w + 1) * GROUP) for w in range(3))

    s_same, s_next, vaug = {}, {}, {}
    first_block = -1 if has_prev else 0
    for j in range(first_block, subs):
        k_j = xp_ref[:, col_k] if j < 0 else x_ref[rows_of(j), col_k]
        v_j = xp_ref[:, col_v] if j < 0 else x_ref[rows_of(j), col_v]
        parts = ([0] if j >= 0 else []) + ([1] if j + 1 < subs else [])
        q_cat = jnp.concatenate([x_ref[rows_of(j + e), col_q] for e in parts], axis=0)
        bias = bias_ref[:, parts[0] * TILE:(parts[-1] + 1) * TILE]
        st = _dot_nt(_tile_rows(k_j, HEADS, kmask), q_cat) + bias
        if j < 0:
            st = st + jnp.where(first, NEG, 0.0)
        for pos, e in enumerate(parts):
            (s_same if e == 0 else s_next)[j + e] = st[:, pos * TILE:(pos + 1) * TILE]
        v_t = v_j.astype(F32).T.astype(BF16)
        vaug[j] = [jnp.concatenate([v_t[h * HEAD_DIM:(h + 1) * HEAD_DIM], ones_rows], axis=0)
                   for h in range(HEADS)]

    p_same, p_next, tops = {}, {}, {}
    for i in range(subs):
        ps, pn, tp = [], [], []
        for h in range(HEADS):
            hs = slice(h * TILE, (h + 1) * TILE)
            a = s_same[i][hs]
            m = jnp.max(a, axis=0, keepdims=True)
            if i in s_next:
                b = s_next[i][hs]
                m = jnp.maximum(m, jnp.max(b, axis=0, keepdims=True))
                pn.append(jnp.exp2(b - m).astype(BF16))
            ps.append(jnp.exp2(a - m).astype(BF16))
            tp.append(m)
        p_same[i], p_next[i], tops[i] = ps, pn, tp

    acc = {i: [None] * HEADS for i in range(subs)}
    for j in range(first_block, subs):
        for h in range(HEADS):
            cols = ([p_same[j][h]] if j >= 0 else []) + ([p_next[j + 1][h]] if j + 1 < subs else [])
            r = _dot(vaug[j][h], jnp.concatenate(cols, axis=1))
            targets = ([j] if j >= 0 else []) + ([j + 1] if j + 1 < subs else [])
            for pos, i in enumerate(targets):
                part = r[:, pos * TILE:(pos + 1) * TILE]
                acc[i][h] = part if acc[i][h] is None else acc[i][h] + part

    outs, lses = [], []
    for i in range(subs):
        o_t, lse_t = [], []
        for h in range(HEADS):
            l = acc[i][h][HEAD_DIM:HEAD_DIM + 1]
            o_t.append(acc[i][h][:HEAD_DIM] / l)
            lse_t.append(jnp.broadcast_to(tops[i][h] + jnp.log2(l), (HEAD_DIM, TILE)))
        outs.append(jnp.concatenate(o_t, axis=0).T)
        lses.append(jnp.concatenate(lse_t, axis=0).T)
    return outs, lses


DIL_SUBBLOCKS = 8


def _unpermutation(dil):
    t = np.arange(DIL_UNIT)
    mat = np.zeros((DIL_UNIT, DIL_UNIT), np.float32)
    mat[t, (t % dil) * (DIL_UNIT // dil) + t // dil] = 1.0
    return jnp.asarray(mat, BF16)


def _dilated_pattern(zc, bias, dil, *, batch, seq, merge_with=None):
    length = seq // dil
    subs = min(max(DIL_SUBBLOCKS // max(dil // 2, 1), 2), length // TILE)
    rows = subs * TILE
    tokens = rows * dil
    zc = zc.reshape(batch, length, dil * 3 * GROUP)
    blk = pl.BlockSpec((None, rows, dil * 3 * GROUP), lambda b, n: (b, n, 0))
    prev = pl.BlockSpec((None, TILE, dil * 3 * GROUP), lambda b, n: (b, jnp.maximum(n * subs - 1, 0), 0))
    nat = pl.BlockSpec((None, tokens, GROUP), lambda b, n: (b, n, 0))
    in_specs = [blk, prev, _const_spec((HEADS * TILE, 2 * TILE)), _const_spec((DIL_UNIT, DIL_UNIT))]
    args = [zc, zc, bias, _unpermutation(dil)]
    if merge_with is None:
        out_specs = [nat, nat]
        out_shape = [jax.ShapeDtypeStruct((batch, seq, GROUP), BF16),
                     jax.ShapeDtypeStruct((batch, seq, GROUP), F32)]
    else:
        in_specs += [nat] * len(merge_with)
        args += list(merge_with)
        out_specs = nat
        out_shape = jax.ShapeDtypeStruct((batch, seq, GROUP), BF16)
    out = pl.pallas_call(
        functools.partial(_dil_body, subs=subs, dil=dil, merge=merge_with is not None,
                          has_prev=length > rows),
        grid=(batch, length // rows),
        in_specs=in_specs, out_specs=out_specs, out_shape=out_shape,
        compiler_params=_params("parallel", "parallel"),
        name=f"dilated_d{dil}",
    )(*args)
    if merge_with is None:
        return out
    return out.reshape(batch * seq, GROUP)


def _dilated(zc_views, dil_bias, *, batch, seq):
    o1, l1 = _dilated_pattern(zc_views[0], dil_bias[0], DIL_PATTERNS[0][1], batch=batch, seq=seq)
    o2, l2 = _dilated_pattern(zc_views[1], dil_bias[1], DIL_PATTERNS[1][1], batch=batch, seq=seq)
    return _dilated_pattern(zc_views[2], dil_bias[2], DIL_PATTERNS[2][1], batch=batch, seq=seq,
                            merge_with=(o1, l1, o2, l2))


DIFF_Q = 256
DIFF_K = 256
DIFF_GROUPS = 2 * HEADS


DIFF_SAFE_LOG2 = 60.0
DIFF_BOUND_SLACK = 1.01


def _diff_running_max(q, qi, last, pairs, bias_tiles, kexp_ref, vaug_ref, acc_ref, sta_ref, stb_ref):
    def scores(s_ref, j):
        bias = bias_tiles(j)
        raw = _dot_nt(kexp_ref[jnp.minimum(j, last)], q)
        tops = []
        for g in range(DIFF_GROUPS):
            s = raw[g * DIFF_K:(g + 1) * DIFF_K] + bias[g % HEADS]
            s_ref[g * DIFF_K:(g + 1) * DIFF_K, :] = s
            tops.append(jnp.max(s, axis=0, keepdims=True))
        return tuple(tops)

    def consume(s_ref, tops, j, carry):
        ms, ls = carry
        jv = jnp.minimum(j, last)
        new_ms, new_ls = [], []
        for g in range(DIFF_GROUPS):
            mp, h = divmod(g, HEADS)
            m_new = jnp.maximum(ms[g], tops[g])
            p = jnp.exp2(s_ref[g * DIFF_K:(g + 1) * DIFF_K, :] - m_new).astype(BF16)
            alpha = jnp.exp2(ms[g] - m_new)
            r = _dot(vaug_ref[jv, h], p)
            acc_ref[mp, h] = alpha * acc_ref[mp, h] + r[:HEAD_DIM]
            new_ls.append(alpha * ls[g] + r[HEAD_DIM:HEAD_DIM + 1])
            new_ms.append(m_new)
        return tuple(new_ms), tuple(new_ls)

    def pair(jj, carry):
        tops_a, state = carry
        j = 2 * jj
        tops_b = scores(stb_ref, j + 1)
        state = consume(sta_ref, tops_a, j, state)
        tops_a = scores(sta_ref, j + 2)
        return tops_a, consume(stb_ref, tops_b, j + 1, state)

    init = (tuple(jnp.full((1, DIFF_Q), NEG, F32) for _ in range(DIFF_GROUPS)),
            tuple(jnp.zeros((1, DIFF_Q), F32) for _ in range(DIFF_GROUPS)))
    _, (_, ls) = lax.fori_loop(0, pairs, pair, (scores(sta_ref, 0), init))
    return ls


def _diff_body(q_ref, k_ref, vt_ref, bias_ref, lam_ref, sg_ref, lim_ref, y_ref,
               kexp_ref, vaug_ref, acc_ref, sta_ref, stb_ref, pa_ref, pb_ref, knorm_ref,
               *, lam_init, key_steps):
    qi = pl.program_id(1)
    qk_group = _group_mask(GROUP, GROUP, DIFF_QK_HALF, DIFF_QK_HALF).astype(BF16)

    @pl.when(qi == 0)
    def _():
        grp = lax.broadcasted_iota(jnp.int32, (DIFF_GROUPS * DIFF_K, GROUP), 0) // DIFF_K
        slot = lax.broadcasted_iota(jnp.int32, (DIFF_GROUPS * DIFF_K, GROUP), 1) // DIFF_QK_HALF
        kmask = slot == 2 * (grp % HEADS) + grp // HEADS
        ones_rows = (lax.broadcasted_iota(jnp.int32, (AUG_ROWS - HEAD_DIM, DIFF_K), 0) == 0).astype(BF16)

        def build(j, kmax):
            k_t = k_ref[pl.ds(pl.multiple_of(j * DIFF_K, DIFF_K), DIFF_K), :]
            kexp_ref[j] = jnp.where(kmask, jnp.concatenate([k_t] * DIFF_GROUPS, axis=0),
                                    jnp.zeros((), BF16))
            vt = jnp.concatenate([vt_ref[2 * j], vt_ref[2 * j + 1]], axis=1)
            for h in range(HEADS):
                vaug_ref[j, h] = jnp.concatenate([vt[h * HEAD_DIM:(h + 1) * HEAD_DIM], ones_rows], axis=0)
            k_f = k_t.astype(F32)
            return jnp.maximum(kmax, jnp.max(_group_sum(k_f * k_f, qk_group), axis=0, keepdims=True))

        knorm_ref[...] = lax.fori_loop(0, key_steps, build, jnp.zeros((1, GROUP), F32))

    acc_ref[...] = jnp.zeros_like(acc_ref)
    q = q_ref[...]
    last = key_steps - 1
    key_steps_needed = (qi + 1) * (DIFF_Q // DIFF_K)
    pairs = (key_steps_needed + 1) // 2

    def bias_tiles(j):
        base = (DIFF_Q // TILE) * qi - (DIFF_K // TILE) * j - DIFF_MIN_OFFSET
        tiles = {d: bias_ref[jnp.maximum(base + d, 0)]
                 for d in range(1 - DIFF_K // TILE, DIFF_Q // TILE)}
        return [jnp.concatenate(
            [jnp.concatenate([tiles[a - b][h * TILE:(h + 1) * TILE] for a in range(DIFF_Q // TILE)], axis=1)
             for b in range(DIFF_K // TILE)], axis=0) for h in range(HEADS)]

    q_f = q.astype(F32)
    bound2 = jnp.max(_group_sum(q_f * q_f, qk_group) * knorm_ref[...])
    no_overflow = bound2 * DIFF_BOUND_SLACK <= lim_ref[0, 0]

    def unshifted():
        def weights(p_ref, j):
            bias = bias_tiles(j)
            raw = _dot_nt(kexp_ref[jnp.minimum(j, last)], q)
            for g in range(DIFF_GROUPS):
                rows = slice(g * DIFF_K, (g + 1) * DIFF_K)
                p_ref[rows, :] = jnp.exp2(raw[rows] + bias[g % HEADS]).astype(BF16)

        def accumulate(p_ref, j, ls):
            jv = jnp.minimum(j, last)
            new_ls = []
            for g in range(DIFF_GROUPS):
                mp, h = divmod(g, HEADS)
                r = _dot(vaug_ref[jv, h], p_ref[g * DIFF_K:(g + 1) * DIFF_K, :])
                acc_ref[mp, h] = acc_ref[mp, h] + r[:HEAD_DIM]
                new_ls.append(ls[g] + r[HEAD_DIM:HEAD_DIM + 1])
            return tuple(new_ls)

        def pair(jj, ls):
            j = 2 * jj
            weights(pb_ref, j + 1)
            ls = accumulate(pa_ref, j, ls)
            weights(pa_ref, j + 2)
            return accumulate(pb_ref, j + 1, ls)

        weights(pa_ref, 0)
        return lax.fori_loop(0, pairs, pair,
                             tuple(jnp.zeros((1, DIFF_Q), F32) for _ in range(DIFF_GROUPS)))

    def running_max():
        return _diff_running_max(q, qi, last, pairs, bias_tiles, kexp_ref, vaug_ref, acc_ref,
                                 sta_ref, stb_ref)

    ls = lax.cond(no_overflow, unshifted, running_max)

    lv = lam_ref[...]
    lam = (jnp.exp(jnp.sum(lv[0:1] * lv[1:2], axis=-1, keepdims=True))
           - jnp.exp(jnp.sum(lv[2:3] * lv[3:4], axis=-1, keepdims=True)) + lam_init)
    outs = []
    for h in range(HEADS):
        o = acc_ref[0, h] / ls[h] - lam * (acc_ref[1, h] / ls[HEADS + h])
        ms_o = jnp.mean(o * o, axis=0, keepdims=True)
        outs.append(o * lax.rsqrt(ms_o + SUBLN_EPS) * sg_ref[...] * (1.0 - lam_init))
    y_ref[...] = jnp.concatenate(outs, axis=0).T.astype(BF16)


def _diff_attention(zd, vt, bias, score_limit2, lam_vecs, subln_cols, *, lam_init, batch, seq):
    zd = zd.reshape(batch, seq, 2 * GROUP)
    key_tiles = seq // TILE
    key_steps = seq // DIFF_K
    vt = vt.reshape(batch, key_tiles, GROUP, TILE)
    y = pl.pallas_call(
        functools.partial(_diff_body, lam_init=lam_init, key_steps=key_steps),
        grid=(batch, seq // DIFF_Q),
        in_specs=[pl.BlockSpec((None, DIFF_Q, GROUP), lambda b, i: (b, i, 0)),
                  pl.BlockSpec((None, seq, GROUP), lambda b, i: (b, 0, 1), pipeline_mode=pl.Buffered(1)),
                  pl.BlockSpec((None, key_tiles, GROUP, TILE), lambda b, i: (b, 0, 0, 0),
                               pipeline_mode=pl.Buffered(1)),
                  _const_spec((key_tiles - DIFF_MIN_OFFSET, HEADS * TILE, TILE)),
                  _const_spec((4, DIFF_QK_HALF)), _const_spec((HEAD_DIM, DIFF_Q)),
                  pl.BlockSpec(memory_space=pltpu.SMEM)],
        out_specs=pl.BlockSpec((None, DIFF_Q, GROUP), lambda b, i: (b, i, 0)),
        out_shape=jax.ShapeDtypeStruct((batch, seq, GROUP), BF16),
        scratch_shapes=[pltpu.VMEM((key_steps, DIFF_GROUPS * DIFF_K, GROUP), BF16),
                        pltpu.VMEM((key_steps, HEADS, AUG_ROWS, DIFF_K), BF16),
                        pltpu.VMEM((2, HEADS, HEAD_DIM, DIFF_Q), F32),
                        pltpu.VMEM((DIFF_GROUPS * DIFF_K, DIFF_Q), F32),
                        pltpu.VMEM((DIFF_GROUPS * DIFF_K, DIFF_Q), F32),
                        pltpu.VMEM((DIFF_GROUPS * DIFF_K, DIFF_Q), BF16),
                        pltpu.VMEM((DIFF_GROUPS * DIFF_K, DIFF_Q), BF16),
                        pltpu.VMEM((1, GROUP), F32)],
        compiler_params=_params("parallel", "arbitrary"),
        name="diff_attn",
    )(zd, zd, vt, bias, lam_vecs, subln_cols, score_limit2)
    return y.reshape(batch * seq, GROUP)


KV_ROWS = 512


def _mem_kv_body(m_ref, g_ref, w_ref, k_ref, v_ref):
    u = _rms(m_ref[...], g_ref[...]).astype(BF16)
    for c in range(D_MODEL // GROUP):
        sl = slice(c * GROUP, (c + 1) * GROUP)
        k_ref[:, sl] = _dot(u, w_ref[:, sl]).astype(BF16)
        v_ref[:, sl] = _dot(u, w_ref[:, D_MODEL + c * GROUP: D_MODEL + (c + 1) * GROUP]).astype(BF16)


def _mem_kv(mem, g, w):
    n = mem.shape[0]
    row = pl.BlockSpec((KV_ROWS, D_MODEL), lambda i: (i, 0))
    return pl.pallas_call(
        _mem_kv_body,
        grid=(n // KV_ROWS,),
        in_specs=[row, _const_spec((1, D_MODEL)), _const_spec((D_MODEL, 2 * D_MODEL))],
        out_specs=[row, row],
        out_shape=[jax.ShapeDtypeStruct((n, D_MODEL), BF16)] * 2,
        compiler_params=_params("parallel"),
        name="mem_kv",
    )(mem, g, w)


XATTN_ROWS = 1024


def _xattn_body(x_ref, ya_ref, yb_ref, yc_ref, yd_ref, wout_ref, g_ref, wq_ref, k_ref, v_ref, wo_ref,
                o_ref, q_scr, a_scr):
    x = x_ref[...]
    for gi, y_ref in enumerate((ya_ref, yb_ref, yc_ref, yd_ref)):
        x = x + _dot(y_ref[...], wout_ref[gi * GROUP:(gi + 1) * GROUP, :])
    u = _rms(x, g_ref[...]).astype(BF16)
    for c in range(D_MODEL // GROUP):
        sl = slice(c * GROUP, (c + 1) * GROUP)
        q_scr[:, sl] = _dot(u, wq_ref[:, sl]).astype(BF16)
    for h in range(MEM_HEADS):
        sl = slice(h * MEM_HEAD_DIM, (h + 1) * MEM_HEAD_DIM)
        s = _dot_nt(q_scr[:, sl], k_ref[:, sl]) * (MEM_HEAD_DIM ** -0.5)
        e = jnp.exp(s - jnp.max(s, axis=-1, keepdims=True))
        l = jnp.sum(e, axis=-1, keepdims=True)
        a_scr[:, sl] = (_dot(e.astype(BF16), v_ref[:, sl]) / l).astype(BF16)
    o_ref[...] = x + _dot(a_scr[...], wo_ref[...])


def _xattn(h, ys, w_out, g, wq, k, v, wo, *, batch, seq):
    h3 = h.reshape(batch, seq, D_MODEL)
    ys = [y.reshape(batch, seq, GROUP) for y in ys]
    k3 = k.reshape(batch, MEM_LEN, D_MODEL)
    v3 = v.reshape(batch, MEM_LEN, D_MODEL)
    row = pl.BlockSpec((None, XATTN_ROWS, D_MODEL), lambda b, i: (b, i, 0))
    grp = pl.BlockSpec((None, XATTN_ROWS, GROUP), lambda b, i: (b, i, 0))
    mem = pl.BlockSpec((None, MEM_LEN, D_MODEL), lambda b, i: (b, 0, 0))
    weight = _const_spec((D_MODEL, D_MODEL))
    out = pl.pallas_call(
        _xattn_body,
        grid=(batch, seq // XATTN_ROWS),
        in_specs=[row, grp, grp, grp, grp, weight, _const_spec((1, D_MODEL)), weight, mem, mem, weight],
        out_specs=row,
        out_shape=jax.ShapeDtypeStruct((batch, seq, D_MODEL), F32),
        scratch_shapes=[pltpu.VMEM((XATTN_ROWS, D_MODEL), BF16), pltpu.VMEM((XATTN_ROWS, D_MODEL), BF16)],
        compiler_params=_params("parallel", "parallel"),
        name="xattn",
    )(h3, *ys, w_out, g, wq, k3, v3, wo)
    return out.reshape(batch * seq, D_MODEL)


def _per_head_lanes(x):
    return jnp.repeat(x, HEAD_DIM, axis=-1)


def _in_proj_weight(w_in):
    g = GROUP
    q_a, k_a, v_a, o_a = (w_in[:, i * g:(i + 1) * g] for i in range(4))
    ig = w_in[:, 4 * g:4 * g + HEADS]
    fg = w_in[:, 4 * g + HEADS:4 * g + 2 * HEADS]
    rest = w_in[:, 4 * g + 2 * HEADS:]
    pool, q_c, k_c, v_c, q_d, k_d, v_d = (rest[:, i * g:(i + 1) * g] for i in range(7))
    q_c = q_c * DIL_SCORE_SCALE
    k_d = k_d * DIFF_SCORE_SCALE
    cols = [q_a, k_a, v_a, o_a, _per_head_lanes(ig), _per_head_lanes(fg), pool,
            q_c, k_c, v_c, q_d, k_d]
    return jnp.concatenate(cols, axis=1).astype(BF16), v_d.astype(BF16)


def _block_diag(w):
    g, c, _ = w.shape
    eye = jnp.eye(g, dtype=w.dtype)
    return (eye[:, None, :, None] * w[:, :, None, :]).reshape(g * c, g * c)


def kernel(x, mem, t5_bias, ffn1_norm, ffn1_w_gate, ffn1_w_up, ffn1_w_down, mix_norm, w_in,
           mlstm_conv_w, mlstm_conv_b, mlstm_gate_b, mlstm_norm, pool_w, pool_scale,
           diff_lambda, diff_subln, w_out, xattn_norm, mem_norm, xattn_wq, xattn_wkv, xattn_wo,
           ffn2_norm, ffn2_w_gate, ffn2_w_up, ffn2_w_down, final_norm):
    batch, seq, _ = x.shape
    n = batch * seq
    dil_bias, diff_bias, diff_limit2 = _bias_tiles(t5_bias, seq)
    h = x.reshape(n, D_MODEL)
    mem2 = mem.reshape(batch * MEM_LEN, D_MODEL)
    row = lambda v: v.reshape(1, -1)
    for l in range(DEPTH):
        lam_init = 0.8 - 0.6 * math.exp(-0.3 * l)
        h = _ffn(h, row(ffn1_norm[l]), ffn1_w_gate[l].astype(BF16), ffn1_w_up[l].astype(BF16),
                 ffn1_w_down[l].astype(BF16), row(final_norm), final=False)
        za, zg, zp, zc, zd, vt, zc4, zc16 = _in_proj(h, row(mix_norm[l]), *_in_proj_weight(w_in[l]))
        ya = _mlstm(za, zg, mlstm_conv_w[l], row(mlstm_conv_b[l]),
                    row(_per_head_lanes(mlstm_gate_b[l].reshape(2, HEADS))), row(mlstm_norm[l]),
                    batch=batch, seq=seq)
        yb = _pool(zp, _block_diag(pool_w[l]).astype(BF16), row(pool_scale[l]), batch=batch, seq=seq)
        yc = _dilated((zc, zc4, zc16), dil_bias, batch=batch, seq=seq)
        yd = _diff_attention(zd, vt, diff_bias, diff_limit2, diff_lambda[l],
                             jnp.broadcast_to(diff_subln[l][:, None], (HEAD_DIM, DIFF_Q)),
                             lam_init=lam_init, batch=batch, seq=seq)
        k_mem, v_mem = _mem_kv(mem2, row(mem_norm[l]), xattn_wkv[l].astype(BF16))
        h = _xattn(h, (ya, yb, yc, yd), w_out[l].astype(BF16),
                   row(xattn_norm[l]), xattn_wq[l].astype(BF16), k_mem, v_mem,
                   xattn_wo[l].astype(BF16), batch=batch, seq=seq)
        h = _ffn(h, row(ffn2_norm[l]), ffn2_w_gate[l].astype(BF16), ffn2_w_up[l].astype(BF16),
                 ffn2_w_down[l].astype(BF16), row(final_norm), final=(l == DEPTH - 1))
    return h.reshape(batch, seq, D_MODEL)
```

```python
import functools
import math

import jax
import jax.numpy as jnp
import numpy as np
from jax import lax
from jax.experimental import pallas as pl
from jax.experimental.pallas import tpu as pltpu

F32 = jnp.float32
BF16 = jnp.bfloat16

D_MODEL = 1024
D_FF = 2816
DEPTH = 4
GROUP = 256
HEADS = 4
HEAD_DIM = GROUP // HEADS
MEM_LEN = 256
MEM_HEADS = 4
MEM_HEAD_DIM = D_MODEL // MEM_HEADS
MLSTM_CHUNK = 64
CONV_WIDTH = 4
POOL_WINDOWS = (2, 4, 8, 16)
DIL_PATTERNS = ((128, 1), (512, 4), (2048, 16))
DIL_BACK = 128
DIFF_QK_HALF = HEAD_DIM // 2
T5_BUCKETS = 32
T5_MAX_DIST = 2048
RMS_EPS = 1e-6
SUBLN_EPS = 1e-5
NEG = -1e30
LOG2E = math.log2(math.e)
DIFF_SCORE_SCALE = (DIFF_QK_HALF ** -0.5) * LOG2E
DIL_SCORE_SCALE = (HEAD_DIM ** -0.5) * LOG2E
AUG_ROWS = HEAD_DIM + 16
DIFF_MIN_OFFSET = -3
TILE = 128

VMEM_LIMIT_BYTES = 56 * 1024 * 1024


def _rms(xf, g, eps=RMS_EPS):
    return xf * lax.rsqrt(jnp.mean(xf * xf, axis=-1, keepdims=True) + eps) * g


def _const_spec(shape):
    zeros = (0,) * len(shape)
    return pl.BlockSpec(shape, lambda *_: zeros, pipeline_mode=pl.Buffered(1))


def _params(*sem):
    return pltpu.CompilerParams(dimension_semantics=sem, vmem_limit_bytes=VMEM_LIMIT_BYTES)


def _group_mask(rows, cols, row_group, col_group):
    r = lax.broadcasted_iota(jnp.int32, (rows, cols), 0) // row_group
    c = lax.broadcasted_iota(jnp.int32, (rows, cols), 1) // col_group
    return r == c


def _tile_rows(x, reps, mask):
    return jnp.where(mask, jnp.concatenate([x] * reps, axis=0), jnp.zeros((), x.dtype))


def _dot(a, b):
    return jnp.dot(a, b, preferred_element_type=F32)


def _dot_nt(a, b):
    return lax.dot_general(a, b, (((1,), (1,)), ((), ())), preferred_element_type=F32)


def _dot_tn(a, b):
    return lax.dot_general(a, b, (((0,), (0,)), ((), ())), preferred_element_type=F32)


def _group_sum(x, ones_bd):
    hi = x.astype(BF16)
    lo = (x - hi.astype(F32)).astype(BF16)
    return _dot(hi, ones_bd) + _dot(lo, ones_bd)


FFN_ROWS = 1024
FFN_COLS = 256


def _ffn_body(x_ref, g_ref, wg_ref, wu_ref, wd_ref, fg_ref, o_ref, act_ref, *, final):
    x = x_ref[...]
    u = _rms(x, g_ref[...]).astype(BF16)
    for c in range(D_FF // FFN_COLS):
        sl = slice(c * FFN_COLS, (c + 1) * FFN_COLS)
        gate = _dot(u, wg_ref[:, sl])
        up = _dot(u, wu_ref[:, sl])
        act_ref[:, sl] = (gate * jax.nn.sigmoid(gate) * up).astype(BF16)
    y = x + 0.5 * _dot(act_ref[...], wd_ref[...])
    if final:
        y = _rms(y, fg_ref[...])
    o_ref[...] = y


def _ffn(h, g, wg, wu, wd, fg, *, final):
    n = h.shape[0]
    row = pl.BlockSpec((FFN_ROWS, D_MODEL), lambda i: (i, 0))
    return pl.pallas_call(
        functools.partial(_ffn_body, final=final),
        grid=(n // FFN_ROWS,),
        in_specs=[row, _const_spec((1, D_MODEL)), _const_spec((D_MODEL, D_FF)),
                  _const_spec((D_MODEL, D_FF)), _const_spec((D_FF, D_MODEL)),
                  _const_spec((1, D_MODEL))],
        out_specs=row,
        out_shape=jax.ShapeDtypeStruct((n, D_MODEL), F32),
        scratch_shapes=[pltpu.VMEM((FFN_ROWS, D_FF), BF16)],
        compiler_params=_params("parallel"),
        name="ffn_final" if final else "ffn",
    )(h, g, wg, wu, wd, fg)


PROJ_ROWS = 1024
PROJ_OUTS = (("a", 4 * GROUP, F32), ("g", 2 * GROUP, F32), ("p", GROUP, F32),
             ("c", 3 * GROUP, BF16), ("d", 2 * GROUP, BF16))
PROJ_WIDTH = sum(w for _, w, _ in PROJ_OUTS)


PROJ_DILATIONS = tuple(d for _, d in DIL_PATTERNS if d > 1)


def _in_proj_body(x_ref, g_ref, w_ref, wv_ref, perm_ref, *o_refs):
    u = _rms(x_ref[...], g_ref[...]).astype(BF16)
    off = 0
    for o_ref, (_, width, dtype) in zip(o_refs, PROJ_OUTS):
        for c in range(width // GROUP):
            z = _dot(u, w_ref[:, off + c * GROUP: off + (c + 1) * GROUP])
            o_ref[:, c * GROUP:(c + 1) * GROUP] = z.astype(dtype)
        off += width
    vt_ref = o_refs[len(PROJ_OUTS)]
    v = _dot(u, wv_ref[...])
    for t in range(PROJ_ROWS // TILE):
        vt_ref[t] = v[t * TILE:(t + 1) * TILE].T.astype(BF16)
    for pi, d in enumerate(PROJ_DILATIONS):
        per_class = PROJ_UNIT // d
        for un in range(PROJ_ROWS // PROJ_UNIT):
            zp = _dot(perm_ref[pi], o_refs[3][un * PROJ_UNIT:(un + 1) * PROJ_UNIT, :]).astype(BF16)
            for r in range(d):
                o_refs[len(PROJ_OUTS) + 1 + pi][un * per_class:(un + 1) * per_class,
                                                r * 3 * GROUP:(r + 1) * 3 * GROUP] = \
                    zp[r * per_class:(r + 1) * per_class]


PROJ_UNIT = 2 * TILE


def _class_permutations():
    mats = np.zeros((len(PROJ_DILATIONS), PROJ_UNIT, PROJ_UNIT), np.float32)
    for pi, d in enumerate(PROJ_DILATIONS):
        t = np.arange(PROJ_UNIT)
        mats[pi, (t % d) * (PROJ_UNIT // d) + t // d, t] = 1.0
    return jnp.asarray(mats, BF16)


def _in_proj(h, g, w, wv):
    n = h.shape[0]
    tiles = PROJ_ROWS // TILE
    return pl.pallas_call(
        _in_proj_body,
        grid=(n // PROJ_ROWS,),
        in_specs=[pl.BlockSpec((PROJ_ROWS, D_MODEL), lambda i: (i, 0)),
                  _const_spec((1, D_MODEL)), _const_spec((D_MODEL, PROJ_WIDTH)),
                  _const_spec((D_MODEL, GROUP)),
                  _const_spec((len(PROJ_DILATIONS), PROJ_UNIT, PROJ_UNIT))],
        out_specs=[pl.BlockSpec((PROJ_ROWS, w_), lambda i: (i, 0)) for _, w_, _ in PROJ_OUTS]
        + [pl.BlockSpec((tiles, GROUP, TILE), lambda i: (i, 0, 0))]
        + [pl.BlockSpec((PROJ_ROWS // d, d * 3 * GROUP), lambda i: (i, 0)) for d in PROJ_DILATIONS],
        out_shape=[jax.ShapeDtypeStruct((n, w_), dt) for _, w_, dt in PROJ_OUTS]
        + [jax.ShapeDtypeStruct((n // TILE, GROUP, TILE), BF16)]
        + [jax.ShapeDtypeStruct((n // d, d * 3 * GROUP), BF16) for d in PROJ_DILATIONS],
        compiler_params=_params("parallel"),
        name="in_proj",
    )(h, g, w, wv, _class_permutations())


ML_ROWS = 512
ML_HALO = 8


def _chunk_scan(x, rin, op, fill):
    s = 1
    while s < MLSTM_CHUNK:
        x = op(x, jnp.where(rin >= s, pltpu.roll(x, s, 0), fill))
        s *= 2
    return x


def _mlstm_body(za_ref, zg_ref, cw_ref, cb_ref, gb_ref, ng_ref, y_ref,
                buf_ref, tail_ref, hh_ref, c_ref, n_ref, m_ref):
    L = MLSTM_CHUNK

    @pl.when(pl.program_id(1) == 0)
    def _():
        tail_ref[...] = jnp.zeros_like(tail_ref)
        c_ref[...] = jnp.zeros_like(c_ref)
        n_ref[...] = jnp.zeros_like(n_ref)
        m_ref[...] = jnp.zeros_like(m_ref)

    buf_ref[0:ML_HALO, :] = tail_ref[...]
    buf_ref[ML_HALO:, :] = za_ref[:, 0:2 * GROUP]
    tail_ref[...] = za_ref[ML_ROWS - ML_HALO:, 0:2 * GROUP]
    conv = cb_ref[...]
    for j in range(CONV_WIDTH):
        conv = conv + buf_ref[pl.ds(ML_HALO - (CONV_WIDTH - 1) + j, ML_ROWS), :] * cw_ref[j:j + 1, :]
    qk = conv * jax.nn.sigmoid(conv)
    q = qk[:, :GROUP]
    k = qk[:, GROUP:] * (HEAD_DIM ** -0.5)
    v = za_ref[:, 2 * GROUP:3 * GROUP].astype(BF16)
    q_bf, k_bf = q.astype(BF16), k.astype(BF16)

    ii = zg_ref[:, :GROUP] + gb_ref[:, :GROUP]
    fx = zg_ref[:, GROUP:] + gb_ref[:, GROUP:]
    lf = jnp.minimum(fx, 0.0) - jnp.log1p(jnp.exp(-jnp.abs(fx)))
    rin = lax.broadcasted_iota(jnp.int32, (ML_ROWS, GROUP), 0) % L
    b = _chunk_scan(lf, rin, jnp.add, 0.0)
    a = ii - b
    ca = _chunk_scan(a, rin, jnp.maximum, NEG)

    bd = _group_mask(GROUP, GROUP, HEAD_DIM, HEAD_DIM)
    ones_bd = bd.astype(BF16)
    row = lax.broadcasted_iota(jnp.int32, (L, GROUP), 0)
    key = lax.broadcasted_iota(jnp.int32, (L, GROUP), 1) % L
    causal = key <= row
    diag = key == row

    m_prev = m_ref[...]
    for c in range(ML_ROWS // L):
        rs = slice(c * L, (c + 1) * L)
        q_c, k_c, v_c = q[rs], k[rs], v[rs]
        q_b = q_bf[rs]
        a_c, b_c = a[rs], b[rs]
        g = jnp.maximum(m_prev, ca[rs])
        g_last = g[L - 1:L]
        a_row = jnp.sum(jnp.where(diag, a_c, 0.0), axis=0, keepdims=True)
        decay = jnp.exp(jnp.where(causal, a_row - g, NEG))
        sc = _dot_nt(q_b, _tile_rows(k_bf[rs], HEADS, bd)) * decay
        inter = jnp.exp(m_prev - g)
        num = inter * _dot(q_b, c_ref[...].astype(BF16)) + _dot(sc.astype(BF16), _tile_rows(v_c, HEADS, bd))
        den = inter * _group_sum(q_c * n_ref[...], ones_bd) + _group_sum(sc, ones_bd)
        hh_ref[rs, :] = num / jnp.maximum(jnp.abs(den), jnp.exp(-(b_c + g)))
        kw = k_c * jnp.exp(a_c - g_last)
        carry = jnp.exp(m_prev - g_last)
        c_ref[...] = carry * c_ref[...] + jnp.where(bd, _dot_tn(kw.astype(BF16), v_c), 0.0)
        n_ref[...] = carry * n_ref[...] + jnp.sum(kw, axis=0, keepdims=True)
        m_prev = b_c[L - 1:L] + g_last
    m_ref[...] = m_prev

    hh = hh_ref[...]
    mu = _group_sum(hh, ones_bd) * (1.0 / HEAD_DIM)
    dev = hh - mu
    var = _group_sum(dev * dev, ones_bd) * (1.0 / HEAD_DIM)
    o_gate = jax.nn.sigmoid(za_ref[:, 3 * GROUP:])
    y_ref[...] = (dev * lax.rsqrt(var + RMS_EPS) * ng_ref[...] * o_gate).astype(BF16)


def _mlstm(za, zg, conv_w, conv_b, gate_b, norm_g, *, batch, seq):
    za = za.reshape(batch, seq, 4 * GROUP)
    zg = zg.reshape(batch, seq, 2 * GROUP)
    y = pl.pallas_call(
        _mlstm_body,
        grid=(batch, seq // ML_ROWS),
        in_specs=[pl.BlockSpec((None, ML_ROWS, 4 * GROUP), lambda b, j: (b, j, 0)),
                  pl.BlockSpec((None, ML_ROWS, 2 * GROUP), lambda b, j: (b, j, 0)),
                  _const_spec((CONV_WIDTH, 2 * GROUP)), _const_spec((1, 2 * GROUP)),
                  _const_spec((1, 2 * GROUP)), _const_spec((1, GROUP))],
        out_specs=pl.BlockSpec((None, ML_ROWS, GROUP), lambda b, j: (b, j, 0)),
        out_shape=jax.ShapeDtypeStruct((batch, seq, GROUP), BF16),
        scratch_shapes=[pltpu.VMEM((ML_ROWS + ML_HALO, 2 * GROUP), F32),
                        pltpu.VMEM((ML_HALO, 2 * GROUP), F32),
                        pltpu.VMEM((ML_ROWS, GROUP), F32),
                        pltpu.VMEM((GROUP, GROUP), F32),
                        pltpu.VMEM((1, GROUP), F32),
                        pltpu.VMEM((1, GROUP), F32)],
        compiler_params=_params("parallel", "arbitrary"),
        name="mlstm",
    )(za, zg, conv_w, conv_b, gate_b, norm_g)
    return y.reshape(batch * seq, GROUP)


POOL_ROWS = 1024
POOL_HALO = 16


def _pool_body(u_ref, w_ref, s_ref, y_ref, buf_ref, tail_ref):
    j = pl.program_id(1)

    @pl.when(j == 0)
    def _():
        tail_ref[...] = jnp.zeros_like(tail_ref)

    buf_ref[0:POOL_HALO, :] = tail_ref[...]
    buf_ref[POOL_HALO:, :] = u_ref[...]
    tail_ref[...] = u_ref[POOL_ROWS - POOL_HALO:, :]
    sums, s = [], buf_ref[...]
    for shift in (1, 2, 4, 8):
        s = s + pltpu.roll(s, shift, 0)
        sums.append(s[POOL_HALO:])
    u = u_ref[...]
    lane_group = lax.broadcasted_iota(jnp.int32, (POOL_ROWS, GROUP), 1) // HEAD_DIM
    t = j * POOL_ROWS + lax.broadcasted_iota(jnp.int32, (POOL_ROWS, GROUP), 0)
    total, win = sums[3], jnp.full((POOL_ROWS, GROUP), POOL_WINDOWS[3], jnp.int32)
    for gi in (2, 1, 0):
        total = jnp.where(lane_group == gi, sums[gi], total)
        win = jnp.where(lane_group == gi, POOL_WINDOWS[gi], win)
    mean = total / jnp.minimum(t + 1, win).astype(F32)
    y = _dot((mean - u).astype(BF16), w_ref[...]) * s_ref[...]
    y_ref[...] = y.astype(BF16)


def _pool(zp, w_bd, scale, *, batch, seq):
    zp = zp.reshape(batch, seq, GROUP)
    y = pl.pallas_call(
        _pool_body,
        grid=(batch, seq // POOL_ROWS),
        in_specs=[pl.BlockSpec((None, POOL_ROWS, GROUP), lambda b, j: (b, j, 0)),
                  _const_spec((GROUP, GROUP)), _const_spec((1, GROUP))],
        out_specs=pl.BlockSpec((None, POOL_ROWS, GROUP), lambda b, j: (b, j, 0)),
        out_shape=jax.ShapeDtypeStruct((batch, seq, GROUP), BF16),
        scratch_shapes=[pltpu.VMEM((POOL_ROWS + POOL_HALO, GROUP), F32),
                        pltpu.VMEM((POOL_HALO, GROUP), F32)],
        compiler_params=_params("parallel", "arbitrary"),
        name="pool",
    )(zp, w_bd, scale)
    return y.reshape(batch * seq, GROUP)


TOEPLITZ_BATCH = 4


def _toeplitz_body(w_ref, o_ref):
    for t in range(TOEPLITZ_BATCH):
        x = jnp.broadcast_to(w_ref[t], (TILE, 2 * TILE))
        o_ref[t] = pltpu.roll(x, 0, 1, stride=1, stride_axis=0)[:, :TILE]


def _toeplitz(rows):
    n = rows.shape[0]
    assert n % TOEPLITZ_BATCH == 0
    return pl.pallas_call(
        _toeplitz_body,
        grid=(n // TOEPLITZ_BATCH,),
        in_specs=[pl.BlockSpec((TOEPLITZ_BATCH, 1, 2 * TILE), lambda i: (i, 0, 0))],
        out_specs=pl.BlockSpec((TOEPLITZ_BATCH, TILE, TILE), lambda i: (i, 0, 0)),
        out_shape=jax.ShapeDtypeStruct((n, TILE, TILE), F32),
        compiler_params=_params("parallel"),
        name="toeplitz",
    )(rows.reshape(n, 1, 2 * TILE))


_TOEPLITZ_X = np.where(np.arange(2 * TILE) <= TILE, -np.arange(2 * TILE), 2 * TILE - np.arange(2 * TILE))


def _t5_bucket(dist):
    max_exact = T5_BUCKETS // 2
    d = jnp.maximum(dist, 1).astype(F32)
    large = max_exact + (jnp.log(d / max_exact) / math.log(T5_MAX_DIST / max_exact)
                         * (T5_BUCKETS - max_exact)).astype(jnp.int32)
    large = jnp.minimum(large, T5_BUCKETS - 1)
    return jnp.where(dist < max_exact, dist, large)


def _bias_tiles(t5_bias, seq):
    x = jnp.asarray(_TOEPLITZ_X, jnp.int32)
    table = t5_bias * LOG2E
    delta = jnp.clip(jnp.stack([-x, DIL_BACK - x]), 0, DIL_BACK)
    valid = jnp.stack([x <= 0, x >= 0])[None, :, :, None]
    rows = table[_t5_bucket(jnp.stack([delta * d for _, d in DIL_PATTERNS])), :HEADS]
    rows = jnp.where(valid, rows, NEG).transpose(0, 3, 1, 2)
    dil = _toeplitz(rows.reshape(-1, 2 * TILE))
    dil = dil.reshape(len(DIL_PATTERNS), HEADS, 2, TILE, TILE).transpose(0, 1, 3, 2, 4)
    dil = dil.reshape(len(DIL_PATTERNS), HEADS * TILE, 2 * TILE)
    noff = seq // TILE - DIFF_MIN_OFFSET
    dist = (jnp.arange(noff)[:, None] + DIFF_MIN_OFFSET) * TILE - x[None, :]
    rows = table[_t5_bucket(jnp.clip(dist, 0, seq - 1)), HEADS:]
    rows = jnp.where((dist >= 0)[:, :, None], rows, NEG).transpose(0, 2, 1)
    diff = _toeplitz(rows.reshape(-1, 2 * TILE))
    room = jnp.maximum(DIFF_SAFE_LOG2 - jnp.max(jnp.abs(table[:, HEADS:])), 0.0)
    return dil, diff.reshape(noff, HEADS * TILE, TILE), (room * room).reshape(1, 1)


DIL_UNIT = 2 * TILE


def _dil_body(*refs, subs, dil, merge, has_prev):
    if merge:
        x_ref, xp_ref, bias_ref, unperm_ref, o1_ref, l1_ref, o2_ref, l2_ref, y_ref = refs
    else:
        x_ref, xp_ref, bias_ref, unperm_ref, o_ref, lse_ref = refs
    per_class = [_dil_class(x_ref, xp_ref, bias_ref, rc=rc, subs=subs, has_prev=has_prev)
                 for rc in range(dil)]
    piece = DIL_UNIT // dil
    class_o = [jnp.concatenate(pc[0], axis=0) for pc in per_class]
    class_l = [jnp.concatenate(pc[1], axis=0) for pc in per_class]
    for u in range(dil * subs * TILE // DIL_UNIT):
        take = lambda arrs: jnp.concatenate([a[u * piece:(u + 1) * piece] for a in arrs], axis=0)
        if dil == 1:
            o, lse = take(class_o), take(class_l)
        else:
            o_c = take(class_o).astype(BF16)
            l_c = take(class_l)
            l_hi = l_c.astype(BF16)
            l_lo = (l_c - l_hi.astype(F32)).astype(BF16)
            o = _dot(unperm_ref[...], o_c)
            lse = _dot(unperm_ref[...], l_hi) + _dot(unperm_ref[...], l_lo)
        rs = slice(u * DIL_UNIT, (u + 1) * DIL_UNIT)
        if merge:
            o1, l1 = o1_ref[rs, :].astype(F32), l1_ref[rs, :]
            o2, l2 = o2_ref[rs, :].astype(F32), l2_ref[rs, :]
            top = jnp.maximum(jnp.maximum(l1, l2), lse)
            w1, w2, w3 = jnp.exp2(l1 - top), jnp.exp2(l2 - top), jnp.exp2(lse - top)
            y_ref[rs, :] = ((w1 * o1 + w2 * o2 + w3 * o) / (w1 + w2 + w3)).astype(BF16)
        else:
            o_ref[rs, :] = o.astype(BF16)
            lse_ref[rs, :] = lse


def _dil_class(x_ref, xp_ref, bias_ref, *, rc, subs, has_prev):
    first = pl.program_id(1) == 0
    kmask = _group_mask(HEADS * TILE, GROUP, TILE, HEAD_DIM)
    ones_rows = (lax.broadcasted_iota(jnp.int32, (AUG_ROWS - HEAD_DIM, TILE), 0) == 0).astype(BF16)
    rows_of = lambda sb: slice(sb * TILE, (sb + 1) * TILE)
    col_q, col_k, col_v = (slice((3 * rc + w) * GROUP, (3 * rc + w + 1) * GROUP) for w in range(3))

    s_same, s_next, vaug = {}, {}, {}
    first_block = -1 if has_prev else 0
    for j in range(first_block, subs):
        k_j = xp_ref[:, col_k] if j < 0 else x_ref[rows_of(j), col_k]
        v_j = xp_ref[:, col_v] if j < 0 else x_ref[rows_of(j), col_v]
        parts = ([0] if j >= 0 else []) + ([1] if j + 1 < subs else [])
        q_cat = jnp.concatenate([x_ref[rows_of(j + e), col_q] for e in parts], axis=0)
        bias = bias_ref[:, parts[0] * TILE:(parts[-1] + 1) * TILE]
        st = _dot_nt(_tile_rows(k_j, HEADS, kmask), q_cat) + bias
        if j < 0:
            st = st + jnp.where(first, NEG, 0.0)
        for pos, e in enumerate(parts):
            (s_same if e == 0 else s_next)[j + e] = st[:, pos * TILE:(pos + 1) * TILE]
        v_t = v_j.astype(F32).T.astype(BF16)
        vaug[j] = [jnp.concatenate([v_t[h * HEAD_DIM:(h + 1) * HEAD_DIM], ones_rows], axis=0)
                   for h in range(HEADS)]

    p_same, p_next, tops = {}, {}, {}
    for i in range(subs):
        ps, pn, tp = [], [], []
        for h in range(HEADS):
            hs = slice(h * TILE, (h + 1) * TILE)
            a = s_same[i][hs]
            m = jnp.max(a, axis=0, keepdims=True)
            if i in s_next:
                b = s_next[i][hs]
                m = jnp.maximum(m, jnp.max(b, axis=0, keepdims=True))
                pn.append(jnp.exp2(b - m).astype(BF16))
            ps.append(jnp.exp2(a - m).astype(BF16))
            tp.append(m)
        p_same[i], p_next[i], tops[i] = ps, pn, tp

    acc = {i: [None] * HEADS for i in range(subs)}
    for j in range(first_block, subs):
        for h in range(HEADS):
            cols = ([p_same[j][h]] if j >= 0 else []) + ([p_next[j + 1][h]] if j + 1 < subs else [])
            r = _dot(vaug[j][h], jnp.concatenate(cols, axis=1))
            targets = ([j] if j >= 0 else []) + ([j + 1] if j + 1 < subs else [])
            for pos, i in enumerate(targets):
                part = r[:, pos * TILE:(pos + 1) * TILE]
                acc[i][h] = part if acc[i][h] is None else acc[i][h] + part

    outs, lses = [], []
    for i in range(subs):
        o_t, lse_t = [], []
        for h in range(HEADS):
            l = acc[i][h][HEAD_DIM:HEAD_DIM + 1]
            o_t.append(acc[i][h][:HEAD_DIM] / l)
            lse_t.append(jnp.broadcast_to(tops[i][h] + jnp.log2(l), (HEAD_DIM, TILE)))
        outs.append(jnp.concatenate(o_t, axis=0).T)
        lses.append(jnp.concatenate(lse_t, axis=0).T)
    return outs, lses


DIL_SUBBLOCKS = 8


def _unpermutation(dil):
    t = np.arange(DIL_UNIT)
    mat = np.zeros((DIL_UNIT, DIL_UNIT), np.float32)
    mat[t, (t % dil) * (DIL_UNIT // dil) + t // dil] = 1.0
    return jnp.asarray(mat, BF16)


def _dilated_pattern(zc, bias, dil, *, batch, seq, merge_with=None):
    length = seq // dil
    subs = min(max(DIL_SUBBLOCKS // max(dil // 2, 1), 2), length // TILE)
    rows = subs * TILE
    tokens = rows * dil
    zc = zc.reshape(batch, length, dil * 3 * GROUP)
    blk = pl.BlockSpec((None, rows, dil * 3 * GROUP), lambda b, n: (b, n, 0))
    prev = pl.BlockSpec((None, TILE, dil * 3 * GROUP), lambda b, n: (b, jnp.maximum(n * subs - 1, 0), 0))
    nat = pl.BlockSpec((None, tokens, GROUP), lambda b, n: (b, n, 0))
    in_specs = [blk, prev, _const_spec((HEADS * TILE, 2 * TILE)), _const_spec((DIL_UNIT, DIL_UNIT))]
    args = [zc, zc, bias, _unpermutation(dil)]
    if merge_with is None:
        out_specs = [nat, nat]
        out_shape = [jax.ShapeDtypeStruct((batch, seq, GROUP), BF16),
                     jax.ShapeDtypeStruct((batch, seq, GROUP), F32)]
    else:
        in_specs += [nat] * len(merge_with)
        args += list(merge_with)
        out_specs = nat
        out_shape = jax.ShapeDtypeStruct((batch, seq, GROUP), BF16)
    out = pl.pallas_call(
        functools.partial(_dil_body, subs=subs, dil=dil, merge=merge_with is not None,
                          has_prev=length > rows),
        grid=(batch, length // rows),
        in_specs=in_specs, out_specs=out_specs, out_shape=out_shape,
        compiler_params=_params("parallel", "parallel"),
        name=f"dilated_d{dil}",
    )(*args)
    if merge_with is None:
        return out
    return out.reshape(batch * seq, GROUP)


def _dilated(zc_views, dil_bias, *, batch, seq):
    o1, l1 = _dilated_pattern(zc_views[0], dil_bias[0], DIL_PATTERNS[0][1], batch=batch, seq=seq)
    o2, l2 = _dilated_pattern(zc_views[1], dil_bias[1], DIL_PATTERNS[1][1], batch=batch, seq=seq)
    return _dilated_pattern(zc_views[2], dil_bias[2], DIL_PATTERNS[2][1], batch=batch, seq=seq,
                            merge_with=(o1, l1, o2, l2))


DIFF_Q = 512
DIFF_K = 256
DIFF_GROUPS = 2 * HEADS


DIFF_SAFE_LOG2 = 60.0
DIFF_BOUND_SLACK = 1.01


def _diff_running_max(q, qi, last, pairs, bias_tiles, kexp_ref, vaug_ref, acc_ref, sta_ref, stb_ref):
    def scores(s_ref, j):
        bias = bias_tiles(j)
        raw = _dot_nt(kexp_ref[jnp.minimum(j, last)], q)
        tops = []
        for g in range(DIFF_GROUPS):
            s = raw[g * DIFF_K:(g + 1) * DIFF_K] + bias[g % HEADS]
            s_ref[g * DIFF_K:(g + 1) * DIFF_K, :] = s
            tops.append(jnp.max(s, axis=0, keepdims=True))
        return tuple(tops)

    def consume(s_ref, tops, j, carry):
        ms, ls = carry
        jv = jnp.minimum(j, last)
        new_ms, new_ls = [], []
        for g in range(DIFF_GROUPS):
            mp, h = divmod(g, HEADS)
            m_new = jnp.maximum(ms[g], tops[g])
            p = jnp.exp2(s_ref[g * DIFF_K:(g + 1) * DIFF_K, :] - m_new).astype(BF16)
            alpha = jnp.exp2(ms[g] - m_new)
            r = _dot(vaug_ref[jv, h], p)
            acc_ref[mp, h] = alpha * acc_ref[mp, h] + r[:HEAD_DIM]
            new_ls.append(alpha * ls[g] + r[HEAD_DIM:HEAD_DIM + 1])
            new_ms.append(m_new)
        return tuple(new_ms), tuple(new_ls)

    def pair(jj, carry):
        tops_a, state = carry
        j = 2 * jj
        tops_b = scores(stb_ref, j + 1)
        state = consume(sta_ref, tops_a, j, state)
        tops_a = scores(sta_ref, j + 2)
        return tops_a, consume(stb_ref, tops_b, j + 1, state)

    init = (tuple(jnp.full((1, DIFF_Q), NEG, F32) for _ in range(DIFF_GROUPS)),
            tuple(jnp.zeros((1, DIFF_Q), F32) for _ in range(DIFF_GROUPS)))
    _, (_, ls) = lax.fori_loop(0, pairs, pair, (scores(sta_ref, 0), init))
    return ls


def _diff_body(q_ref, k_ref, vt_ref, bias_ref, lam_ref, sg_ref, lim_ref, y_ref,
               kexp_ref, vaug_ref, acc_ref, sta_ref, stb_ref, pa_ref, pb_ref, knorm_ref,
               *, lam_init, key_steps):
    qi = pl.program_id(1)
    qk_group = _group_mask(GROUP, GROUP, DIFF_QK_HALF, DIFF_QK_HALF).astype(BF16)

    @pl.when(qi == 0)
    def _():
        grp = lax.broadcasted_iota(jnp.int32, (DIFF_GROUPS * DIFF_K, GROUP), 0) // DIFF_K
        slot = lax.broadcasted_iota(jnp.int32, (DIFF_GROUPS * DIFF_K, GROUP), 1) // DIFF_QK_HALF
        kmask = slot == 2 * (grp % HEADS) + grp // HEADS
        ones_rows = (lax.broadcasted_iota(jnp.int32, (AUG_ROWS - HEAD_DIM, DIFF_K), 0) == 0).astype(BF16)

        def build(j, kmax):
            k_t = k_ref[pl.ds(pl.multiple_of(j * DIFF_K, DIFF_K), DIFF_K), :]
            kexp_ref[j] = jnp.where(kmask, jnp.concatenate([k_t] * DIFF_GROUPS, axis=0),
                                    jnp.zeros((), BF16))
            vt = jnp.concatenate([vt_ref[2 * j], vt_ref[2 * j + 1]], axis=1)
            for h in range(HEADS):
                vaug_ref[j, h] = jnp.concatenate([vt[h * HEAD_DIM:(h + 1) * HEAD_DIM], ones_rows], axis=0)
            k_f = k_t.astype(F32)
            return jnp.maximum(kmax, jnp.max(_group_sum(k_f * k_f, qk_group), axis=0, keepdims=True))

        knorm_ref[...] = lax.fori_loop(0, key_steps, build, jnp.zeros((1, GROUP), F32))

    acc_ref[...] = jnp.zeros_like(acc_ref)
    q = q_ref[...]
    last = key_steps - 1
    key_steps_needed = (qi + 1) * (DIFF_Q // DIFF_K)
    pairs = (key_steps_needed + 1) // 2

    def bias_tiles(j):
        base = (DIFF_Q // TILE) * qi - (DIFF_K // TILE) * j - DIFF_MIN_OFFSET
        tiles = {d: bias_ref[jnp.maximum(base + d, 0)]
                 for d in range(1 - DIFF_K // TILE, DIFF_Q // TILE)}
        return [jnp.concatenate(
            [jnp.concatenate([tiles[a - b][h * TILE:(h + 1) * TILE] for a in range(DIFF_Q // TILE)], axis=1)
             for b in range(DIFF_K // TILE)], axis=0) for h in range(HEADS)]

    q_f = q.astype(F32)
    bound2 = jnp.max(_group_sum(q_f * q_f, qk_group) * knorm_ref[...])
    no_overflow = bound2 * DIFF_BOUND_SLACK <= lim_ref[0, 0]

    def unshifted():
        def weights(p_ref, j):
            bias = bias_tiles(j)
            raw = _dot_nt(kexp_ref[jnp.minimum(j, last)], q)
            for g in range(DIFF_GROUPS):
                rows = slice(g * DIFF_K, (g + 1) * DIFF_K)
                p_ref[rows, :] = jnp.exp2(raw[rows] + bias[g % HEADS]).astype(BF16)

        def accumulate(p_ref, j, ls):
            jv = jnp.minimum(j, last)
            new_ls = []
            for g in range(DIFF_GROUPS):
                mp, h = divmod(g, HEADS)
                r = _dot(vaug_ref[jv, h], p_ref[g * DIFF_K:(g + 1) * DIFF_K, :])
                acc_ref[mp, h] = acc_ref[mp, h] + r[:HEAD_DIM]
                new_ls.append(ls[g] + r[HEAD_DIM:HEAD_DIM + 1])
            return tuple(new_ls)

        def pair(jj, ls):
            j = 2 * jj
            weights(pb_ref, j + 1)
            ls = accumulate(pa_ref, j, ls)
            weights(pa_ref, j + 2)
            return accumulate(pb_ref, j + 1, ls)

        weights(pa_ref, 0)
        return lax.fori_loop(0, pairs, pair,
                             tuple(jnp.zeros((1, DIFF_Q), F32) for _ in range(DIFF_GROUPS)))

    def running_max():
        return _diff_running_max(q, qi, last, pairs, bias_tiles, kexp_ref, vaug_ref, acc_ref,
                                 sta_ref, stb_ref)

    ls = lax.cond(no_overflow, unshifted, running_max)

    lv = lam_ref[...]
    lam = (jnp.exp(jnp.sum(lv[0:1] * lv[1:2], axis=-1, keepdims=True))
           - jnp.exp(jnp.sum(lv[2:3] * lv[3:4], axis=-1, keepdims=True)) + lam_init)
    outs = []
    for h in range(HEADS):
        o = acc_ref[0, h] / ls[h] - lam * (acc_ref[1, h] / ls[HEADS + h])
        ms_o = jnp.mean(o * o, axis=0, keepdims=True)
        outs.append(o * lax.rsqrt(ms_o + SUBLN_EPS) * sg_ref[...] * (1.0 - lam_init))
    y_ref[...] = jnp.concatenate(outs, axis=0).T.astype(BF16)


def _diff_attention(zd, vt, bias, score_limit2, lam_vecs, subln_cols, *, lam_init, batch, seq):
    zd = zd.reshape(batch, seq, 2 * GROUP)
    key_tiles = seq // TILE
    key_steps = seq // DIFF_K
    vt = vt.reshape(batch, key_tiles, GROUP, TILE)
    y = pl.pallas_call(
        functools.partial(_diff_body, lam_init=lam_init, key_steps=key_steps),
        grid=(batch, seq // DIFF_Q),
        in_specs=[pl.BlockSpec((None, DIFF_Q, GROUP), lambda b, i: (b, i, 0)),
                  pl.BlockSpec((None, seq, GROUP), lambda b, i: (b, 0, 1), pipeline_mode=pl.Buffered(1)),
                  pl.BlockSpec((None, key_tiles, GROUP, TILE), lambda b, i: (b, 0, 0, 0),
                               pipeline_mode=pl.Buffered(1)),
                  _const_spec((key_tiles - DIFF_MIN_OFFSET, HEADS * TILE, TILE)),
                  _const_spec((4, DIFF_QK_HALF)), _const_spec((HEAD_DIM, DIFF_Q)),
                  pl.BlockSpec(memory_space=pltpu.SMEM)],
        out_specs=pl.BlockSpec((None, DIFF_Q, GROUP), lambda b, i: (b, i, 0)),
        out_shape=jax.ShapeDtypeStruct((batch, seq, GROUP), BF16),
        scratch_shapes=[pltpu.VMEM((key_steps, DIFF_GROUPS * DIFF_K, GROUP), BF16),
                        pltpu.VMEM((key_steps, HEADS, AUG_ROWS, DIFF_K), BF16),
                        pltpu.VMEM((2, HEADS, HEAD_DIM, DIFF_Q), F32),
                        pltpu.VMEM((DIFF_GROUPS * DIFF_K, DIFF_Q), F32),
                        pltpu.VMEM((DIFF_GROUPS * DIFF_K, DIFF_Q), F32),
                        pltpu.VMEM((DIFF_GROUPS * DIFF_K, DIFF_Q), BF16),
                        pltpu.VMEM((DIFF_GROUPS * DIFF_K, DIFF_Q), BF16),
                        pltpu.VMEM((1, GROUP), F32)],
        compiler_params=_params("parallel", "arbitrary"),
        name="diff_attn",
    )(zd, zd, vt, bias, lam_vecs, subln_cols, score_limit2)
    return y.reshape(batch * seq, GROUP)


KV_ROWS = 512


def _mem_kv_body(m_ref, g_ref, w_ref, k_ref, v_ref):
    u = _rms(m_ref[...], g_ref[...]).astype(BF16)
    for c in range(D_MODEL // GROUP):
        sl = slice(c * GROUP, (c + 1) * GROUP)
        k_ref[:, sl] = _dot(u, w_ref[:, sl]).astype(BF16)
        v_ref[:, sl] = _dot(u, w_ref[:, D_MODEL + c * GROUP: D_MODEL + (c + 1) * GROUP]).astype(BF16)


def _mem_kv(mem, g, w):
    n = mem.shape[0]
    row = pl.BlockSpec((KV_ROWS, D_MODEL), lambda i: (i, 0))
    return pl.pallas_call(
        _mem_kv_body,
        grid=(n // KV_ROWS,),
        in_specs=[row, _const_spec((1, D_MODEL)), _const_spec((D_MODEL, 2 * D_MODEL))],
        out_specs=[row, row],
        out_shape=[jax.ShapeDtypeStruct((n, D_MODEL), BF16)] * 2,
        compiler_params=_params("parallel"),
        name="mem_kv",
    )(mem, g, w)


XATTN_ROWS = 1024


def _xattn_body(x_ref, ya_ref, yb_ref, yc_ref, yd_ref, wout_ref, g_ref, wq_ref, k_ref, v_ref, wo_ref,
                o_ref, q_scr, a_scr):
    x = x_ref[...]
    for gi, y_ref in enumerate((ya_ref, yb_ref, yc_ref, yd_ref)):
        x = x + _dot(y_ref[...], wout_ref[gi * GROUP:(gi + 1) * GROUP, :])
    u = _rms(x, g_ref[...]).astype(BF16)
    for c in range(D_MODEL // GROUP):
        sl = slice(c * GROUP, (c + 1) * GROUP)
        q_scr[:, sl] = _dot(u, wq_ref[:, sl]).astype(BF16)
    for h in range(MEM_HEADS):
        sl = slice(h * MEM_HEAD_DIM, (h + 1) * MEM_HEAD_DIM)
        s = _dot_nt(q_scr[:, sl], k_ref[:, sl]) * (MEM_HEAD_DIM ** -0.5)
        e = jnp.exp(s - jnp.max(s, axis=-1, keepdims=True))
        l = jnp.sum(e, axis=-1, keepdims=True)
        a_scr[:, sl] = (_dot(e.astype(BF16), v_ref[:, sl]) / l).astype(BF16)
    o_ref[...] = x + _dot(a_scr[...], wo_ref[...])


def _xattn(h, ys, w_out, g, wq, k, v, wo, *, batch, seq):
    h3 = h.reshape(batch, seq, D_MODEL)
    ys = [y.reshape(batch, seq, GROUP) for y in ys]
    k3 = k.reshape(batch, MEM_LEN, D_MODEL)
    v3 = v.reshape(batch, MEM_LEN, D_MODEL)
    row = pl.BlockSpec((None, XATTN_ROWS, D_MODEL), lambda b, i: (b, i, 0))
    grp = pl.BlockSpec((None, XATTN_ROWS, GROUP), lambda b, i: (b, i, 0))
    mem = pl.BlockSpec((None, MEM_LEN, D_MODEL), lambda b, i: (b, 0, 0))
    weight = _const_spec((D_MODEL, D_MODEL))
    out = pl.pallas_call(
        _xattn_body,
        grid=(batch, seq // XATTN_ROWS),
        in_specs=[row, grp, grp, grp, grp, weight, _const_spec((1, D_MODEL)), weight, mem, mem, weight],
        out_specs=row,
        out_shape=jax.ShapeDtypeStruct((batch, seq, D_MODEL), F32),
        scratch_shapes=[pltpu.VMEM((XATTN_ROWS, D_MODEL), BF16), pltpu.VMEM((XATTN_ROWS, D_MODEL), BF16)],
        compiler_params=_params("parallel", "parallel"),
        name="xattn",
    )(h3, *ys, w_out, g, wq, k3, v3, wo)
    return out.reshape(batch * seq, D_MODEL)


def _per_head_lanes(x):
    return jnp.repeat(x, HEAD_DIM, axis=-1)


def _in_proj_weight(w_in):
    g = GROUP
    q_a, k_a, v_a, o_a = (w_in[:, i * g:(i + 1) * g] for i in range(4))
    ig = w_in[:, 4 * g:4 * g + HEADS]
    fg = w_in[:, 4 * g + HEADS:4 * g + 2 * HEADS]
    rest = w_in[:, 4 * g + 2 * HEADS:]
    pool, q_c, k_c, v_c, q_d, k_d, v_d = (rest[:, i * g:(i + 1) * g] for i in range(7))
    q_c = q_c * DIL_SCORE_SCALE
    k_d = k_d * DIFF_SCORE_SCALE
    cols = [q_a, k_a, v_a, o_a, _per_head_lanes(ig), _per_head_lanes(fg), pool,
            q_c, k_c, v_c, q_d, k_d]
    return jnp.concatenate(cols, axis=1).astype(BF16), v_d.astype(BF16)


def _block_diag(w):
    g, c, _ = w.shape
    eye = jnp.eye(g, dtype=w.dtype)
    return (eye[:, None, :, None] * w[:, :, None, :]).reshape(g * c, g * c)


def kernel(x, mem, t5_bias, ffn1_norm, ffn1_w_gate, ffn1_w_up, ffn1_w_down, mix_norm, w_in,
           mlstm_conv_w, mlstm_conv_b, mlstm_gate_b, mlstm_norm, pool_w, pool_scale,
           diff_lambda, diff_subln, w_out, xattn_norm, mem_norm, xattn_wq, xattn_wkv, xattn_wo,
           ffn2_norm, ffn2_w_gate, ffn2_w_up, ffn2_w_down, final_norm):
    batch, seq, _ = x.shape
    n = batch * seq
    dil_bias, diff_bias, diff_limit2 = _bias_tiles(t5_bias, seq)
    h = x.reshape(n, D_MODEL)
    mem2 = mem.reshape(batch * MEM_LEN, D_MODEL)
    row = lambda v: v.reshape(1, -1)
    for l in range(DEPTH):
        lam_init = 0.8 - 0.6 * math.exp(-0.3 * l)
        h = _ffn(h, row(ffn1_norm[l]), ffn1_w_gate[l].astype(BF16), ffn1_w_up[l].astype(BF16),
                 ffn1_w_down[l].astype(BF16), row(final_norm), final=False)
        za, zg, zp, zc, zd, vt, zc4, zc16 = _in_proj(h, row(mix_norm[l]), *_in_proj_weight(w_in[l]))
        ya = _mlstm(za, zg, mlstm_conv_w[l], row(mlstm_conv_b[l]),
                    row(_per_head_lanes(mlstm_gate_b[l].reshape(2, HEADS))), row(mlstm_norm[l]),
                    batch=batch, seq=seq)
        yb = _pool(zp, _block_diag(pool_w[l]).astype(BF16), row(pool_scale[l]), batch=batch, seq=seq)
        yc = _dilated((zc, zc4, zc16), dil_bias, batch=batch, seq=seq)
        yd = _diff_attention(zd, vt, diff_bias, diff_limit2, diff_lambda[l],
                             jnp.broadcast_to(diff_subln[l][:, None], (HEAD_DIM, DIFF_Q)),
                             lam_init=lam_init, batch=batch, seq=seq)
        k_mem, v_mem = _mem_kv(mem2, row(mem_norm[l]), xattn_wkv[l].astype(BF16))
        h = _xattn(h, (ya, yb, yc, yd), w_out[l].astype(BF16),
                   row(xattn_norm[l]), xattn_wq[l].astype(BF16), k_mem, v_mem,
                   xattn_wo[l].astype(BF16), batch=batch, seq=seq)
        h = _ffn(h, row(ffn2_norm[l]), ffn2_w_gate[l].astype(BF16), ffn2_w_up[l].astype(BF16),
                 ffn2_w_down[l].astype(BF16), row(final_norm), final=(l == DEPTH - 1))
    return h.reshape(batch, seq, D_MODEL)
```

```python
import functools
import math

import jax
import jax.numpy as jnp
import numpy as np
from jax import lax
from jax.experimental import pallas as pl
from jax.experimental.pallas import tpu as pltpu

F32 = jnp.float32
BF16 = jnp.bfloat16

D_MODEL = 1024
D_FF = 2816
DEPTH = 4
GROUP = 256
HEADS = 4
HEAD_DIM = GROUP // HEADS
MEM_LEN = 256
MEM_HEADS = 4
MEM_HEAD_DIM = D_MODEL // MEM_HEADS
MLSTM_CHUNK = 64
CONV_WIDTH = 4
POOL_WINDOWS = (2, 4, 8, 16)
DIL_PATTERNS = ((128, 1), (512, 4), (2048, 16))
DIL_BACK = 128
DIFF_QK_HALF = HEAD_DIM // 2
T5_BUCKETS = 32
T5_MAX_DIST = 2048
RMS_EPS = 1e-6
SUBLN_EPS = 1e-5
NEG = -1e30
LOG2E = math.log2(math.e)
DIFF_SCORE_SCALE = (DIFF_QK_HALF ** -0.5) * LOG2E
DIL_SCORE_SCALE = (HEAD_DIM ** -0.5) * LOG2E
AUG_ROWS = HEAD_DIM + 16
DIFF_MIN_OFFSET = -3
TILE = 128

VMEM_LIMIT_BYTES = 56 * 1024 * 1024


def _rms(xf, g, eps=RMS_EPS):
    return xf * lax.rsqrt(jnp.mean(xf * xf, axis=-1, keepdims=True) + eps) * g


def _const_spec(shape):
    zeros = (0,) * len(shape)
    return pl.BlockSpec(shape, lambda *_: zeros, pipeline_mode=pl.Buffered(1))


def _params(*sem):
    return pltpu.CompilerParams(dimension_semantics=sem, vmem_limit_bytes=VMEM_LIMIT_BYTES)


def _group_mask(rows, cols, row_group, col_group):
    r = lax.broadcasted_iota(jnp.int32, (rows, cols), 0) // row_group
    c = lax.broadcasted_iota(jnp.int32, (rows, cols), 1) // col_group
    return r == c


def _tile_rows(x, reps, mask):
    return jnp.where(mask, jnp.concatenate([x] * reps, axis=0), jnp.zeros((), x.dtype))


def _dot(a, b):
    return jnp.dot(a, b, preferred_element_type=F32)


def _dot_nt(a, b):
    return lax.dot_general(a, b, (((1,), (1,)), ((), ())), preferred_element_type=F32)


def _dot_tn(a, b):
    return lax.dot_general(a, b, (((0,), (0,)), ((), ())), preferred_element_type=F32)


def _group_sum(x, ones_bd):
    hi = x.astype(BF16)
    lo = (x - hi.astype(F32)).astype(BF16)
    return _dot(hi, ones_bd) + _dot(lo, ones_bd)


FFN_ROWS = 1024
FFN_COLS = 256


def _ffn_body(x_ref, g_ref, wg_ref, wu_ref, wd_ref, fg_ref, o_ref, act_ref, *, final):
    x = x_ref[...]
    u = _rms(x, g_ref[...]).astype(BF16)
    for c in range(D_FF // FFN_COLS):
        sl = slice(c * FFN_COLS, (c + 1) * FFN_COLS)
        gate = _dot(u, wg_ref[:, sl])
        up = _dot(u, wu_ref[:, sl])
        act_ref[:, sl] = (gate * jax.nn.sigmoid(gate) * up).astype(BF16)
    y = x + 0.5 * _dot(act_ref[...], wd_ref[...])
    if final:
        y = _rms(y, fg_ref[...])
    o_ref[...] = y


def _ffn(h, g, wg, wu, wd, fg, *, final):
    n = h.shape[0]
    row = pl.BlockSpec((FFN_ROWS, D_MODEL), lambda i: (i, 0))
    return pl.pallas_call(
        functools.partial(_ffn_body, final=final),
        grid=(n // FFN_ROWS,),
        in_specs=[row, _const_spec((1, D_MODEL)), _const_spec((D_MODEL, D_FF)),
                  _const_spec((D_MODEL, D_FF)), _const_spec((D_FF, D_MODEL)),
                  _const_spec((1, D_MODEL))],
        out_specs=row,
        out_shape=jax.ShapeDtypeStruct((n, D_MODEL), F32),
        scratch_shapes=[pltpu.VMEM((FFN_ROWS, D_FF), BF16)],
        compiler_params=_params("parallel"),
        name="ffn_final" if final else "ffn",
    )(h, g, wg, wu, wd, fg)


PROJ_ROWS = 1024
PROJ_OUTS = (("a", 4 * GROUP, F32), ("g", 2 * GROUP, F32), ("p", GROUP, F32),
             ("c", 3 * GROUP, BF16), ("d", 2 * GROUP, BF16))
PROJ_WIDTH = sum(w for _, w, _ in PROJ_OUTS)


PROJ_DILATIONS = tuple(d for _, d in DIL_PATTERNS if d > 1)


def _in_proj_body(x_ref, g_ref, w_ref, wv_ref, perm_ref, *o_refs):
    u = _rms(x_ref[...], g_ref[...]).astype(BF16)
    off = 0
    for o_ref, (_, width, dtype) in zip(o_refs, PROJ_OUTS):
        for c in range(width // GROUP):
            z = _dot(u, w_ref[:, off + c * GROUP: off + (c + 1) * GROUP])
            o_ref[:, c * GROUP:(c + 1) * GROUP] = z.astype(dtype)
        off += width
    vt_ref = o_refs[len(PROJ_OUTS)]
    v = _dot(u, wv_ref[...])
    for t in range(PROJ_ROWS // TILE):
        vt_ref[t] = v[t * TILE:(t + 1) * TILE].T.astype(BF16)
    for pi, d in enumerate(PROJ_DILATIONS):
        per_class = PROJ_UNIT // d
        for un in range(PROJ_ROWS // PROJ_UNIT):
            zp = _dot(perm_ref[pi], o_refs[3][un * PROJ_UNIT:(un + 1) * PROJ_UNIT, :]).astype(BF16)
            for r in range(d):
                o_refs[len(PROJ_OUTS) + 1 + pi][un * per_class:(un + 1) * per_class,
                                                r * 3 * GROUP:(r + 1) * 3 * GROUP] = \
                    zp[r * per_class:(r + 1) * per_class]


PROJ_UNIT = 2 * TILE


def _class_permutations():
    mats = np.zeros((len(PROJ_DILATIONS), PROJ_UNIT, PROJ_UNIT), np.float32)
    for pi, d in enumerate(PROJ_DILATIONS):
        t = np.arange(PROJ_UNIT)
        mats[pi, (t % d) * (PROJ_UNIT // d) + t // d, t] = 1.0
    return jnp.asarray(mats, BF16)


def _in_proj(h, g, w, wv):
    n = h.shape[0]
    tiles = PROJ_ROWS // TILE
    return pl.pallas_call(
        _in_proj_body,
        grid=(n // PROJ_ROWS,),
        in_specs=[pl.BlockSpec((PROJ_ROWS, D_MODEL), lambda i: (i, 0)),
                  _const_spec((1, D_MODEL)), _const_spec((D_MODEL, PROJ_WIDTH)),
                  _const_spec((D_MODEL, GROUP)),
                  _const_spec((len(PROJ_DILATIONS), PROJ_UNIT, PROJ_UNIT))],
        out_specs=[pl.BlockSpec((PROJ_ROWS, w_), lambda i: (i, 0)) for _, w_, _ in PROJ_OUTS]
        + [pl.BlockSpec((tiles, GROUP, TILE), lambda i: (i, 0, 0))]
        + [pl.BlockSpec((PROJ_ROWS // d, d * 3 * GROUP), lambda i: (i, 0)) for d in PROJ_DILATIONS],
        out_shape=[jax.ShapeDtypeStruct((n, w_), dt) for _, w_, dt in PROJ_OUTS]
        + [jax.ShapeDtypeStruct((n // TILE, GROUP, TILE), BF16)]
        + [jax.ShapeDtypeStruct((n // d, d * 3 * GROUP), BF16) for d in PROJ_DILATIONS],
        compiler_params=_params("parallel"),
        name="in_proj",
    )(h, g, w, wv, _class_permutations())


ML_ROWS = 512
ML_HALO = 8


def _chunk_scan(x, rin, op, fill):
    s = 1
    while s < MLSTM_CHUNK:
        x = op(x, jnp.where(rin >= s, pltpu.roll(x, s, 0), fill))
        s *= 2
    return x


def _mlstm_body(za_ref, zg_ref, cw_ref, cb_ref, gb_ref, ng_ref, y_ref,
                buf_ref, tail_ref, hh_ref, c_ref, n_ref, m_ref):
    L = MLSTM_CHUNK

    @pl.when(pl.program_id(1) == 0)
    def _():
        tail_ref[...] = jnp.zeros_like(tail_ref)
        c_ref[...] = jnp.zeros_like(c_ref)
        n_ref[...] = jnp.zeros_like(n_ref)
        m_ref[...] = jnp.zeros_like(m_ref)

    buf_ref[0:ML_HALO, :] = tail_ref[...]
    buf_ref[ML_HALO:, :] = za_ref[:, 0:2 * GROUP]
    tail_ref[...] = za_ref[ML_ROWS - ML_HALO:, 0:2 * GROUP]
    conv = cb_ref[...]
    for j in range(CONV_WIDTH):
        conv = conv + buf_ref[pl.ds(ML_HALO - (CONV_WIDTH - 1) + j, ML_ROWS), :] * cw_ref[j:j + 1, :]
    qk = conv * jax.nn.sigmoid(conv)
    q = qk[:, :GROUP]
    k = qk[:, GROUP:] * (HEAD_DIM ** -0.5)
    v = za_ref[:, 2 * GROUP:3 * GROUP].astype(BF16)
    q_bf, k_bf = q.astype(BF16), k.astype(BF16)

    ii = zg_ref[:, :GROUP] + gb_ref[:, :GROUP]
    fx = zg_ref[:, GROUP:] + gb_ref[:, GROUP:]
    lf = jnp.minimum(fx, 0.0) - jnp.log(1.0 + jnp.exp(-jnp.abs(fx)))
    rin = lax.broadcasted_iota(jnp.int32, (ML_ROWS, GROUP), 0) % L
    b = _chunk_scan(lf, rin, jnp.add, 0.0)
    a = ii - b
    ca = _chunk_scan(a, rin, jnp.maximum, NEG)

    bd = _group_mask(GROUP, GROUP, HEAD_DIM, HEAD_DIM)
    ones_bd = bd.astype(BF16)
    row = lax.broadcasted_iota(jnp.int32, (L, GROUP), 0)
    key = lax.broadcasted_iota(jnp.int32, (L, GROUP), 1) % L
    causal = key <= row
    diag = key == row

    m_prev = m_ref[...]
    for c in range(ML_ROWS // L):
        rs = slice(c * L, (c + 1) * L)
        q_c, k_c, v_c = q[rs], k[rs], v[rs]
        q_b = q_bf[rs]
        a_c, b_c = a[rs], b[rs]
        g = jnp.maximum(m_prev, ca[rs])
        g_last = g[L - 1:L]
        a_row = jnp.sum(jnp.where(diag, a_c, 0.0), axis=0, keepdims=True)
        decay = jnp.exp(jnp.where(causal, a_row - g, NEG))
        sc = _dot_nt(q_b, _tile_rows(k_bf[rs], HEADS, bd)) * decay
        inter = jnp.exp(m_prev - g)
        num = inter * _dot(q_b, c_ref[...].astype(BF16)) + _dot(sc.astype(BF16), _tile_rows(v_c, HEADS, bd))
        den = inter * _group_sum(q_c * n_ref[...], ones_bd) + _group_sum(sc, ones_bd)
        hh_ref[rs, :] = num / jnp.maximum(jnp.abs(den), jnp.exp(-(b_c + g)))
        kw = k_c * jnp.exp(a_c - g_last)
        carry = jnp.exp(m_prev - g_last)
        c_ref[...] = carry * c_ref[...] + jnp.where(bd, _dot_tn(kw.astype(BF16), v_c), 0.0)
        n_ref[...] = carry * n_ref[...] + jnp.sum(kw, axis=0, keepdims=True)
        m_prev = b_c[L - 1:L] + g_last
    m_ref[...] = m_prev

    hh = hh_ref[...]
    mu = _group_sum(hh, ones_bd) * (1.0 / HEAD_DIM)
    dev = hh - mu
    var = _group_sum(dev * dev, ones_bd) * (1.0 / HEAD_DIM)
    o_gate = jax.nn.sigmoid(za_ref[:, 3 * GROUP:])
    y_ref[...] = (dev * lax.rsqrt(var + RMS_EPS) * ng_ref[...] * o_gate).astype(BF16)


def _mlstm(za, zg, conv_w, conv_b, gate_b, norm_g, *, batch, seq):
    za = za.reshape(batch, seq, 4 * GROUP)
    zg = zg.reshape(batch, seq, 2 * GROUP)
    y = pl.pallas_call(
        _mlstm_body,
        grid=(batch, seq // ML_ROWS),
        in_specs=[pl.BlockSpec((None, ML_ROWS, 4 * GROUP), lambda b, j: (b, j, 0)),
                  pl.BlockSpec((None, ML_ROWS, 2 * GROUP), lambda b, j: (b, j, 0)),
                  _const_spec((CONV_WIDTH, 2 * GROUP)), _const_spec((1, 2 * GROUP)),
                  _const_spec((1, 2 * GROUP)), _const_spec((1, GROUP))],
        out_specs=pl.BlockSpec((None, ML_ROWS, GROUP), lambda b, j: (b, j, 0)),
        out_shape=jax.ShapeDtypeStruct((batch, seq, GROUP), BF16),
        scratch_shapes=[pltpu.VMEM((ML_ROWS + ML_HALO, 2 * GROUP), F32),
                        pltpu.VMEM((ML_HALO, 2 * GROUP), F32),
                        pltpu.VMEM((ML_ROWS, GROUP), F32),
                        pltpu.VMEM((GROUP, GROUP), F32),
                        pltpu.VMEM((1, GROUP), F32),
                        pltpu.VMEM((1, GROUP), F32)],
        compiler_params=_params("parallel", "arbitrary"),
        name="mlstm",
    )(za, zg, conv_w, conv_b, gate_b, norm_g)
    return y.reshape(batch * seq, GROUP)


POOL_ROWS = 1024
POOL_HALO = 16


def _pool_body(u_ref, w_ref, s_ref, y_ref, buf_ref, tail_ref):
    j = pl.program_id(1)

    @pl.when(j == 0)
    def _():
        tail_ref[...] = jnp.zeros_like(tail_ref)

    buf_ref[0:POOL_HALO, :] = tail_ref[...]
    buf_ref[POOL_HALO:, :] = u_ref[...]
    tail_ref[...] = u_ref[POOL_ROWS - POOL_HALO:, :]
    sums, s = [], buf_ref[...]
    for shift in (1, 2, 4, 8):
        s = s + pltpu.roll(s, shift, 0)
        sums.append(s[POOL_HALO:])
    u = u_ref[...]
    lane_group = lax.broadcasted_iota(jnp.int32, (POOL_ROWS, GROUP), 1) // HEAD_DIM
    t = j * POOL_ROWS + lax.broadcasted_iota(jnp.int32, (POOL_ROWS, GROUP), 0)
    total, win = sums[3], jnp.full((POOL_ROWS, GROUP), POOL_WINDOWS[3], jnp.int32)
    for gi in (2, 1, 0):
        total = jnp.where(lane_group == gi, sums[gi], total)
        win = jnp.where(lane_group == gi, POOL_WINDOWS[gi], win)
    mean = total / jnp.minimum(t + 1, win).astype(F32)
    y = _dot((mean - u).astype(BF16), w_ref[...]) * s_ref[...]
    y_ref[...] = y.astype(BF16)


def _pool(zp, w_bd, scale, *, batch, seq):
    zp = zp.reshape(batch, seq, GROUP)
    y = pl.pallas_call(
        _pool_body,
        grid=(batch, seq // POOL_ROWS),
        in_specs=[pl.BlockSpec((None, POOL_ROWS, GROUP), lambda b, j: (b, j, 0)),
                  _const_spec((GROUP, GROUP)), _const_spec((1, GROUP))],
        out_specs=pl.BlockSpec((None, POOL_ROWS, GROUP), lambda b, j: (b, j, 0)),
        out_shape=jax.ShapeDtypeStruct((batch, seq, GROUP), BF16),
        scratch_shapes=[pltpu.VMEM((POOL_ROWS + POOL_HALO, GROUP), F32),
                        pltpu.VMEM((POOL_HALO, GROUP), F32)],
        compiler_params=_params("parallel", "arbitrary"),
        name="pool",
    )(zp, w_bd, scale)
    return y.reshape(batch * seq, GROUP)


TOEPLITZ_BATCH = 4


def _toeplitz_body(w_ref, o_ref):
    for t in range(TOEPLITZ_BATCH):
        x = jnp.broadcast_to(w_ref[t], (TILE, 2 * TILE))
        o_ref[t] = pltpu.roll(x, 0, 1, stride=1, stride_axis=0)[:, :TILE]


def _toeplitz(rows):
    n = rows.shape[0]
    assert n % TOEPLITZ_BATCH == 0
    return pl.pallas_call(
        _toeplitz_body,
        grid=(n // TOEPLITZ_BATCH,),
        in_specs=[pl.BlockSpec((TOEPLITZ_BATCH, 1, 2 * TILE), lambda i: (i, 0, 0))],
        out_specs=pl.BlockSpec((TOEPLITZ_BATCH, TILE, TILE), lambda i: (i, 0, 0)),
        out_shape=jax.ShapeDtypeStruct((n, TILE, TILE), F32),
        compiler_params=_params("parallel"),
        name="toeplitz",
    )(rows.reshape(n, 1, 2 * TILE))


_TOEPLITZ_X = np.where(np.arange(2 * TILE) <= TILE, -np.arange(2 * TILE), 2 * TILE - np.arange(2 * TILE))


def _t5_bucket(dist):
    max_exact = T5_BUCKETS // 2
    d = jnp.maximum(dist, 1).astype(F32)
    large = max_exact + (jnp.log(d / max_exact) / math.log(T5_MAX_DIST / max_exact)
                         * (T5_BUCKETS - max_exact)).astype(jnp.int32)
    large = jnp.minimum(large, T5_BUCKETS - 1)
    return jnp.where(dist < max_exact, dist, large)


def _bias_tiles(t5_bias, seq):
    x = jnp.asarray(_TOEPLITZ_X, jnp.int32)
    table = t5_bias * LOG2E
    delta = jnp.clip(jnp.stack([-x, DIL_BACK - x]), 0, DIL_BACK)
    valid = jnp.stack([x <= 0, x >= 0])[None, :, :, None]
    rows = table[_t5_bucket(jnp.stack([delta * d for _, d in DIL_PATTERNS])), :HEADS]
    rows = jnp.where(valid, rows, NEG).transpose(0, 3, 1, 2)
    dil = _toeplitz(rows.reshape(-1, 2 * TILE))
    dil = dil.reshape(len(DIL_PATTERNS), HEADS, 2, TILE, TILE).transpose(0, 1, 3, 2, 4)
    dil = dil.reshape(len(DIL_PATTERNS), HEADS * TILE, 2 * TILE)
    noff = seq // TILE - DIFF_MIN_OFFSET
    dist = (jnp.arange(noff)[:, None] + DIFF_MIN_OFFSET) * TILE - x[None, :]
    rows = table[_t5_bucket(jnp.clip(dist, 0, seq - 1)), HEADS:]
    rows = jnp.where((dist >= 0)[:, :, None], rows, NEG).transpose(0, 2, 1)
    diff = _toeplitz(rows.reshape(-1, 2 * TILE))
    room = jnp.maximum(DIFF_SAFE_LOG2 - jnp.max(jnp.abs(table[:, HEADS:])), 0.0)
    return dil, diff.reshape(noff, HEADS * TILE, TILE), (room * room).reshape(1, 1)


DIL_UNIT = 2 * TILE


def _dil_body(*refs, subs, dil, merge, has_prev):
    if merge:
        x_ref, xp_ref, bias_ref, unperm_ref, o1_ref, l1_ref, o2_ref, l2_ref, y_ref = refs
    else:
        x_ref, xp_ref, bias_ref, unperm_ref, o_ref, lse_ref = refs
    per_class = [_dil_class(x_ref, xp_ref, bias_ref, rc=rc, subs=subs, has_prev=has_prev)
                 for rc in range(dil)]
    piece = DIL_UNIT // dil
    class_o = [jnp.concatenate(pc[0], axis=0) for pc in per_class]
    class_l = [jnp.concatenate(pc[1], axis=0) for pc in per_class]
    for u in range(dil * subs * TILE // DIL_UNIT):
        take = lambda arrs: jnp.concatenate([a[u * piece:(u + 1) * piece] for a in arrs], axis=0)
        if dil == 1:
            o, lse = take(class_o), take(class_l)
        else:
            o_c = take(class_o).astype(BF16)
            l_c = take(class_l)
            l_hi = l_c.astype(BF16)
            l_lo = (l_c - l_hi.astype(F32)).astype(BF16)
            o = _dot(unperm_ref[...], o_c)
            lse = _dot(unperm_ref[...], l_hi) + _dot(unperm_ref[...], l_lo)
        rs = slice(u * DIL_UNIT, (u + 1) * DIL_UNIT)
        if merge:
            o1, l1 = o1_ref[rs, :].astype(F32), l1_ref[rs, :]
            o2, l2 = o2_ref[rs, :].astype(F32), l2_ref[rs, :]
            top = jnp.maximum(jnp.maximum(l1, l2), lse)
            w1, w2, w3 = jnp.exp2(l1 - top), jnp.exp2(l2 - top), jnp.exp2(lse - top)
            y_ref[rs, :] = ((w1 * o1 + w2 * o2 + w3 * o) / (w1 + w2 + w3)).astype(BF16)
        else:
            o_ref[rs, :] = o.astype(BF16)
            lse_ref[rs, :] = lse


def _dil_class(x_ref, xp_ref, bias_ref, *, rc, subs, has_prev):
    first = pl.program_id(1) == 0
    kmask = _group_mask(HEADS * TILE, GROUP, TILE, HEAD_DIM)
    ones_rows = (lax.broadcasted_iota(jnp.int32, (AUG_ROWS - HEAD_DIM, TILE), 0) == 0).astype(BF16)
    rows_of = lambda sb: slice(sb * TILE, (sb + 1) * TILE)
    col_q, col_k, col_v = (slice((3 * rc + w) * GROUP, (3 * rc + w + 1) * GROUP) for w in range(3))

    s_same, s_next, vaug = {}, {}, {}
    first_block = -1 if has_prev else 0
    for j in range(first_block, subs):
        k_j = xp_ref[:, col_k] if j < 0 else x_ref[rows_of(j), col_k]
        v_j = xp_ref[:, col_v] if j < 0 else x_ref[rows_of(j), col_v]
        parts = ([0] if j >= 0 else []) + ([1] if j + 1 < subs else [])
        q_cat = jnp.concatenate([x_ref[rows_of(j + e), col_q] for e in parts], axis=0)
        bias = bias_ref[:, parts[0] * TILE:(parts[-1] + 1) * TILE]
        st = _dot_nt(_tile_rows(k_j, HEADS, kmask), q_cat) + bias
        if j < 0:
            st = st + jnp.where(first, NEG, 0.0)
        for pos, e in enumerate(parts):
            (s_same if e == 0 else s_next)[j + e] = st[:, pos * TILE:(pos + 1) * TILE]
        v_t = v_j.astype(F32).T.astype(BF16)
        vaug[j] = [jnp.concatenate([v_t[h * HEAD_DIM:(h + 1) * HEAD_DIM], ones_rows], axis=0)
                   for h in range(HEADS)]

    p_same, p_next, tops = {}, {}, {}
    for i in range(subs):
        ps, pn, tp = [], [], []
        for h in range(HEADS):
            hs = slice(h * TILE, (h + 1) * TILE)
            a = s_same[i][hs]
            m = jnp.max(a, axis=0, keepdims=True)
            if i in s_next:
                b = s_next[i][hs]
                m = jnp.maximum(m, jnp.max(b, axis=0, keepdims=True))
                pn.append(jnp.exp2(b - m).astype(BF16))
            ps.append(jnp.exp2(a - m).astype(BF16))
            tp.append(m)
        p_same[i], p_next[i], tops[i] = ps, pn, tp

    acc = {i: [None] * HEADS for i in range(subs)}
    for j in range(first_block, subs):
        for h in range(HEADS):
            cols = ([p_same[j][h]] if j >= 0 else []) + ([p_next[j + 1][h]] if j + 1 < subs else [])
            r = _dot(vaug[j][h], jnp.concatenate(cols, axis=1))
            targets = ([j] if j >= 0 else []) + ([j + 1] if j + 1 < subs else [])
            for pos, i in enumerate(targets):
                part = r[:, pos * TILE:(pos + 1) * TILE]
                acc[i][h] = part if acc[i][h] is None else acc[i][h] + part

    outs, lses = [], []
    for i in range(subs):
        o_t, lse_t = [], []
        for h in range(HEADS):
            l = acc[i][h][HEAD_DIM:HEAD_DIM + 1]
            o_t.append(acc[i][h][:HEAD_DIM] / l)
            lse_t.append(jnp.broadcast_to(tops[i][h] + jnp.log2(l), (HEAD_DIM, TILE)))
        outs.append(jnp.concatenate(o_t, axis=0).T)
        lses.append(jnp.concatenate(lse_t, axis=0).T)
    return outs, lses


DIL_SUBBLOCKS = 16


def _unpermutation(dil):
    t = np.arange(DIL_UNIT)
    mat = np.zeros((DIL_UNIT, DIL_UNIT), np.float32)
    mat[t, (t % dil) * (DIL_UNIT // dil) + t // dil] = 1.0
    return jnp.asarray(mat, BF16)


def _dilated_pattern(zc, bias, dil, *, batch, seq, merge_with=None):
    length = seq // dil
    subs = min(max(DIL_SUBBLOCKS // max(dil // 2, 1), 2), length // TILE)
    rows = subs * TILE
    tokens = rows * dil
    zc = zc.reshape(batch, length, dil * 3 * GROUP)
    blk = pl.BlockSpec((None, rows, dil * 3 * GROUP), lambda b, n: (b, n, 0))
    prev = pl.BlockSpec((None, TILE, dil * 3 * GROUP), lambda b, n: (b, jnp.maximum(n * subs - 1, 0), 0))
    nat = pl.BlockSpec((None, tokens, GROUP), lambda b, n: (b, n, 0))
    in_specs = [blk, prev, _const_spec((HEADS * TILE, 2 * TILE)), _const_spec((DIL_UNIT, DIL_UNIT))]
    args = [zc, zc, bias, _unpermutation(dil)]
    if merge_with is None:
        out_specs = [nat, nat]
        out_shape = [jax.ShapeDtypeStruct((batch, seq, GROUP), BF16),
                     jax.ShapeDtypeStruct((batch, seq, GROUP), F32)]
    else:
        in_specs += [nat] * len(merge_with)
        args += list(merge_with)
        out_specs = nat
        out_shape = jax.ShapeDtypeStruct((batch, seq, GROUP), BF16)
    out = pl.pallas_call(
        functools.partial(_dil_body, subs=subs, dil=dil, merge=merge_with is not None,
                          has_prev=length > rows),
        grid=(batch, length // rows),
        in_specs=in_specs, out_specs=out_specs, out_shape=out_shape,
        compiler_params=_params("parallel", "parallel"),
        name=f"dilated_d{dil}",
    )(*args)
    if merge_with is None:
        return out
    return out.reshape(batch * seq, GROUP)


def _dilated(zc_views, dil_bias, *, batch, seq):
    o1, l1 = _dilated_pattern(zc_views[0], dil_bias[0], DIL_PATTERNS[0][1], batch=batch, seq=seq)
    o2, l2 = _dilated_pattern(zc_views[1], dil_bias[1], DIL_PATTERNS[1][1], batch=batch, seq=seq)
    return _dilated_pattern(zc_views[2], dil_bias[2], DIL_PATTERNS[2][1], batch=batch, seq=seq,
                            merge_with=(o1, l1, o2, l2))


DIFF_Q = 512
DIFF_K = 256
DIFF_GROUPS = 2 * HEADS


DIFF_SAFE_LOG2 = 60.0
DIFF_BOUND_SLACK = 1.01


def _diff_running_max(q, qi, last, pairs, bias_tiles, kexp_ref, vaug_ref, acc_ref, sta_ref, stb_ref):
    def scores(s_ref, j):
        bias = bias_tiles(j)
        raw = _dot_nt(kexp_ref[jnp.minimum(j, last)], q)
        tops = []
        for g in range(DIFF_GROUPS):
            s = raw[g * DIFF_K:(g + 1) * DIFF_K] + bias[g % HEADS]
            s_ref[g * DIFF_K:(g + 1) * DIFF_K, :] = s
            tops.append(jnp.max(s, axis=0, keepdims=True))
        return tuple(tops)

    def consume(s_ref, tops, j, carry):
        ms, ls = carry
        jv = jnp.minimum(j, last)
        new_ms, new_ls = [], []
        for g in range(DIFF_GROUPS):
            mp, h = divmod(g, HEADS)
            m_new = jnp.maximum(ms[g], tops[g])
            p = jnp.exp2(s_ref[g * DIFF_K:(g + 1) * DIFF_K, :] - m_new).astype(BF16)
            alpha = jnp.exp2(ms[g] - m_new)
            r = _dot(vaug_ref[jv, h], p)
            acc_ref[mp, h] = alpha * acc_ref[mp, h] + r[:HEAD_DIM]
            new_ls.append(alpha * ls[g] + r[HEAD_DIM:HEAD_DIM + 1])
            new_ms.append(m_new)
        return tuple(new_ms), tuple(new_ls)

    def pair(jj, carry):
        tops_a, state = carry
        j = 2 * jj
        tops_b = scores(stb_ref, j + 1)
        state = consume(sta_ref, tops_a, j, state)
        tops_a = scores(sta_ref, j + 2)
        return tops_a, consume(stb_ref, tops_b, j + 1, state)

    init = (tuple(jnp.full((1, DIFF_Q), NEG, F32) for _ in range(DIFF_GROUPS)),
            tuple(jnp.zeros((1, DIFF_Q), F32) for _ in range(DIFF_GROUPS)))
    _, (_, ls) = lax.fori_loop(0, pairs, pair, (scores(sta_ref, 0), init))
    return ls


def _diff_body(q_ref, k_ref, vt_ref, bias_ref, lam_ref, sg_ref, lim_ref, y_ref,
               kexp_ref, vaug_ref, acc_ref, sta_ref, stb_ref, pa_ref, pb_ref, knorm_ref,
               *, lam_init, key_steps):
    qi = pl.program_id(1)
    qk_group = _group_mask(GROUP, GROUP, DIFF_QK_HALF, DIFF_QK_HALF).astype(BF16)

    @pl.when(qi == 0)
    def _():
        grp = lax.broadcasted_iota(jnp.int32, (DIFF_GROUPS * DIFF_K, GROUP), 0) // DIFF_K
        slot = lax.broadcasted_iota(jnp.int32, (DIFF_GROUPS * DIFF_K, GROUP), 1) // DIFF_QK_HALF
        kmask = slot == 2 * (grp % HEADS) + grp // HEADS
        ones_rows = (lax.broadcasted_iota(jnp.int32, (AUG_ROWS - HEAD_DIM, DIFF_K), 0) == 0).astype(BF16)

        def build(j, kmax):
            k_t = k_ref[pl.ds(pl.multiple_of(j * DIFF_K, DIFF_K), DIFF_K), :]
            kexp_ref[j] = jnp.where(kmask, jnp.concatenate([k_t] * DIFF_GROUPS, axis=0),
                                    jnp.zeros((), BF16))
            vt = jnp.concatenate([vt_ref[2 * j], vt_ref[2 * j + 1]], axis=1)
            for h in range(HEADS):
                vaug_ref[j, h] = jnp.concatenate([vt[h * HEAD_DIM:(h + 1) * HEAD_DIM], ones_rows], axis=0)
            k_f = k_t.astype(F32)
            return jnp.maximum(kmax, jnp.max(_group_sum(k_f * k_f, qk_group), axis=0, keepdims=True))

        knorm_ref[...] = lax.fori_loop(0, key_steps, build, jnp.zeros((1, GROUP), F32))

    acc_ref[...] = jnp.zeros_like(acc_ref)
    q = q_ref[...]
    last = key_steps - 1
    key_steps_needed = (qi + 1) * (DIFF_Q // DIFF_K)
    pairs = (key_steps_needed + 1) // 2

    def bias_tiles(j):
        base = (DIFF_Q // TILE) * qi - (DIFF_K // TILE) * j - DIFF_MIN_OFFSET
        tiles = {d: bias_ref[jnp.maximum(base + d, 0)]
                 for d in range(1 - DIFF_K // TILE, DIFF_Q // TILE)}
        return [jnp.concatenate(
            [jnp.concatenate([tiles[a - b][h * TILE:(h + 1) * TILE] for a in range(DIFF_Q // TILE)], axis=1)
             for b in range(DIFF_K // TILE)], axis=0) for h in range(HEADS)]

    q_f = q.astype(F32)
    bound2 = jnp.max(_group_sum(q_f * q_f, qk_group) * knorm_ref[...])
    no_overflow = bound2 * DIFF_BOUND_SLACK <= lim_ref[0, 0]

    def unshifted():
        def weights(p_ref, j):
            bias = bias_tiles(j)
            raw = _dot_nt(kexp_ref[jnp.minimum(j, last)], q)
            for g in range(DIFF_GROUPS):
                rows = slice(g * DIFF_K, (g + 1) * DIFF_K)
                p_ref[rows, :] = jnp.exp2(raw[rows] + bias[g % HEADS]).astype(BF16)

        def accumulate(p_ref, j, ls):
            jv = jnp.minimum(j, last)
            new_ls = []
            for g in range(DIFF_GROUPS):
                mp, h = divmod(g, HEADS)
                r = _dot(vaug_ref[jv, h], p_ref[g * DIFF_K:(g + 1) * DIFF_K, :])
                acc_ref[mp, h] = acc_ref[mp, h] + r[:HEAD_DIM]
                new_ls.append(ls[g] + r[HEAD_DIM:HEAD_DIM + 1])
            return tuple(new_ls)

        def pair(jj, ls):
            j = 2 * jj
            weights(pb_ref, j + 1)
            ls = accumulate(pa_ref, j, ls)
            weights(pa_ref, j + 2)
            return accumulate(pb_ref, j + 1, ls)

        weights(pa_ref, 0)
        return lax.fori_loop(0, pairs, pair,
                             tuple(jnp.zeros((1, DIFF_Q), F32) for _ in range(DIFF_GROUPS)))

    def running_max():
        return _diff_running_max(q, qi, last, pairs, bias_tiles, kexp_ref, vaug_ref, acc_ref,
                                 sta_ref, stb_ref)

    ls = lax.cond(no_overflow, unshifted, running_max)

    lv = lam_ref[...]
    lam = (jnp.exp(jnp.sum(lv[0:1] * lv[1:2], axis=-1, keepdims=True))
           - jnp.exp(jnp.sum(lv[2:3] * lv[3:4], axis=-1, keepdims=True)) + lam_init)
    outs = []
    for h in range(HEADS):
        o = acc_ref[0, h] / ls[h] - lam * (acc_ref[1, h] / ls[HEADS + h])
        ms_o = jnp.mean(o * o, axis=0, keepdims=True)
        outs.append(o * lax.rsqrt(ms_o + SUBLN_EPS) * sg_ref[...] * (1.0 - lam_init))
    y_ref[...] = jnp.concatenate(outs, axis=0).T.astype(BF16)


def _diff_attention(zd, vt, bias, score_limit2, lam_vecs, subln_cols, *, lam_init, batch, seq):
    zd = zd.reshape(batch, seq, 2 * GROUP)
    key_tiles = seq // TILE
    key_steps = seq // DIFF_K
    vt = vt.reshape(batch, key_tiles, GROUP, TILE)
    y = pl.pallas_call(
        functools.partial(_diff_body, lam_init=lam_init, key_steps=key_steps),
        grid=(batch, seq // DIFF_Q),
        in_specs=[pl.BlockSpec((None, DIFF_Q, GROUP), lambda b, i: (b, i, 0)),
                  pl.BlockSpec((None, seq, GROUP), lambda b, i: (b, 0, 1), pipeline_mode=pl.Buffered(1)),
                  pl.BlockSpec((None, key_tiles, GROUP, TILE), lambda b, i: (b, 0, 0, 0),
                               pipeline_mode=pl.Buffered(1)),
                  _const_spec((key_tiles - DIFF_MIN_OFFSET, HEADS * TILE, TILE)),
                  _const_spec((4, DIFF_QK_HALF)), _const_spec((HEAD_DIM, DIFF_Q)),
                  pl.BlockSpec(memory_space=pltpu.SMEM)],
        out_specs=pl.BlockSpec((None, DIFF_Q, GROUP), lambda b, i: (b, i, 0)),
        out_shape=jax.ShapeDtypeStruct((batch, seq, GROUP), BF16),
        scratch_shapes=[pltpu.VMEM((key_steps, DIFF_GROUPS * DIFF_K, GROUP), BF16),
                        pltpu.VMEM((key_steps, HEADS, AUG_ROWS, DIFF_K), BF16),
                        pltpu.VMEM((2, HEADS, HEAD_DIM, DIFF_Q), F32),
                        pltpu.VMEM((DIFF_GROUPS * DIFF_K, DIFF_Q), F32),
                        pltpu.VMEM((DIFF_GROUPS * DIFF_K, DIFF_Q), F32),
                        pltpu.VMEM((DIFF_GROUPS * DIFF_K, DIFF_Q), BF16),
                        pltpu.VMEM((DIFF_GROUPS * DIFF_K, DIFF_Q), BF16),
                        pltpu.VMEM((1, GROUP), F32)],
        compiler_params=_params("parallel", "arbitrary"),
        name="diff_attn",
    )(zd, zd, vt, bias, lam_vecs, subln_cols, score_limit2)
    return y.reshape(batch * seq, GROUP)


KV_ROWS = 512


def _mem_kv_body(m_ref, g_ref, w_ref, k_ref, v_ref):
    u = _rms(m_ref[...], g_ref[...]).astype(BF16)
    for c in range(D_MODEL // GROUP):
        sl = slice(c * GROUP, (c + 1) * GROUP)
        k_ref[:, sl] = _dot(u, w_ref[:, sl]).astype(BF16)
        v_ref[:, sl] = _dot(u, w_ref[:, D_MODEL + c * GROUP: D_MODEL + (c + 1) * GROUP]).astype(BF16)


def _mem_kv(mem, g, w):
    n = mem.shape[0]
    row = pl.BlockSpec((KV_ROWS, D_MODEL), lambda i: (i, 0))
    return pl.pallas_call(
        _mem_kv_body,
        grid=(n // KV_ROWS,),
        in_specs=[row, _const_spec((1, D_MODEL)), _const_spec((D_MODEL, 2 * D_MODEL))],
        out_specs=[row, row],
        out_shape=[jax.ShapeDtypeStruct((n, D_MODEL), BF16)] * 2,
        compiler_params=_params("parallel"),
        name="mem_kv",
    )(mem, g, w)


XATTN_ROWS = 1024


def _xattn_body(x_ref, ya_ref, yb_ref, yc_ref, yd_ref, wout_ref, g_ref, wq_ref, k_ref, v_ref, wo_ref,
                o_ref, q_scr, a_scr):
    x = x_ref[...]
    for gi, y_ref in enumerate((ya_ref, yb_ref, yc_ref, yd_ref)):
        x = x + _dot(y_ref[...], wout_ref[gi * GROUP:(gi + 1) * GROUP, :])
    u = _rms(x, g_ref[...]).astype(BF16)
    for c in range(D_MODEL // GROUP):
        sl = slice(c * GROUP, (c + 1) * GROUP)
        q_scr[:, sl] = _dot(u, wq_ref[:, sl]).astype(BF16)
    for h in range(MEM_HEADS):
        sl = slice(h * MEM_HEAD_DIM, (h + 1) * MEM_HEAD_DIM)
        s = _dot_nt(q_scr[:, sl], k_ref[:, sl]) * (MEM_HEAD_DIM ** -0.5)
        e = jnp.exp(s - jnp.max(s, axis=-1, keepdims=True))
        l = jnp.sum(e, axis=-1, keepdims=True)
        a_scr[:, sl] = (_dot(e.astype(BF16), v_ref[:, sl]) / l).astype(BF16)
    o_ref[...] = x + _dot(a_scr[...], wo_ref[...])


def _xattn(h, ys, w_out, g, wq, k, v, wo, *, batch, seq):
    h3 = h.reshape(batch, seq, D_MODEL)
    ys = [y.reshape(batch, seq, GROUP) for y in ys]
    k3 = k.reshape(batch, MEM_LEN, D_MODEL)
    v3 = v.reshape(batch, MEM_LEN, D_MODEL)
    row = pl.BlockSpec((None, XATTN_ROWS, D_MODEL), lambda b, i: (b, i, 0))
    grp = pl.BlockSpec((None, XATTN_ROWS, GROUP), lambda b, i: (b, i, 0))
    mem = pl.BlockSpec((None, MEM_LEN, D_MODEL), lambda b, i: (b, 0, 0))
    weight = _const_spec((D_MODEL, D_MODEL))
    out = pl.pallas_call(
        _xattn_body,
        grid=(batch, seq // XATTN_ROWS),
        in_specs=[row, grp, grp, grp, grp, weight, _const_spec((1, D_MODEL)), weight, mem, mem, weight],
        out_specs=row,
        out_shape=jax.ShapeDtypeStruct((batch, seq, D_MODEL), F32),
        scratch_shapes=[pltpu.VMEM((XATTN_ROWS, D_MODEL), BF16), pltpu.VMEM((XATTN_ROWS, D_MODEL), BF16)],
        compiler_params=_params("parallel", "parallel"),
        name="xattn",
    )(h3, *ys, w_out, g, wq, k3, v3, wo)
    return out.reshape(batch * seq, D_MODEL)


def _per_head_lanes(x):
    return jnp.repeat(x, HEAD_DIM, axis=-1)


def _in_proj_weight(w_in):
    g = GROUP
    q_a, k_a, v_a, o_a = (w_in[:, i * g:(i + 1) * g] for i in range(4))
    ig = w_in[:, 4 * g:4 * g + HEADS]
    fg = w_in[:, 4 * g + HEADS:4 * g + 2 * HEADS]
    rest = w_in[:, 4 * g + 2 * HEADS:]
    pool, q_c, k_c, v_c, q_d, k_d, v_d = (rest[:, i * g:(i + 1) * g] for i in range(7))
    q_c = q_c * DIL_SCORE_SCALE
    k_d = k_d * DIFF_SCORE_SCALE
    cols = [q_a, k_a, v_a, o_a, _per_head_lanes(ig), _per_head_lanes(fg), pool,
            q_c, k_c, v_c, q_d, k_d]
    return jnp.concatenate(cols, axis=1).astype(BF16), v_d.astype(BF16)


def _block_diag(w):
    g, c, _ = w.shape
    eye = jnp.eye(g, dtype=w.dtype)
    return (eye[:, None, :, None] * w[:, :, None, :]).reshape(g * c, g * c)


def kernel(x, mem, t5_bias, ffn1_norm, ffn1_w_gate, ffn1_w_up, ffn1_w_down, mix_norm, w_in,
           mlstm_conv_w, mlstm_conv_b, mlstm_gate_b, mlstm_norm, pool_w, pool_scale,
           diff_lambda, diff_subln, w_out, xattn_norm, mem_norm, xattn_wq, xattn_wkv, xattn_wo,
           ffn2_norm, ffn2_w_gate, ffn2_w_up, ffn2_w_down, final_norm):
    batch, seq, _ = x.shape
    n = batch * seq
    dil_bias, diff_bias, diff_limit2 = _bias_tiles(t5_bias, seq)
    h = x.reshape(n, D_MODEL)
    mem2 = mem.reshape(batch * MEM_LEN, D_MODEL)
    row = lambda v: v.reshape(1, -1)
    for l in range(DEPTH):
        lam_init = 0.8 - 0.6 * math.exp(-0.3 * l)
        h = _ffn(h, row(ffn1_norm[l]), ffn1_w_gate[l].astype(BF16), ffn1_w_up[l].astype(BF16),
                 ffn1_w_down[l].astype(BF16), row(final_norm), final=False)
        za, zg, zp, zc, zd, vt, zc4, zc16 = _in_proj(h, row(mix_norm[l]), *_in_proj_weight(w_in[l]))
        ya = _mlstm(za, zg, mlstm_conv_w[l], row(mlstm_conv_b[l]),
                    row(_per_head_lanes(mlstm_gate_b[l].reshape(2, HEADS))), row(mlstm_norm[l]),
                    batch=batch, seq=seq)
        yb = _pool(zp, _block_diag(pool_w[l]).astype(BF16), row(pool_scale[l]), batch=batch, seq=seq)
        yc = _dilated((zc, zc4, zc16), dil_bias, batch=batch, seq=seq)
        yd = _diff_attention(zd, vt, diff_bias, diff_limit2, diff_lambda[l],
                             jnp.broadcast_to(diff_subln[l][:, None], (HEAD_DIM, DIFF_Q)),
                             lam_init=lam_init, batch=batch, seq=seq)
        k_mem, v_mem = _mem_kv(mem2, row(mem_norm[l]), xattn_wkv[l].astype(BF16))
        h = _xattn(h, (ya, yb, yc, yd), w_out[l].astype(BF16),
                   row(xattn_norm[l]), xattn_wq[l].astype(BF16), k_mem, v_mem,
                   xattn_wo[l].astype(BF16), batch=batch, seq=seq)
        h = _ffn(h, row(ffn2_norm[l]), ffn2_w_gate[l].astype(BF16), ffn2_w_up[l].astype(BF16),
                 ffn2_w_down[l].astype(BF16), row(final_norm), final=(l == DEPTH - 1))
    return h.reshape(batch, seq, D_MODEL)
```

```python
import functools
import math

import jax
import jax.numpy as jnp
import numpy as np
from jax import lax
from jax.experimental import pallas as pl
from jax.experimental.pallas import tpu as pltpu

F32 = jnp.float32
BF16 = jnp.bfloat16

D_MODEL = 1024
D_FF = 2816
DEPTH = 4
GROUP = 256
HEADS = 4
HEAD_DIM = GROUP // HEADS
MEM_LEN = 256
MEM_HEADS = 4
MEM_HEAD_DIM = D_MODEL // MEM_HEADS
MLSTM_CHUNK = 64
CONV_WIDTH = 4
POOL_WINDOWS = (2, 4, 8, 16)
DIL_PATTERNS = ((128, 1), (512, 4), (2048, 16))
DIL_BACK = 128
DIFF_QK_HALF = HEAD_DIM // 2
T5_BUCKETS = 32
T5_MAX_DIST = 2048
RMS_EPS = 1e-6
SUBLN_EPS = 1e-5
NEG = -1e30
LOG2E = math.log2(math.e)
DIFF_SCORE_SCALE = (DIFF_QK_HALF ** -0.5) * LOG2E
DIL_SCORE_SCALE = (HEAD_DIM ** -0.5) * LOG2E
AUG_ROWS = HEAD_DIM + 16
DIFF_MIN_OFFSET = -3
TILE = 128

VMEM_LIMIT_BYTES = 56 * 1024 * 1024


def _rms(xf, g, eps=RMS_EPS):
    return xf * lax.rsqrt(jnp.mean(xf * xf, axis=-1, keepdims=True) + eps) * g


def _const_spec(shape):
    zeros = (0,) * len(shape)
    return pl.BlockSpec(shape, lambda *_: zeros, pipeline_mode=pl.Buffered(1))


def _params(*sem):
    return pltpu.CompilerParams(dimension_semantics=sem, vmem_limit_bytes=VMEM_LIMIT_BYTES)


def _group_mask(rows, cols, row_group, col_group):
    r = lax.broadcasted_iota(jnp.int32, (rows, cols), 0) // row_group
    c = lax.broadcasted_iota(jnp.int32, (rows, cols), 1) // col_group
    return r == c


def _tile_rows(x, reps, mask):
    return jnp.where(mask, jnp.concatenate([x] * reps, axis=0), jnp.zeros((), x.dtype))


def _dot(a, b):
    return jnp.dot(a, b, preferred_element_type=F32)


def _dot_nt(a, b):
    return lax.dot_general(a, b, (((1,), (1,)), ((), ())), preferred_element_type=F32)


def _dot_tn(a, b):
    return lax.dot_general(a, b, (((0,), (0,)), ((), ())), preferred_element_type=F32)


def _group_sum(x, ones_bd):
    hi = x.astype(BF16)
    lo = (x - hi.astype(F32)).astype(BF16)
    return _dot(hi, ones_bd) + _dot(lo, ones_bd)


FFN_ROWS = 1024
FFN_COLS = 256


def _ffn_body(x_ref, g_ref, wg_ref, wu_ref, wd_ref, fg_ref, o_ref, act_ref, *, final):
    x = x_ref[...]
    u = _rms(x, g_ref[...]).astype(BF16)
    for c in range(D_FF // FFN_COLS):
        sl = slice(c * FFN_COLS, (c + 1) * FFN_COLS)
        gate = _dot(u, wg_ref[:, sl])
        up = _dot(u, wu_ref[:, sl])
        act_ref[:, sl] = (gate * jax.nn.sigmoid(gate) * up).astype(BF16)
    y = x + 0.5 * _dot(act_ref[...], wd_ref[...])
    if final:
        y = _rms(y, fg_ref[...])
    o_ref[...] = y


def _ffn(h, g, wg, wu, wd, fg, *, final):
    n = h.shape[0]
    row = pl.BlockSpec((FFN_ROWS, D_MODEL), lambda i: (i, 0))
    return pl.pallas_call(
        functools.partial(_ffn_body, final=final),
        grid=(n // FFN_ROWS,),
        in_specs=[row, _const_spec((1, D_MODEL)), _const_spec((D_MODEL, D_FF)),
                  _const_spec((D_MODEL, D_FF)), _const_spec((D_FF, D_MODEL)),
                  _const_spec((1, D_MODEL))],
        out_specs=row,
        out_shape=jax.ShapeDtypeStruct((n, D_MODEL), F32),
        scratch_shapes=[pltpu.VMEM((FFN_ROWS, D_FF), BF16)],
        compiler_params=_params("parallel"),
        name="ffn_final" if final else "ffn",
    )(h, g, wg, wu, wd, fg)


PROJ_ROWS = 1024
PROJ_OUTS = (("a", 4 * GROUP, F32), ("g", 2 * GROUP, F32), ("p", GROUP, F32),
             ("c", 3 * GROUP, BF16), ("d", 2 * GROUP, BF16))
PROJ_WIDTH = sum(w for _, w, _ in PROJ_OUTS)


PROJ_DILATIONS = tuple(d for _, d in DIL_PATTERNS if d > 1)


def _in_proj_body(x_ref, g_ref, w_ref, wv_ref, perm_ref, *o_refs):
    u = _rms(x_ref[...], g_ref[...]).astype(BF16)
    off = 0
    for o_ref, (_, width, dtype) in zip(o_refs, PROJ_OUTS):
        for c in range(width // GROUP):
            z = _dot(u, w_ref[:, off + c * GROUP: off + (c + 1) * GROUP])
            o_ref[:, c * GROUP:(c + 1) * GROUP] = z.astype(dtype)
        off += width
    vt_ref = o_refs[len(PROJ_OUTS)]
    v = _dot(u, wv_ref[...])
    for t in range(PROJ_ROWS // TILE):
        vt_ref[t] = v[t * TILE:(t + 1) * TILE].T.astype(BF16)
    for pi, d in enumerate(PROJ_DILATIONS):
        per_class = PROJ_UNIT // d
        for un in range(PROJ_ROWS // PROJ_UNIT):
            zp = _dot(perm_ref[pi], o_refs[3][un * PROJ_UNIT:(un + 1) * PROJ_UNIT, :]).astype(BF16)
            for r in range(d):
                o_refs[len(PROJ_OUTS) + 1 + pi][un * per_class:(un + 1) * per_class,
                                                r * 3 * GROUP:(r + 1) * 3 * GROUP] = \
                    zp[r * per_class:(r + 1) * per_class]


PROJ_UNIT = 2 * TILE


def _class_permutations():
    mats = np.zeros((len(PROJ_DILATIONS), PROJ_UNIT, PROJ_UNIT), np.float32)
    for pi, d in enumerate(PROJ_DILATIONS):
        t = np.arange(PROJ_UNIT)
        mats[pi, (t % d) * (PROJ_UNIT // d) + t // d, t] = 1.0
    return jnp.asarray(mats, BF16)


def _in_proj(h, g, w, wv):
    n = h.shape[0]
    tiles = PROJ_ROWS // TILE
    return pl.pallas_call(
        _in_proj_body,
        grid=(n // PROJ_ROWS,),
        in_specs=[pl.BlockSpec((PROJ_ROWS, D_MODEL), lambda i: (i, 0)),
                  _const_spec((1, D_MODEL)), _const_spec((D_MODEL, PROJ_WIDTH)),
                  _const_spec((D_MODEL, GROUP)),
                  _const_spec((len(PROJ_DILATIONS), PROJ_UNIT, PROJ_UNIT))],
        out_specs=[pl.BlockSpec((PROJ_ROWS, w_), lambda i: (i, 0)) for _, w_, _ in PROJ_OUTS]
        + [pl.BlockSpec((tiles, GROUP, TILE), lambda i: (i, 0, 0))]
        + [pl.BlockSpec((PROJ_ROWS // d, d * 3 * GROUP), lambda i: (i, 0)) for d in PROJ_DILATIONS],
        out_shape=[jax.ShapeDtypeStruct((n, w_), dt) for _, w_, dt in PROJ_OUTS]
        + [jax.ShapeDtypeStruct((n // TILE, GROUP, TILE), BF16)]
        + [jax.ShapeDtypeStruct((n // d, d * 3 * GROUP), BF16) for d in PROJ_DILATIONS],
        compiler_params=_params("parallel"),
        name="in_proj",
    )(h, g, w, wv, _class_permutations())


ML_ROWS = 512
ML_HALO = 8


def _chunk_scan(x, rin, op, fill):
    s = 1
    while s < MLSTM_CHUNK:
        x = op(x, jnp.where(rin >= s, pltpu.roll(x, s, 0), fill))
        s *= 2
    return x


def _mlstm_body(za_ref, zg_ref, cw_ref, cb_ref, gb_ref, ng_ref, y_ref,
                buf_ref, tail_ref, hh_ref, c_ref, n_ref, m_ref):
    L = MLSTM_CHUNK

    @pl.when(pl.program_id(1) == 0)
    def _():
        tail_ref[...] = jnp.zeros_like(tail_ref)
        c_ref[...] = jnp.zeros_like(c_ref)
        n_ref[...] = jnp.zeros_like(n_ref)
        m_ref[...] = jnp.zeros_like(m_ref)

    buf_ref[0:ML_HALO, :] = tail_ref[...]
    buf_ref[ML_HALO:, :] = za_ref[:, 0:2 * GROUP]
    tail_ref[...] = za_ref[ML_ROWS - ML_HALO:, 0:2 * GROUP]
    conv = cb_ref[...]
    for j in range(CONV_WIDTH):
        conv = conv + buf_ref[pl.ds(ML_HALO - (CONV_WIDTH - 1) + j, ML_ROWS), :] * cw_ref[j:j + 1, :]
    qk = conv * jax.nn.sigmoid(conv)
    q = qk[:, :GROUP]
    k = qk[:, GROUP:] * (HEAD_DIM ** -0.5)
    v = za_ref[:, 2 * GROUP:3 * GROUP].astype(BF16)
    q_bf, k_bf = q.astype(BF16), k.astype(BF16)

    ii = zg_ref[:, :GROUP] + gb_ref[:, :GROUP]
    fx = zg_ref[:, GROUP:] + gb_ref[:, GROUP:]
    lf = jnp.minimum(fx, 0.0) - jnp.log(1.0 + jnp.exp(-jnp.abs(fx)))
    rin = lax.broadcasted_iota(jnp.int32, (ML_ROWS, GROUP), 0) % L
    b = _chunk_scan(lf, rin, jnp.add, 0.0)
    a = ii - b
    ca = _chunk_scan(a, rin, jnp.maximum, NEG)

    bd = _group_mask(GROUP, GROUP, HEAD_DIM, HEAD_DIM)
    ones_bd = bd.astype(BF16)
    row = lax.broadcasted_iota(jnp.int32, (L, GROUP), 0)
    key = lax.broadcasted_iota(jnp.int32, (L, GROUP), 1) % L
    causal = key <= row
    diag = key == row

    m_prev = m_ref[...]
    for c in range(ML_ROWS // L):
        rs = slice(c * L, (c + 1) * L)
        q_c, k_c, v_c = q[rs], k[rs], v[rs]
        q_b = q_bf[rs]
        a_c, b_c = a[rs], b[rs]
        g = jnp.maximum(m_prev, ca[rs])
        g_last = g[L - 1:L]
        a_row = jnp.sum(jnp.where(diag, a_c, 0.0), axis=0, keepdims=True)
        decay = jnp.exp(jnp.where(causal, a_row - g, NEG))
        sc = _dot_nt(q_b, _tile_rows(k_bf[rs], HEADS, bd)) * decay
        inter = jnp.exp(m_prev - g)
        num = inter * _dot(q_b, c_ref[...].astype(BF16)) + _dot(sc.astype(BF16), _tile_rows(v_c, HEADS, bd))
        den = inter * _group_sum(q_c * n_ref[...], ones_bd) + _group_sum(sc, ones_bd)
        hh_ref[rs, :] = num / jnp.maximum(jnp.abs(den), jnp.exp(-(b_c + g)))
        kw = k_c * jnp.exp(a_c - g_last)
        carry = jnp.exp(m_prev - g_last)
        c_ref[...] = carry * c_ref[...] + jnp.where(bd, _dot_tn(kw.astype(BF16), v_c), 0.0)
        n_ref[...] = carry * n_ref[...] + jnp.sum(kw, axis=0, keepdims=True)
        m_prev = b_c[L - 1:L] + g_last
    m_ref[...] = m_prev

    hh = hh_ref[...]
    mu = _group_sum(hh, ones_bd) * (1.0 / HEAD_DIM)
    dev = hh - mu
    var = _group_sum(dev * dev, ones_bd) * (1.0 / HEAD_DIM)
    o_gate = jax.nn.sigmoid(za_ref[:, 3 * GROUP:])
    y_ref[...] = (dev * lax.rsqrt(var + RMS_EPS) * ng_ref[...] * o_gate).astype(BF16)


def _mlstm(za, zg, conv_w, conv_b, gate_b, norm_g, *, batch, seq):
    za = za.reshape(batch, seq, 4 * GROUP)
    zg = zg.reshape(batch, seq, 2 * GROUP)
    y = pl.pallas_call(
        _mlstm_body,
        grid=(batch, seq // ML_ROWS),
        in_specs=[pl.BlockSpec((None, ML_ROWS, 4 * GROUP), lambda b, j: (b, j, 0)),
                  pl.BlockSpec((None, ML_ROWS, 2 * GROUP), lambda b, j: (b, j, 0)),
                  _const_spec((CONV_WIDTH, 2 * GROUP)), _const_spec((1, 2 * GROUP)),
                  _const_spec((1, 2 * GROUP)), _const_spec((1, GROUP))],
        out_specs=pl.BlockSpec((None, ML_ROWS, GROUP), lambda b, j: (b, j, 0)),
        out_shape=jax.ShapeDtypeStruct((batch, seq, GROUP), BF16),
        scratch_shapes=[pltpu.VMEM((ML_ROWS + ML_HALO, 2 * GROUP), F32),
                        pltpu.VMEM((ML_HALO, 2 * GROUP), F32),
                        pltpu.VMEM((ML_ROWS, GROUP), F32),
                        pltpu.VMEM((GROUP, GROUP), F32),
                        pltpu.VMEM((1, GROUP), F32),
                        pltpu.VMEM((1, GROUP), F32)],
        compiler_params=_params("parallel", "arbitrary"),
        name="mlstm",
    )(za, zg, conv_w, conv_b, gate_b, norm_g)
    return y.reshape(batch * seq, GROUP)


POOL_ROWS = 2048
POOL_HALO = 16


def _pool_body(u_ref, w_ref, s_ref, y_ref, buf_ref, tail_ref):
    j = pl.program_id(1)

    @pl.when(j == 0)
    def _():
        tail_ref[...] = jnp.zeros_like(tail_ref)

    buf_ref[0:POOL_HALO, :] = tail_ref[...]
    buf_ref[POOL_HALO:, :] = u_ref[...]
    tail_ref[...] = u_ref[POOL_ROWS - POOL_HALO:, :]
    sums, s = [], buf_ref[...]
    for shift in (1, 2, 4, 8):
        s = s + pltpu.roll(s, shift, 0)
        sums.append(s[POOL_HALO:])
    u = u_ref[...]
    lane_group = lax.broadcasted_iota(jnp.int32, (POOL_ROWS, GROUP), 1) // HEAD_DIM
    t = j * POOL_ROWS + lax.broadcasted_iota(jnp.int32, (POOL_ROWS, GROUP), 0)
    total, win = sums[3], jnp.full((POOL_ROWS, GROUP), POOL_WINDOWS[3], jnp.int32)
    for gi in (2, 1, 0):
        total = jnp.where(lane_group == gi, sums[gi], total)
        win = jnp.where(lane_group == gi, POOL_WINDOWS[gi], win)
    mean = total / jnp.minimum(t + 1, win).astype(F32)
    y = _dot((mean - u).astype(BF16), w_ref[...]) * s_ref[...]
    y_ref[...] = y.astype(BF16)


def _pool(zp, w_bd, scale, *, batch, seq):
    zp = zp.reshape(batch, seq, GROUP)
    y = pl.pallas_call(
        _pool_body,
        grid=(batch, seq // POOL_ROWS),
        in_specs=[pl.BlockSpec((None, POOL_ROWS, GROUP), lambda b, j: (b, j, 0)),
                  _const_spec((GROUP, GROUP)), _const_spec((1, GROUP))],
        out_specs=pl.BlockSpec((None, POOL_ROWS, GROUP), lambda b, j: (b, j, 0)),
        out_shape=jax.ShapeDtypeStruct((batch, seq, GROUP), BF16),
        scratch_shapes=[pltpu.VMEM((POOL_ROWS + POOL_HALO, GROUP), F32),
                        pltpu.VMEM((POOL_HALO, GROUP), F32)],
        compiler_params=_params("parallel", "arbitrary"),
        name="pool",
    )(zp, w_bd, scale)
    return y.reshape(batch * seq, GROUP)


TOEPLITZ_BATCH = 4


def _toeplitz_body(w_ref, o_ref):
    for t in range(TOEPLITZ_BATCH):
        x = jnp.broadcast_to(w_ref[t], (TILE, 2 * TILE))
        o_ref[t] = pltpu.roll(x, 0, 1, stride=1, stride_axis=0)[:, :TILE]


def _toeplitz(rows):
    n = rows.shape[0]
    assert n % TOEPLITZ_BATCH == 0
    return pl.pallas_call(
        _toeplitz_body,
        grid=(n // TOEPLITZ_BATCH,),
        in_specs=[pl.BlockSpec((TOEPLITZ_BATCH, 1, 2 * TILE), lambda i: (i, 0, 0))],
        out_specs=pl.BlockSpec((TOEPLITZ_BATCH, TILE, TILE), lambda i: (i, 0, 0)),
        out_shape=jax.ShapeDtypeStruct((n, TILE, TILE), F32),
        compiler_params=_params("parallel"),
        name="toeplitz",
    )(rows.reshape(n, 1, 2 * TILE))


_TOEPLITZ_X = np.where(np.arange(2 * TILE) <= TILE, -np.arange(2 * TILE), 2 * TILE - np.arange(2 * TILE))


def _t5_bucket(dist):
    max_exact = T5_BUCKETS // 2
    d = jnp.maximum(dist, 1).astype(F32)
    large = max_exact + (jnp.log(d / max_exact) / math.log(T5_MAX_DIST / max_exact)
                         * (T5_BUCKETS - max_exact)).astype(jnp.int32)
    large = jnp.minimum(large, T5_BUCKETS - 1)
    return jnp.where(dist < max_exact, dist, large)


def _bucket_rows(table, dist):
    onehot = _t5_bucket(dist)[..., None, None] == jnp.arange(T5_BUCKETS)[:, None]
    return jnp.sum(jnp.where(onehot, table, 0.0), axis=-2)


def _bias_tiles(t5_bias, seq):
    x = jnp.asarray(_TOEPLITZ_X, jnp.int32)
    table = t5_bias * LOG2E
    delta = jnp.clip(jnp.stack([-x, DIL_BACK - x]), 0, DIL_BACK)
    valid = jnp.stack([x <= 0, x >= 0])[None, :, :, None]
    rows = _bucket_rows(table[:, :HEADS], jnp.stack([delta * d for _, d in DIL_PATTERNS]))
    rows = jnp.where(valid, rows, NEG).transpose(0, 3, 1, 2)
    dil = _toeplitz(rows.reshape(-1, 2 * TILE))
    dil = dil.reshape(len(DIL_PATTERNS), HEADS, 2, TILE, TILE).transpose(0, 1, 3, 2, 4)
    dil = dil.reshape(len(DIL_PATTERNS), HEADS * TILE, 2 * TILE)
    noff = seq // TILE - DIFF_MIN_OFFSET
    dist = (jnp.arange(noff)[:, None] + DIFF_MIN_OFFSET) * TILE - x[None, :]
    rows = _bucket_rows(table[:, HEADS:], jnp.clip(dist, 0, seq - 1))
    rows = jnp.where((dist >= 0)[:, :, None], rows, NEG).transpose(0, 2, 1)
    diff = _toeplitz(rows.reshape(-1, 2 * TILE))
    room = jnp.maximum(DIFF_SAFE_LOG2 - jnp.max(jnp.abs(table[:, HEADS:])), 0.0)
    return dil, diff.reshape(noff, HEADS * TILE, TILE), (room * room).reshape(1, 1)


DIL_UNIT = 2 * TILE


def _dil_body(*refs, subs, dil, merge, has_prev):
    if merge:
        x_ref, xp_ref, bias_ref, unperm_ref, o1_ref, l1_ref, o2_ref, l2_ref, y_ref = refs
    else:
        x_ref, xp_ref, bias_ref, unperm_ref, o_ref, lse_ref = refs
    per_class = [_dil_class(x_ref, xp_ref, bias_ref, rc=rc, subs=subs, has_prev=has_prev)
                 for rc in range(dil)]
    piece = DIL_UNIT // dil
    class_o = [jnp.concatenate(pc[0], axis=0) for pc in per_class]
    class_l = [jnp.concatenate(pc[1], axis=0) for pc in per_class]
    for u in range(dil * subs * TILE // DIL_UNIT):
        take = lambda arrs: jnp.concatenate([a[u * piece:(u + 1) * piece] for a in arrs], axis=0)
        if dil == 1:
            o, lse = take(class_o), take(class_l)
        else:
            o_c = take(class_o).astype(BF16)
            l_c = take(class_l)
            l_hi = l_c.astype(BF16)
            l_lo = (l_c - l_hi.astype(F32)).astype(BF16)
            o = _dot(unperm_ref[...], o_c)
            lse = _dot(unperm_ref[...], l_hi) + _dot(unperm_ref[...], l_lo)
        rs = slice(u * DIL_UNIT, (u + 1) * DIL_UNIT)
        if merge:
            o1, l1 = o1_ref[rs, :].astype(F32), l1_ref[rs, :]
            o2, l2 = o2_ref[rs, :].astype(F32), l2_ref[rs, :]
            top = jnp.maximum(jnp.maximum(l1, l2), lse)
            w1, w2, w3 = jnp.exp2(l1 - top), jnp.exp2(l2 - top), jnp.exp2(lse - top)
            y_ref[rs, :] = ((w1 * o1 + w2 * o2 + w3 * o) / (w1 + w2 + w3)).astype(BF16)
        else:
            o_ref[rs, :] = o.astype(BF16)
            lse_ref[rs, :] = lse


def _dil_class(x_ref, xp_ref, bias_ref, *, rc, subs, has_prev):
    first = pl.program_id(1) == 0
    kmask = _group_mask(HEADS * TILE, GROUP, TILE, HEAD_DIM)
    ones_rows = (lax.broadcasted_iota(jnp.int32, (AUG_ROWS - HEAD_DIM, TILE), 0) == 0).astype(BF16)
    rows_of = lambda sb: slice(sb * TILE, (sb + 1) * TILE)
    col_q, col_k, col_v = (slice((3 * rc + w) * GROUP, (3 * rc + w + 1) * GROUP) for w in range(3))

    s_same, s_next, vaug = {}, {}, {}
    first_block = -1 if has_prev else 0
    for j in range(first_block, subs):
        k_j = xp_ref[:, col_k] if j < 0 else x_ref[rows_of(j), col_k]
        v_j = xp_ref[:, col_v] if j < 0 else x_ref[rows_of(j), col_v]
        parts = ([0] if j >= 0 else []) + ([1] if j + 1 < subs else [])
        q_cat = jnp.concatenate([x_ref[rows_of(j + e), col_q] for e in parts], axis=0)
        bias = bias_ref[:, parts[0] * TILE:(parts[-1] + 1) * TILE]
        st = _dot_nt(_tile_rows(k_j, HEADS, kmask), q_cat) + bias
        if j < 0:
            st = st + jnp.where(first, NEG, 0.0)
        for pos, e in enumerate(parts):
            (s_same if e == 0 else s_next)[j + e] = st[:, pos * TILE:(pos + 1) * TILE]
        v_t = v_j.astype(F32).T.astype(BF16)
        vaug[j] = [jnp.concatenate([v_t[h * HEAD_DIM:(h + 1) * HEAD_DIM], ones_rows], axis=0)
                   for h in range(HEADS)]

    p_same, p_next, tops = {}, {}, {}
    for i in range(subs):
        ps, pn, tp = [], [], []
        for h in range(HEADS):
            hs = slice(h * TILE, (h + 1) * TILE)
            a = s_same[i][hs]
            m = jnp.max(a, axis=0, keepdims=True)
            if i in s_next:
                b = s_next[i][hs]
                m = jnp.maximum(m, jnp.max(b, axis=0, keepdims=True))
                pn.append(jnp.exp2(b - m).astype(BF16))
            ps.append(jnp.exp2(a - m).astype(BF16))
            tp.append(m)
        p_same[i], p_next[i], tops[i] = ps, pn, tp

    acc = {i: [None] * HEADS for i in range(subs)}
    for j in range(first_block, subs):
        for h in range(HEADS):
            cols = ([p_same[j][h]] if j >= 0 else []) + ([p_next[j + 1][h]] if j + 1 < subs else [])
            r = _dot(vaug[j][h], jnp.concatenate(cols, axis=1))
            targets = ([j] if j >= 0 else []) + ([j + 1] if j + 1 < subs else [])
            for pos, i in enumerate(targets):
                part = r[:, pos * TILE:(pos + 1) * TILE]
                acc[i][h] = part if acc[i][h] is None else acc[i][h] + part

    outs, lses = [], []
    for i in range(subs):
        o_t, lse_t = [], []
        for h in range(HEADS):
            l = acc[i][h][HEAD_DIM:HEAD_DIM + 1]
            o_t.append(acc[i][h][:HEAD_DIM] / l)
            lse_t.append(jnp.broadcast_to(tops[i][h] + jnp.log2(l), (HEAD_DIM, TILE)))
        outs.append(jnp.concatenate(o_t, axis=0).T)
        lses.append(jnp.concatenate(lse_t, axis=0).T)
    return outs, lses


DIL_SUBBLOCKS = 16


def _unpermutation(dil):
    t = np.arange(DIL_UNIT)
    mat = np.zeros((DIL_UNIT, DIL_UNIT), np.float32)
    mat[t, (t % dil) * (DIL_UNIT // dil) + t // dil] = 1.0
    return jnp.asarray(mat, BF16)


def _dilated_pattern(zc, bias, dil, *, batch, seq, merge_with=None):
    length = seq // dil
    subs = min(max(DIL_SUBBLOCKS // max(dil // 2, 1), 2), length // TILE)
    rows = subs * TILE
    tokens = rows * dil
    zc = zc.reshape(batch, length, dil * 3 * GROUP)
    blk = pl.BlockSpec((None, rows, dil * 3 * GROUP), lambda b, n: (b, n, 0))
    prev = pl.BlockSpec((None, TILE, dil * 3 * GROUP), lambda b, n: (b, jnp.maximum(n * subs - 1, 0), 0))
    nat = pl.BlockSpec((None, tokens, GROUP), lambda b, n: (b, n, 0))
    in_specs = [blk, prev, _const_spec((HEADS * TILE, 2 * TILE)), _const_spec((DIL_UNIT, DIL_UNIT))]
    args = [zc, zc, bias, _unpermutation(dil)]
    if merge_with is None:
        out_specs = [nat, nat]
        out_shape = [jax.ShapeDtypeStruct((batch, seq, GROUP), BF16),
                     jax.ShapeDtypeStruct((batch, seq, GROUP), F32)]
    else:
        in_specs += [nat] * len(merge_with)
        args += list(merge_with)
        out_specs = nat
        out_shape = jax.ShapeDtypeStruct((batch, seq, GROUP), BF16)
    out = pl.pallas_call(
        functools.partial(_dil_body, subs=subs, dil=dil, merge=merge_with is not None,
                          has_prev=length > rows),
        grid=(batch, length // rows),
        in_specs=in_specs, out_specs=out_specs, out_shape=out_shape,
        compiler_params=_params("parallel", "parallel"),
        name=f"dilated_d{dil}",
    )(*args)
    if merge_with is None:
        return out
    return out.reshape(batch * seq, GROUP)


def _dilated(zc_views, dil_bias, *, batch, seq):
    o1, l1 = _dilated_pattern(zc_views[0], dil_bias[0], DIL_PATTERNS[0][1], batch=batch, seq=seq)
    o2, l2 = _dilated_pattern(zc_views[1], dil_bias[1], DIL_PATTERNS[1][1], batch=batch, seq=seq)
    return _dilated_pattern(zc_views[2], dil_bias[2], DIL_PATTERNS[2][1], batch=batch, seq=seq,
                            merge_with=(o1, l1, o2, l2))


DIFF_Q = 512
DIFF_K = 256
DIFF_GROUPS = 2 * HEADS


DIFF_SAFE_LOG2 = 60.0
DIFF_BOUND_SLACK = 1.01


def _diff_running_max(q, qi, last, pairs, bias_tiles, kexp_ref, vaug_ref, acc_ref, sta_ref, stb_ref):
    def scores(s_ref, j):
        bias = bias_tiles(j)
        raw = _dot_nt(kexp_ref[jnp.minimum(j, last)], q)
        tops = []
        for g in range(DIFF_GROUPS):
            s = raw[g * DIFF_K:(g + 1) * DIFF_K] + bias[g % HEADS]
            s_ref[g * DIFF_K:(g + 1) * DIFF_K, :] = s
            tops.append(jnp.max(s, axis=0, keepdims=True))
        return tuple(tops)

    def consume(s_ref, tops, j, carry):
        ms, ls = carry
        jv = jnp.minimum(j, last)
        new_ms, new_ls = [], []
        for g in range(DIFF_GROUPS):
            mp, h = divmod(g, HEADS)
            m_new = jnp.maximum(ms[g], tops[g])
            p = jnp.exp2(s_ref[g * DIFF_K:(g + 1) * DIFF_K, :] - m_new).astype(BF16)
            alpha = jnp.exp2(ms[g] - m_new)
            r = _dot(vaug_ref[jv, h], p)
            acc_ref[mp, h] = alpha * acc_ref[mp, h] + r[:HEAD_DIM]
            new_ls.append(alpha * ls[g] + r[HEAD_DIM:HEAD_DIM + 1])
            new_ms.append(m_new)
        return tuple(new_ms), tuple(new_ls)

    def pair(jj, carry):
        tops_a, state = carry
        j = 2 * jj
        tops_b = scores(stb_ref, j + 1)
        state = consume(sta_ref, tops_a, j, state)
        tops_a = scores(sta_ref, j + 2)
        return tops_a, consume(stb_ref, tops_b, j + 1, state)

    init = (tuple(jnp.full((1, DIFF_Q), NEG, F32) for _ in range(DIFF_GROUPS)),
            tuple(jnp.zeros((1, DIFF_Q), F32) for _ in range(DIFF_GROUPS)))
    _, (_, ls) = lax.fori_loop(0, pairs, pair, (scores(sta_ref, 0), init))
    return ls


def _diff_body(q_ref, k_ref, vt_ref, bias_ref, lam_ref, sg_ref, lim_ref, y_ref,
               kexp_ref, vaug_ref, acc_ref, sta_ref, stb_ref, pa_ref, pb_ref, knorm_ref,
               *, lam_init, key_steps):
    qi = pl.program_id(1)
    qk_group = _group_mask(GROUP, GROUP, DIFF_QK_HALF, DIFF_QK_HALF).astype(BF16)

    @pl.when(qi == 0)
    def _():
        grp = lax.broadcasted_iota(jnp.int32, (DIFF_GROUPS * DIFF_K, GROUP), 0) // DIFF_K
        slot = lax.broadcasted_iota(jnp.int32, (DIFF_GROUPS * DIFF_K, GROUP), 1) // DIFF_QK_HALF
        kmask = slot == 2 * (grp % HEADS) + grp // HEADS
        ones_rows = (lax.broadcasted_iota(jnp.int32, (AUG_ROWS - HEAD_DIM, DIFF_K), 0) == 0).astype(BF16)

        def build(j, kmax):
            k_t = k_ref[pl.ds(pl.multiple_of(j * DIFF_K, DIFF_K), DIFF_K), :]
            kexp_ref[j] = jnp.where(kmask, jnp.concatenate([k_t] * DIFF_GROUPS, axis=0),
                                    jnp.zeros((), BF16))
            vt = jnp.concatenate([vt_ref[2 * j], vt_ref[2 * j + 1]], axis=1)
            for h in range(HEADS):
                vaug_ref[j, h] = jnp.concatenate([vt[h * HEAD_DIM:(h + 1) * HEAD_DIM], ones_rows], axis=0)
            k_f = k_t.astype(F32)
            return jnp.maximum(kmax, jnp.max(_group_sum(k_f * k_f, qk_group), axis=0, keepdims=True))

        knorm_ref[...] = lax.fori_loop(0, key_steps, build, jnp.zeros((1, GROUP), F32))

    acc_ref[...] = jnp.zeros_like(acc_ref)
    q = q_ref[...]
    last = key_steps - 1
    key_steps_needed = (qi + 1) * (DIFF_Q // DIFF_K)
    pairs = (key_steps_needed + 1) // 2

    def bias_tiles(j):
        base = (DIFF_Q // TILE) * qi - (DIFF_K // TILE) * j - DIFF_MIN_OFFSET
        tiles = {d: bias_ref[jnp.maximum(base + d, 0)]
                 for d in range(1 - DIFF_K // TILE, DIFF_Q // TILE)}
        return [jnp.concatenate(
            [jnp.concatenate([tiles[a - b][h * TILE:(h + 1) * TILE] for a in range(DIFF_Q // TILE)], axis=1)
             for b in range(DIFF_K // TILE)], axis=0) for h in range(HEADS)]

    q_f = q.astype(F32)
    bound2 = jnp.max(_group_sum(q_f * q_f, qk_group) * knorm_ref[...])
    no_overflow = bound2 * DIFF_BOUND_SLACK <= lim_ref[0, 0]

    def unshifted():
        def weights(p_ref, j):
            bias = bias_tiles(j)
            raw = _dot_nt(kexp_ref[jnp.minimum(j, last)], q)
            for g in range(DIFF_GROUPS):
                rows = slice(g * DIFF_K, (g + 1) * DIFF_K)
                p_ref[rows, :] = jnp.exp2(raw[rows] + bias[g % HEADS]).astype(BF16)

        def accumulate(p_ref, j, ls):
            jv = jnp.minimum(j, last)
            new_ls = []
            for g in range(DIFF_GROUPS):
                mp, h = divmod(g, HEADS)
                r = _dot(vaug_ref[jv, h], p_ref[g * DIFF_K:(g + 1) * DIFF_K, :])
                acc_ref[mp, h] = acc_ref[mp, h] + r[:HEAD_DIM]
                new_ls.append(ls[g] + r[HEAD_DIM:HEAD_DIM + 1])
            return tuple(new_ls)

        def pair(jj, ls):
            j = 2 * jj
            weights(pb_ref, j + 1)
            ls = accumulate(pa_ref, j, ls)
            weights(pa_ref, j + 2)
            return accumulate(pb_ref, j + 1, ls)

        weights(pa_ref, 0)
        return lax.fori_loop(0, pairs, pair,
                             tuple(jnp.zeros((1, DIFF_Q), F32) for _ in range(DIFF_GROUPS)))

    def running_max():
        return _diff_running_max(q, qi, last, pairs, bias_tiles, kexp_ref, vaug_ref, acc_ref,
                                 sta_ref, stb_ref)

    ls = lax.cond(no_overflow, unshifted, running_max)

    lv = lam_ref[...]
    lam = (jnp.exp(jnp.sum(lv[0:1] * lv[1:2], axis=-1, keepdims=True))
           - jnp.exp(jnp.sum(lv[2:3] * lv[3:4], axis=-1, keepdims=True)) + lam_init)
    outs = []
    for h in range(HEADS):
        o = acc_ref[0, h] / ls[h] - lam * (acc_ref[1, h] / ls[HEADS + h])
        ms_o = jnp.mean(o * o, axis=0, keepdims=True)
        outs.append(o * lax.rsqrt(ms_o + SUBLN_EPS) * sg_ref[...] * (1.0 - lam_init))
    y_ref[...] = jnp.concatenate(outs, axis=0).T.astype(BF16)


def _diff_attention(zd, vt, bias, score_limit2, lam_vecs, subln_cols, *, lam_init, batch, seq):
    zd = zd.reshape(batch, seq, 2 * GROUP)
    key_tiles = seq // TILE
    key_steps = seq // DIFF_K
    vt = vt.reshape(batch, key_tiles, GROUP, TILE)
    y = pl.pallas_call(
        functools.partial(_diff_body, lam_init=lam_init, key_steps=key_steps),
        grid=(batch, seq // DIFF_Q),
        in_specs=[pl.BlockSpec((None, DIFF_Q, GROUP), lambda b, i: (b, i, 0)),
                  pl.BlockSpec((None, seq, GROUP), lambda b, i: (b, 0, 1), pipeline_mode=pl.Buffered(1)),
                  pl.BlockSpec((None, key_tiles, GROUP, TILE), lambda b, i: (b, 0, 0, 0),
                               pipeline_mode=pl.Buffered(1)),
                  _const_spec((key_tiles - DIFF_MIN_OFFSET, HEADS * TILE, TILE)),
                  _const_spec((4, DIFF_QK_HALF)), _const_spec((HEAD_DIM, DIFF_Q)),
                  pl.BlockSpec(memory_space=pltpu.SMEM)],
        out_specs=pl.BlockSpec((None, DIFF_Q, GROUP), lambda b, i: (b, i, 0)),
        out_shape=jax.ShapeDtypeStruct((batch, seq, GROUP), BF16),
        scratch_shapes=[pltpu.VMEM((key_steps, DIFF_GROUPS * DIFF_K, GROUP), BF16),
                        pltpu.VMEM((key_steps, HEADS, AUG_ROWS, DIFF_K), BF16),
                        pltpu.VMEM((2, HEADS, HEAD_DIM, DIFF_Q), F32),
                        pltpu.VMEM((DIFF_GROUPS * DIFF_K, DIFF_Q), F32),
                        pltpu.VMEM((DIFF_GROUPS * DIFF_K, DIFF_Q), F32),
                        pltpu.VMEM((DIFF_GROUPS * DIFF_K, DIFF_Q), BF16),
                        pltpu.VMEM((DIFF_GROUPS * DIFF_K, DIFF_Q), BF16),
                        pltpu.VMEM((1, GROUP), F32)],
        compiler_params=_params("parallel", "arbitrary"),
        name="diff_attn",
    )(zd, zd, vt, bias, lam_vecs, subln_cols, score_limit2)
    return y.reshape(batch * seq, GROUP)


KV_ROWS = 512


def _mem_kv_body(m_ref, g_ref, w_ref, k_ref, v_ref):
    u = _rms(m_ref[...], g_ref[...]).astype(BF16)
    for c in range(D_MODEL // GROUP):
        sl = slice(c * GROUP, (c + 1) * GROUP)
        k_ref[:, sl] = _dot(u, w_ref[:, sl]).astype(BF16)
        v_ref[:, sl] = _dot(u, w_ref[:, D_MODEL + c * GROUP: D_MODEL + (c + 1) * GROUP]).astype(BF16)


def _mem_kv(mem, g, w):
    n = mem.shape[0]
    row = pl.BlockSpec((KV_ROWS, D_MODEL), lambda i: (i, 0))
    return pl.pallas_call(
        _mem_kv_body,
        grid=(n // KV_ROWS,),
        in_specs=[row, _const_spec((1, D_MODEL)), _const_spec((D_MODEL, 2 * D_MODEL))],
        out_specs=[row, row],
        out_shape=[jax.ShapeDtypeStruct((n, D_MODEL), BF16)] * 2,
        compiler_params=_params("parallel"),
        name="mem_kv",
    )(mem, g, w)


XATTN_ROWS = 1024


def _xattn_body(x_ref, ya_ref, yb_ref, yc_ref, yd_ref, wout_ref, g_ref, wq_ref, k_ref, v_ref, wo_ref,
                o_ref, q_scr, a_scr):
    x = x_ref[...]
    for gi, y_ref in enumerate((ya_ref, yb_ref, yc_ref, yd_ref)):
        x = x + _dot(y_ref[...], wout_ref[gi * GROUP:(gi + 1) * GROUP, :])
    u = _rms(x, g_ref[...]).astype(BF16)
    for c in range(D_MODEL // GROUP):
        sl = slice(c * GROUP, (c + 1) * GROUP)
        q_scr[:, sl] = _dot(u, wq_ref[:, sl]).astype(BF16)
    for h in range(MEM_HEADS):
        sl = slice(h * MEM_HEAD_DIM, (h + 1) * MEM_HEAD_DIM)
        s = _dot_nt(q_scr[:, sl], k_ref[:, sl]) * (MEM_HEAD_DIM ** -0.5)
        e = jnp.exp(s - jnp.max(s, axis=-1, keepdims=True))
        l = jnp.sum(e, axis=-1, keepdims=True)
        a_scr[:, sl] = (_dot(e.astype(BF16), v_ref[:, sl]) / l).astype(BF16)
    o_ref[...] = x + _dot(a_scr[...], wo_ref[...])


def _xattn(h, ys, w_out, g, wq, k, v, wo, *, batch, seq):
    h3 = h.reshape(batch, seq, D_MODEL)
    ys = [y.reshape(batch, seq, GROUP) for y in ys]
    k3 = k.reshape(batch, MEM_LEN, D_MODEL)
    v3 = v.reshape(batch, MEM_LEN, D_MODEL)
    row = pl.BlockSpec((None, XATTN_ROWS, D_MODEL), lambda b, i: (b, i, 0))
    grp = pl.BlockSpec((None, XATTN_ROWS, GROUP), lambda b, i: (b, i, 0))
    mem = pl.BlockSpec((None, MEM_LEN, D_MODEL), lambda b, i: (b, 0, 0))
    weight = _const_spec((D_MODEL, D_MODEL))
    out = pl.pallas_call(
        _xattn_body,
        grid=(batch, seq // XATTN_ROWS),
        in_specs=[row, grp, grp, grp, grp, weight, _const_spec((1, D_MODEL)), weight, mem, mem, weight],
        out_specs=row,
        out_shape=jax.ShapeDtypeStruct((batch, seq, D_MODEL), F32),
        scratch_shapes=[pltpu.VMEM((XATTN_ROWS, D_MODEL), BF16), pltpu.VMEM((XATTN_ROWS, D_MODEL), BF16)],
        compiler_params=_params("parallel", "parallel"),
        name="xattn",
    )(h3, *ys, w_out, g, wq, k3, v3, wo)
    return out.reshape(batch * seq, D_MODEL)


def _per_head_lanes(x):
    return jnp.repeat(x, HEAD_DIM, axis=-1)


def _in_proj_weight(w_in):
    g = GROUP
    q_a, k_a, v_a, o_a = (w_in[:, i * g:(i + 1) * g] for i in range(4))
    ig = w_in[:, 4 * g:4 * g + HEADS]
    fg = w_in[:, 4 * g + HEADS:4 * g + 2 * HEADS]
    rest = w_in[:, 4 * g + 2 * HEADS:]
    pool, q_c, k_c, v_c, q_d, k_d, v_d = (rest[:, i * g:(i + 1) * g] for i in range(7))
    q_c = q_c * DIL_SCORE_SCALE
    k_d = k_d * DIFF_SCORE_SCALE
    cols = [q_a, k_a, v_a, o_a, _per_head_lanes(ig), _per_head_lanes(fg), pool,
            q_c, k_c, v_c, q_d, k_d]
    return jnp.concatenate(cols, axis=1).astype(BF16), v_d.astype(BF16)


def _block_diag(w):
    g, c, _ = w.shape
    eye = jnp.eye(g, dtype=w.dtype)
    return (eye[:, None, :, None] * w[:, :, None, :]).reshape(g * c, g * c)


def kernel(x, mem, t5_bias, ffn1_norm, ffn1_w_gate, ffn1_w_up, ffn1_w_down, mix_norm, w_in,
           mlstm_conv_w, mlstm_conv_b, mlstm_gate_b, mlstm_norm, pool_w, pool_scale,
           diff_lambda, diff_subln, w_out, xattn_norm, mem_norm, xattn_wq, xattn_wkv, xattn_wo,
           ffn2_norm, ffn2_w_gate, ffn2_w_up, ffn2_w_down, final_norm):
    batch, seq, _ = x.shape
    n = batch * seq
    dil_bias, diff_bias, diff_limit2 = _bias_tiles(t5_bias, seq)
    h = x.reshape(n, D_MODEL)
    mem2 = mem.reshape(batch * MEM_LEN, D_MODEL)
    row = lambda v: v.reshape(1, -1)
    for l in range(DEPTH):
        lam_init = 0.8 - 0.6 * math.exp(-0.3 * l)
        h = _ffn(h, row(ffn1_norm[l]), ffn1_w_gate[l].astype(BF16), ffn1_w_up[l].astype(BF16),
                 ffn1_w_down[l].astype(BF16), row(final_norm), final=False)
        za, zg, zp, zc, zd, vt, zc4, zc16 = _in_proj(h, row(mix_norm[l]), *_in_proj_weight(w_in[l]))
        ya = _mlstm(za, zg, mlstm_conv_w[l], row(mlstm_conv_b[l]),
                    row(_per_head_lanes(mlstm_gate_b[l].reshape(2, HEADS))), row(mlstm_norm[l]),
                    batch=batch, seq=seq)
        yb = _pool(zp, _block_diag(pool_w[l]).astype(BF16), row(pool_scale[l]), batch=batch, seq=seq)
        yc = _dilated((zc, zc4, zc16), dil_bias, batch=batch, seq=seq)
        yd = _diff_attention(zd, vt, diff_bias, diff_limit2, diff_lambda[l],
                             jnp.broadcast_to(diff_subln[l][:, None], (HEAD_DIM, DIFF_Q)),
                             lam_init=lam_init, batch=batch, seq=seq)
        k_mem, v_mem = _mem_kv(mem2, row(mem_norm[l]), xattn_wkv[l].astype(BF16))
        h = _xattn(h, (ya, yb, yc, yd), w_out[l].astype(BF16),
                   row(xattn_norm[l]), xattn_wq[l].astype(BF16), k_mem, v_mem,
                   xattn_wo[l].astype(BF16), batch=batch, seq=seq)
        h = _ffn(h, row(ffn2_norm[l]), ffn2_w_gate[l].astype(BF16), ffn2_w_up[l].astype(BF16),
                 ffn2_w_down[l].astype(BF16), row(final_norm), final=(l == DEPTH - 1))
    return h.reshape(batch, seq, D_MODEL)
```

```python
import functools
import math

import jax
import jax.numpy as jnp
import numpy as np
from jax import lax
from jax.experimental import pallas as pl
from jax.experimental.pallas import tpu as pltpu

F32 = jnp.float32
BF16 = jnp.bfloat16

D_MODEL = 1024
D_FF = 2816
DEPTH = 4
GROUP = 256
HEADS = 4
HEAD_DIM = GROUP // HEADS
MEM_LEN = 256
MEM_HEADS = 4
MEM_HEAD_DIM = D_MODEL // MEM_HEADS
MLSTM_CHUNK = 64
CONV_WIDTH = 4
POOL_WINDOWS = (2, 4, 8, 16)
DIL_PATTERNS = ((128, 1), (512, 4), (2048, 16))
DIL_BACK = 128
DIFF_QK_HALF = HEAD_DIM // 2
T5_BUCKETS = 32
T5_MAX_DIST = 2048
RMS_EPS = 1e-6
SUBLN_EPS = 1e-5
NEG = -1e30
LOG2E = math.log2(math.e)
DIFF_SCORE_SCALE = (DIFF_QK_HALF ** -0.5) * LOG2E
DIL_SCORE_SCALE = (HEAD_DIM ** -0.5) * LOG2E
AUG_ROWS = HEAD_DIM + 16
DIFF_MIN_OFFSET = -3
TILE = 128

VMEM_LIMIT_BYTES = 56 * 1024 * 1024


def _rms(xf, g, eps=RMS_EPS):
    return xf * lax.rsqrt(jnp.mean(xf * xf, axis=-1, keepdims=True) + eps) * g


def _const_spec(shape):
    zeros = (0,) * len(shape)
    return pl.BlockSpec(shape, lambda *_: zeros, pipeline_mode=pl.Buffered(1))


def _params(*sem):
    return pltpu.CompilerParams(dimension_semantics=sem, vmem_limit_bytes=VMEM_LIMIT_BYTES)


def _group_mask(rows, cols, row_group, col_group):
    r = lax.broadcasted_iota(jnp.int32, (rows, cols), 0) // row_group
    c = lax.broadcasted_iota(jnp.int32, (rows, cols), 1) // col_group
    return r == c


def _tile_rows(x, reps, mask):
    return jnp.where(mask, jnp.concatenate([x] * reps, axis=0), jnp.zeros((), x.dtype))


def _dot(a, b):
    return jnp.dot(a, b, preferred_element_type=F32)


def _dot_nt(a, b):
    return lax.dot_general(a, b, (((1,), (1,)), ((), ())), preferred_element_type=F32)


def _dot_tn(a, b):
    return lax.dot_general(a, b, (((0,), (0,)), ((), ())), preferred_element_type=F32)


def _group_sum(x, ones_bd):
    hi = x.astype(BF16)
    lo = (x - hi.astype(F32)).astype(BF16)
    return _dot(hi, ones_bd) + _dot(lo, ones_bd)


FFN_ROWS = 1024
FFN_COLS = 256


def _ffn_body(x_ref, g_ref, wg_ref, wu_ref, wd_ref, fg_ref, o_ref, act_ref, *, final):
    x = x_ref[...]
    u = _rms(x, g_ref[...]).astype(BF16)
    for c in range(D_FF // FFN_COLS):
        sl = slice(c * FFN_COLS, (c + 1) * FFN_COLS)
        gate = _dot(u, wg_ref[:, sl])
        up = _dot(u, wu_ref[:, sl])
        act_ref[:, sl] = (gate * jax.nn.sigmoid(gate) * up).astype(BF16)
    y = x + 0.5 * _dot(act_ref[...], wd_ref[...])
    if final:
        y = _rms(y, fg_ref[...])
    o_ref[...] = y


def _ffn(h, g, wg, wu, wd, fg, *, final):
    n = h.shape[0]
    row = pl.BlockSpec((FFN_ROWS, D_MODEL), lambda i: (i, 0))
    return pl.pallas_call(
        functools.partial(_ffn_body, final=final),
        grid=(n // FFN_ROWS,),
        in_specs=[row, _const_spec((1, D_MODEL)), _const_spec((D_MODEL, D_FF)),
                  _const_spec((D_MODEL, D_FF)), _const_spec((D_FF, D_MODEL)),
                  _const_spec((1, D_MODEL))],
        out_specs=row,
        out_shape=jax.ShapeDtypeStruct((n, D_MODEL), F32),
        scratch_shapes=[pltpu.VMEM((FFN_ROWS, D_FF), BF16)],
        compiler_params=_params("parallel"),
        name="ffn_final" if final else "ffn",
    )(h, g, wg, wu, wd, fg)


PROJ_ROWS = 1024
PROJ_OUTS = (("a", 4 * GROUP, F32), ("g", 2 * GROUP, F32), ("p", GROUP, F32),
             ("c", 3 * GROUP, BF16), ("d", 2 * GROUP, BF16))
PROJ_WIDTH = sum(w for _, w, _ in PROJ_OUTS)


PROJ_DILATIONS = tuple(d for _, d in DIL_PATTERNS if d > 1)


def _in_proj_body(x_ref, g_ref, w_ref, wv_ref, perm_ref, *o_refs):
    u = _rms(x_ref[...], g_ref[...]).astype(BF16)
    off = 0
    for o_ref, (_, width, dtype) in zip(o_refs, PROJ_OUTS):
        for c in range(width // GROUP):
            z = _dot(u, w_ref[:, off + c * GROUP: off + (c + 1) * GROUP])
            o_ref[:, c * GROUP:(c + 1) * GROUP] = z.astype(dtype)
        off += width
    vt_ref = o_refs[len(PROJ_OUTS)]
    v = _dot(u, wv_ref[...])
    for t in range(PROJ_ROWS // TILE):
        vt_ref[t] = v[t * TILE:(t + 1) * TILE].T.astype(BF16)
    for pi, d in enumerate(PROJ_DILATIONS):
        per_class = PROJ_UNIT // d
        for un in range(PROJ_ROWS // PROJ_UNIT):
            zp = _dot(perm_ref[pi], o_refs[3][un * PROJ_UNIT:(un + 1) * PROJ_UNIT, :]).astype(BF16)
            for r in range(d):
                o_refs[len(PROJ_OUTS) + 1 + pi][un * per_class:(un + 1) * per_class,
                                                r * 3 * GROUP:(r + 1) * 3 * GROUP] = \
                    zp[r * per_class:(r + 1) * per_class]


PROJ_UNIT = 2 * TILE


def _class_permutations():
    mats = np.zeros((len(PROJ_DILATIONS), PROJ_UNIT, PROJ_UNIT), np.float32)
    for pi, d in enumerate(PROJ_DILATIONS):
        t = np.arange(PROJ_UNIT)
        mats[pi, (t % d) * (PROJ_UNIT // d) + t // d, t] = 1.0
    return jnp.asarray(mats, BF16)


def _in_proj(h, g, w, wv):
    n = h.shape[0]
    tiles = PROJ_ROWS // TILE
    return pl.pallas_call(
        _in_proj_body,
        grid=(n // PROJ_ROWS,),
        in_specs=[pl.BlockSpec((PROJ_ROWS, D_MODEL), lambda i: (i, 0)),
                  _const_spec((1, D_MODEL)), _const_spec((D_MODEL, PROJ_WIDTH)),
                  _const_spec((D_MODEL, GROUP)),
                  _const_spec((len(PROJ_DILATIONS), PROJ_UNIT, PROJ_UNIT))],
        out_specs=[pl.BlockSpec((PROJ_ROWS, w_), lambda i: (i, 0)) for _, w_, _ in PROJ_OUTS]
        + [pl.BlockSpec((tiles, GROUP, TILE), lambda i: (i, 0, 0))]
        + [pl.BlockSpec((PROJ_ROWS // d, d * 3 * GROUP), lambda i: (i, 0)) for d in PROJ_DILATIONS],
        out_shape=[jax.ShapeDtypeStruct((n, w_), dt) for _, w_, dt in PROJ_OUTS]
        + [jax.ShapeDtypeStruct((n // TILE, GROUP, TILE), BF16)]
        + [jax.ShapeDtypeStruct((n // d, d * 3 * GROUP), BF16) for d in PROJ_DILATIONS],
        compiler_params=_params("parallel"),
        name="in_proj",
    )(h, g, w, wv, _class_permutations())


ML_ROWS = 512
ML_HALO = 8


def _chunk_scan(x, rin, op, fill):
    s = 1
    while s < MLSTM_CHUNK:
        x = op(x, jnp.where(rin >= s, pltpu.roll(x, s, 0), fill))
        s *= 2
    return x


def _mlstm_body(za_ref, zg_ref, cw_ref, cb_ref, gb_ref, ng_ref, y_ref,
                buf_ref, tail_ref, hh_ref, c_ref, n_ref, m_ref):
    L = MLSTM_CHUNK

    @pl.when(pl.program_id(1) == 0)
    def _():
        tail_ref[...] = jnp.zeros_like(tail_ref)
        c_ref[...] = jnp.zeros_like(c_ref)
        n_ref[...] = jnp.zeros_like(n_ref)
        m_ref[...] = jnp.zeros_like(m_ref)

    buf_ref[0:ML_HALO, :] = tail_ref[...]
    buf_ref[ML_HALO:, :] = za_ref[:, 0:2 * GROUP]
    tail_ref[...] = za_ref[ML_ROWS - ML_HALO:, 0:2 * GROUP]
    conv = cb_ref[...]
    for j in range(CONV_WIDTH):
        conv = conv + buf_ref[pl.ds(ML_HALO - (CONV_WIDTH - 1) + j, ML_ROWS), :] * cw_ref[j:j + 1, :]
    qk = conv * jax.nn.sigmoid(conv)
    q = qk[:, :GROUP]
    k = qk[:, GROUP:] * (HEAD_DIM ** -0.5)
    v = za_ref[:, 2 * GROUP:3 * GROUP].astype(BF16)
    q_bf, k_bf = q.astype(BF16), k.astype(BF16)

    ii = zg_ref[:, :GROUP] + gb_ref[:, :GROUP]
    fx = zg_ref[:, GROUP:] + gb_ref[:, GROUP:]
    lf = jnp.minimum(fx, 0.0) - jnp.log(1.0 + jnp.exp(-jnp.abs(fx)))
    rin = lax.broadcasted_iota(jnp.int32, (ML_ROWS, GROUP), 0) % L
    b = _chunk_scan(lf, rin, jnp.add, 0.0)
    a = ii - b
    ca = _chunk_scan(a, rin, jnp.maximum, NEG)

    bd = _group_mask(GROUP, GROUP, HEAD_DIM, HEAD_DIM)
    ones_bd = bd.astype(BF16)
    row = lax.broadcasted_iota(jnp.int32, (L, GROUP), 0)
    key = lax.broadcasted_iota(jnp.int32, (L, GROUP), 1) % L
    causal = key <= row
    diag = key == row

    m_prev = m_ref[...]
    for c in range(ML_ROWS // L):
        rs = slice(c * L, (c + 1) * L)
        q_c, k_c, v_c = q[rs], k[rs], v[rs]
        q_b = q_bf[rs]
        a_c, b_c = a[rs], b[rs]
        g = jnp.maximum(m_prev, ca[rs])
        g_last = g[L - 1:L]
        a_row = jnp.sum(jnp.where(diag, a_c, 0.0), axis=0, keepdims=True)
        decay = jnp.exp(jnp.where(causal, a_row - g, NEG))
        sc = _dot_nt(q_b, _tile_rows(k_bf[rs], HEADS, bd)) * decay
        inter = jnp.exp(m_prev - g)
        num = inter * _dot(q_b, c_ref[...].astype(BF16)) + _dot(sc.astype(BF16), _tile_rows(v_c, HEADS, bd))
        den = inter * _group_sum(q_c * n_ref[...], ones_bd) + _group_sum(sc, ones_bd)
        hh_ref[rs, :] = num / jnp.maximum(jnp.abs(den), jnp.exp(-(b_c + g)))
        kw = k_c * jnp.exp(a_c - g_last)
        carry = jnp.exp(m_prev - g_last)
        c_ref[...] = carry * c_ref[...] + jnp.where(bd, _dot_tn(kw.astype(BF16), v_c), 0.0)
        n_ref[...] = carry * n_ref[...] + jnp.sum(kw, axis=0, keepdims=True)
        m_prev = b_c[L - 1:L] + g_last
    m_ref[...] = m_prev

    hh = hh_ref[...]
    mu = _group_sum(hh, ones_bd) * (1.0 / HEAD_DIM)
    dev = hh - mu
    var = _group_sum(dev * dev, ones_bd) * (1.0 / HEAD_DIM)
    o_gate = jax.nn.sigmoid(za_ref[:, 3 * GROUP:])
    y_ref[...] = (dev * lax.rsqrt(var + RMS_EPS) * ng_ref[...] * o_gate).astype(BF16)


def _mlstm(za, zg, conv_w, conv_b, gate_b, norm_g, *, batch, seq):
    za = za.reshape(batch, seq, 4 * GROUP)
    zg = zg.reshape(batch, seq, 2 * GROUP)
    y = pl.pallas_call(
        _mlstm_body,
        grid=(batch, seq // ML_ROWS),
        in_specs=[pl.BlockSpec((None, ML_ROWS, 4 * GROUP), lambda b, j: (b, j, 0)),
                  pl.BlockSpec((None, ML_ROWS, 2 * GROUP), lambda b, j: (b, j, 0)),
                  _const_spec((CONV_WIDTH, 2 * GROUP)), _const_spec((1, 2 * GROUP)),
                  _const_spec((1, 2 * GROUP)), _const_spec((1, GROUP))],
        out_specs=pl.BlockSpec((None, ML_ROWS, GROUP), lambda b, j: (b, j, 0)),
        out_shape=jax.ShapeDtypeStruct((batch, seq, GROUP), BF16),
        scratch_shapes=[pltpu.VMEM((ML_ROWS + ML_HALO, 2 * GROUP), F32),
                        pltpu.VMEM((ML_HALO, 2 * GROUP), F32),
                        pltpu.VMEM((ML_ROWS, GROUP), F32),
                        pltpu.VMEM((GROUP, GROUP), F32),
                        pltpu.VMEM((1, GROUP), F32),
                        pltpu.VMEM((1, GROUP), F32)],
        compiler_params=_params("parallel", "arbitrary"),
        name="mlstm",
    )(za, zg, conv_w, conv_b, gate_b, norm_g)
    return y.reshape(batch * seq, GROUP)


POOL_ROWS = 2048
POOL_HALO = 16


def _pool_body(u_ref, w_ref, s_ref, y_ref, buf_ref, tail_ref):
    j = pl.program_id(1)

    @pl.when(j == 0)
    def _():
        tail_ref[...] = jnp.zeros_like(tail_ref)

    buf_ref[0:POOL_HALO, :] = tail_ref[...]
    buf_ref[POOL_HALO:, :] = u_ref[...]
    tail_ref[...] = u_ref[POOL_ROWS - POOL_HALO:, :]
    sums, s = [], buf_ref[...]
    for shift in (1, 2, 4, 8):
        s = s + pltpu.roll(s, shift, 0)
        sums.append(s[POOL_HALO:])
    u = u_ref[...]
    lane_group = lax.broadcasted_iota(jnp.int32, (POOL_ROWS, GROUP), 1) // HEAD_DIM
    t = j * POOL_ROWS + lax.broadcasted_iota(jnp.int32, (POOL_ROWS, GROUP), 0)
    total, win = sums[3], jnp.full((POOL_ROWS, GROUP), POOL_WINDOWS[3], jnp.int32)
    for gi in (2, 1, 0):
        total = jnp.where(lane_group == gi, sums[gi], total)
        win = jnp.where(lane_group == gi, POOL_WINDOWS[gi], win)
    mean = total / jnp.minimum(t + 1, win).astype(F32)
    y = _dot((mean - u).astype(BF16), w_ref[...]) * s_ref[...]
    y_ref[...] = y.astype(BF16)


def _pool(zp, w_bd, scale, *, batch, seq):
    zp = zp.reshape(batch, seq, GROUP)
    y = pl.pallas_call(
        _pool_body,
        grid=(batch, seq // POOL_ROWS),
        in_specs=[pl.BlockSpec((None, POOL_ROWS, GROUP), lambda b, j: (b, j, 0)),
                  _const_spec((GROUP, GROUP)), _const_spec((1, GROUP))],
        out_specs=pl.BlockSpec((None, POOL_ROWS, GROUP), lambda b, j: (b, j, 0)),
        out_shape=jax.ShapeDtypeStruct((batch, seq, GROUP), BF16),
        scratch_shapes=[pltpu.VMEM((POOL_ROWS + POOL_HALO, GROUP), F32),
                        pltpu.VMEM((POOL_HALO, GROUP), F32)],
        compiler_params=_params("parallel", "arbitrary"),
        name="pool",
    )(zp, w_bd, scale)
    return y.reshape(batch * seq, GROUP)


TOEPLITZ_BATCH = 4


def _toeplitz_body(w_ref, o_ref):
    for t in range(TOEPLITZ_BATCH):
        x = jnp.broadcast_to(w_ref[t], (TILE, 2 * TILE))
        o_ref[t] = pltpu.roll(x, 0, 1, stride=1, stride_axis=0)[:, :TILE]


def _toeplitz(rows):
    n = rows.shape[0]
    assert n % TOEPLITZ_BATCH == 0
    return pl.pallas_call(
        _toeplitz_body,
        grid=(n // TOEPLITZ_BATCH,),
        in_specs=[pl.BlockSpec((TOEPLITZ_BATCH, 1, 2 * TILE), lambda i: (i, 0, 0))],
        out_specs=pl.BlockSpec((TOEPLITZ_BATCH, TILE, TILE), lambda i: (i, 0, 0)),
        out_shape=jax.ShapeDtypeStruct((n, TILE, TILE), F32),
        compiler_params=_params("parallel"),
        name="toeplitz",
    )(rows.reshape(n, 1, 2 * TILE))


_TOEPLITZ_X = np.where(np.arange(2 * TILE) <= TILE, -np.arange(2 * TILE), 2 * TILE - np.arange(2 * TILE))


def _t5_bucket(dist):
    max_exact = T5_BUCKETS // 2
    d = jnp.maximum(dist, 1).astype(F32)
    large = max_exact + (jnp.log(d / max_exact) / math.log(T5_MAX_DIST / max_exact)
                         * (T5_BUCKETS - max_exact)).astype(jnp.int32)
    large = jnp.minimum(large, T5_BUCKETS - 1)
    return jnp.where(dist < max_exact, dist, large)


def _bucket_rows(table, dist):
    onehot = _t5_bucket(dist)[..., None, None] == jnp.arange(T5_BUCKETS)[:, None]
    return jnp.sum(jnp.where(onehot, table, 0.0), axis=-2)


def _bias_tiles(t5_bias, seq):
    x = jnp.asarray(_TOEPLITZ_X, jnp.int32)
    table = t5_bias * LOG2E
    delta = jnp.clip(jnp.stack([-x, DIL_BACK - x]), 0, DIL_BACK)
    valid = jnp.stack([x <= 0, x >= 0])[None, :, :, None]
    rows = _bucket_rows(table[:, :HEADS], jnp.stack([delta * d for _, d in DIL_PATTERNS]))
    rows = jnp.where(valid, rows, NEG).transpose(0, 3, 1, 2)
    dil = _toeplitz(rows.reshape(-1, 2 * TILE))
    dil = dil.reshape(len(DIL_PATTERNS), HEADS, 2, TILE, TILE).transpose(0, 1, 3, 2, 4)
    dil = dil.reshape(len(DIL_PATTERNS), HEADS * TILE, 2 * TILE)
    noff = seq // TILE - DIFF_MIN_OFFSET
    dist = (jnp.arange(noff)[:, None] + DIFF_MIN_OFFSET) * TILE - x[None, :]
    rows = _bucket_rows(table[:, HEADS:], jnp.clip(dist, 0, seq - 1))
    rows = jnp.where((dist >= 0)[:, :, None], rows, NEG).transpose(0, 2, 1)
    diff = _toeplitz(rows.reshape(-1, 2 * TILE))
    room = jnp.maximum(DIFF_SAFE_LOG2 - jnp.max(jnp.abs(table[:, HEADS:])), 0.0)
    return dil, diff.reshape(noff, HEADS * TILE, TILE), (room * room).reshape(1, 1)


DIL_UNIT = 2 * TILE


def _dil_body(*refs, subs, dil, merge, has_prev):
    if merge:
        x_ref, xp_ref, bias_ref, unperm_ref, o1_ref, l1_ref, o2_ref, l2_ref, y_ref = refs
    else:
        x_ref, xp_ref, bias_ref, unperm_ref, o_ref, lse_ref = refs
    per_class = [_dil_class(x_ref, xp_ref, bias_ref, rc=rc, subs=subs, has_prev=has_prev)
                 for rc in range(dil)]
    piece = DIL_UNIT // dil
    class_o = [jnp.concatenate(pc[0], axis=0) for pc in per_class]
    class_l = [jnp.concatenate(pc[1], axis=0) for pc in per_class]
    for u in range(dil * subs * TILE // DIL_UNIT):
        take = lambda arrs: jnp.concatenate([a[u * piece:(u + 1) * piece] for a in arrs], axis=0)
        if dil == 1:
            o, lse = take(class_o), take(class_l)
        else:
            o_c = take(class_o).astype(BF16)
            l_c = take(class_l)
            l_hi = l_c.astype(BF16)
            l_lo = (l_c - l_hi.astype(F32)).astype(BF16)
            o = _dot(unperm_ref[...], o_c)
            lse = _dot(unperm_ref[...], l_hi) + _dot(unperm_ref[...], l_lo)
        rs = slice(u * DIL_UNIT, (u + 1) * DIL_UNIT)
        if merge:
            o1, l1 = o1_ref[rs, :].astype(F32), l1_ref[rs, :]
            o2, l2 = o2_ref[rs, :].astype(F32), l2_ref[rs, :]
            top = jnp.maximum(jnp.maximum(l1, l2), lse)
            w1, w2, w3 = jnp.exp2(l1 - top), jnp.exp2(l2 - top), jnp.exp2(lse - top)
            y_ref[rs, :] = ((w1 * o1 + w2 * o2 + w3 * o) / (w1 + w2 + w3)).astype(BF16)
        else:
            o_ref[rs, :] = o.astype(BF16)
            lse_ref[rs, :] = lse


def _dil_class(x_ref, xp_ref, bias_ref, *, rc, subs, has_prev):
    first = pl.program_id(1) == 0
    kmask = _group_mask(HEADS * TILE, GROUP, TILE, HEAD_DIM)
    ones_rows = (lax.broadcasted_iota(jnp.int32, (AUG_ROWS - HEAD_DIM, TILE), 0) == 0).astype(BF16)
    rows_of = lambda sb: slice(sb * TILE, (sb + 1) * TILE)
    col_q, col_k, col_v = (slice((3 * rc + w) * GROUP, (3 * rc + w + 1) * GROUP) for w in range(3))

    s_same, s_next, vaug = {}, {}, {}
    first_block = -1 if has_prev else 0
    for j in range(first_block, subs):
        k_j = xp_ref[:, col_k] if j < 0 else x_ref[rows_of(j), col_k]
        v_j = xp_ref[:, col_v] if j < 0 else x_ref[rows_of(j), col_v]
        parts = ([0] if j >= 0 else []) + ([1] if j + 1 < subs else [])
        q_cat = jnp.concatenate([x_ref[rows_of(j + e), col_q] for e in parts], axis=0)
        bias = bias_ref[:, parts[0] * TILE:(parts[-1] + 1) * TILE]
        st = _dot_nt(_tile_rows(k_j, HEADS, kmask), q_cat) + bias
        if j < 0:
            st = st + jnp.where(first, NEG, 0.0)
        for pos, e in enumerate(parts):
            (s_same if e == 0 else s_next)[j + e] = st[:, pos * TILE:(pos + 1) * TILE]
        v_t = v_j.astype(F32).T.astype(BF16)
        vaug[j] = [jnp.concatenate([v_t[h * HEAD_DIM:(h + 1) * HEAD_DIM], ones_rows], axis=0)
                   for h in range(HEADS)]

    p_same, p_next, tops = {}, {}, {}
    for i in range(subs):
        ps, pn, tp = [], [], []
        for h in range(HEADS):
            hs = slice(h * TILE, (h + 1) * TILE)
            a = s_same[i][hs]
            m = jnp.max(a, axis=0, keepdims=True)
            if i in s_next:
                b = s_next[i][hs]
                m = jnp.maximum(m, jnp.max(b, axis=0, keepdims=True))
                pn.append(jnp.exp2(b - m).astype(BF16))
            ps.append(jnp.exp2(a - m).astype(BF16))
            tp.append(m)
        p_same[i], p_next[i], tops[i] = ps, pn, tp

    acc = {i: [None] * HEADS for i in range(subs)}
    for j in range(first_block, subs):
        for h in range(HEADS):
            cols = ([p_same[j][h]] if j >= 0 else []) + ([p_next[j + 1][h]] if j + 1 < subs else [])
            r = _dot(vaug[j][h], jnp.concatenate(cols, axis=1))
            targets = ([j] if j >= 0 else []) + ([j + 1] if j + 1 < subs else [])
            for pos, i in enumerate(targets):
                part = r[:, pos * TILE:(pos + 1) * TILE]
                acc[i][h] = part if acc[i][h] is None else acc[i][h] + part

    outs, lses = [], []
    for i in range(subs):
        o_t, lse_t = [], []
        for h in range(HEADS):
            l = acc[i][h][HEAD_DIM:HEAD_DIM + 1]
            o_t.append(acc[i][h][:HEAD_DIM] / l)
            lse_t.append(jnp.broadcast_to(tops[i][h] + jnp.log2(l), (HEAD_DIM, TILE)))
        outs.append(jnp.concatenate(o_t, axis=0).T)
        lses.append(jnp.concatenate(lse_t, axis=0).T)
    return outs, lses


DIL_SUBBLOCKS = 16


def _unpermutation(dil):
    t = np.arange(DIL_UNIT)
    mat = np.zeros((DIL_UNIT, DIL_UNIT), np.float32)
    mat[t, (t % dil) * (DIL_UNIT // dil) + t // dil] = 1.0
    return jnp.asarray(mat, BF16)


def _dilated_pattern(zc, bias, dil, *, batch, seq, merge_with=None):
    length = seq // dil
    subs = min(max(DIL_SUBBLOCKS // max(dil // 2, 1), 2), length // TILE)
    rows = subs * TILE
    tokens = rows * dil
    zc = zc.reshape(batch, length, dil * 3 * GROUP)
    blk = pl.BlockSpec((None, rows, dil * 3 * GROUP), lambda b, n: (b, n, 0))
    prev = pl.BlockSpec((None, TILE, dil * 3 * GROUP), lambda b, n: (b, jnp.maximum(n * subs - 1, 0), 0))
    nat = pl.BlockSpec((None, tokens, GROUP), lambda b, n: (b, n, 0))
    in_specs = [blk, prev, _const_spec((HEADS * TILE, 2 * TILE)), _const_spec((DIL_UNIT, DIL_UNIT))]
    args = [zc, zc, bias, _unpermutation(dil)]
    if merge_with is None:
        out_specs = [nat, nat]
        out_shape = [jax.ShapeDtypeStruct((batch, seq, GROUP), BF16),
                     jax.ShapeDtypeStruct((batch, seq, GROUP), F32)]
    else:
        in_specs += [nat] * len(merge_with)
        args += list(merge_with)
        out_specs = nat
        out_shape = jax.ShapeDtypeStruct((batch, seq, GROUP), BF16)
    out = pl.pallas_call(
        functools.partial(_dil_body, subs=subs, dil=dil, merge=merge_with is not None,
                          has_prev=length > rows),
        grid=(batch, length // rows),
        in_specs=in_specs, out_specs=out_specs, out_shape=out_shape,
        compiler_params=_params("parallel", "parallel"),
        name=f"dilated_d{dil}",
    )(*args)
    if merge_with is None:
        return out
    return out.reshape(batch * seq, GROUP)


def _dilated(zc_views, dil_bias, *, batch, seq):
    o1, l1 = _dilated_pattern(zc_views[0], dil_bias[0], DIL_PATTERNS[0][1], batch=batch, seq=seq)
    o2, l2 = _dilated_pattern(zc_views[1], dil_bias[1], DIL_PATTERNS[1][1], batch=batch, seq=seq)
    return _dilated_pattern(zc_views[2], dil_bias[2], DIL_PATTERNS[2][1], batch=batch, seq=seq,
                            merge_with=(o1, l1, o2, l2))


DIFF_Q = 512
DIFF_K = 256
DIFF_GROUPS = 2 * HEADS


DIFF_SAFE_LOG2 = 60.0
DIFF_BOUND_SLACK = 1.01


def _diff_running_max(q, qi, last, pairs, bias_tiles, kexp_ref, vaug_ref, acc_ref, sta_ref, stb_ref):
    def scores(s_ref, j):
        bias = bias_tiles(j)
        raw = _dot_nt(kexp_ref[jnp.minimum(j, last)], q)
        tops = []
        for g in range(DIFF_GROUPS):
            s = raw[g * DIFF_K:(g + 1) * DIFF_K] + bias[g % HEADS]
            s_ref[g * DIFF_K:(g + 1) * DIFF_K, :] = s
            tops.append(jnp.max(s, axis=0, keepdims=True))
        return tuple(tops)

    def consume(s_ref, tops, j, carry):
        ms, ls = carry
        jv = jnp.minimum(j, last)
        new_ms, new_ls = [], []
        for g in range(DIFF_GROUPS):
            mp, h = divmod(g, HEADS)
            m_new = jnp.maximum(ms[g], tops[g])
            p = jnp.exp2(s_ref[g * DIFF_K:(g + 1) * DIFF_K, :] - m_new).astype(BF16)
            alpha = jnp.exp2(ms[g] - m_new)
            r = _dot(vaug_ref[jv, h], p)
            acc_ref[mp, h] = alpha * acc_ref[mp, h] + r[:HEAD_DIM]
            new_ls.append(alpha * ls[g] + r[HEAD_DIM:HEAD_DIM + 1])
            new_ms.append(m_new)
        return tuple(new_ms), tuple(new_ls)

    def pair(jj, carry):
        tops_a, state = carry
        j = 2 * jj
        tops_b = scores(stb_ref, j + 1)
        state = consume(sta_ref, tops_a, j, state)
        tops_a = scores(sta_ref, j + 2)
        return tops_a, consume(stb_ref, tops_b, j + 1, state)

    init = (tuple(jnp.full((1, DIFF_Q), NEG, F32) for _ in range(DIFF_GROUPS)),
            tuple(jnp.zeros((1, DIFF_Q), F32) for _ in range(DIFF_GROUPS)))
    _, (_, ls) = lax.fori_loop(0, pairs, pair, (scores(sta_ref, 0), init))
    return ls


def _diff_body(q_ref, k_ref, vt_ref, bias_ref, lam_ref, sg_ref, lim_ref, y_ref,
               kexp_ref, vaug_ref, acc_ref, sta_ref, stb_ref, pa_ref, pb_ref, knorm_ref,
               *, lam_init, key_steps):
    qi = pl.program_id(1)
    qk_group = _group_mask(GROUP, GROUP, DIFF_QK_HALF, DIFF_QK_HALF).astype(BF16)

    @pl.when(qi == 0)
    def _():
        grp = lax.broadcasted_iota(jnp.int32, (DIFF_GROUPS * DIFF_K, GROUP), 0) // DIFF_K
        slot = lax.broadcasted_iota(jnp.int32, (DIFF_GROUPS * DIFF_K, GROUP), 1) // DIFF_QK_HALF
        kmask = slot == 2 * (grp % HEADS) + grp // HEADS
        ones_rows = (lax.broadcasted_iota(jnp.int32, (AUG_ROWS - HEAD_DIM, DIFF_K), 0) == 0).astype(BF16)

        def build(j, kmax):
            k_t = k_ref[pl.ds(pl.multiple_of(j * DIFF_K, DIFF_K), DIFF_K), :]
            kexp_ref[j] = jnp.where(kmask, jnp.concatenate([k_t] * DIFF_GROUPS, axis=0),
                                    jnp.zeros((), BF16))
            vt = jnp.concatenate([vt_ref[2 * j], vt_ref[2 * j + 1]], axis=1)
            for h in range(HEADS):
                vaug_ref[j, h] = jnp.concatenate([vt[h * HEAD_DIM:(h + 1) * HEAD_DIM], ones_rows], axis=0)
            k_f = k_t.astype(F32)
            return jnp.maximum(kmax, jnp.max(_group_sum(k_f * k_f, qk_group), axis=0, keepdims=True))

        knorm_ref[...] = lax.fori_loop(0, key_steps, build, jnp.zeros((1, GROUP), F32))

    acc_ref[...] = jnp.zeros_like(acc_ref)
    q = q_ref[...]
    last = key_steps - 1
    key_steps_needed = (qi + 1) * (DIFF_Q // DIFF_K)
    pairs = (key_steps_needed + 1) // 2

    def bias_tiles(j):
        base = (DIFF_Q // TILE) * qi - (DIFF_K // TILE) * j - DIFF_MIN_OFFSET
        tiles = {d: bias_ref[jnp.maximum(base + d, 0)]
                 for d in range(1 - DIFF_K // TILE, DIFF_Q // TILE)}
        return [jnp.concatenate(
            [jnp.concatenate([tiles[a - b][h * TILE:(h + 1) * TILE] for a in range(DIFF_Q // TILE)], axis=1)
             for b in range(DIFF_K // TILE)], axis=0) for h in range(HEADS)]

    q_f = q.astype(F32)
    bound2 = jnp.max(_group_sum(q_f * q_f, qk_group) * knorm_ref[...])
    no_overflow = bound2 * DIFF_BOUND_SLACK <= lim_ref[0, 0]

    def unshifted():
        def weights(p_ref, j):
            bias = bias_tiles(j)
            raw = _dot_nt(kexp_ref[jnp.minimum(j, last)], q)
            for g in range(DIFF_GROUPS):
                rows = slice(g * DIFF_K, (g + 1) * DIFF_K)
                p_ref[rows, :] = jnp.exp2(raw[rows] + bias[g % HEADS]).astype(BF16)

        def accumulate(p_ref, j, ls):
            jv = jnp.minimum(j, last)
            new_ls = []
            for g in range(DIFF_GROUPS):
                mp, h = divmod(g, HEADS)
                r = _dot(vaug_ref[jv, h], p_ref[g * DIFF_K:(g + 1) * DIFF_K, :])
                acc_ref[mp, h] = acc_ref[mp, h] + r[:HEAD_DIM]
                new_ls.append(ls[g] + r[HEAD_DIM:HEAD_DIM + 1])
            return tuple(new_ls)

        def pair(jj, ls):
            j = 2 * jj
            weights(pb_ref, j + 1)
            ls = accumulate(pa_ref, j, ls)
            weights(pa_ref, j + 2)
            return accumulate(pb_ref, j + 1, ls)

        weights(pa_ref, 0)
        return lax.fori_loop(0, pairs, pair,
                             tuple(jnp.zeros((1, DIFF_Q), F32) for _ in range(DIFF_GROUPS)))

    def running_max():
        return _diff_running_max(q, qi, last, pairs, bias_tiles, kexp_ref, vaug_ref, acc_ref,
                                 sta_ref, stb_ref)

    ls = lax.cond(no_overflow, unshifted, running_max)

    lv = lam_ref[...]
    lam = (jnp.exp(jnp.sum(lv[0:1] * lv[1:2], axis=-1, keepdims=True))
           - jnp.exp(jnp.sum(lv[2:3] * lv[3:4], axis=-1, keepdims=True)) + lam_init)
    outs = []
    for h in range(HEADS):
        o = acc_ref[0, h] / ls[h] - lam * (acc_ref[1, h] / ls[HEADS + h])
        ms_o = jnp.mean(o * o, axis=0, keepdims=True)
        outs.append(o * lax.rsqrt(ms_o + SUBLN_EPS) * sg_ref[...] * (1.0 - lam_init))
    y_ref[...] = jnp.concatenate(outs, axis=0).T.astype(BF16)


def _diff_attention(zd, vt, bias, score_limit2, lam_vecs, subln_cols, *, lam_init, batch, seq):
    zd = zd.reshape(batch, seq, 2 * GROUP)
    key_tiles = seq // TILE
    key_steps = seq // DIFF_K
    vt = vt.reshape(batch, key_tiles, GROUP, TILE)
    y = pl.pallas_call(
        functools.partial(_diff_body, lam_init=lam_init, key_steps=key_steps),
        grid=(batch, seq // DIFF_Q),
        in_specs=[pl.BlockSpec((None, DIFF_Q, GROUP), lambda b, i: (b, i, 0)),
                  pl.BlockSpec((None, seq, GROUP), lambda b, i: (b, 0, 1), pipeline_mode=pl.Buffered(1)),
                  pl.BlockSpec((None, key_tiles, GROUP, TILE), lambda b, i: (b, 0, 0, 0),
                               pipeline_mode=pl.Buffered(1)),
                  _const_spec((key_tiles - DIFF_MIN_OFFSET, HEADS * TILE, TILE)),
                  _const_spec((4, DIFF_QK_HALF)), _const_spec((HEAD_DIM, DIFF_Q)),
                  pl.BlockSpec(memory_space=pltpu.SMEM)],
        out_specs=pl.BlockSpec((None, DIFF_Q, GROUP), lambda b, i: (b, i, 0)),
        out_shape=jax.ShapeDtypeStruct((batch, seq, GROUP), BF16),
        scratch_shapes=[pltpu.VMEM((key_steps, DIFF_GROUPS * DIFF_K, GROUP), BF16),
                        pltpu.VMEM((key_steps, HEADS, AUG_ROWS, DIFF_K), BF16),
                        pltpu.VMEM((2, HEADS, HEAD_DIM, DIFF_Q), F32),
                        pltpu.VMEM((DIFF_GROUPS * DIFF_K, DIFF_Q), F32),
                        pltpu.VMEM((DIFF_GROUPS * DIFF_K, DIFF_Q), F32),
                        pltpu.VMEM((DIFF_GROUPS * DIFF_K, DIFF_Q), BF16),
                        pltpu.VMEM((DIFF_GROUPS * DIFF_K, DIFF_Q), BF16),
                        pltpu.VMEM((1, GROUP), F32)],
        compiler_params=_params("parallel", "arbitrary"),
        name="diff_attn",
    )(zd, zd, vt, bias, lam_vecs, subln_cols, score_limit2)
    return y.reshape(batch * seq, GROUP)


KV_ROWS = 512


def _mem_kv_body(m_ref, g_ref, w_ref, k_ref, v_ref):
    u = _rms(m_ref[...], g_ref[...]).astype(BF16)
    for c in range(D_MODEL // GROUP):
        sl = slice(c * GROUP, (c + 1) * GROUP)
        k_ref[:, sl] = _dot(u, w_ref[:, sl]).astype(BF16)
        v_ref[:, sl] = _dot(u, w_ref[:, D_MODEL + c * GROUP: D_MODEL + (c + 1) * GROUP]).astype(BF16)


def _mem_kv(mem, g, w):
    n = mem.shape[0]
    row = pl.BlockSpec((KV_ROWS, D_MODEL), lambda i: (i, 0))
    return pl.pallas_call(
        _mem_kv_body,
        grid=(n // KV_ROWS,),
        in_specs=[row, _const_spec((1, D_MODEL)), _const_spec((D_MODEL, 2 * D_MODEL))],
        out_specs=[row, row],
        out_shape=[jax.ShapeDtypeStruct((n, D_MODEL), BF16)] * 2,
        compiler_params=_params("parallel"),
        name="mem_kv",
    )(mem, g, w)


XATTN_ROWS = 1024


def _xattn_body(x_ref, ya_ref, yb_ref, yc_ref, yd_ref, wout_ref, g_ref, wq_ref, k_ref, v_ref, wo_ref,
                o_ref, q_scr, a_scr):
    x = x_ref[...]
    for gi, y_ref in enumerate((ya_ref, yb_ref, yc_ref, yd_ref)):
        x = x + _dot(y_ref[...], wout_ref[gi * GROUP:(gi + 1) * GROUP, :])
    u = _rms(x, g_ref[...]).astype(BF16)
    for c in range(D_MODEL // GROUP):
        sl = slice(c * GROUP, (c + 1) * GROUP)
        q_scr[:, sl] = _dot(u, wq_ref[:, sl]).astype(BF16)
    for h in range(MEM_HEADS):
        sl = slice(h * MEM_HEAD_DIM, (h + 1) * MEM_HEAD_DIM)
        s = _dot_nt(q_scr[:, sl], k_ref[:, sl]) * (MEM_HEAD_DIM ** -0.5)
        e = jnp.exp(s - jnp.max(s, axis=-1, keepdims=True))
        l = jnp.sum(e, axis=-1, keepdims=True)
        a_scr[:, sl] = (_dot(e.astype(BF16), v_ref[:, sl]) / l).astype(BF16)
    o_ref[...] = x + _dot(a_scr[...], wo_ref[...])


def _xattn(h, ys, w_out, g, wq, k, v, wo, *, batch, seq):
    h3 = h.reshape(batch, seq, D_MODEL)
    ys = [y.reshape(batch, seq, GROUP) for y in ys]
    k3 = k.reshape(batch, MEM_LEN, D_MODEL)
    v3 = v.reshape(batch, MEM_LEN, D_MODEL)
    row = pl.BlockSpec((None, XATTN_ROWS, D_MODEL), lambda b, i: (b, i, 0))
    grp = pl.BlockSpec((None, XATTN_ROWS, GROUP), lambda b, i: (b, i, 0))
    mem = pl.BlockSpec((None, MEM_LEN, D_MODEL), lambda b, i: (b, 0, 0))
    weight = _const_spec((D_MODEL, D_MODEL))
    out = pl.pallas_call(
        _xattn_body,
        grid=(batch, seq // XATTN_ROWS),
        in_specs=[row, grp, grp, grp, grp, weight, _const_spec((1, D_MODEL)), weight, mem, mem, weight],
        out_specs=row,
        out_shape=jax.ShapeDtypeStruct((batch, seq, D_MODEL), F32),
        scratch_shapes=[pltpu.VMEM((XATTN_ROWS, D_MODEL), BF16), pltpu.VMEM((XATTN_ROWS, D_MODEL), BF16)],
        compiler_params=_params("parallel", "parallel"),
        name="xattn",
    )(h3, *ys, w_out, g, wq, k3, v3, wo)
    return out.reshape(batch * seq, D_MODEL)


def _per_head_lanes(x):
    return jnp.repeat(x, HEAD_DIM, axis=-1)


def _in_proj_weight(w_in):
    g = GROUP
    q_a, k_a, v_a, o_a = (w_in[..., i * g:(i + 1) * g] for i in range(4))
    ig = w_in[..., 4 * g:4 * g + HEADS]
    fg = w_in[..., 4 * g + HEADS:4 * g + 2 * HEADS]
    rest = w_in[..., 4 * g + 2 * HEADS:]
    pool, q_c, k_c, v_c, q_d, k_d, v_d = (rest[..., i * g:(i + 1) * g] for i in range(7))
    q_c = q_c * DIL_SCORE_SCALE
    k_d = k_d * DIFF_SCORE_SCALE
    cols = [q_a, k_a, v_a, o_a, _per_head_lanes(ig), _per_head_lanes(fg), pool,
            q_c, k_c, v_c, q_d, k_d]
    return jnp.concatenate(cols, axis=-1).astype(BF16), v_d.astype(BF16)


def _block_diag(w):
    g, c, _ = w.shape
    eye = jnp.eye(g, dtype=w.dtype)
    return (eye[:, None, :, None] * w[:, :, None, :]).reshape(g * c, g * c)


def kernel(x, mem, t5_bias, ffn1_norm, ffn1_w_gate, ffn1_w_up, ffn1_w_down, mix_norm, w_in,
           mlstm_conv_w, mlstm_conv_b, mlstm_gate_b, mlstm_norm, pool_w, pool_scale,
           diff_lambda, diff_subln, w_out, xattn_norm, mem_norm, xattn_wq, xattn_wkv, xattn_wo,
           ffn2_norm, ffn2_w_gate, ffn2_w_up, ffn2_w_down, final_norm):
    batch, seq, _ = x.shape
    n = batch * seq
    dil_bias, diff_bias, diff_limit2 = _bias_tiles(t5_bias, seq)
    h = x.reshape(n, D_MODEL)
    mem2 = mem.reshape(batch * MEM_LEN, D_MODEL)
    row = lambda v: v.reshape(1, -1)
    w_proj, w_diff_v = _in_proj_weight(w_in)
    for l in range(DEPTH):
        lam_init = 0.8 - 0.6 * math.exp(-0.3 * l)
        h = _ffn(h, row(ffn1_norm[l]), ffn1_w_gate[l].astype(BF16), ffn1_w_up[l].astype(BF16),
                 ffn1_w_down[l].astype(BF16), row(final_norm), final=False)
        za, zg, zp, zc, zd, vt, zc4, zc16 = _in_proj(h, row(mix_norm[l]), w_proj[l], w_diff_v[l])
        ya = _mlstm(za, zg, mlstm_conv_w[l], row(mlstm_conv_b[l]),
                    row(_per_head_lanes(mlstm_gate_b[l].reshape(2, HEADS))), row(mlstm_norm[l]),
                    batch=batch, seq=seq)
        yb = _pool(zp, _block_diag(pool_w[l]).astype(BF16), row(pool_scale[l]), batch=batch, seq=seq)
        yc = _dilated((zc, zc4, zc16), dil_bias, batch=batch, seq=seq)
        yd = _diff_attention(zd, vt, diff_bias, diff_limit2, diff_lambda[l],
                             jnp.broadcast_to(diff_subln[l][:, None], (HEAD_DIM, DIFF_Q)),
                             lam_init=lam_init, batch=batch, seq=seq)
        k_mem, v_mem = _mem_kv(mem2, row(mem_norm[l]), xattn_wkv[l].astype(BF16))
        h = _xattn(h, (ya, yb, yc, yd), w_out[l].astype(BF16),
                   row(xattn_norm[l]), xattn_wq[l].astype(BF16), k_mem, v_mem,
                   xattn_wo[l].astype(BF16), batch=batch, seq=seq)
        h = _ffn(h, row(ffn2_norm[l]), ffn2_w_gate[l].astype(BF16), ffn2_w_up[l].astype(BF16),
                 ffn2_w_down[l].astype(BF16), row(final_norm), final=(l == DEPTH - 1))
    return h.reshape(batch, seq, D_MODEL)
```

```python
import functools
import math

import jax
import jax.numpy as jnp
import numpy as np
from jax import lax
from jax.experimental import pallas as pl
from jax.experimental.pallas import tpu as pltpu

F32 = jnp.float32
BF16 = jnp.bfloat16

D_MODEL = 1024
D_FF = 2816
DEPTH = 4
GROUP = 256
HEADS = 4
HEAD_DIM = GROUP // HEADS
MEM_LEN = 256
MEM_HEADS = 4
MEM_HEAD_DIM = D_MODEL // MEM_HEADS
MLSTM_CHUNK = 64
CONV_WIDTH = 4
POOL_WINDOWS = (2, 4, 8, 16)
DIL_PATTERNS = ((128, 1), (512, 4), (2048, 16))
DIL_BACK = 128
DIFF_QK_HALF = HEAD_DIM // 2
T5_BUCKETS = 32
T5_MAX_DIST = 2048
RMS_EPS = 1e-6
SUBLN_EPS = 1e-5
NEG = -1e30
LOG2E = math.log2(math.e)
DIFF_SCORE_SCALE = (DIFF_QK_HALF ** -0.5) * LOG2E
DIL_SCORE_SCALE = (HEAD_DIM ** -0.5) * LOG2E
AUG_ROWS = HEAD_DIM + 16
DIFF_MIN_OFFSET = -3
TILE = 128

VMEM_LIMIT_BYTES = 56 * 1024 * 1024


def _rms(xf, g, eps=RMS_EPS):
    return xf * lax.rsqrt(jnp.mean(xf * xf, axis=-1, keepdims=True) + eps) * g


def _const_spec(shape):
    zeros = (0,) * len(shape)
    return pl.BlockSpec(shape, lambda *_: zeros, pipeline_mode=pl.Buffered(1))


def _params(*sem):
    return pltpu.CompilerParams(dimension_semantics=sem, vmem_limit_bytes=VMEM_LIMIT_BYTES)


def _group_mask(rows, cols, row_group, col_group):
    r = lax.broadcasted_iota(jnp.int32, (rows, cols), 0) // row_group
    c = lax.broadcasted_iota(jnp.int32, (rows, cols), 1) // col_group
    return r == c


def _tile_rows(x, reps, mask):
    return jnp.where(mask, jnp.concatenate([x] * reps, axis=0), jnp.zeros((), x.dtype))


def _dot(a, b):
    return jnp.dot(a, b, preferred_element_type=F32)


def _dot_nt(a, b):
    return lax.dot_general(a, b, (((1,), (1,)), ((), ())), preferred_element_type=F32)


def _dot_tn(a, b):
    return lax.dot_general(a, b, (((0,), (0,)), ((), ())), preferred_element_type=F32)


def _group_sum(x, ones_bd):
    hi = x.astype(BF16)
    lo = (x - hi.astype(F32)).astype(BF16)
    return _dot(hi, ones_bd) + _dot(lo, ones_bd)


FFN_ROWS = 1024
FFN_COLS = 256


def _ffn_body(x_ref, g_ref, wg_ref, wu_ref, wd_ref, fg_ref, o_ref, act_ref, *, final):
    x = x_ref[...]
    u = _rms(x, g_ref[...]).astype(BF16)
    for c in range(D_FF // FFN_COLS):
        sl = slice(c * FFN_COLS, (c + 1) * FFN_COLS)
        gate = _dot(u, wg_ref[:, sl])
        up = _dot(u, wu_ref[:, sl])
        act_ref[:, sl] = (gate * jax.nn.sigmoid(gate) * up).astype(BF16)
    y = x + 0.5 * _dot(act_ref[...], wd_ref[...])
    if final:
        y = _rms(y, fg_ref[...])
    o_ref[...] = y


def _ffn(h, g, wg, wu, wd, fg, *, final):
    n = h.shape[0]
    row = pl.BlockSpec((FFN_ROWS, D_MODEL), lambda i: (i, 0))
    return pl.pallas_call(
        functools.partial(_ffn_body, final=final),
        grid=(n // FFN_ROWS,),
        in_specs=[row, _const_spec((1, D_MODEL)), _const_spec((D_MODEL, D_FF)),
                  _const_spec((D_MODEL, D_FF)), _const_spec((D_FF, D_MODEL)),
                  _const_spec((1, D_MODEL))],
        out_specs=row,
        out_shape=jax.ShapeDtypeStruct((n, D_MODEL), F32),
        scratch_shapes=[pltpu.VMEM((FFN_ROWS, D_FF), BF16)],
        compiler_params=_params("parallel"),
        name="ffn_final" if final else "ffn",
    )(h, g, wg, wu, wd, fg)


PROJ_ROWS = 1024
PROJ_OUTS = (("a", 4 * GROUP, F32), ("g", 2 * GROUP, F32), ("p", GROUP, F32),
             ("c", 3 * GROUP, BF16), ("d", 2 * GROUP, BF16))
PROJ_WIDTH = sum(w for _, w, _ in PROJ_OUTS)


PROJ_DILATIONS = tuple(d for _, d in DIL_PATTERNS if d > 1)


def _in_proj_body(x_ref, g_ref, w_ref, wv_ref, perm_ref, *o_refs):
    u = _rms(x_ref[...], g_ref[...]).astype(BF16)
    off = 0
    for o_ref, (_, width, dtype) in zip(o_refs, PROJ_OUTS):
        for c in range(width // GROUP):
            z = _dot(u, w_ref[:, off + c * GROUP: off + (c + 1) * GROUP])
            o_ref[:, c * GROUP:(c + 1) * GROUP] = z.astype(dtype)
        off += width
    vt_ref = o_refs[len(PROJ_OUTS)]
    v = _dot(u, wv_ref[...])
    for t in range(PROJ_ROWS // TILE):
        vt_ref[t] = v[t * TILE:(t + 1) * TILE].T.astype(BF16)
    for pi, d in enumerate(PROJ_DILATIONS):
        per_class = PROJ_UNIT // d
        for un in range(PROJ_ROWS // PROJ_UNIT):
            zp = _dot(perm_ref[pi], o_refs[3][un * PROJ_UNIT:(un + 1) * PROJ_UNIT, :]).astype(BF16)
            for r in range(d):
                o_refs[len(PROJ_OUTS) + 1 + pi][un * per_class:(un + 1) * per_class,
                                                r * 3 * GROUP:(r + 1) * 3 * GROUP] = \
                    zp[r * per_class:(r + 1) * per_class]


PROJ_UNIT = 2 * TILE


def _class_permutations():
    mats = np.zeros((len(PROJ_DILATIONS), PROJ_UNIT, PROJ_UNIT), np.float32)
    for pi, d in enumerate(PROJ_DILATIONS):
        t = np.arange(PROJ_UNIT)
        mats[pi, (t % d) * (PROJ_UNIT // d) + t // d, t] = 1.0
    return jnp.asarray(mats, BF16)


def _in_proj(h, g, w, wv):
    n = h.shape[0]
    tiles = PROJ_ROWS // TILE
    return pl.pallas_call(
        _in_proj_body,
        grid=(n // PROJ_ROWS,),
        in_specs=[pl.BlockSpec((PROJ_ROWS, D_MODEL), lambda i: (i, 0)),
                  _const_spec((1, D_MODEL)), _const_spec((D_MODEL, PROJ_WIDTH)),
                  _const_spec((D_MODEL, GROUP)),
                  _const_spec((len(PROJ_DILATIONS), PROJ_UNIT, PROJ_UNIT))],
        out_specs=[pl.BlockSpec((PROJ_ROWS, w_), lambda i: (i, 0)) for _, w_, _ in PROJ_OUTS]
        + [pl.BlockSpec((tiles, GROUP, TILE), lambda i: (i, 0, 0))]
        + [pl.BlockSpec((PROJ_ROWS // d, d * 3 * GROUP), lambda i: (i, 0)) for d in PROJ_DILATIONS],
        out_shape=[jax.ShapeDtypeStruct((n, w_), dt) for _, w_, dt in PROJ_OUTS]
        + [jax.ShapeDtypeStruct((n // TILE, GROUP, TILE), BF16)]
        + [jax.ShapeDtypeStruct((n // d, d * 3 * GROUP), BF16) for d in PROJ_DILATIONS],
        compiler_params=_params("parallel"),
        name="in_proj",
    )(h, g, w, wv, _class_permutations())


ML_ROWS = 1024
ML_HALO = 8


def _chunk_scan(x, rin, op, fill):
    s = 1
    while s < MLSTM_CHUNK:
        x = op(x, jnp.where(rin >= s, pltpu.roll(x, s, 0), fill))
        s *= 2
    return x


def _mlstm_body(za_ref, zg_ref, cw_ref, cb_ref, gb_ref, ng_ref, y_ref,
                buf_ref, tail_ref, hh_ref, c_ref, n_ref, m_ref):
    L = MLSTM_CHUNK

    @pl.when(pl.program_id(1) == 0)
    def _():
        tail_ref[...] = jnp.zeros_like(tail_ref)
        c_ref[...] = jnp.zeros_like(c_ref)
        n_ref[...] = jnp.zeros_like(n_ref)
        m_ref[...] = jnp.zeros_like(m_ref)

    buf_ref[0:ML_HALO, :] = tail_ref[...]
    buf_ref[ML_HALO:, :] = za_ref[:, 0:2 * GROUP]
    tail_ref[...] = za_ref[ML_ROWS - ML_HALO:, 0:2 * GROUP]
    conv = cb_ref[...]
    for j in range(CONV_WIDTH):
        conv = conv + buf_ref[pl.ds(ML_HALO - (CONV_WIDTH - 1) + j, ML_ROWS), :] * cw_ref[j:j + 1, :]
    qk = conv * jax.nn.sigmoid(conv)
    q = qk[:, :GROUP]
    k = qk[:, GROUP:] * (HEAD_DIM ** -0.5)
    v = za_ref[:, 2 * GROUP:3 * GROUP].astype(BF16)
    q_bf, k_bf = q.astype(BF16), k.astype(BF16)

    ii = zg_ref[:, :GROUP] + gb_ref[:, :GROUP]
    fx = zg_ref[:, GROUP:] + gb_ref[:, GROUP:]
    lf = jnp.minimum(fx, 0.0) - jnp.log(1.0 + jnp.exp(-jnp.abs(fx)))
    rin = lax.broadcasted_iota(jnp.int32, (ML_ROWS, GROUP), 0) % L
    b = _chunk_scan(lf, rin, jnp.add, 0.0)
    a = ii - b
    ca = _chunk_scan(a, rin, jnp.maximum, NEG)

    bd = _group_mask(GROUP, GROUP, HEAD_DIM, HEAD_DIM)
    ones_bd = bd.astype(BF16)
    row = lax.broadcasted_iota(jnp.int32, (L, GROUP), 0)
    key = lax.broadcasted_iota(jnp.int32, (L, GROUP), 1) % L
    causal = key <= row
    diag = key == row

    m_prev = m_ref[...]
    for c in range(ML_ROWS // L):
        rs = slice(c * L, (c + 1) * L)
        q_c, k_c, v_c = q[rs], k[rs], v[rs]
        q_b = q_bf[rs]
        a_c, b_c = a[rs], b[rs]
        g = jnp.maximum(m_prev, ca[rs])
        g_last = g[L - 1:L]
        a_row = jnp.sum(jnp.where(diag, a_c, 0.0), axis=0, keepdims=True)
        decay = jnp.exp(jnp.where(causal, a_row - g, NEG))
        sc = _dot_nt(q_b, _tile_rows(k_bf[rs], HEADS, bd)) * decay
        inter = jnp.exp(m_prev - g)
        num = inter * _dot(q_b, c_ref[...].astype(BF16)) + _dot(sc.astype(BF16), _tile_rows(v_c, HEADS, bd))
        den = inter * _group_sum(q_c * n_ref[...], ones_bd) + _group_sum(sc, ones_bd)
        hh_ref[rs, :] = num / jnp.maximum(jnp.abs(den), jnp.exp(-(b_c + g)))
        kw = k_c * jnp.exp(a_c - g_last)
        carry = jnp.exp(m_prev - g_last)
        c_ref[...] = carry * c_ref[...] + jnp.where(bd, _dot_tn(kw.astype(BF16), v_c), 0.0)
        n_ref[...] = carry * n_ref[...] + jnp.sum(kw, axis=0, keepdims=True)
        m_prev = b_c[L - 1:L] + g_last
    m_ref[...] = m_prev

    hh = hh_ref[...]
    mu = _group_sum(hh, ones_bd) * (1.0 / HEAD_DIM)
    dev = hh - mu
    var = _group_sum(dev * dev, ones_bd) * (1.0 / HEAD_DIM)
    o_gate = jax.nn.sigmoid(za_ref[:, 3 * GROUP:])
    y_ref[...] = (dev * lax.rsqrt(var + RMS_EPS) * ng_ref[...] * o_gate).astype(BF16)


def _mlstm(za, zg, conv_w, conv_b, gate_b, norm_g, *, batch, seq):
    za = za.reshape(batch, seq, 4 * GROUP)
    zg = zg.reshape(batch, seq, 2 * GROUP)
    y = pl.pallas_call(
        _mlstm_body,
        grid=(batch, seq // ML_ROWS),
        in_specs=[pl.BlockSpec((None, ML_ROWS, 4 * GROUP), lambda b, j: (b, j, 0)),
                  pl.BlockSpec((None, ML_ROWS, 2 * GROUP), lambda b, j: (b, j, 0)),
                  _const_spec((CONV_WIDTH, 2 * GROUP)), _const_spec((1, 2 * GROUP)),
                  _const_spec((1, 2 * GROUP)), _const_spec((1, GROUP))],
        out_specs=pl.BlockSpec((None, ML_ROWS, GROUP), lambda b, j: (b, j, 0)),
        out_shape=jax.ShapeDtypeStruct((batch, seq, GROUP), BF16),
        scratch_shapes=[pltpu.VMEM((ML_ROWS + ML_HALO, 2 * GROUP), F32),
                        pltpu.VMEM((ML_HALO, 2 * GROUP), F32),
                        pltpu.VMEM((ML_ROWS, GROUP), F32),
                        pltpu.VMEM((GROUP, GROUP), F32),
                        pltpu.VMEM((1, GROUP), F32),
                        pltpu.VMEM((1, GROUP), F32)],
        compiler_params=_params("parallel", "arbitrary"),
        name="mlstm",
    )(za, zg, conv_w, conv_b, gate_b, norm_g)
    return y.reshape(batch * seq, GROUP)


POOL_ROWS = 2048
POOL_HALO = 16


def _pool_body(u_ref, w_ref, s_ref, y_ref, buf_ref, tail_ref):
    j = pl.program_id(1)

    @pl.when(j == 0)
    def _():
        tail_ref[...] = jnp.zeros_like(tail_ref)

    buf_ref[0:POOL_HALO, :] = tail_ref[...]
    buf_ref[POOL_HALO:, :] = u_ref[...]
    tail_ref[...] = u_ref[POOL_ROWS - POOL_HALO:, :]
    sums, s = [], buf_ref[...]
    for shift in (1, 2, 4, 8):
        s = s + pltpu.roll(s, shift, 0)
        sums.append(s[POOL_HALO:])
    u = u_ref[...]
    lane_group = lax.broadcasted_iota(jnp.int32, (POOL_ROWS, GROUP), 1) // HEAD_DIM
    t = j * POOL_ROWS + lax.broadcasted_iota(jnp.int32, (POOL_ROWS, GROUP), 0)
    total, win = sums[3], jnp.full((POOL_ROWS, GROUP), POOL_WINDOWS[3], jnp.int32)
    for gi in (2, 1, 0):
        total = jnp.where(lane_group == gi, sums[gi], total)
        win = jnp.where(lane_group == gi, POOL_WINDOWS[gi], win)
    mean = total / jnp.minimum(t + 1, win).astype(F32)
    y = _dot((mean - u).astype(BF16), w_ref[...]) * s_ref[...]
    y_ref[...] = y.astype(BF16)


def _pool(zp, w_bd, scale, *, batch, seq):
    zp = zp.reshape(batch, seq, GROUP)
    y = pl.pallas_call(
        _pool_body,
        grid=(batch, seq // POOL_ROWS),
        in_specs=[pl.BlockSpec((None, POOL_ROWS, GROUP), lambda b, j: (b, j, 0)),
                  _const_spec((GROUP, GROUP)), _const_spec((1, GROUP))],
        out_specs=pl.BlockSpec((None, POOL_ROWS, GROUP), lambda b, j: (b, j, 0)),
        out_shape=jax.ShapeDtypeStruct((batch, seq, GROUP), BF16),
        scratch_shapes=[pltpu.VMEM((POOL_ROWS + POOL_HALO, GROUP), F32),
                        pltpu.VMEM((POOL_HALO, GROUP), F32)],
        compiler_params=_params("parallel", "arbitrary"),
        name="pool",
    )(zp, w_bd, scale)
    return y.reshape(batch * seq, GROUP)


TOEPLITZ_BATCH = 4


def _toeplitz_body(w_ref, o_ref):
    for t in range(TOEPLITZ_BATCH):
        x = jnp.broadcast_to(w_ref[t], (TILE, 2 * TILE))
        o_ref[t] = pltpu.roll(x, 0, 1, stride=1, stride_axis=0)[:, :TILE]


def _toeplitz(rows):
    n = rows.shape[0]
    assert n % TOEPLITZ_BATCH == 0
    return pl.pallas_call(
        _toeplitz_body,
        grid=(n // TOEPLITZ_BATCH,),
        in_specs=[pl.BlockSpec((TOEPLITZ_BATCH, 1, 2 * TILE), lambda i: (i, 0, 0))],
        out_specs=pl.BlockSpec((TOEPLITZ_BATCH, TILE, TILE), lambda i: (i, 0, 0)),
        out_shape=jax.ShapeDtypeStruct((n, TILE, TILE), F32),
        compiler_params=_params("parallel"),
        name="toeplitz",
    )(rows.reshape(n, 1, 2 * TILE))


_TOEPLITZ_X = np.where(np.arange(2 * TILE) <= TILE, -np.arange(2 * TILE), 2 * TILE - np.arange(2 * TILE))


def _t5_bucket(dist):
    max_exact = T5_BUCKETS // 2
    d = jnp.maximum(dist, 1).astype(F32)
    large = max_exact + (jnp.log(d / max_exact) / math.log(T5_MAX_DIST / max_exact)
                         * (T5_BUCKETS - max_exact)).astype(jnp.int32)
    large = jnp.minimum(large, T5_BUCKETS - 1)
    return jnp.where(dist < max_exact, dist, large)


def _bucket_rows(table, dist):
    onehot = _t5_bucket(dist)[..., None, None] == jnp.arange(T5_BUCKETS)[:, None]
    return jnp.sum(jnp.where(onehot, table, 0.0), axis=-2)


def _bias_tiles(t5_bias, seq):
    x = jnp.asarray(_TOEPLITZ_X, jnp.int32)
    table = t5_bias * LOG2E
    delta = jnp.clip(jnp.stack([-x, DIL_BACK - x]), 0, DIL_BACK)
    valid = jnp.stack([x <= 0, x >= 0])[None, :, :, None]
    rows = _bucket_rows(table[:, :HEADS], jnp.stack([delta * d for _, d in DIL_PATTERNS]))
    rows = jnp.where(valid, rows, NEG).transpose(0, 3, 1, 2)
    dil = _toeplitz(rows.reshape(-1, 2 * TILE))
    dil = dil.reshape(len(DIL_PATTERNS), HEADS, 2, TILE, TILE).transpose(0, 1, 3, 2, 4)
    dil = dil.reshape(len(DIL_PATTERNS), HEADS * TILE, 2 * TILE)
    noff = seq // TILE - DIFF_MIN_OFFSET
    dist = (jnp.arange(noff)[:, None] + DIFF_MIN_OFFSET) * TILE - x[None, :]
    rows = _bucket_rows(table[:, HEADS:], jnp.clip(dist, 0, seq - 1))
    rows = jnp.where((dist >= 0)[:, :, None], rows, NEG).transpose(0, 2, 1)
    diff = _toeplitz(rows.reshape(-1, 2 * TILE))
    room = jnp.maximum(DIFF_SAFE_LOG2 - jnp.max(jnp.abs(table[:, HEADS:])), 0.0)
    return dil, diff.reshape(noff, HEADS * TILE, TILE), (room * room).reshape(1, 1)


DIL_UNIT = 2 * TILE


def _dil_body(*refs, subs, dil, merge, has_prev):
    if merge:
        x_ref, xp_ref, bias_ref, unperm_ref, o1_ref, l1_ref, o2_ref, l2_ref, y_ref = refs
    else:
        x_ref, xp_ref, bias_ref, unperm_ref, o_ref, lse_ref = refs
    per_class = [_dil_class(x_ref, xp_ref, bias_ref, rc=rc, subs=subs, has_prev=has_prev)
                 for rc in range(dil)]
    piece = DIL_UNIT // dil
    class_o = [jnp.concatenate(pc[0], axis=0) for pc in per_class]
    class_l = [jnp.concatenate(pc[1], axis=0) for pc in per_class]
    for u in range(dil * subs * TILE // DIL_UNIT):
        take = lambda arrs: jnp.concatenate([a[u * piece:(u + 1) * piece] for a in arrs], axis=0)
        if dil == 1:
            o, lse = take(class_o), take(class_l)
        else:
            o_c = take(class_o).astype(BF16)
            l_c = take(class_l)
            l_hi = l_c.astype(BF16)
            l_lo = (l_c - l_hi.astype(F32)).astype(BF16)
            o = _dot(unperm_ref[...], o_c)
            lse = _dot(unperm_ref[...], l_hi) + _dot(unperm_ref[...], l_lo)
        rs = slice(u * DIL_UNIT, (u + 1) * DIL_UNIT)
        if merge:
            o1, l1 = o1_ref[rs, :].astype(F32), l1_ref[rs, :]
            o2, l2 = o2_ref[rs, :].astype(F32), l2_ref[rs, :]
            top = jnp.maximum(jnp.maximum(l1, l2), lse)
            w1, w2, w3 = jnp.exp2(l1 - top), jnp.exp2(l2 - top), jnp.exp2(lse - top)
            y_ref[rs, :] = ((w1 * o1 + w2 * o2 + w3 * o) / (w1 + w2 + w3)).astype(BF16)
        else:
            o_ref[rs, :] = o.astype(BF16)
            lse_ref[rs, :] = lse


def _dil_class(x_ref, xp_ref, bias_ref, *, rc, subs, has_prev):
    first = pl.program_id(1) == 0
    kmask = _group_mask(HEADS * TILE, GROUP, TILE, HEAD_DIM)
    ones_rows = (lax.broadcasted_iota(jnp.int32, (AUG_ROWS - HEAD_DIM, TILE), 0) == 0).astype(BF16)
    rows_of = lambda sb: slice(sb * TILE, (sb + 1) * TILE)
    col_q, col_k, col_v = (slice((3 * rc + w) * GROUP, (3 * rc + w + 1) * GROUP) for w in range(3))

    s_same, s_next, vaug = {}, {}, {}
    first_block = -1 if has_prev else 0
    for j in range(first_block, subs):
        k_j = xp_ref[:, col_k] if j < 0 else x_ref[rows_of(j), col_k]
        v_j = xp_ref[:, col_v] if j < 0 else x_ref[rows_of(j), col_v]
        parts = ([0] if j >= 0 else []) + ([1] if j + 1 < subs else [])
        q_cat = jnp.concatenate([x_ref[rows_of(j + e), col_q] for e in parts], axis=0)
        bias = bias_ref[:, parts[0] * TILE:(parts[-1] + 1) * TILE]
        st = _dot_nt(_tile_rows(k_j, HEADS, kmask), q_cat) + bias
        if j < 0:
            st = st + jnp.where(first, NEG, 0.0)
        for pos, e in enumerate(parts):
            (s_same if e == 0 else s_next)[j + e] = st[:, pos * TILE:(pos + 1) * TILE]
        v_t = v_j.astype(F32).T.astype(BF16)
        vaug[j] = [jnp.concatenate([v_t[h * HEAD_DIM:(h + 1) * HEAD_DIM], ones_rows], axis=0)
                   for h in range(HEADS)]

    p_same, p_next, tops = {}, {}, {}
    for i in range(subs):
        ps, pn, tp = [], [], []
        for h in range(HEADS):
            hs = slice(h * TILE, (h + 1) * TILE)
            a = s_same[i][hs]
            m = jnp.max(a, axis=0, keepdims=True)
            if i in s_next:
                b = s_next[i][hs]
                m = jnp.maximum(m, jnp.max(b, axis=0, keepdims=True))
                pn.append(jnp.exp2(b - m).astype(BF16))
            ps.append(jnp.exp2(a - m).astype(BF16))
            tp.append(m)
        p_same[i], p_next[i], tops[i] = ps, pn, tp

    acc = {i: [None] * HEADS for i in range(subs)}
    for j in range(first_block, subs):
        for h in range(HEADS):
            cols = ([p_same[j][h]] if j >= 0 else []) + ([p_next[j + 1][h]] if j + 1 < subs else [])
            r = _dot(vaug[j][h], jnp.concatenate(cols, axis=1))
            targets = ([j] if j >= 0 else []) + ([j + 1] if j + 1 < subs else [])
            for pos, i in enumerate(targets):
                part = r[:, pos * TILE:(pos + 1) * TILE]
                acc[i][h] = part if acc[i][h] is None else acc[i][h] + part

    outs, lses = [], []
    for i in range(subs):
        o_t, lse_t = [], []
        for h in range(HEADS):
            l = acc[i][h][HEAD_DIM:HEAD_DIM + 1]
            o_t.append(acc[i][h][:HEAD_DIM] / l)
            lse_t.append(jnp.broadcast_to(tops[i][h] + jnp.log2(l), (HEAD_DIM, TILE)))
        outs.append(jnp.concatenate(o_t, axis=0).T)
        lses.append(jnp.concatenate(lse_t, axis=0).T)
    return outs, lses


DIL_SUBBLOCKS = 16


def _unpermutation(dil):
    t = np.arange(DIL_UNIT)
    mat = np.zeros((DIL_UNIT, DIL_UNIT), np.float32)
    mat[t, (t % dil) * (DIL_UNIT // dil) + t // dil] = 1.0
    return jnp.asarray(mat, BF16)


def _dilated_pattern(zc, bias, dil, *, batch, seq, merge_with=None):
    length = seq // dil
    subs = min(max(DIL_SUBBLOCKS // max(dil // 2, 1), 2), length // TILE)
    rows = subs * TILE
    tokens = rows * dil
    zc = zc.reshape(batch, length, dil * 3 * GROUP)
    blk = pl.BlockSpec((None, rows, dil * 3 * GROUP), lambda b, n: (b, n, 0))
    prev = pl.BlockSpec((None, TILE, dil * 3 * GROUP), lambda b, n: (b, jnp.maximum(n * subs - 1, 0), 0))
    nat = pl.BlockSpec((None, tokens, GROUP), lambda b, n: (b, n, 0))
    in_specs = [blk, prev, _const_spec((HEADS * TILE, 2 * TILE)), _const_spec((DIL_UNIT, DIL_UNIT))]
    args = [zc, zc, bias, _unpermutation(dil)]
    if merge_with is None:
        out_specs = [nat, nat]
        out_shape = [jax.ShapeDtypeStruct((batch, seq, GROUP), BF16),
                     jax.ShapeDtypeStruct((batch, seq, GROUP), F32)]
    else:
        in_specs += [nat] * len(merge_with)
        args += list(merge_with)
        out_specs = nat
        out_shape = jax.ShapeDtypeStruct((batch, seq, GROUP), BF16)
    out = pl.pallas_call(
        functools.partial(_dil_body, subs=subs, dil=dil, merge=merge_with is not None,
                          has_prev=length > rows),
        grid=(batch, length // rows),
        in_specs=in_specs, out_specs=out_specs, out_shape=out_shape,
        compiler_params=_params("parallel", "parallel"),
        name=f"dilated_d{dil}",
    )(*args)
    if merge_with is None:
        return out
    return out.reshape(batch * seq, GROUP)


def _dilated(zc_views, dil_bias, *, batch, seq):
    o1, l1 = _dilated_pattern(zc_views[0], dil_bias[0], DIL_PATTERNS[0][1], batch=batch, seq=seq)
    o2, l2 = _dilated_pattern(zc_views[1], dil_bias[1], DIL_PATTERNS[1][1], batch=batch, seq=seq)
    return _dilated_pattern(zc_views[2], dil_bias[2], DIL_PATTERNS[2][1], batch=batch, seq=seq,
                            merge_with=(o1, l1, o2, l2))


DIFF_Q = 512
DIFF_K = 256
DIFF_GROUPS = 2 * HEADS


DIFF_SAFE_LOG2 = 60.0
DIFF_BOUND_SLACK = 1.01


def _diff_running_max(q, qi, last, pairs, bias_tiles, kexp_ref, vaug_ref, acc_ref, sta_ref, stb_ref):
    def scores(s_ref, j):
        bias = bias_tiles(j)
        raw = _dot_nt(kexp_ref[jnp.minimum(j, last)], q)
        tops = []
        for g in range(DIFF_GROUPS):
            s = raw[g * DIFF_K:(g + 1) * DIFF_K] + bias[g % HEADS]
            s_ref[g * DIFF_K:(g + 1) * DIFF_K, :] = s
            tops.append(jnp.max(s, axis=0, keepdims=True))
        return tuple(tops)

    def consume(s_ref, tops, j, carry):
        ms, ls = carry
        jv = jnp.minimum(j, last)
        new_ms, new_ls = [], []
        for g in range(DIFF_GROUPS):
            mp, h = divmod(g, HEADS)
            m_new = jnp.maximum(ms[g], tops[g])
            p = jnp.exp2(s_ref[g * DIFF_K:(g + 1) * DIFF_K, :] - m_new).astype(BF16)
            alpha = jnp.exp2(ms[g] - m_new)
            r = _dot(vaug_ref[jv, h], p)
            acc_ref[mp, h] = alpha * acc_ref[mp, h] + r[:HEAD_DIM]
            new_ls.append(alpha * ls[g] + r[HEAD_DIM:HEAD_DIM + 1])
            new_ms.append(m_new)
        return tuple(new_ms), tuple(new_ls)

    def pair(jj, carry):
        tops_a, state = carry
        j = 2 * jj
        tops_b = scores(stb_ref, j + 1)
        state = consume(sta_ref, tops_a, j, state)
        tops_a = scores(sta_ref, j + 2)
        return tops_a, consume(stb_ref, tops_b, j + 1, state)

    init = (tuple(jnp.full((1, DIFF_Q), NEG, F32) for _ in range(DIFF_GROUPS)),
            tuple(jnp.zeros((1, DIFF_Q), F32) for _ in range(DIFF_GROUPS)))
    _, (_, ls) = lax.fori_loop(0, pairs, pair, (scores(sta_ref, 0), init))
    return ls


def _diff_body(q_ref, k_ref, vt_ref, bias_ref, lam_ref, sg_ref, lim_ref, y_ref,
               kexp_ref, vaug_ref, acc_ref, sta_ref, stb_ref, pa_ref, pb_ref, knorm_ref,
               *, lam_init, key_steps):
    qi = pl.program_id(1)
    qk_group = _group_mask(GROUP, GROUP, DIFF_QK_HALF, DIFF_QK_HALF).astype(BF16)

    @pl.when(qi == 0)
    def _():
        grp = lax.broadcasted_iota(jnp.int32, (DIFF_GROUPS * DIFF_K, GROUP), 0) // DIFF_K
        slot = lax.broadcasted_iota(jnp.int32, (DIFF_GROUPS * DIFF_K, GROUP), 1) // DIFF_QK_HALF
        kmask = slot == 2 * (grp % HEADS) + grp // HEADS
        ones_rows = (lax.broadcasted_iota(jnp.int32, (AUG_ROWS - HEAD_DIM, DIFF_K), 0) == 0).astype(BF16)

        def build(j, kmax):
            k_t = k_ref[pl.ds(pl.multiple_of(j * DIFF_K, DIFF_K), DIFF_K), :]
            kexp_ref[j] = jnp.where(kmask, jnp.concatenate([k_t] * DIFF_GROUPS, axis=0),
                                    jnp.zeros((), BF16))
            vt = jnp.concatenate([vt_ref[2 * j], vt_ref[2 * j + 1]], axis=1)
            for h in range(HEADS):
                vaug_ref[j, h] = jnp.concatenate([vt[h * HEAD_DIM:(h + 1) * HEAD_DIM], ones_rows], axis=0)
            k_f = k_t.astype(F32)
            return jnp.maximum(kmax, jnp.max(_group_sum(k_f * k_f, qk_group), axis=0, keepdims=True))

        knorm_ref[...] = lax.fori_loop(0, key_steps, build, jnp.zeros((1, GROUP), F32))

    acc_ref[...] = jnp.zeros_like(acc_ref)
    q = q_ref[...]
    last = key_steps - 1
    key_steps_needed = (qi + 1) * (DIFF_Q // DIFF_K)
    pairs = (key_steps_needed + 1) // 2

    def bias_tiles(j):
        base = (DIFF_Q // TILE) * qi - (DIFF_K // TILE) * j - DIFF_MIN_OFFSET
        tiles = {d: bias_ref[jnp.maximum(base + d, 0)]
                 for d in range(1 - DIFF_K // TILE, DIFF_Q // TILE)}
        return [jnp.concatenate(
            [jnp.concatenate([tiles[a - b][h * TILE:(h + 1) * TILE] for a in range(DIFF_Q // TILE)], axis=1)
             for b in range(DIFF_K // TILE)], axis=0) for h in range(HEADS)]

    q_f = q.astype(F32)
    bound2 = jnp.max(_group_sum(q_f * q_f, qk_group) * knorm_ref[...])
    no_overflow = bound2 * DIFF_BOUND_SLACK <= lim_ref[0, 0]

    def unshifted():
        def weights(p_ref, j):
            bias = bias_tiles(j)
            raw = _dot_nt(kexp_ref[jnp.minimum(j, last)], q)
            for g in range(DIFF_GROUPS):
                rows = slice(g * DIFF_K, (g + 1) * DIFF_K)
                p_ref[rows, :] = jnp.exp2(raw[rows] + bias[g % HEADS]).astype(BF16)

        def accumulate(p_ref, j, ls):
            jv = jnp.minimum(j, last)
            new_ls = []
            for g in range(DIFF_GROUPS):
                mp, h = divmod(g, HEADS)
                r = _dot(vaug_ref[jv, h], p_ref[g * DIFF_K:(g + 1) * DIFF_K, :])
                acc_ref[mp, h] = acc_ref[mp, h] + r[:HEAD_DIM]
                new_ls.append(ls[g] + r[HEAD_DIM:HEAD_DIM + 1])
            return tuple(new_ls)

        def pair(jj, ls):
            j = 2 * jj
            weights(pb_ref, j + 1)
            ls = accumulate(pa_ref, j, ls)
            weights(pa_ref, j + 2)
            return accumulate(pb_ref, j + 1, ls)

        weights(pa_ref, 0)
        return lax.fori_loop(0, pairs, pair,
                             tuple(jnp.zeros((1, DIFF_Q), F32) for _ in range(DIFF_GROUPS)))

    def running_max():
        return _diff_running_max(q, qi, last, pairs, bias_tiles, kexp_ref, vaug_ref, acc_ref,
                                 sta_ref, stb_ref)

    ls = lax.cond(no_overflow, unshifted, running_max)

    lv = lam_ref[...]
    lam = (jnp.exp(jnp.sum(lv[0:1] * lv[1:2], axis=-1, keepdims=True))
           - jnp.exp(jnp.sum(lv[2:3] * lv[3:4], axis=-1, keepdims=True)) + lam_init)
    outs = []
    for h in range(HEADS):
        o = acc_ref[0, h] / ls[h] - lam * (acc_ref[1, h] / ls[HEADS + h])
        ms_o = jnp.mean(o * o, axis=0, keepdims=True)
        outs.append(o * lax.rsqrt(ms_o + SUBLN_EPS) * sg_ref[...] * (1.0 - lam_init))
    y_ref[...] = jnp.concatenate(outs, axis=0).T.astype(BF16)


def _diff_attention(zd, vt, bias, score_limit2, lam_vecs, subln_cols, *, lam_init, batch, seq):
    zd = zd.reshape(batch, seq, 2 * GROUP)
    key_tiles = seq // TILE
    key_steps = seq // DIFF_K
    vt = vt.reshape(batch, key_tiles, GROUP, TILE)
    y = pl.pallas_call(
        functools.partial(_diff_body, lam_init=lam_init, key_steps=key_steps),
        grid=(batch, seq // DIFF_Q),
        in_specs=[pl.BlockSpec((None, DIFF_Q, GROUP), lambda b, i: (b, i, 0)),
                  pl.BlockSpec((None, seq, GROUP), lambda b, i: (b, 0, 1), pipeline_mode=pl.Buffered(1)),
                  pl.BlockSpec((None, key_tiles, GROUP, TILE), lambda b, i: (b, 0, 0, 0),
                               pipeline_mode=pl.Buffered(1)),
                  _const_spec((key_tiles - DIFF_MIN_OFFSET, HEADS * TILE, TILE)),
                  _const_spec((4, DIFF_QK_HALF)), _const_spec((HEAD_DIM, DIFF_Q)),
                  pl.BlockSpec(memory_space=pltpu.SMEM)],
        out_specs=pl.BlockSpec((None, DIFF_Q, GROUP), lambda b, i: (b, i, 0)),
        out_shape=jax.ShapeDtypeStruct((batch, seq, GROUP), BF16),
        scratch_shapes=[pltpu.VMEM((key_steps, DIFF_GROUPS * DIFF_K, GROUP), BF16),
                        pltpu.VMEM((key_steps, HEADS, AUG_ROWS, DIFF_K), BF16),
                        pltpu.VMEM((2, HEADS, HEAD_DIM, DIFF_Q), F32),
                        pltpu.VMEM((DIFF_GROUPS * DIFF_K, DIFF_Q), F32),
                        pltpu.VMEM((DIFF_GROUPS * DIFF_K, DIFF_Q), F32),
                        pltpu.VMEM((DIFF_GROUPS * DIFF_K, DIFF_Q), BF16),
                        pltpu.VMEM((DIFF_GROUPS * DIFF_K, DIFF_Q), BF16),
                        pltpu.VMEM((1, GROUP), F32)],
        compiler_params=_params("parallel", "arbitrary"),
        name="diff_attn",
    )(zd, zd, vt, bias, lam_vecs, subln_cols, score_limit2)
    return y.reshape(batch * seq, GROUP)


KV_ROWS = 512


def _mem_kv_body(m_ref, g_ref, w_ref, k_ref, v_ref):
    u = _rms(m_ref[...], g_ref[...]).astype(BF16)
    for c in range(D_MODEL // GROUP):
        sl = slice(c * GROUP, (c + 1) * GROUP)
        k_ref[:, sl] = _dot(u, w_ref[:, sl]).astype(BF16)
        v_ref[:, sl] = _dot(u, w_ref[:, D_MODEL + c * GROUP: D_MODEL + (c + 1) * GROUP]).astype(BF16)


def _mem_kv(mem, g, w):
    n = mem.shape[0]
    row = pl.BlockSpec((KV_ROWS, D_MODEL), lambda i: (i, 0))
    return pl.pallas_call(
        _mem_kv_body,
        grid=(n // KV_ROWS,),
        in_specs=[row, _const_spec((1, D_MODEL)), _const_spec((D_MODEL, 2 * D_MODEL))],
        out_specs=[row, row],
        out_shape=[jax.ShapeDtypeStruct((n, D_MODEL), BF16)] * 2,
        compiler_params=_params("parallel"),
        name="mem_kv",
    )(mem, g, w)


XATTN_ROWS = 1024


def _xattn_body(x_ref, ya_ref, yb_ref, yc_ref, yd_ref, wout_ref, g_ref, wq_ref, k_ref, v_ref, wo_ref,
                o_ref, q_scr, a_scr):
    x = x_ref[...]
    for gi, y_ref in enumerate((ya_ref, yb_ref, yc_ref, yd_ref)):
        x = x + _dot(y_ref[...], wout_ref[gi * GROUP:(gi + 1) * GROUP, :])
    u = _rms(x, g_ref[...]).astype(BF16)
    for c in range(D_MODEL // GROUP):
        sl = slice(c * GROUP, (c + 1) * GROUP)
        q_scr[:, sl] = _dot(u, wq_ref[:, sl]).astype(BF16)
    for h in range(MEM_HEADS):
        sl = slice(h * MEM_HEAD_DIM, (h + 1) * MEM_HEAD_DIM)
        s = _dot_nt(q_scr[:, sl], k_ref[:, sl]) * (MEM_HEAD_DIM ** -0.5)
        e = jnp.exp(s - jnp.max(s, axis=-1, keepdims=True))
        l = jnp.sum(e, axis=-1, keepdims=True)
        a_scr[:, sl] = (_dot(e.astype(BF16), v_ref[:, sl]) / l).astype(BF16)
    o_ref[...] = x + _dot(a_scr[...], wo_ref[...])


def _xattn(h, ys, w_out, g, wq, k, v, wo, *, batch, seq):
    h3 = h.reshape(batch, seq, D_MODEL)
    ys = [y.reshape(batch, seq, GROUP) for y in ys]
    k3 = k.reshape(batch, MEM_LEN, D_MODEL)
    v3 = v.reshape(batch, MEM_LEN, D_MODEL)
    row = pl.BlockSpec((None, XATTN_ROWS, D_MODEL), lambda b, i: (b, i, 0))
    grp = pl.BlockSpec((None, XATTN_ROWS, GROUP), lambda b, i: (b, i, 0))
    mem = pl.BlockSpec((None, MEM_LEN, D_MODEL), lambda b, i: (b, 0, 0))
    weight = _const_spec((D_MODEL, D_MODEL))
    out = pl.pallas_call(
        _xattn_body,
        grid=(batch, seq // XATTN_ROWS),
        in_specs=[row, grp, grp, grp, grp, weight, _const_spec((1, D_MODEL)), weight, mem, mem, weight],
        out_specs=row,
        out_shape=jax.ShapeDtypeStruct((batch, seq, D_MODEL), F32),
        scratch_shapes=[pltpu.VMEM((XATTN_ROWS, D_MODEL), BF16), pltpu.VMEM((XATTN_ROWS, D_MODEL), BF16)],
        compiler_params=_params("parallel", "parallel"),
        name="xattn",
    )(h3, *ys, w_out, g, wq, k3, v3, wo)
    return out.reshape(batch * seq, D_MODEL)


def _per_head_lanes(x):
    return jnp.repeat(x, HEAD_DIM, axis=-1)


def _in_proj_weight(w_in):
    g = GROUP
    q_a, k_a, v_a, o_a = (w_in[:, i * g:(i + 1) * g] for i in range(4))
    ig = w_in[:, 4 * g:4 * g + HEADS]
    fg = w_in[:, 4 * g + HEADS:4 * g + 2 * HEADS]
    rest = w_in[:, 4 * g + 2 * HEADS:]
    pool, q_c, k_c, v_c, q_d, k_d, v_d = (rest[:, i * g:(i + 1) * g] for i in range(7))
    q_c = q_c * DIL_SCORE_SCALE
    k_d = k_d * DIFF_SCORE_SCALE
    cols = [q_a, k_a, v_a, o_a, _per_head_lanes(ig), _per_head_lanes(fg), pool,
            q_c, k_c, v_c, q_d, k_d]
    return jnp.concatenate(cols, axis=1).astype(BF16), v_d.astype(BF16)


def _block_diag(w):
    g, c, _ = w.shape
    eye = jnp.eye(g, dtype=w.dtype)
    return (eye[:, None, :, None] * w[:, :, None, :]).reshape(g * c, g * c)


def kernel(x, mem, t5_bias, ffn1_norm, ffn1_w_gate, ffn1_w_up, ffn1_w_down, mix_norm, w_in,
           mlstm_conv_w, mlstm_conv_b, mlstm_gate_b, mlstm_norm, pool_w, pool_scale,
           diff_lambda, diff_subln, w_out, xattn_norm, mem_norm, xattn_wq, xattn_wkv, xattn_wo,
           ffn2_norm, ffn2_w_gate, ffn2_w_up, ffn2_w_down, final_norm):
    batch, seq, _ = x.shape
    n = batch * seq
    dil_bias, diff_bias, diff_limit2 = _bias_tiles(t5_bias, seq)
    h = x.reshape(n, D_MODEL)
    mem2 = mem.reshape(batch * MEM_LEN, D_MODEL)
    row = lambda v: v.reshape(1, -1)
    for l in range(DEPTH):
        lam_init = 0.8 - 0.6 * math.exp(-0.3 * l)
        h = _ffn(h, row(ffn1_norm[l]), ffn1_w_gate[l].astype(BF16), ffn1_w_up[l].astype(BF16),
                 ffn1_w_down[l].astype(BF16), row(final_norm), final=False)
        za, zg, zp, zc, zd, vt, zc4, zc16 = _in_proj(h, row(mix_norm[l]), *_in_proj_weight(w_in[l]))
        ya = _mlstm(za, zg, mlstm_conv_w[l], row(mlstm_conv_b[l]),
                    row(_per_head_lanes(mlstm_gate_b[l].reshape(2, HEADS))), row(mlstm_norm[l]),
                    batch=batch, seq=seq)
        yb = _pool(zp, _block_diag(pool_w[l]).astype(BF16), row(pool_scale[l]), batch=batch, seq=seq)
        yc = _dilated((zc, zc4, zc16), dil_bias, batch=batch, seq=seq)
        yd = _diff_attention(zd, vt, diff_bias, diff_limit2, diff_lambda[l],
                             jnp.broadcast_to(diff_subln[l][:, None], (HEAD_DIM, DIFF_Q)),
                             lam_init=lam_init, batch=batch, seq=seq)
        k_mem, v_mem = _mem_kv(mem2, row(mem_norm[l]), xattn_wkv[l].astype(BF16))
        h = _xattn(h, (ya, yb, yc, yd), w_out[l].astype(BF16),
                   row(xattn_norm[l]), xattn_wq[l].astype(BF16), k_mem, v_mem,
                   xattn_wo[l].astype(BF16), batch=batch, seq=seq)
        h = _ffn(h, row(ffn2_norm[l]), ffn2_w_gate[l].astype(BF16), ffn2_w_up[l].astype(BF16),
                 ffn2_w_down[l].astype(BF16), row(final_norm), final=(l == DEPTH - 1))
    return h.reshape(batch, seq, D_MODEL)
```
